```python
import jax
import jax.numpy as jnp
from jax import lax
import numpy as np

D_MODEL = 1024
BATCH = 8
SEQ = 2048
DEPTH = 2

GRID_W = 64
CTX_LEN = 256
HEAD_DIM = 64
NORM_EPS = 1e-6
NEG_INF = -1e30
N_MOD = 6

A_HEADS = 4
A_KV_HEADS = 2
A_WINDOW = 128
A_BLOCK = 128
ROPE_BASE = 10000.0
ROPE_AXIS_DIM = HEAD_DIM // 2

B_HEADS = 4
NA_ROWS = 8
NA_COLS = 16

C_HEADS = 8
C_DECAY_LORA = 64
C_ICLR_LORA = 64
C_GATE_LORA = 128
C_CONV = 3
C_GN_EPS = 64e-5
C_DECAY_SCALE = 0.6065306597126334

A_Q = A_HEADS * HEAD_DIM
A_KV = A_KV_HEADS * HEAD_DIM
B_W = B_HEADS * HEAD_DIM
C_W = C_HEADS * HEAD_DIM
MIX_W = A_Q + B_W + C_W
C_IN = 3 * C_W + 2 * C_DECAY_LORA + 2 * C_ICLR_LORA + C_GATE_LORA
IN_W = A_Q + 2 * A_KV + 3 * B_W + C_IN

PEER_HEADS = 8
PEER_NKEYS = 128
PEER_EXPERTS = PEER_NKEYS * PEER_NKEYS
PEER_QDIM = 256
PEER_TOPK = 16
PEER_CHUNK = 128

kernel_name = 'hybrid_dit_swa_natten_rwkv7_peer'


def rms_norm(x, g):
    xf = x.astype(jnp.float32)
    y = xf * lax.rsqrt(jnp.mean(xf * xf, axis=-1, keepdims=True) + NORM_EPS)
    return (y * g.astype(jnp.float32)).astype(x.dtype)


def split_heads(t, n_heads):
    return t.reshape(t.shape[0], t.shape[1], n_heads, HEAD_DIM)


def axial_rope_tables(n_tokens):
    t = np.arange(n_tokens)
    inv_freq = ROPE_BASE ** (-np.arange(0, ROPE_AXIS_DIM, 2) / ROPE_AXIS_DIM)
    ang = np.stack([(t // GRID_W)[:, None] * inv_freq[None], (t % GRID_W)[:, None] * inv_freq[None]], axis=1)
    return jnp.asarray(np.cos(ang), jnp.float32), jnp.asarray(np.sin(ang), jnp.float32)


def apply_axial_rope(x, cos, sin):
    b, s, h, d = x.shape
    xa = x.reshape(b, s, h, 2, 2, ROPE_AXIS_DIM // 2)
    x1, x2 = xa[..., 0, :], xa[..., 1, :]
    cs = cos[None, :, None].astype(x.dtype)
    sn = sin[None, :, None].astype(x.dtype)
    return jnp.stack([x1 * cs - x2 * sn, x2 * cs + x1 * sn], axis=-2).reshape(b, s, h, d)


def dense_context_attention(q, k, v, sink):
    b, l, hq, d = q.shape
    hkv = k.shape[2]
    qg = q.reshape(b, l, hkv, hq // hkv, d)
    s = jnp.einsum('bqhgd,bkhd->bhgqk', qg, k).astype(jnp.float32) * d ** -0.5
    if sink is not None:
        s_sink = jnp.broadcast_to(sink.astype(jnp.float32).reshape(hkv, hq // hkv, 1, 1), s.shape[:-1] + (1,))
        p = jax.nn.softmax(jnp.concatenate([s, s_sink], axis=-1), axis=-1)[..., :-1]
    else:
        p = jax.nn.softmax(s, axis=-1)
    o = jnp.einsum('bhgqk,bkhd->bqhgd', p.astype(v.dtype), v)
    return o.reshape(b, l, hq * d)


def banded_window_attention(q, k, v, kc, vc, sink):
    b, s, hq, d = q.shape
    hkv = k.shape[2]
    g = hq // hkv
    nb = s // A_BLOCK
    qb = q.reshape(b, nb, A_BLOCK, hkv, g, d)

    def band(t):
        tp = jnp.pad(t, ((0, 0), (A_BLOCK, A_BLOCK), (0, 0), (0, 0))).reshape(b, nb + 2, A_BLOCK, hkv, d)
        return jnp.concatenate([tp[:, :-2], tp[:, 1:-1], tp[:, 2:]], axis=2)

    kb, vb = band(k), band(v)
    qpos = np.arange(s).reshape(nb, A_BLOCK)
    kpos = np.arange(nb)[:, None] * A_BLOCK - A_BLOCK + np.arange(3 * A_BLOCK)[None, :]
    valid = ((kpos[:, None, :] >= 0) & (kpos[:, None, :] < s)
             & (np.abs(kpos[:, None, :] - qpos[:, :, None]) <= A_WINDOW))
    scale = d ** -0.5
    s_loc = jnp.einsum('bnqhgd,bnkhd->bnhgqk', qb, kb).astype(jnp.float32) * scale
    s_loc = jnp.where(valid[None, :, None, None], s_loc, NEG_INF)
    s_ctx = jnp.einsum('bnqhgd,bchd->bnhgqc', qb, kc).astype(jnp.float32) * scale
    s_sink = jnp.broadcast_to(sink.astype(jnp.float32).reshape(1, 1, hkv, g, 1, 1), s_ctx.shape[:-1] + (1,))
    p = jax.nn.softmax(jnp.concatenate([s_loc, s_ctx, s_sink], axis=-1), axis=-1).astype(v.dtype)
    nk = 3 * A_BLOCK
    o = (jnp.einsum('bnhgqk,bnkhd->bnqhgd', p[..., :nk], vb)
         + jnp.einsum('bnhgqc,bchd->bnqhgd', p[..., nk:nk + kc.shape[1]], vc))
    return o.reshape(b, s, hq * d)


def neighbourhood_attention(q, k, v, kc, vc, rpb):
    b, s, h, d = q.shape
    rows = s // GRID_W
    kr = min(NA_ROWS, rows)
    ncb = GRID_W // NA_COLS
    reg_w = 2 * NA_COLS
    n_reg = kr * reg_w
    row = np.arange(rows)
    row_start = np.clip(row - kr // 2, 0, rows - kr)
    blk = np.arange(ncb)
    reg_start = np.clip(blk * NA_COLS - NA_COLS // 2, 0, GRID_W - reg_w)
    key_row = row_start[:, None] + np.arange(kr)[None, :]
    key_col = reg_start[:, None] + np.arange(reg_w)[None, :]
    idx = (key_row[:, None, :, None] * GRID_W + key_col[None, :, None, :]).reshape(-1)
    kg = jnp.take(k, jnp.asarray(idx), axis=1).reshape(b, rows, ncb, n_reg, h, d)
    vg = jnp.take(v, jnp.asarray(idx), axis=1).reshape(b, rows, ncb, n_reg, h, d)
    q_col = blk[:, None] * NA_COLS + np.arange(NA_COLS)[None, :]
    win_start = np.clip(q_col - NA_COLS // 2, 0, GRID_W - NA_COLS)
    col_ok = ((key_col[:, None, :] >= win_start[:, :, None])
              & (key_col[:, None, :] < win_start[:, :, None] + NA_COLS))
    valid = np.broadcast_to(col_ok[:, :, None, :], (ncb, NA_COLS, kr, reg_w)).reshape(ncb, NA_COLS, n_reg)
    dr = key_row - row[:, None] + NA_ROWS - 1
    dc = np.clip(key_col[:, None, :] - q_col[:, :, None] + NA_COLS - 1, 0, 2 * NA_COLS - 2)
    bias = rpb[:, dr[:, None, None, :, None], dc[None, :, :, None, :]].astype(jnp.float32)
    bias = bias.reshape(h, rows, ncb, NA_COLS, n_reg).transpose(1, 2, 0, 3, 4)
    qb = q.reshape(b, rows, ncb, NA_COLS, h, d)
    scale = d ** -0.5
    s_loc = jnp.einsum('brjqhd,brjkhd->brjhqk', qb, kg).astype(jnp.float32) * scale + bias[None]
    s_loc = jnp.where(valid[None, None, :, None], s_loc, NEG_INF)
    s_ctx = jnp.einsum('brjqhd,bchd->brjhqc', qb, kc).astype(jnp.float32) * scale
    p = jax.nn.softmax(jnp.concatenate([s_loc, s_ctx], axis=-1), axis=-1).astype(v.dtype)
    o = (jnp.einsum('brjhqk,brjkhd->brjqhd', p[..., :n_reg], vg)
         + jnp.einsum('brjhqc,bchd->brjqhd', p[..., n_reg:], vc))
    return o.reshape(b, s, h * d)


def centred_conv(p, w):
    pp = jnp.pad(p, ((0, 0), (1, 1), (0, 0)))
    return pp[:, :-2] * w[0] + pp[:, 1:-1] * w[1] + pp[:, 2:] * w[2]


def rwkv7_prepare(p, lp):
    b, t, _ = p.shape
    p = centred_conv(p, lp['r7_conv'])
    cuts = [C_W, 2 * C_W, 3 * C_W, 3 * C_W + 2 * C_DECAY_LORA, 3 * C_W + 2 * C_DECAY_LORA + 2 * C_ICLR_LORA]
    r, k, v, wd, ad, gd = jnp.split(p, cuts, axis=-1)
    g = jax.nn.sigmoid(gd) @ lp['r7_g2']
    kkh = split_heads(k * lp['r7_kk'], C_HEADS).astype(jnp.float32)
    kk = kkh / jnp.maximum(jnp.sqrt(jnp.sum(kkh * kkh, axis=-1, keepdims=True)), 1e-12)
    z_w = lp['r7_w0'] + jnp.einsum('btdr,drc->btdc', jnp.tanh(wd.reshape(b, t, 2, C_DECAY_LORA)), lp['r7_w2'])
    decay = jnp.exp(-C_DECAY_SCALE * jax.nn.sigmoid(z_w.astype(jnp.float32)))
    z_a = lp['r7_a0'] + jnp.einsum('btdr,drc->btdc', ad.reshape(b, t, 2, C_ICLR_LORA), lp['r7_a2'])
    a = jax.nn.sigmoid(z_a.astype(jnp.float32))
    k_dir = k[:, :, None].astype(jnp.float32) * (1.0 + (a - 1.0) * lp['r7_ka'])
    return {'r': r, 'v': v, 'g': g, 'kk': kk, 'decay': decay, 'a': a, 'k': k_dir}


def wkv_scan(state0, r, w, k, v, kk, a, reverse):
    def step(S, inp):
        r_t, w_t, k_t, v_t, kk_t, a_t = inp
        sa = jnp.einsum('bhvk,bhk->bhv', S, -kk_t)
        S = S * w_t[:, :, None, :] + sa[..., None] * (kk_t * a_t)[:, :, None, :] + v_t[..., None] * k_t[:, :, None, :]
        return S, jnp.einsum('bhvk,bhk->bhv', S, r_t)

    xs = tuple(jnp.moveaxis(t, 1, 0) for t in (r, w, k, v, kk, a))
    s_final, ys = lax.scan(step, state0, xs, reverse=reverse)
    return s_final, jnp.moveaxis(ys, 0, 1)


def rwkv7_output(prep, y, lp):
    b, t = y.shape[:2]
    mu = jnp.mean(y, axis=-1, keepdims=True)
    var = jnp.mean(jnp.square(y - mu), axis=-1, keepdims=True)
    yn = ((y - mu) * lax.rsqrt(var + C_GN_EPS)).reshape(b, t, C_W) * lp['r7_lnw'] + lp['r7_lnb']
    rh = split_heads(prep['r'], C_HEADS).astype(jnp.float32)[:, :, None]
    kh = prep['k'].reshape(b, t, 2, C_HEADS, HEAD_DIM)
    bonus = jnp.sum(jnp.sum(rh * kh * lp['r7_rk'], axis=-1), axis=2)
    bonus = (bonus[..., None] * split_heads(prep['v'], C_HEADS).astype(jnp.float32)).reshape(b, t, C_W)
    return ((yn + bonus) * prep['g'].astype(jnp.float32)).astype(prep['r'].dtype)


def rwkv7_mixer(p_lat, p_ctx, lp, with_ctx_out):
    lat, ctx = rwkv7_prepare(p_lat, lp), rwkv7_prepare(p_ctx, lp)

    def scan_inputs(prep, d):
        f = lambda t: split_heads(t, C_HEADS).astype(jnp.float32)
        return (f(prep['r']), f(prep['decay'][:, :, d]), f(prep['k'][:, :, d]), f(prep['v']),
                prep['kk'], f(prep['a'][:, :, d]))

    state0 = jnp.zeros((p_lat.shape[0], C_HEADS, HEAD_DIM, HEAD_DIM), jnp.float32)
    s_f, yc_f = wkv_scan(state0, *scan_inputs(ctx, 0), reverse=False)
    _, yl_f = wkv_scan(s_f, *scan_inputs(lat, 0), reverse=False)
    s_b, yc_b = wkv_scan(state0, *scan_inputs(ctx, 1), reverse=True)
    _, yl_b = wkv_scan(s_b, *scan_inputs(lat, 1), reverse=True)
    out_lat = rwkv7_output(lat, yl_f + yl_b, lp)
    out_ctx = rwkv7_output(ctx, yc_f + yc_b, lp) if with_ctx_out else None
    return out_lat, out_ctx


def mixer_block(h_lat, h_ctx, lp, rope, with_ctx_out):
    cos, sin = rope
    cuts = [A_Q, A_Q + A_KV, A_Q + 2 * A_KV, A_Q + 2 * A_KV + B_W, A_Q + 2 * A_KV + 2 * B_W, A_Q + 2 * A_KV + 3 * B_W]
    aq, ak, av, bq, bk, bv, cx = jnp.split(h_lat @ lp['w_in'], cuts, axis=-1)
    aqc, akc, avc, bqc, bkc, bvc, cxc = jnp.split(h_ctx @ lp['w_in'], cuts, axis=-1)
    qa = apply_axial_rope(rms_norm(split_heads(aq, A_HEADS), lp['a_qnorm']), cos, sin)
    ka = apply_axial_rope(rms_norm(split_heads(ak, A_KV_HEADS), lp['a_knorm']), cos, sin)
    kac = rms_norm(split_heads(akc, A_KV_HEADS), lp['a_knorm'])
    vac = split_heads(avc, A_KV_HEADS)
    o_a = banded_window_attention(qa, ka, split_heads(av, A_KV_HEADS), kac, vac, lp['a_sink'])
    qb = rms_norm(split_heads(bq, B_HEADS), lp['b_qnorm'])
    kb = rms_norm(split_heads(bk, B_HEADS), lp['b_knorm'])
    kbc = rms_norm(split_heads(bkc, B_HEADS), lp['b_knorm'])
    vbc = split_heads(bvc, B_HEADS)
    o_b = neighbourhood_attention(qb, kb, split_heads(bv, B_HEADS), kbc, vbc, lp['b_rpb'])
    o_c, o_c_ctx = rwkv7_mixer(cx, cxc, lp, with_ctx_out)
    y_lat = jnp.concatenate([o_a, o_b, o_c], axis=-1) @ lp['w_out']
    if not with_ctx_out:
        return y_lat, None
    o_ac = dense_context_attention(rms_norm(split_heads(aqc, A_HEADS), lp['a_qnorm']), kac, vac, lp['a_sink'])
    o_bc = dense_context_attention(rms_norm(split_heads(bqc, B_HEADS), lp['b_qnorm']), kbc, vbc, None)
    y_ctx = jnp.concatenate([o_ac, o_bc, o_c_ctx], axis=-1) @ lp['w_out']
    return y_lat, y_ctx


def peer_ffn(h, wq, sub_keys, u, v):
    b, t, d = h.shape
    hc = h.reshape(b * t // PEER_CHUNK, PEER_CHUNK, d)

    def chunk(hb):
        q = (hb @ wq).reshape(PEER_CHUNK, PEER_HEADS, 2, PEER_QDIM // 2)
        s = jnp.einsum('thpd,hpnd->thpn', q, sub_keys).astype(jnp.float32)
        sv, si = lax.top_k(s, PEER_TOPK)
        cand = (sv[:, :, 0, :, None] + sv[:, :, 1, None, :]).reshape(PEER_CHUNK, PEER_HEADS, PEER_TOPK * PEER_TOPK)
        best, ci = lax.top_k(cand, PEER_TOPK)
        i1 = jnp.take_along_axis(si[:, :, 0], ci // PEER_TOPK, axis=-1)
        i2 = jnp.take_along_axis(si[:, :, 1], ci % PEER_TOPK, axis=-1)
        expert = i1 * PEER_NKEYS + i2
        gate = jax.nn.softmax(best, axis=-1)
        ue = jnp.take(u, expert, axis=0)
        ve = jnp.take(v, expert, axis=0)
        act = jax.nn.gelu(jnp.einsum('td,thkd->thk', hb, ue).astype(jnp.float32), approximate=False)
        return jnp.einsum('thk,thkd->td', (gate * act).astype(hb.dtype), ve)

    return lax.map(chunk, hc).reshape(b, t, d)


def setup_inputs(seed: int = 0) -> dict:
    key = jax.random.key(seed)
    ks = iter(jax.random.split(key, 40))

    def nrm(shape, scale):
        return scale * jax.random.normal(next(ks), shape, jnp.float32)

    L, D = DEPTH, D_MODEL
    decay_base = -6.0 + 5.5 * jnp.linspace(0.0, 1.0, C_W, dtype=jnp.float32) ** 0.9
    conv_base = jnp.array([0.25, 1.0, 0.25], jnp.float32)[None, :, None]
    return {
        'x': nrm((BATCH, SEQ, D), 1.0),
        'c': nrm((BATCH, D), 1.0),
        'ctx': nrm((BATCH, CTX_LEN, D), 1.0),
        'c_ctx': nrm((D,), 1.0),
        'norm_mix': 1.0 + nrm((L, D), 0.05),
        'norm_ffn': 1.0 + nrm((L, D), 0.05),
        'w_mod': nrm((L, D, N_MOD * D), 0.5 * D ** -0.5),
        'b_mod': nrm((L, N_MOD * D), 0.02),
        'w_in': nrm((L, D, IN_W), D ** -0.5),
        'w_out': nrm((L, MIX_W, D), MIX_W ** -0.5),
        'a_qnorm': 1.0 + nrm((L, HEAD_DIM), 0.05),
        'a_knorm': 1.0 + nrm((L, HEAD_DIM), 0.05),
        'a_sink': nrm((L, A_HEADS), 0.5),
        'b_qnorm': 1.0 + nrm((L, HEAD_DIM), 0.05),
        'b_knorm': 1.0 + nrm((L, HEAD_DIM), 0.05),
        'b_rpb': nrm((L, B_HEADS, 2 * NA_ROWS - 1, 2 * NA_COLS - 1), 0.1),
        'r7_conv': conv_base + nrm((L, C_CONV, C_IN), 0.05),
        'r7_w0': decay_base + nrm((L, 2, C_W), 0.3),
        'r7_w2': nrm((L, 2, C_DECAY_LORA, C_W), 0.5 * C_DECAY_LORA ** -0.5),
        'r7_a0': nrm((L, 2, C_W), 0.3),
        'r7_a2': nrm((L, 2, C_ICLR_LORA, C_W), 0.5 * C_ICLR_LORA ** -0.5),
        'r7_g2': nrm((L, C_GATE_LORA, C_W), C_GATE_LORA ** -0.5),
        'r7_kk': 0.85 + nrm((L, C_W), 0.05),
        'r7_ka': 1.0 + nrm((L, C_W), 0.05),
        'r7_rk': nrm((L, C_HEADS, HEAD_DIM), 0.1),
        'r7_lnw': 1.0 + nrm((L, C_W), 0.05),
        'r7_lnb': nrm((L, C_W), 0.02),
        'peer_wq': nrm((L, D, PEER_HEADS * PEER_QDIM), D ** -0.5),
        'peer_keys': nrm((L, PEER_HEADS, 2, PEER_NKEYS, PEER_QDIM // 2), (PEER_QDIM // 2) ** -0.5),
        'peer_u': nrm((L, PEER_EXPERTS, D), D ** -0.5),
        'peer_v': nrm((L, PEER_EXPERTS, D), 0.3),
    }


def reference(x, c, ctx, c_ctx, norm_mix, norm_ffn, w_mod, b_mod, w_in, w_out,
              a_qnorm, a_knorm, a_sink, b_qnorm, b_knorm, b_rpb,
              r7_conv, r7_w0, r7_w2, r7_a0, r7_a2, r7_g2, r7_kk, r7_ka, r7_rk, r7_lnw, r7_lnb,
              peer_wq, peer_keys, peer_u, peer_v):
    rope = axial_rope_tables(x.shape[1])
    x_lat, x_ctx = x, ctx
    silu_c = jax.nn.silu(c)
    silu_cc = jax.nn.silu(c_ctx)
    for l in range(DEPTH):
        with_ctx = l < DEPTH - 1
        mod = (silu_c @ w_mod[l] + b_mod[l])[:, None, :]
        mod_c = silu_cc @ w_mod[l] + b_mod[l]
        sh1, sc1, g1, sh2, sc2, g2 = jnp.split(mod, N_MOD, axis=-1)
        csh1, csc1, cg1, csh2, csc2, cg2 = jnp.split(mod_c, N_MOD, axis=-1)
        lp = {
            'w_in': w_in[l], 'w_out': w_out[l],
            'a_qnorm': a_qnorm[l], 'a_knorm': a_knorm[l], 'a_sink': a_sink[l],
            'b_qnorm': b_qnorm[l], 'b_knorm': b_knorm[l], 'b_rpb': b_rpb[l],
            'r7_conv': r7_conv[l], 'r7_w0': r7_w0[l], 'r7_w2': r7_w2[l], 'r7_a0': r7_a0[l],
            'r7_a2': r7_a2[l], 'r7_g2': r7_g2[l], 'r7_kk': r7_kk[l], 'r7_ka': r7_ka[l],
            'r7_rk': r7_rk[l], 'r7_lnw': r7_lnw[l], 'r7_lnb': r7_lnb[l],
        }
        h_lat = rms_norm(x_lat, norm_mix[l]) * (1.0 + sc1) + sh1
        h_ctx = rms_norm(x_ctx, norm_mix[l]) * (1.0 + csc1) + csh1
        y_lat, y_ctx = mixer_block(h_lat, h_ctx, lp, rope, with_ctx)
        x_lat = x_lat + g1 * y_lat
        h2 = rms_norm(x_lat, norm_ffn[l]) * (1.0 + sc2) + sh2
        x_lat = x_lat + g2 * peer_ffn(h2, peer_wq[l], peer_keys[l], peer_u[l], peer_v[l])
        if with_ctx:
            x_ctx = x_ctx + cg1 * y_ctx
            h2c = rms_norm(x_ctx, norm_ffn[l]) * (1.0 + csc2) + csh2
            x_ctx = x_ctx + cg2 * peer_ffn(h2c, peer_wq[l], peer_keys[l], peer_u[l], peer_v[l])
    return x_lat
```

```python
import functools

import numpy as np
import jax
import jax.numpy as jnp
from jax import lax
from jax.experimental import pallas as pl
from jax.experimental.pallas import tpu as pltpu

F32 = jnp.float32
BF16 = jnp.bfloat16
HI = lax.Precision.HIGHEST

D = 1024
DEPTH = 2
GRID_W = 64
HD = 64
EPS = 1e-6
NEG = -1e30
A_HEADS, A_KV, A_BLOCK, A_WINDOW = 4, 2, 128, 128
B_HEADS, NA_ROWS, NA_COLS = 4, 8, 16
C_HEADS = 8
C_W = 512
C_IN = 1920
AB_W = 1280
IN_W = AB_W + C_IN
GN_EPS = 64e-5
DECAY_SCALE = 0.6065306597126334
ROPE_BASE = 10000.0
P_HEADS, P_NKEYS, P_QDIM, P_TOPK = 8, 128, 256, 16
P_EXPERTS = P_NKEYS * P_NKEYS
SCAN_C = 64
N_CAND = 80
VMEM_LIMIT = 48 * 1024 * 1024


def _cp(*sem):
    return pltpu.CompilerParams(dimension_semantics=sem, vmem_limit_bytes=VMEM_LIMIT)


def _dot(a, b, prec=None):
    return jnp.dot(a, b, precision=prec, preferred_element_type=F32)


def _dot_nt(a, b, prec=None):
    return lax.dot_general(a, b, (((1,), (1,)), ((), ())), precision=prec, preferred_element_type=F32)


def _block_diag_ones(n):
    i = np.arange(n) // HD
    return jnp.asarray((i[:, None] == i[None, :]).astype(np.float32))


def _mod_kernel(c_ref, w_ref, b_ref, o_ref):
    c = c_ref[...]
    s = c * jax.nn.sigmoid(c)
    o_ref[...] = _dot(s, w_ref[...], HI) + b_ref[...]


def _modulation(cc, w_mod, b_mod):
    L, _, n = w_mod.shape
    tn = 512
    return pl.pallas_call(
        _mod_kernel,
        grid=(L, n // tn),
        in_specs=[pl.BlockSpec((16, D), lambda l, j: (0, 0)),
                  pl.BlockSpec((None, D, tn), lambda l, j: (l, 0, j)),
                  pl.BlockSpec((None, 1, tn), lambda l, j: (l, 0, j))],
        out_specs=pl.BlockSpec((None, 16, tn), lambda l, j: (l, 0, j)),
        out_shape=jax.ShapeDtypeStruct((L, 16, n), F32),
        compiler_params=_cp("parallel", "parallel"),
        name="modulation",
    )(cc, w_mod, b_mod.reshape(L, 1, n))


def _swap16(x):
    n = x.shape[-1]
    lane = lax.broadcasted_iota(jnp.int32, x.shape, 1)
    fwd = pltpu.roll(x, n - 16, 1)
    bwd = pltpu.roll(x, 16, 1)
    return jnp.where((lane % 32) < 16, fwd, bwd)


def _head_rms(x, bd, gain):
    ss = _dot(x * x, bd, HI)
    return x * lax.rsqrt(ss * (1.0 / HD) + EPS) * gain


def _inproj_kernel(*refs, rope):
    if rope:
        (x_ref, g_ref, sc_ref, sh_ref, w_ref, ga_ref, gb_ref, bda_ref, bdb_ref, cos_ref, sin_ref,
         ab_ref, c_ref) = refs
    else:
        (x_ref, g_ref, sc_ref, sh_ref, w_ref, ga_ref, gb_ref, bda_ref, bdb_ref, ab_ref, c_ref) = refs
    x = x_ref[...]
    y = x * lax.rsqrt(jnp.mean(x * x, axis=-1, keepdims=True) + EPS) * g_ref[...]
    h = y * (1.0 + sc_ref[...]) + sh_ref[...]
    acc = _dot(h.astype(BF16), w_ref[...])
    qa = _head_rms(acc[:, 0:384], bda_ref[...], ga_ref[...])
    if rope:
        qa = qa * cos_ref[...] + _swap16(qa) * sin_ref[...]
    qb = _head_rms(acc[:, 512:1024], bdb_ref[...], gb_ref[...])
    ab_ref[:, 0:384] = qa.astype(BF16)
    ab_ref[:, 384:512] = acc[:, 384:512].astype(BF16)
    ab_ref[:, 512:1024] = qb.astype(BF16)
    ab_ref[:, 1024:1280] = acc[:, 1024:1280].astype(BF16)
    c_ref[...] = acc[:, AB_W:IN_W]


def _inproj(x2, mod3, mod_row_of_batch, norm_g, w_in_bf, gain_a, gain_b, rope_tabs, nb, t):
    tm = 256
    nt = t // tm
    rope = rope_tabs is not None
    in_specs = [
        pl.BlockSpec((tm, D), lambda b, i: (b * nt + i, 0)),
        pl.BlockSpec((1, D), lambda b, i: (0, 0)),
        pl.BlockSpec((None, 1, D), lambda b, i: (mod_row_of_batch(b), 0, 1)),
        pl.BlockSpec((None, 1, D), lambda b, i: (mod_row_of_batch(b), 0, 0)),
        pl.BlockSpec((D, IN_W), lambda b, i: (0, 0)),
        pl.BlockSpec((1, 384), lambda b, i: (0, 0)),
        pl.BlockSpec((1, 512), lambda b, i: (0, 0)),
        pl.BlockSpec((384, 384), lambda b, i: (0, 0)),
        pl.BlockSpec((512, 512), lambda b, i: (0, 0)),
    ]
    args = [x2, norm_g.reshape(1, D), mod3, mod3, w_in_bf, gain_a, gain_b,
            _block_diag_ones(384), _block_diag_ones(512)]
    if rope:
        in_specs += [pl.BlockSpec((tm, 384), lambda b, i: (i, 0)),
                     pl.BlockSpec((tm, 384), lambda b, i: (i, 0))]
        args += list(rope_tabs)
    return pl.pallas_call(
        functools.partial(_inproj_kernel, rope=rope),
        grid=(nb, nt),
        in_specs=in_specs,
        out_specs=[pl.BlockSpec((tm, AB_W), lambda b, i: (b * nt + i, 0)),
                   pl.BlockSpec((tm, C_IN), lambda b, i: (b * nt + i, 0))],
        out_shape=[jax.ShapeDtypeStruct((nb * t, AB_W), BF16),
                   jax.ShapeDtypeStruct((nb * t, C_IN), F32)],
        compiler_params=_cp("parallel", "parallel"),
        name="inproj_rope" if rope else "inproj_ctx",
    )(*args)


def _rope_tables(s):
    tok = np.arange(s)
    inv = ROPE_BASE ** (-np.arange(0, 32, 2) / 32.0)
    ar = (tok // GRID_W)[:, None] * inv[None]
    ac = (tok % GRID_W)[:, None] * inv[None]
    cos = np.concatenate([np.cos(ar), np.cos(ar), np.cos(ac), np.cos(ac)], axis=1)
    sin = np.concatenate([-np.sin(ar), np.sin(ar), -np.sin(ac), np.sin(ac)], axis=1)
    return (jnp.asarray(np.tile(cos, (1, 6)), F32), jnp.asarray(np.tile(sin, (1, 6)), F32))


def _softmax_pv(s, v, sink=None):
    m = jnp.max(s, axis=-1, keepdims=True)
    if sink is not None:
        m = jnp.maximum(m, sink)
    p = jnp.exp(s - m)
    den = jnp.sum(p, axis=-1, keepdims=True)
    if sink is not None:
        den = den + jnp.exp(sink - m)
    return _dot(p.astype(BF16), v) / den


def _attn_a_kernel(q_ref, k0_ref, k1_ref, k2_ref, v0_ref, v1_ref, v2_ref, kc_ref, vc_ref, sink_ref, o_ref,
                   *, nblk):
    n = pl.program_id(1)
    q = q_ref[...]
    k_all = jnp.concatenate([k0_ref[...], k1_ref[...], k2_ref[...], kc_ref[...]], axis=0)
    v_all = jnp.concatenate([v0_ref[...], v1_ref[...], v2_ref[...], vc_ref[...]], axis=0)
    nk = 3 * A_BLOCK + kc_ref.shape[0]
    g = A_HEADS // A_KV
    row = lax.broadcasted_iota(jnp.int32, (g * A_BLOCK, nk), 0) % A_BLOCK
    col = lax.broadcasted_iota(jnp.int32, (g * A_BLOCK, nk), 1)
    rel = col - A_BLOCK - row
    ok = (jnp.abs(rel) <= A_WINDOW) & ((n > 0) | (col >= A_BLOCK)) & ((n < nblk - 1) | (col < 2 * A_BLOCK))
    ok = ok | (col >= 3 * A_BLOCK)
    for hk in range(A_KV):
        qg = jnp.concatenate([q[:, (hk * g + j) * HD:(hk * g + j + 1) * HD] for j in range(g)], axis=0)
        s = _dot_nt(qg, k_all[:, hk * HD:(hk + 1) * HD])
        s = jnp.where(ok, s, NEG)
        sink = jnp.concatenate(
            [jnp.broadcast_to(sink_ref[:, hk * g + j:hk * g + j + 1], (A_BLOCK, 1)) for j in range(g)], axis=0)
        o = _softmax_pv(s, v_all[:, hk * HD:(hk + 1) * HD], sink)
        for j in range(g):
            hq = hk * g + j
            o_ref[:, hq * HD:(hq + 1) * HD] = o[j * A_BLOCK:(j + 1) * A_BLOCK].astype(BF16)


def _attn_a(ab_lat, ab_ctx, sink, nb, s, lc):
    nblk = s // A_BLOCK

    def kv(col, d):
        return pl.BlockSpec((A_BLOCK, 128), lambda b, n: (b * nblk + jnp.clip(n + d, 0, nblk - 1), col))

    return pl.pallas_call(
        functools.partial(_attn_a_kernel, nblk=nblk),
        grid=(nb, nblk),
        in_specs=[pl.BlockSpec((A_BLOCK, 256), lambda b, n: (b * nblk + n, 0)),
                  kv(2, -1), kv(2, 0), kv(2, 1), kv(3, -1), kv(3, 0), kv(3, 1),
                  pl.BlockSpec((lc, 128), lambda b, n: (b, 2)),
                  pl.BlockSpec((lc, 128), lambda b, n: (b, 3)),
                  pl.BlockSpec((1, A_HEADS), lambda b, n: (0, 0))],
        out_specs=pl.BlockSpec((A_BLOCK, 256), lambda b, n: (b * nblk + n, 0)),
        out_shape=jax.ShapeDtypeStruct((nb * s, 256), BF16),
        compiler_params=_cp("parallel", "parallel"),
        name="attn_a",
    )(ab_lat, ab_lat, ab_lat, ab_lat, ab_lat, ab_lat, ab_lat, ab_ctx, ab_ctx, sink.reshape(1, A_HEADS))


def _attn_b_kernel(q_ref, k_ref, v_ref, kc_ref, vc_ref, bias_ref, o_ref, *, rows):
    r = pl.program_id(1)
    rs = jnp.clip(r - NA_ROWS // 2, 0, rows - NA_ROWS)
    start = pl.multiple_of(rs * GRID_W, GRID_W)
    nwin = NA_ROWS * GRID_W
    q = q_ref[...]
    kw = k_ref[pl.ds(start, nwin), :]
    vw = v_ref[pl.ds(start, nwin), :]
    kc = kc_ref[...]
    vc = vc_ref[...]
    for h in range(B_HEADS):
        sl = slice(h * HD, (h + 1) * HD)
        s_loc = _dot_nt(q[:, sl], kw[:, sl]) + bias_ref[h]
        s_ctx = _dot_nt(q[:, sl], kc[:, sl])
        m = jnp.maximum(jnp.max(s_loc, axis=-1, keepdims=True), jnp.max(s_ctx, axis=-1, keepdims=True))
        p_loc = jnp.exp(s_loc - m)
        p_ctx = jnp.exp(s_ctx - m)
        den = jnp.sum(p_loc, axis=-1, keepdims=True) + jnp.sum(p_ctx, axis=-1, keepdims=True)
        o = (_dot(p_loc.astype(BF16), vw[:, sl]) + _dot(p_ctx.astype(BF16), vc[:, sl])) / den
        o_ref[:, sl] = o.astype(BF16)


def _na_bias_tables(rpb, rows):
    kr = NA_ROWS
    qc = np.arange(GRID_W)
    win_start = np.clip(qc - NA_COLS // 2, 0, GRID_W - NA_COLS)
    kcol = np.arange(GRID_W)
    valid = (kcol[None, :] >= win_start[:, None]) & (kcol[None, :] < win_start[:, None] + NA_COLS)
    dc = np.clip(kcol[None, :] - qc[:, None] + NA_COLS - 1, 0, 2 * NA_COLS - 2)
    tabs = []
    for off in range(kr):
        dr = off + np.arange(kr)
        dr_ok = (dr >= 0) & (dr < 2 * NA_ROWS - 1)
        drc = np.clip(dr, 0, 2 * NA_ROWS - 2)
        b = rpb[:, drc[:, None, None], dc[None, :, :]].astype(F32)
        b = jnp.where(jnp.asarray(valid[None, None] & dr_ok[None, :, None, None]), b, NEG)
        tabs.append(b.transpose(0, 2, 1, 3).reshape(rpb.shape[0], GRID_W, kr * GRID_W))
    return jnp.stack(tabs, axis=0)


def _attn_b(ab_lat, ab_ctx, bias_tabs, nb, s, lc):
    rows = s // GRID_W

    def tab_index(b, r):
        return (jnp.clip(r - NA_ROWS // 2, 0, rows - NA_ROWS) - r + NA_ROWS - 1, 0, 0, 0)

    return pl.pallas_call(
        functools.partial(_attn_b_kernel, rows=rows),
        grid=(nb, rows),
        in_specs=[pl.BlockSpec((GRID_W, 256), lambda b, r: (b * rows + r, 2)),
                  pl.BlockSpec((s, 256), lambda b, r: (b, 3)),
                  pl.BlockSpec((s, 256), lambda b, r: (b, 4)),
                  pl.BlockSpec((lc, 256), lambda b, r: (b, 3)),
                  pl.BlockSpec((lc, 256), lambda b, r: (b, 4)),
                  pl.BlockSpec((None, B_HEADS, GRID_W, NA_ROWS * GRID_W), tab_index)],
        out_specs=pl.BlockSpec((GRID_W, 256), lambda b, r: (b * rows + r, 0)),
        out_shape=jax.ShapeDtypeStruct((nb * s, 256), BF16),
        compiler_params=_cp("parallel", "arbitrary"),
        name="attn_b",
    )(ab_lat, ab_lat, ab_lat, ab_ctx, ab_ctx, bias_tabs)


def _attn_ctx_kernel(ab_ref, sink_ref, oa_ref, ob_ref):
    lc = ab_ref.shape[0]
    g = A_HEADS // A_KV
    for hk in range(A_KV):
        qg = jnp.concatenate([ab_ref[:, (hk * g + j) * HD:(hk * g + j + 1) * HD] for j in range(g)], axis=0)
        s = _dot_nt(qg, ab_ref[:, 256 + hk * HD:256 + (hk + 1) * HD])
        sink = jnp.concatenate(
            [jnp.broadcast_to(sink_ref[:, hk * g + j:hk * g + j + 1], (lc, 1)) for j in range(g)], axis=0)
        o = _softmax_pv(s, ab_ref[:, 384 + hk * HD:384 + (hk + 1) * HD], sink)
        for j in range(g):
            hq = hk * g + j
            oa_ref[:, hq * HD:(hq + 1) * HD] = o[j * lc:(j + 1) * lc].astype(BF16)
    for h in range(B_HEADS):
        s = _dot_nt(ab_ref[:, 512 + h * HD:512 + (h + 1) * HD], ab_ref[:, 768 + h * HD:768 + (h + 1) * HD])
        o = _softmax_pv(s, ab_ref[:, 1024 + h * HD:1024 + (h + 1) * HD])
        ob_ref[:, h * HD:(h + 1) * HD] = o.astype(BF16)


def _attn_ctx(ab_ctx, sink, nb, lc):
    return pl.pallas_call(
        _attn_ctx_kernel,
        grid=(nb,),
        in_specs=[pl.BlockSpec((lc, AB_W), lambda b: (b, 0)),
                  pl.BlockSpec((1, A_HEADS), lambda b: (0, 0))],
        out_specs=[pl.BlockSpec((lc, 256), lambda b: (b, 0)),
                   pl.BlockSpec((lc, 256), lambda b: (b, 0))],
        out_shape=[jax.ShapeDtypeStruct((nb * lc, 256), BF16),
                   jax.ShapeDtypeStruct((nb * lc, 256), BF16)],
        compiler_params=_cp("parallel"),
        name="attn_ctx",
    )(ab_ctx, sink.reshape(1, A_HEADS))


def _r7prep_kernel(x_ref, prev_ref, next_ref, cw_ref, kkw_ref, ka_ref, rk_ref, w0_ref, w2_ref, a0_ref, a2_ref,
                   g2_ref, bd_ref,
                   r_ref, v_ref, kk_ref, lw_ref, beta_ref, kd_ref, g_ref, bonus_ref, pad_ref, *, nt):
    i = pl.program_id(1)
    tt = x_ref.shape[0]
    x = x_ref[...]
    pad_ref[8:8 + tt, :] = x
    pad_ref[7:8, :] = jnp.where(i > 0, prev_ref[7:8, :], 0.0)
    pad_ref[8 + tt:9 + tt, :] = jnp.where(i < nt - 1, next_ref[0:1, :], 0.0)
    xc = pad_ref[7:7 + tt, :] * cw_ref[0:1, :] + x * cw_ref[1:2, :] + pad_ref[9:9 + tt, :] * cw_ref[2:3, :]
    r = xc[:, 0:512]
    k = xc[:, 512:1024]
    v = xc[:, 1024:1536]
    wd = xc[:, 1536:1664]
    ad = xc[:, 1664:1792]
    gd = xc[:, 1792:1920]
    bd = bd_ref[...]
    kkh = k * kkw_ref[...]
    kk = kkh / jnp.maximum(jnp.sqrt(_dot(kkh * kkh, bd, HI)), 1e-12)
    zw = w0_ref[...] + _dot(jnp.tanh(wd), w2_ref[...], HI)
    za = a0_ref[...] + _dot(ad, a2_ref[...], HI)
    a = jax.nn.sigmoid(za)
    ka = ka_ref[...]
    kd0 = k * (1.0 + (a[:, 0:512] - 1.0) * ka)
    kd1 = k * (1.0 + (a[:, 512:1024] - 1.0) * ka)
    r_ref[...] = r
    v_ref[...] = v
    kk_ref[...] = kk
    lw_ref[...] = -DECAY_SCALE * jax.nn.sigmoid(zw)
    beta_ref[:, 0:512] = kk * a[:, 0:512]
    beta_ref[:, 512:1024] = kk * a[:, 512:1024]
    kd_ref[:, 0:512] = kd0
    kd_ref[:, 512:1024] = kd1
    g_ref[...] = _dot(jax.nn.sigmoid(gd), g2_ref[...], HI)
    bonus_ref[...] = _dot(r * (kd0 + kd1) * rk_ref[...], bd, HI) * v


def _bd2(w):
    z = jnp.zeros_like(w[0])
    return jnp.concatenate([jnp.concatenate([w[0], z], axis=1), jnp.concatenate([z, w[1]], axis=1)], axis=0)


def _r7prep(pc, lp, nb, t):
    tt = 256
    nt = t // tt
    nrow8 = t // 8

    def full(shape):
        return pl.BlockSpec(shape, lambda b, i: (0,) * len(shape))

    def rowspec(w):
        return pl.BlockSpec((tt, w), lambda b, i: (b * nt + i, 0))

    outs = [C_W, C_W, C_W, 2 * C_W, 2 * C_W, 2 * C_W, C_W, C_W]
    return pl.pallas_call(
        functools.partial(_r7prep_kernel, nt=nt),
        grid=(nb, nt),
        in_specs=[rowspec(C_IN),
                  pl.BlockSpec((8, C_IN), lambda b, i: (b * nrow8 + jnp.maximum(i * (tt // 8) - 1, 0), 0)),
                  pl.BlockSpec((8, C_IN), lambda b, i: (b * nrow8 + jnp.minimum((i + 1) * (tt // 8), nrow8 - 1), 0)),
                  full((3, C_IN)), full((1, C_W)), full((1, C_W)), full((1, C_W)),
                  full((1, 2 * C_W)), full((128, 2 * C_W)), full((1, 2 * C_W)), full((128, 2 * C_W)),
                  full((128, C_W)), full((C_W, C_W))],
        out_specs=[rowspec(w) for w in outs],
        out_shape=[jax.ShapeDtypeStruct((nb * t, w), F32) for w in outs],
        scratch_shapes=[pltpu.VMEM((tt + 16, C_IN), F32)],
        compiler_params=_cp("parallel", "parallel"),
        name="r7prep",
    )(pc, pc, pc, lp['r7_conv'], lp['r7_kk'].reshape(1, C_W), lp['r7_ka'].reshape(1, C_W),
      lp['r7_rk'].reshape(1, C_W), lp['r7_w0'].reshape(1, 2 * C_W), _bd2(lp['r7_w2']),
      lp['r7_a0'].reshape(1, 2 * C_W), _bd2(lp['r7_a2']), lp['r7_g2'], _block_diag_ones(C_W))


def _unit_lower_inverse(l_mat, eye):
    t_mat = eye + l_mat
    p = l_mat
    n = 2
    while n < SCAN_C:
        p = _dot(p, p, HI)
        t_mat = t_mat + _dot(t_mat, p, HI)
        n *= 2
    return t_mat


def _scan_kernel(r_ref, v_ref, kk_ref, lw_ref, beta_ref, kd_ref, s0_ref, y_ref, sf_ref, st_ref, *, nc):
    d = pl.program_id(0)
    c = pl.program_id(2)
    cl = SCAN_C

    @pl.when(c == 0)
    def _():
        st_ref[...] = s0_ref[...]

    ti = lax.broadcasted_iota(jnp.int32, (cl, cl), 0)
    si = lax.broadcasted_iota(jnp.int32, (cl, cl), 1)
    rel = (si - ti) * (1 - 2 * d)
    incl = rel <= 0
    strict = rel < 0
    tri = incl.astype(F32)
    eye = (ti == si).astype(F32)
    for h in range(C_HEADS):
        sl = slice(h * HD, (h + 1) * HD)
        lw = lw_ref[:, sl]
        cum = _dot(tri, lw, HI)
        einv = jnp.exp(-cum)
        vv = v_ref[:, sl]
        r_hat = r_ref[:, sl] * jnp.exp(cum)
        a_hat = -kk_ref[:, sl] * jnp.exp(cum - lw)
        b_til = beta_ref[:, sl] * einv
        k_til = kd_ref[:, sl] * einv
        ar = jnp.concatenate([a_hat, r_hat], axis=0)
        bk = jnp.concatenate([b_til, k_til], axis=0)
        m = _dot_nt(ar, bk, HI)
        l_b = jnp.where(strict, m[0:cl, 0:cl], 0.0)
        l_k = jnp.where(strict, m[0:cl, cl:2 * cl], 0.0)
        r_b = jnp.where(incl, m[cl:2 * cl, 0:cl], 0.0)
        r_k = jnp.where(incl, m[cl:2 * cl, cl:2 * cl], 0.0)
        t_inv = _unit_lower_inverse(l_b, eye)
        ua = _dot(t_inv, jnp.concatenate([_dot(l_k, vv, HI), a_hat], axis=1), HI)
        y0 = _dot(r_k, vv, HI)
        s0 = st_ref[h]
        as0 = _dot_nt(jnp.concatenate([ua[:, HD:2 * HD], r_hat], axis=0), s0, HI)
        u = ua[:, 0:HD] + as0[0:cl]
        y_ref[:, sl] = y0 + as0[cl:2 * cl] + _dot(r_b, u, HI)
        uv_t = jnp.concatenate([u, vv], axis=0).T
        wtot = jnp.exp(jnp.sum(lw, axis=0, keepdims=True))
        st_ref[h] = (s0 + _dot(uv_t, bk, HI)) * wtot

    @pl.when(c == nc - 1)
    def _():
        sf_ref[...] = st_ref[...]


def _scan(prep, s0, nb, t):
    r, v, kk, lw, beta, kd = prep
    nc = t // SCAN_C

    def rows(d, b, c):
        return b * nc + c + d * (nc - 1 - 2 * c)

    shared = pl.BlockSpec((SCAN_C, C_W), lambda d, b, c: (rows(d, b, c), 0))
    perdir = pl.BlockSpec((SCAN_C, C_W), lambda d, b, c: (rows(d, b, c), d))
    st_spec = pl.BlockSpec((None, None, C_HEADS, HD, HD), lambda d, b, c: (d, b, 0, 0, 0))
    return pl.pallas_call(
        functools.partial(_scan_kernel, nc=nc),
        grid=(2, nb, nc),
        in_specs=[shared, shared, shared, perdir, perdir, perdir, st_spec],
        out_specs=[pl.BlockSpec((None, SCAN_C, C_W), lambda d, b, c: (d, rows(d, b, c), 0)), st_spec],
        out_shape=[jax.ShapeDtypeStruct((2, nb * t, C_W), F32),
                   jax.ShapeDtypeStruct((2, nb, C_HEADS, HD, HD), F32)],
        scratch_shapes=[pltpu.VMEM((C_HEADS, HD, HD), F32)],
        compiler_params=_cp("parallel", "parallel", "arbitrary"),
        name="r7scan",
    )(r, v, kk, lw, beta, kd, s0)


def _outproj_kernel(x_ref, oa_ref, ob_ref, y0_ref, y1_ref, bonus_ref, g_ref, lnw_ref, lnb_ref, bd_ref, w_ref,
                    gate_ref, o_ref):
    bd = bd_ref[...]
    y = y0_ref[...] + y1_ref[...]
    mu = _dot(y, bd, HI) * (1.0 / HD)
    yc = y - mu
    var = _dot(yc * yc, bd, HI) * (1.0 / HD)
    yn = yc * lax.rsqrt(var + GN_EPS) * lnw_ref[...] + lnb_ref[...]
    oc = ((yn + bonus_ref[...]) * g_ref[...]).astype(BF16)
    acc = (_dot(oa_ref[...], w_ref[0:256, :]) + _dot(ob_ref[...], w_ref[256:512, :])
           + _dot(oc, w_ref[512:1024, :]))
    o_ref[...] = x_ref[...] + gate_ref[...] * acc


def _outproj(x2, oa, ob, y, bonus, g, lp, w_out_bf, mod3, mod_row_of_batch, nb, t):
    tm = 256
    nt = t // tm

    def rowspec(w):
        return pl.BlockSpec((tm, w), lambda b, i: (b * nt + i, 0))

    def full(shape):
        return pl.BlockSpec(shape, lambda b, i: (0,) * len(shape))

    return pl.pallas_call(
        _outproj_kernel,
        grid=(nb, nt),
        in_specs=[rowspec(D), rowspec(256), rowspec(256),
                  pl.BlockSpec((None, tm, C_W), lambda b, i: (0, b * nt + i, 0)),
                  pl.BlockSpec((None, tm, C_W), lambda b, i: (1, b * nt + i, 0)),
                  rowspec(C_W), rowspec(C_W), full((1, C_W)), full((1, C_W)), full((C_W, C_W)), full((D, D)),
                  pl.BlockSpec((None, 1, D), lambda b, i: (mod_row_of_batch(b), 0, 2))],
        out_specs=rowspec(D),
        out_shape=jax.ShapeDtypeStruct((nb * t, D), F32),
        compiler_params=_cp("parallel", "parallel"),
        name="outproj",
    )(x2, oa, ob, y, y, bonus, g, lp['r7_lnw'].reshape(1, C_W), lp['r7_lnb'].reshape(1, C_W),
      _block_diag_ones(C_W), w_out_bf, mod3)


def _peer_score_kernel(x_ref, g_ref, sc_ref, sh_ref, wq_ref, keys_ref, h_ref, st_ref):
    x = x_ref[...]
    y = x * lax.rsqrt(jnp.mean(x * x, axis=-1, keepdims=True) + EPS) * g_ref[...]
    h = (y * (1.0 + sc_ref[...]) + sh_ref[...]).astype(BF16)
    h_ref[...] = h
    q = _dot(h, wq_ref[...])
    half = P_QDIM // 2
    for hp in range(2 * P_HEADS):
        st_ref[hp] = _dot_nt(keys_ref[hp], q[:, hp * half:(hp + 1) * half], HI)


def _peer_scores(x2, norm_g, mod3, mod_row_of_batch, wq_bf, keys, nb, t):
    tm = 256
    nt = t // tm
    n = nb * t
    return pl.pallas_call(
        _peer_score_kernel,
        grid=(nb, nt),
        in_specs=[pl.BlockSpec((tm, D), lambda b, i: (b * nt + i, 0)),
                  pl.BlockSpec((1, D), lambda b, i: (0, 0)),
                  pl.BlockSpec((None, 1, D), lambda b, i: (mod_row_of_batch(b), 0, 4)),
                  pl.BlockSpec((None, 1, D), lambda b, i: (mod_row_of_batch(b), 0, 3)),
                  pl.BlockSpec((D, P_HEADS * P_QDIM), lambda b, i: (0, 0)),
                  pl.BlockSpec((2 * P_HEADS, P_NKEYS, P_QDIM // 2), lambda b, i: (0, 0, 0))],
        out_specs=[pl.BlockSpec((tm, D), lambda b, i: (b * nt + i, 0)),
                   pl.BlockSpec((2 * P_HEADS, P_NKEYS, tm), lambda b, i: (0, 0, b * nt + i))],
        out_shape=[jax.ShapeDtypeStruct((n, D), BF16),
                   jax.ShapeDtypeStruct((2 * P_HEADS, P_NKEYS, n), F32)],
        compiler_params=_cp("parallel", "parallel"),
        name="peer_scores",
    )(x2, norm_g.reshape(1, D), mod3, mod3, wq_bf, keys.reshape(2 * P_HEADS, P_NKEYS, P_QDIM // 2))


def _extract_max(val, ids, big):
    m = jnp.max(val, axis=0, keepdims=True)
    first = jnp.min(jnp.where(val == m, ids, big), axis=0, keepdims=True)
    return m, jnp.where(ids == first, -jnp.inf, val)


def _peer_gate_tables(st_ref, tau_ref, al_ref, be_ref, top_ref, cand_ref):
    tm = st_ref.shape[-1]
    ids = lax.broadcasted_iota(jnp.int32, (P_NKEYS, tm), 0)
    cid = lax.broadcasted_iota(jnp.int32, (N_CAND, tm), 0)
    sub = lax.broadcasted_iota(jnp.int32, (8, tm), 0)
    for h in range(P_HEADS):
        for p in range(2):
            s = st_ref[2 * h + p]
            for j in range(P_TOPK):
                m, s = _extract_max(s, ids, P_NKEYS)
                top_ref[p * P_TOPK + j:p * P_TOPK + j + 1, :] = m
        b16 = top_ref[P_TOPK:2 * P_TOPK, :]
        b8 = b16[0:8]
        cand_ref[0:16, :] = top_ref[0:1, :] + b16
        for p in range(2, 9):
            nq = P_TOPK // p
            cand_ref[8 * p:8 * p + 8, :] = jnp.where(sub < nq, top_ref[p - 1:p, :] + b8, -jnp.inf)
        cand_ref[72:80, :] = top_ref[8:16, :] + b16[0:1]
        cand = cand_ref[...]
        best = []
        for j in range(P_TOPK):
            m, cand = _extract_max(cand, cid, N_CAND)
            best.append(m)
        tau = best[-1]
        z = jnp.ones_like(best[0])
        for j in range(1, P_TOPK):
            z = z + jnp.exp(best[j] - best[0])
        s1 = st_ref[2 * h]
        s2 = st_ref[2 * h + 1]
        tau_ref[h:h + 1, :] = tau
        al_ref[h] = jnp.exp(s1 - top_ref[0:1, :]) / z
        be_ref[h] = jnp.exp(s2 - top_ref[P_TOPK:P_TOPK + 1, :])


def _peer_dense_kernel(x_ref, h_ref, st_ref, u_ref, vt_ref, gate_ref, o_ref,
                       tau_ref, al_ref, be_ref, top_ref, cand_ref, w_ref, acc_ref, *, ne, ec):
    e = pl.program_id(1)

    @pl.when(e == 0)
    def _():
        _peer_gate_tables(st_ref, tau_ref, al_ref, be_ref, top_ref, cand_ref)
        acc_ref[...] = jnp.zeros_like(acc_ref)

    act = _dot_nt(u_ref[...], h_ref[...])
    gl = 0.5 * act * (1.0 + lax.erf(act * (2.0 ** -0.5)))
    for ii in range(ec // P_NKEYS):
        i = e * (ec // P_NKEYS) + ii
        wrow = None
        for h in range(P_HEADS):
            s1 = st_ref[2 * h, pl.ds(i, 1), :]
            al = al_ref[h, pl.ds(i, 1), :]
            sel = (s1 + st_ref[2 * h + 1]) >= tau_ref[h:h + 1, :]
            term = jnp.where(sel, al * be_ref[h], 0.0)
            wrow = term if wrow is None else wrow + term
        rows = slice(ii * P_NKEYS, (ii + 1) * P_NKEYS)
        w_ref[rows, :] = (wrow * gl[rows, :]).astype(BF16)
    acc_ref[...] += _dot(vt_ref[...], w_ref[...])

    @pl.when(e == ne - 1)
    def _():
        o_ref[...] = x_ref[...] + gate_ref[...] * acc_ref[...].T


def _peer_dense(x2, h2, st, u_bf, vt_bf, mod3, mod_row_of_batch, nb, t):
    tm = 256
    ec = 1024
    nt = t // tm
    ne = P_EXPERTS // ec
    tab = pltpu.VMEM((P_HEADS, P_NKEYS, tm), F32)
    return pl.pallas_call(
        functools.partial(_peer_dense_kernel, ne=ne, ec=ec),
        grid=(nb * nt, ne),
        in_specs=[pl.BlockSpec((tm, D), lambda i, e: (i, 0)),
                  pl.BlockSpec((tm, D), lambda i, e: (i, 0)),
                  pl.BlockSpec((2 * P_HEADS, P_NKEYS, tm), lambda i, e: (0, 0, i)),
                  pl.BlockSpec((ec, D), lambda i, e: (e, 0)),
                  pl.BlockSpec((D, ec), lambda i, e: (0, e)),
                  pl.BlockSpec((None, 1, D), lambda i, e: (mod_row_of_batch(i // nt), 0, 5))],
        out_specs=pl.BlockSpec((tm, D), lambda i, e: (i, 0)),
        out_shape=jax.ShapeDtypeStruct((nb * t, D), F32),
        scratch_shapes=[pltpu.VMEM((P_HEADS, tm), F32), tab, tab,
                        pltpu.VMEM((2 * P_TOPK, tm), F32),
                        pltpu.VMEM((N_CAND, tm), F32),
                        pltpu.VMEM((ec, tm), BF16),
                        pltpu.VMEM((D, tm), F32)],
        compiler_params=_cp("parallel", "arbitrary"),
        name="peer_dense",
    )(x2, h2, st, u_bf, vt_bf, mod3)


def _peer(x2, norm_g, mod3, mod_row_of_batch, wq_bf, keys, u_bf, vt_bf, nb, t):
    h2, st = _peer_scores(x2, norm_g, mod3, mod_row_of_batch, wq_bf, keys, nb, t)
    return _peer_dense(x2, h2, st, u_bf, vt_bf, mod3, mod_row_of_batch, nb, t)


def kernel(x, c, ctx, c_ctx, norm_mix, norm_ffn, w_mod, b_mod, w_in, w_out, a_qnorm, a_knorm, a_sink, b_qnorm,
           b_knorm, b_rpb, r7_conv, r7_w0, r7_w2, r7_a0, r7_a2, r7_g2, r7_kk, r7_ka, r7_rk, r7_lnw, r7_lnb,
           peer_wq, peer_keys, peer_u, peer_v):
    nb, s, _ = x.shape
    lc = ctx.shape[1]
    depth = w_in.shape[0]
    assert nb < 16 and s % 256 == 0 and lc % 256 == 0
    rows = s // GRID_W

    cc = jnp.zeros((16, D), F32).at[:nb].set(c).at[nb].set(c_ctx)
    mod = _modulation(cc, w_mod, b_mod)
    rope_tabs = _rope_tables(s)
    lat_row = lambda b: b
    ctx_row = lambda b: nb

    x_lat = x.reshape(nb * s, D)
    x_ctx = ctx.reshape(nb * lc, D)
    scale = HD ** -0.5
    for l in range(depth):
        with_ctx = l < depth - 1
        mod3 = mod[l].reshape(16, 1, 6 * D)
        lp = {'r7_conv': r7_conv[l], 'r7_w0': r7_w0[l], 'r7_w2': r7_w2[l], 'r7_a0': r7_a0[l], 'r7_a2': r7_a2[l],
              'r7_g2': r7_g2[l], 'r7_kk': r7_kk[l], 'r7_ka': r7_ka[l], 'r7_rk': r7_rk[l], 'r7_lnw': r7_lnw[l],
              'r7_lnb': r7_lnb[l]}
        w_in_bf = w_in[l].astype(BF16)
        w_out_bf = w_out[l].astype(BF16)
        gain_a = jnp.concatenate([jnp.tile(a_qnorm[l] * scale, A_HEADS), jnp.tile(a_knorm[l], A_KV)]).reshape(1, 384)
        gain_b = jnp.concatenate([jnp.tile(b_qnorm[l] * scale, B_HEADS), jnp.tile(b_knorm[l], B_HEADS)]).reshape(1, 512)

        ab_lat, pc_lat = _inproj(x_lat, mod3, lat_row, norm_mix[l], w_in_bf, gain_a, gain_b, rope_tabs, nb, s)
        ab_ctx, pc_ctx = _inproj(x_ctx, mod3, ctx_row, norm_mix[l], w_in_bf, gain_a, gain_b, None, nb, lc)

        o_a = _attn_a(ab_lat, ab_ctx, a_sink[l], nb, s, lc)
        o_b = _attn_b(ab_lat, ab_ctx, _na_bias_tables(b_rpb[l], rows), nb, s, lc)

        prep_ctx = _r7prep(pc_ctx, lp, nb, lc)
        prep_lat = _r7prep(pc_lat, lp, nb, s)
        zero_state = jnp.zeros((2, nb, C_HEADS, HD, HD), F32)
        y_ctx, s_ctx = _scan(prep_ctx[:6], zero_state, nb, lc)
        y_lat, _ = _scan(prep_lat[:6], s_ctx, nb, s)

        x_lat = _outproj(x_lat, o_a, o_b, y_lat, prep_lat[7], prep_lat[6], lp, w_out_bf, mod3, lat_row, nb, s)
        wq_bf = peer_wq[l].astype(BF16)
        u_bf = peer_u[l].astype(BF16)
        vt_bf = peer_v[l].astype(BF16).T
        x_lat = _peer(x_lat, norm_ffn[l], mod3, lat_row, wq_bf, peer_keys[l], u_bf, vt_bf, nb, s)
        if with_ctx:
            o_ac, o_bc = _attn_ctx(ab_ctx, a_sink[l], nb, lc)
            x_ctx = _outproj(x_ctx, o_ac, o_bc, y_ctx, prep_ctx[7], prep_ctx[6], lp, w_out_bf, mod3, ctx_row, nb, lc)
            x_ctx = _peer(x_ctx, norm_ffn[l], mod3, ctx_row, wq_bf, peer_keys[l], u_bf, vt_bf, nb, lc)
    return x_lat.reshape(nb, s, D)
```

```python
import functools

import numpy as np
import jax
import jax.numpy as jnp
from jax import lax
from jax.experimental import pallas as pl
from jax.experimental.pallas import tpu as pltpu

F32 = jnp.float32
BF16 = jnp.bfloat16
HI = lax.Precision.HIGHEST

D = 1024
DEPTH = 2
GRID_W = 64
HD = 64
EPS = 1e-6
NEG = -1e30
A_HEADS, A_KV, A_BLOCK, A_WINDOW = 4, 2, 128, 128
B_HEADS, NA_ROWS, NA_COLS = 4, 8, 16
C_HEADS = 8
C_W = 512
C_IN = 1920
AB_W = 1280
IN_W = AB_W + C_IN
GN_EPS = 64e-5
DECAY_SCALE = 0.6065306597126334
ROPE_BASE = 10000.0
P_HEADS, P_NKEYS, P_QDIM, P_TOPK = 8, 128, 256, 16
P_EXPERTS = P_NKEYS * P_NKEYS
SCAN_C = 64
N_CAND = 80
PREC_M = PREC_INV = PREC_SEQ = None
VMEM_LIMIT = 48 * 1024 * 1024


def _cp(*sem):
    return pltpu.CompilerParams(dimension_semantics=sem, vmem_limit_bytes=VMEM_LIMIT)


def _dot(a, b, prec=None):
    return jnp.dot(a, b, precision=prec, preferred_element_type=F32)


def _dot_nt(a, b, prec=None):
    return lax.dot_general(a, b, (((1,), (1,)), ((), ())), precision=prec, preferred_element_type=F32)


def _block_diag_ones(n):
    i = np.arange(n) // HD
    return jnp.asarray((i[:, None] == i[None, :]).astype(np.float32))


def _mod_kernel(c_ref, w_ref, b_ref, o_ref):
    c = c_ref[...]
    s = c * jax.nn.sigmoid(c)
    o_ref[...] = _dot(s, w_ref[...], HI) + b_ref[...]


def _modulation(cc, w_mod, b_mod):
    L, _, n = w_mod.shape
    tn = 512
    return pl.pallas_call(
        _mod_kernel,
        grid=(L, n // tn),
        in_specs=[pl.BlockSpec((16, D), lambda l, j: (0, 0)),
                  pl.BlockSpec((None, D, tn), lambda l, j: (l, 0, j)),
                  pl.BlockSpec((None, 1, tn), lambda l, j: (l, 0, j))],
        out_specs=pl.BlockSpec((None, 16, tn), lambda l, j: (l, 0, j)),
        out_shape=jax.ShapeDtypeStruct((L, 16, n), F32),
        compiler_params=_cp("parallel", "parallel"),
        name="modulation",
    )(cc, w_mod, b_mod.reshape(L, 1, n))


def _swap16(x):
    n = x.shape[-1]
    lane = lax.broadcasted_iota(jnp.int32, x.shape, 1)
    fwd = pltpu.roll(x, n - 16, 1)
    bwd = pltpu.roll(x, 16, 1)
    return jnp.where((lane % 32) < 16, fwd, bwd)


def _head_rms(x, bd, gain):
    ss = _dot(x * x, bd, HI)
    return x * lax.rsqrt(ss * (1.0 / HD) + EPS) * gain


def _inproj_kernel(*refs, rope):
    if rope:
        (x_ref, g_ref, sc_ref, sh_ref, w_ref, ga_ref, gb_ref, bda_ref, bdb_ref, cos_ref, sin_ref,
         ab_ref, c_ref) = refs
    else:
        (x_ref, g_ref, sc_ref, sh_ref, w_ref, ga_ref, gb_ref, bda_ref, bdb_ref, ab_ref, c_ref) = refs
    x = x_ref[...]
    y = x * lax.rsqrt(jnp.mean(x * x, axis=-1, keepdims=True) + EPS) * g_ref[...]
    h = y * (1.0 + sc_ref[...]) + sh_ref[...]
    acc = _dot(h.astype(BF16), w_ref[...])
    qa = _head_rms(acc[:, 0:384], bda_ref[...], ga_ref[...])
    if rope:
        qa = qa * cos_ref[...] + _swap16(qa) * sin_ref[...]
    qb = _head_rms(acc[:, 512:1024], bdb_ref[...], gb_ref[...])
    ab_ref[:, 0:384] = qa.astype(BF16)
    ab_ref[:, 384:512] = acc[:, 384:512].astype(BF16)
    ab_ref[:, 512:1024] = qb.astype(BF16)
    ab_ref[:, 1024:1280] = acc[:, 1024:1280].astype(BF16)
    c_ref[...] = acc[:, AB_W:IN_W]


def _inproj(x2, mod3, mod_row_of_batch, norm_g, w_in_bf, gain_a, gain_b, rope_tabs, nb, t):
    tm = 256
    nt = t // tm
    rope = rope_tabs is not None
    in_specs = [
        pl.BlockSpec((tm, D), lambda b, i: (b * nt + i, 0)),
        pl.BlockSpec((1, D), lambda b, i: (0, 0)),
        pl.BlockSpec((None, 1, D), lambda b, i: (mod_row_of_batch(b), 0, 1)),
        pl.BlockSpec((None, 1, D), lambda b, i: (mod_row_of_batch(b), 0, 0)),
        pl.BlockSpec((D, IN_W), lambda b, i: (0, 0)),
        pl.BlockSpec((1, 384), lambda b, i: (0, 0)),
        pl.BlockSpec((1, 512), lambda b, i: (0, 0)),
        pl.BlockSpec((384, 384), lambda b, i: (0, 0)),
        pl.BlockSpec((512, 512), lambda b, i: (0, 0)),
    ]
    args = [x2, norm_g.reshape(1, D), mod3, mod3, w_in_bf, gain_a, gain_b,
            _block_diag_ones(384), _block_diag_ones(512)]
    if rope:
        in_specs += [pl.BlockSpec((tm, 384), lambda b, i: (i, 0)),
                     pl.BlockSpec((tm, 384), lambda b, i: (i, 0))]
        args += list(rope_tabs)
    return pl.pallas_call(
        functools.partial(_inproj_kernel, rope=rope),
        grid=(nb, nt),
        in_specs=in_specs,
        out_specs=[pl.BlockSpec((tm, AB_W), lambda b, i: (b * nt + i, 0)),
                   pl.BlockSpec((tm, C_IN), lambda b, i: (b * nt + i, 0))],
        out_shape=[jax.ShapeDtypeStruct((nb * t, AB_W), BF16),
                   jax.ShapeDtypeStruct((nb * t, C_IN), F32)],
        compiler_params=_cp("parallel", "parallel"),
        name="inproj_rope" if rope else "inproj_ctx",
    )(*args)


def _rope_tables(s):
    tok = np.arange(s)
    inv = ROPE_BASE ** (-np.arange(0, 32, 2) / 32.0)
    ar = (tok // GRID_W)[:, None] * inv[None]
    ac = (tok % GRID_W)[:, None] * inv[None]
    cos = np.concatenate([np.cos(ar), np.cos(ar), np.cos(ac), np.cos(ac)], axis=1)
    sin = np.concatenate([-np.sin(ar), np.sin(ar), -np.sin(ac), np.sin(ac)], axis=1)
    return (jnp.asarray(np.tile(cos, (1, 6)), F32), jnp.asarray(np.tile(sin, (1, 6)), F32))


def _softmax_pv(s, v, sink=None):
    m = jnp.max(s, axis=-1, keepdims=True)
    if sink is not None:
        m = jnp.maximum(m, sink)
    p = jnp.exp(s - m)
    den = jnp.sum(p, axis=-1, keepdims=True)
    if sink is not None:
        den = den + jnp.exp(sink - m)
    return _dot(p.astype(BF16), v) / den


def _attn_a_kernel(q_ref, k0_ref, k1_ref, k2_ref, v0_ref, v1_ref, v2_ref, kc_ref, vc_ref, sink_ref, o_ref,
                   *, nblk):
    n = pl.program_id(1)
    q = q_ref[...]
    k_all = jnp.concatenate([k0_ref[...], k1_ref[...], k2_ref[...], kc_ref[...]], axis=0)
    v_all = jnp.concatenate([v0_ref[...], v1_ref[...], v2_ref[...], vc_ref[...]], axis=0)
    nk = 3 * A_BLOCK + kc_ref.shape[0]
    g = A_HEADS // A_KV
    row = lax.broadcasted_iota(jnp.int32, (g * A_BLOCK, nk), 0) % A_BLOCK
    col = lax.broadcasted_iota(jnp.int32, (g * A_BLOCK, nk), 1)
    rel = col - A_BLOCK - row
    ok = (jnp.abs(rel) <= A_WINDOW) & ((n > 0) | (col >= A_BLOCK)) & ((n < nblk - 1) | (col < 2 * A_BLOCK))
    ok = ok | (col >= 3 * A_BLOCK)
    for hk in range(A_KV):
        qg = jnp.concatenate([q[:, (hk * g + j) * HD:(hk * g + j + 1) * HD] for j in range(g)], axis=0)
        s = _dot_nt(qg, k_all[:, hk * HD:(hk + 1) * HD])
        s = jnp.where(ok, s, NEG)
        sink = jnp.concatenate(
            [jnp.broadcast_to(sink_ref[:, hk * g + j:hk * g + j + 1], (A_BLOCK, 1)) for j in range(g)], axis=0)
        o = _softmax_pv(s, v_all[:, hk * HD:(hk + 1) * HD], sink)
        for j in range(g):
            hq = hk * g + j
            o_ref[:, hq * HD:(hq + 1) * HD] = o[j * A_BLOCK:(j + 1) * A_BLOCK].astype(BF16)


def _attn_a(ab_lat, ab_ctx, sink, nb, s, lc):
    nblk = s // A_BLOCK

    def kv(col, d):
        return pl.BlockSpec((A_BLOCK, 128), lambda b, n: (b * nblk + jnp.clip(n + d, 0, nblk - 1), col))

    return pl.pallas_call(
        functools.partial(_attn_a_kernel, nblk=nblk),
        grid=(nb, nblk),
        in_specs=[pl.BlockSpec((A_BLOCK, 256), lambda b, n: (b * nblk + n, 0)),
                  kv(2, -1), kv(2, 0), kv(2, 1), kv(3, -1), kv(3, 0), kv(3, 1),
                  pl.BlockSpec((lc, 128), lambda b, n: (b, 2)),
                  pl.BlockSpec((lc, 128), lambda b, n: (b, 3)),
                  pl.BlockSpec((1, A_HEADS), lambda b, n: (0, 0))],
        out_specs=pl.BlockSpec((A_BLOCK, 256), lambda b, n: (b * nblk + n, 0)),
        out_shape=jax.ShapeDtypeStruct((nb * s, 256), BF16),
        compiler_params=_cp("parallel", "parallel"),
        name="attn_a",
    )(ab_lat, ab_lat, ab_lat, ab_lat, ab_lat, ab_lat, ab_lat, ab_ctx, ab_ctx, sink.reshape(1, A_HEADS))


def _attn_b_kernel(q_ref, k_ref, v_ref, kc_ref, vc_ref, bias_ref, o_ref, *, rows):
    r = pl.program_id(1)
    rs = jnp.clip(r - NA_ROWS // 2, 0, rows - NA_ROWS)
    start = pl.multiple_of(rs * GRID_W, GRID_W)
    nwin = NA_ROWS * GRID_W
    q = q_ref[...]
    kw = k_ref[pl.ds(start, nwin), :]
    vw = v_ref[pl.ds(start, nwin), :]
    kc = kc_ref[...]
    vc = vc_ref[...]
    for h in range(B_HEADS):
        sl = slice(h * HD, (h + 1) * HD)
        s_loc = _dot_nt(q[:, sl], kw[:, sl]) + bias_ref[h]
        s_ctx = _dot_nt(q[:, sl], kc[:, sl])
        m = jnp.maximum(jnp.max(s_loc, axis=-1, keepdims=True), jnp.max(s_ctx, axis=-1, keepdims=True))
        p_loc = jnp.exp(s_loc - m)
        p_ctx = jnp.exp(s_ctx - m)
        den = jnp.sum(p_loc, axis=-1, keepdims=True) + jnp.sum(p_ctx, axis=-1, keepdims=True)
        o = (_dot(p_loc.astype(BF16), vw[:, sl]) + _dot(p_ctx.astype(BF16), vc[:, sl])) / den
        o_ref[:, sl] = o.astype(BF16)


def _na_bias_tables(rpb, rows):
    kr = NA_ROWS
    qc = np.arange(GRID_W)
    win_start = np.clip(qc - NA_COLS // 2, 0, GRID_W - NA_COLS)
    kcol = np.arange(GRID_W)
    valid = (kcol[None, :] >= win_start[:, None]) & (kcol[None, :] < win_start[:, None] + NA_COLS)
    pad = GRID_W - NA_COLS
    rp = jnp.pad(rpb.astype(F32), ((0, 0), (0, 0), (pad, pad)))
    toe = jnp.stack([rp[:, :, GRID_W - 1 - q:2 * GRID_W - 1 - q] for q in range(GRID_W)], axis=2)
    toe = jnp.where(jnp.asarray(valid[None, None]), toe, NEG)
    tabs = [toe[:, off:off + kr].transpose(0, 2, 1, 3).reshape(rpb.shape[0], GRID_W, kr * GRID_W)
            for off in range(kr)]
    return jnp.stack(tabs, axis=0)


def _attn_b(ab_lat, ab_ctx, bias_tabs, nb, s, lc):
    rows = s // GRID_W

    def tab_index(b, r):
        return (jnp.clip(r - NA_ROWS // 2, 0, rows - NA_ROWS) - r + NA_ROWS - 1, 0, 0, 0)

    return pl.pallas_call(
        functools.partial(_attn_b_kernel, rows=rows),
        grid=(nb, rows),
        in_specs=[pl.BlockSpec((GRID_W, 256), lambda b, r: (b * rows + r, 2)),
                  pl.BlockSpec((s, 256), lambda b, r: (b, 3)),
                  pl.BlockSpec((s, 256), lambda b, r: (b, 4)),
                  pl.BlockSpec((lc, 256), lambda b, r: (b, 3)),
                  pl.BlockSpec((lc, 256), lambda b, r: (b, 4)),
                  pl.BlockSpec((None, B_HEADS, GRID_W, NA_ROWS * GRID_W), tab_index)],
        out_specs=pl.BlockSpec((GRID_W, 256), lambda b, r: (b * rows + r, 0)),
        out_shape=jax.ShapeDtypeStruct((nb * s, 256), BF16),
        compiler_params=_cp("parallel", "arbitrary"),
        name="attn_b",
    )(ab_lat, ab_lat, ab_lat, ab_ctx, ab_ctx, bias_tabs)


def _attn_ctx_kernel(ab_ref, sink_ref, oa_ref, ob_ref):
    lc = ab_ref.shape[0]
    g = A_HEADS // A_KV
    for hk in range(A_KV):
        qg = jnp.concatenate([ab_ref[:, (hk * g + j) * HD:(hk * g + j + 1) * HD] for j in range(g)], axis=0)
        s = _dot_nt(qg, ab_ref[:, 256 + hk * HD:256 + (hk + 1) * HD])
        sink = jnp.concatenate(
            [jnp.broadcast_to(sink_ref[:, hk * g + j:hk * g + j + 1], (lc, 1)) for j in range(g)], axis=0)
        o = _softmax_pv(s, ab_ref[:, 384 + hk * HD:384 + (hk + 1) * HD], sink)
        for j in range(g):
            hq = hk * g + j
            oa_ref[:, hq * HD:(hq + 1) * HD] = o[j * lc:(j + 1) * lc].astype(BF16)
    for h in range(B_HEADS):
        s = _dot_nt(ab_ref[:, 512 + h * HD:512 + (h + 1) * HD], ab_ref[:, 768 + h * HD:768 + (h + 1) * HD])
        o = _softmax_pv(s, ab_ref[:, 1024 + h * HD:1024 + (h + 1) * HD])
        ob_ref[:, h * HD:(h + 1) * HD] = o.astype(BF16)


def _attn_ctx(ab_ctx, sink, nb, lc):
    return pl.pallas_call(
        _attn_ctx_kernel,
        grid=(nb,),
        in_specs=[pl.BlockSpec((lc, AB_W), lambda b: (b, 0)),
                  pl.BlockSpec((1, A_HEADS), lambda b: (0, 0))],
        out_specs=[pl.BlockSpec((lc, 256), lambda b: (b, 0)),
                   pl.BlockSpec((lc, 256), lambda b: (b, 0))],
        out_shape=[jax.ShapeDtypeStruct((nb * lc, 256), BF16),
                   jax.ShapeDtypeStruct((nb * lc, 256), BF16)],
        compiler_params=_cp("parallel"),
        name="attn_ctx",
    )(ab_ctx, sink.reshape(1, A_HEADS))


def _r7prep_kernel(x_ref, prev_ref, next_ref, cw_ref, kkw_ref, ka_ref, rk_ref, w0_ref, w2_ref, a0_ref, a2_ref,
                   g2_ref, bd_ref,
                   r_ref, v_ref, kk_ref, lw_ref, beta_ref, kd_ref, g_ref, bonus_ref, pad_ref, *, nt):
    i = pl.program_id(1)
    tt = x_ref.shape[0]
    x = x_ref[...]
    pad_ref[8:8 + tt, :] = x
    pad_ref[7:8, :] = jnp.where(i > 0, prev_ref[7:8, :], 0.0)
    pad_ref[8 + tt:9 + tt, :] = jnp.where(i < nt - 1, next_ref[0:1, :], 0.0)
    xc = pad_ref[7:7 + tt, :] * cw_ref[0:1, :] + x * cw_ref[1:2, :] + pad_ref[9:9 + tt, :] * cw_ref[2:3, :]
    r = xc[:, 0:512]
    k = xc[:, 512:1024]
    v = xc[:, 1024:1536]
    wd = xc[:, 1536:1664]
    ad = xc[:, 1664:1792]
    gd = xc[:, 1792:1920]
    bd = bd_ref[...]
    kkh = k * kkw_ref[...]
    kk = kkh / jnp.maximum(jnp.sqrt(_dot(kkh * kkh, bd, HI)), 1e-12)
    zw = w0_ref[...] + _dot(jnp.tanh(wd), w2_ref[...], HI)
    za = a0_ref[...] + _dot(ad, a2_ref[...], HI)
    a = jax.nn.sigmoid(za)
    ka = ka_ref[...]
    kd0 = k * (1.0 + (a[:, 0:512] - 1.0) * ka)
    kd1 = k * (1.0 + (a[:, 512:1024] - 1.0) * ka)
    r_ref[...] = r
    v_ref[...] = v
    kk_ref[...] = kk
    lw_ref[...] = -DECAY_SCALE * jax.nn.sigmoid(zw)
    beta_ref[:, 0:512] = kk * a[:, 0:512]
    beta_ref[:, 512:1024] = kk * a[:, 512:1024]
    kd_ref[:, 0:512] = kd0
    kd_ref[:, 512:1024] = kd1
    g_ref[...] = _dot(jax.nn.sigmoid(gd), g2_ref[...], HI)
    bonus_ref[...] = _dot(r * (kd0 + kd1) * rk_ref[...], bd, HI) * v


def _bd2(w):
    z = jnp.zeros_like(w[0])
    return jnp.concatenate([jnp.concatenate([w[0], z], axis=1), jnp.concatenate([z, w[1]], axis=1)], axis=0)


def _r7prep(pc, lp, nb, t):
    tt = 256
    nt = t // tt
    nrow8 = t // 8

    def full(shape):
        return pl.BlockSpec(shape, lambda b, i: (0,) * len(shape))

    def rowspec(w):
        return pl.BlockSpec((tt, w), lambda b, i: (b * nt + i, 0))

    outs = [C_W, C_W, C_W, 2 * C_W, 2 * C_W, 2 * C_W, C_W, C_W]
    return pl.pallas_call(
        functools.partial(_r7prep_kernel, nt=nt),
        grid=(nb, nt),
        in_specs=[rowspec(C_IN),
                  pl.BlockSpec((8, C_IN), lambda b, i: (b * nrow8 + jnp.maximum(i * (tt // 8) - 1, 0), 0)),
                  pl.BlockSpec((8, C_IN), lambda b, i: (b * nrow8 + jnp.minimum((i + 1) * (tt // 8), nrow8 - 1), 0)),
                  full((3, C_IN)), full((1, C_W)), full((1, C_W)), full((1, C_W)),
                  full((1, 2 * C_W)), full((128, 2 * C_W)), full((1, 2 * C_W)), full((128, 2 * C_W)),
                  full((128, C_W)), full((C_W, C_W))],
        out_specs=[rowspec(w) for w in outs],
        out_shape=[jax.ShapeDtypeStruct((nb * t, w), F32) for w in outs],
        scratch_shapes=[pltpu.VMEM((tt + 16, C_IN), F32)],
        compiler_params=_cp("parallel", "parallel"),
        name="r7prep",
    )(pc, pc, pc, lp['r7_conv'], lp['r7_kk'].reshape(1, C_W), lp['r7_ka'].reshape(1, C_W),
      lp['r7_rk'].reshape(1, C_W), lp['r7_w0'].reshape(1, 2 * C_W), _bd2(lp['r7_w2']),
      lp['r7_a0'].reshape(1, 2 * C_W), _bd2(lp['r7_a2']), lp['r7_g2'], _block_diag_ones(C_W))


def _mm(a, b, prec):
    if prec is None:
        return _dot(a.astype(BF16), b.astype(BF16))
    return _dot(a, b, prec)


def _mm_nt(a, b, prec):
    if prec is None:
        return _dot_nt(a.astype(BF16), b.astype(BF16))
    return _dot_nt(a, b, prec)


def _scan_kernel(r_ref, v_ref, kk_ref, lw_ref, beta_ref, kd_ref, s0_ref, y_ref, sf_ref, st_ref, *, nc):
    d = pl.program_id(0)
    c = pl.program_id(2)
    cl = SCAN_C
    heads = range(C_HEADS)

    @pl.when(c == 0)
    def _():
        st_ref[...] = s0_ref[...]

    ti = lax.broadcasted_iota(jnp.int32, (cl, cl), 0)
    si = lax.broadcasted_iota(jnp.int32, (cl, cl), 1)
    rel = (si - ti) * (1 - 2 * d)
    incl = rel <= 0
    strict = rel < 0
    eye = (ti == si).astype(F32)

    lw = lw_ref[...]
    cum = _dot(incl.astype(F32), lw, HI)
    e_neg = jnp.exp(-cum)
    r_hat = r_ref[...] * jnp.exp(cum)
    a_hat = -kk_ref[...] * jnp.exp(cum - lw)
    b_til = beta_ref[...] * e_neg
    k_til = kd_ref[...] * e_neg
    wtot = jnp.exp(jnp.sum(lw, axis=0, keepdims=True))
    vv = v_ref[...]

    def hs(x, h):
        return x[:, h * HD:(h + 1) * HD]

    ar = [jnp.concatenate([hs(a_hat, h), hs(r_hat, h)], axis=0) for h in heads]
    bk = [jnp.concatenate([hs(b_til, h), hs(k_til, h)], axis=0) for h in heads]
    m = [_mm_nt(ar[h], bk[h], PREC_M) for h in heads]
    l_k = [jnp.where(strict, m[h][0:cl, cl:2 * cl], 0.0) for h in heads]
    r_b = [jnp.where(incl, m[h][cl:2 * cl, 0:cl], 0.0) for h in heads]
    r_k = [jnp.where(incl, m[h][cl:2 * cl, cl:2 * cl], 0.0) for h in heads]
    p = [jnp.where(strict, m[h][0:cl, 0:cl], 0.0) for h in heads]
    t_inv = [eye + p[h] for h in heads]
    n = 2
    while n < cl:
        p = [_mm(p[h], p[h], PREC_INV) for h in heads]
        t_inv = [t_inv[h] + _mm(t_inv[h], p[h], PREC_INV) for h in heads]
        n *= 2
    x1 = [_mm(l_k[h], hs(vv, h), PREC_M) for h in heads]
    y0 = [_mm(r_k[h], hs(vv, h), PREC_M) for h in heads]
    ua = [_mm(t_inv[h], jnp.concatenate([x1[h], hs(a_hat, h)], axis=1), PREC_INV) for h in heads]
    s0 = [st_ref[h] for h in heads]
    as0 = [_mm_nt(jnp.concatenate([ua[h][:, HD:2 * HD], hs(r_hat, h)], axis=0), s0[h], PREC_SEQ)
           for h in heads]
    u = [ua[h][:, 0:HD] + as0[h][0:cl] for h in heads]
    y = [y0[h] + as0[h][cl:2 * cl] + _mm(r_b[h], u[h], PREC_SEQ) for h in heads]
    for h in heads:
        y_ref[:, h * HD:(h + 1) * HD] = y[h]
    for h in heads:
        uv_t = jnp.concatenate([u[h], hs(vv, h)], axis=0).T
        st_ref[h] = (s0[h] + _mm(uv_t, bk[h], PREC_SEQ)) * hs(wtot, h)

    @pl.when(c == nc - 1)
    def _():
        sf_ref[...] = st_ref[...]


def _scan(prep, s0, nb, t):
    r, v, kk, lw, beta, kd = prep
    nc = t // SCAN_C

    def rows(d, b, c):
        return b * nc + c + d * (nc - 1 - 2 * c)

    shared = pl.BlockSpec((SCAN_C, C_W), lambda d, b, c: (rows(d, b, c), 0))
    perdir = pl.BlockSpec((SCAN_C, C_W), lambda d, b, c: (rows(d, b, c), d))
    st_spec = pl.BlockSpec((None, None, C_HEADS, HD, HD), lambda d, b, c: (d, b, 0, 0, 0))
    return pl.pallas_call(
        functools.partial(_scan_kernel, nc=nc),
        grid=(2, nb, nc),
        in_specs=[shared, shared, shared, perdir, perdir, perdir, st_spec],
        out_specs=[pl.BlockSpec((None, SCAN_C, C_W), lambda d, b, c: (d, rows(d, b, c), 0)), st_spec],
        out_shape=[jax.ShapeDtypeStruct((2, nb * t, C_W), F32),
                   jax.ShapeDtypeStruct((2, nb, C_HEADS, HD, HD), F32)],
        scratch_shapes=[pltpu.VMEM((C_HEADS, HD, HD), F32)],
        compiler_params=_cp("parallel", "parallel", "arbitrary"),
        name="r7scan",
    )(r, v, kk, lw, beta, kd, s0)


def _outproj_kernel(x_ref, oa_ref, ob_ref, y0_ref, y1_ref, bonus_ref, g_ref, lnw_ref, lnb_ref, bd_ref, w_ref,
                    gate_ref, o_ref):
    bd = bd_ref[...]
    y = y0_ref[...] + y1_ref[...]
    mu = _dot(y, bd, HI) * (1.0 / HD)
    yc = y - mu
    var = _dot(yc * yc, bd, HI) * (1.0 / HD)
    yn = yc * lax.rsqrt(var + GN_EPS) * lnw_ref[...] + lnb_ref[...]
    oc = ((yn + bonus_ref[...]) * g_ref[...]).astype(BF16)
    acc = (_dot(oa_ref[...], w_ref[0:256, :]) + _dot(ob_ref[...], w_ref[256:512, :])
           + _dot(oc, w_ref[512:1024, :]))
    o_ref[...] = x_ref[...] + gate_ref[...] * acc


def _outproj(x2, oa, ob, y, bonus, g, lp, w_out_bf, mod3, mod_row_of_batch, nb, t):
    tm = 256
    nt = t // tm

    def rowspec(w):
        return pl.BlockSpec((tm, w), lambda b, i: (b * nt + i, 0))

    def full(shape):
        return pl.BlockSpec(shape, lambda b, i: (0,) * len(shape))

    return pl.pallas_call(
        _outproj_kernel,
        grid=(nb, nt),
        in_specs=[rowspec(D), rowspec(256), rowspec(256),
                  pl.BlockSpec((None, tm, C_W), lambda b, i: (0, b * nt + i, 0)),
                  pl.BlockSpec((None, tm, C_W), lambda b, i: (1, b * nt + i, 0)),
                  rowspec(C_W), rowspec(C_W), full((1, C_W)), full((1, C_W)), full((C_W, C_W)), full((D, D)),
                  pl.BlockSpec((None, 1, D), lambda b, i: (mod_row_of_batch(b), 0, 2))],
        out_specs=rowspec(D),
        out_shape=jax.ShapeDtypeStruct((nb * t, D), F32),
        compiler_params=_cp("parallel", "parallel"),
        name="outproj",
    )(x2, oa, ob, y, y, bonus, g, lp['r7_lnw'].reshape(1, C_W), lp['r7_lnb'].reshape(1, C_W),
      _block_diag_ones(C_W), w_out_bf, mod3)


def _peer_score_kernel(x_ref, g_ref, sc_ref, sh_ref, wq_ref, keys_ref, h_ref, st_ref):
    x = x_ref[...]
    y = x * lax.rsqrt(jnp.mean(x * x, axis=-1, keepdims=True) + EPS) * g_ref[...]
    h = (y * (1.0 + sc_ref[...]) + sh_ref[...]).astype(BF16)
    h_ref[...] = h
    q = _dot(h, wq_ref[...])
    half = P_QDIM // 2
    for hp in range(2 * P_HEADS):
        st_ref[hp] = _dot_nt(keys_ref[hp], q[:, hp * half:(hp + 1) * half], HI)


def _peer_scores(x2, norm_g, mod3, mod_row_of_batch, wq_bf, keys, nb, t):
    tm = 256
    nt = t // tm
    n = nb * t
    return pl.pallas_call(
        _peer_score_kernel,
        grid=(nb, nt),
        in_specs=[pl.BlockSpec((tm, D), lambda b, i: (b * nt + i, 0)),
                  pl.BlockSpec((1, D), lambda b, i: (0, 0)),
                  pl.BlockSpec((None, 1, D), lambda b, i: (mod_row_of_batch(b), 0, 4)),
                  pl.BlockSpec((None, 1, D), lambda b, i: (mod_row_of_batch(b), 0, 3)),
                  pl.BlockSpec((D, P_HEADS * P_QDIM), lambda b, i: (0, 0)),
                  pl.BlockSpec((2 * P_HEADS, P_NKEYS, P_QDIM // 2), lambda b, i: (0, 0, 0))],
        out_specs=[pl.BlockSpec((tm, D), lambda b, i: (b * nt + i, 0)),
                   pl.BlockSpec((2 * P_HEADS, P_NKEYS, tm), lambda b, i: (0, 0, b * nt + i))],
        out_shape=[jax.ShapeDtypeStruct((n, D), BF16),
                   jax.ShapeDtypeStruct((2 * P_HEADS, P_NKEYS, n), F32)],
        compiler_params=_cp("parallel", "parallel"),
        name="peer_scores",
    )(x2, norm_g.reshape(1, D), mod3, mod3, wq_bf, keys.reshape(2 * P_HEADS, P_NKEYS, P_QDIM // 2))


def _extract_max(val, ids, big):
    m = jnp.max(val, axis=0, keepdims=True)
    first = jnp.min(jnp.where(val == m, ids, big), axis=0, keepdims=True)
    return m, jnp.where(ids == first, -jnp.inf, val)


def _peer_gate_tables(st_ref, tau_ref, al_ref, be_ref, top_ref, cand_ref):
    tm = st_ref.shape[-1]
    ids = lax.broadcasted_iota(jnp.int32, (P_NKEYS, tm), 0)
    cid = lax.broadcasted_iota(jnp.int32, (N_CAND, tm), 0)
    sub = lax.broadcasted_iota(jnp.int32, (8, tm), 0)
    for h in range(P_HEADS):
        for p in range(2):
            s = st_ref[2 * h + p]
            for j in range(P_TOPK):
                m, s = _extract_max(s, ids, P_NKEYS)
                top_ref[p * P_TOPK + j:p * P_TOPK + j + 1, :] = m
        b16 = top_ref[P_TOPK:2 * P_TOPK, :]
        b8 = b16[0:8]
        cand_ref[0:16, :] = top_ref[0:1, :] + b16
        for p in range(2, 9):
            nq = P_TOPK // p
            cand_ref[8 * p:8 * p + 8, :] = jnp.where(sub < nq, top_ref[p - 1:p, :] + b8, -jnp.inf)
        cand_ref[72:80, :] = top_ref[8:16, :] + b16[0:1]
        cand = cand_ref[...]
        best = []
        for j in range(P_TOPK):
            m, cand = _extract_max(cand, cid, N_CAND)
            best.append(m)
        tau = best[-1]
        z = jnp.ones_like(best[0])
        for j in range(1, P_TOPK):
            z = z + jnp.exp(best[j] - best[0])
        s1 = st_ref[2 * h]
        s2 = st_ref[2 * h + 1]
        tau_ref[h:h + 1, :] = tau
        al_ref[h] = jnp.exp(s1 - top_ref[0:1, :]) / z
        be_ref[h] = jnp.exp(s2 - top_ref[P_TOPK:P_TOPK + 1, :])


def _peer_dense_kernel(x_ref, h_ref, st_ref, u_ref, vt_ref, gate_ref, o_ref,
                       tau_ref, al_ref, be_ref, top_ref, cand_ref, w_ref, acc_ref, *, ne, ec):
    e = pl.program_id(1)

    @pl.when(e == 0)
    def _():
        _peer_gate_tables(st_ref, tau_ref, al_ref, be_ref, top_ref, cand_ref)
        acc_ref[...] = jnp.zeros_like(acc_ref)

    act = _dot_nt(u_ref[...], h_ref[...])
    gl = 0.5 * act * (1.0 + lax.erf(act * (2.0 ** -0.5)))
    for ii in range(ec // P_NKEYS):
        i = e * (ec // P_NKEYS) + ii
        wrow = None
        for h in range(P_HEADS):
            s1 = st_ref[2 * h, pl.ds(i, 1), :]
            al = al_ref[h, pl.ds(i, 1), :]
            sel = (s1 + st_ref[2 * h + 1]) >= tau_ref[h:h + 1, :]
            term = jnp.where(sel, al * be_ref[h], 0.0)
            wrow = term if wrow is None else wrow + term
        rows = slice(ii * P_NKEYS, (ii + 1) * P_NKEYS)
        w_ref[rows, :] = (wrow * gl[rows, :]).astype(BF16)
    acc_ref[...] += _dot(vt_ref[...], w_ref[...])

    @pl.when(e == ne - 1)
    def _():
        o_ref[...] = x_ref[...] + gate_ref[...] * acc_ref[...].T


def _peer_dense(x2, h2, st, u_bf, vt_bf, mod3, mod_row_of_batch, nb, t):
    tm = 256
    ec = 1024
    nt = t // tm
    ne = P_EXPERTS // ec
    tab = pltpu.VMEM((P_HEADS, P_NKEYS, tm), F32)
    return pl.pallas_call(
        functools.partial(_peer_dense_kernel, ne=ne, ec=ec),
        grid=(nb * nt, ne),
        in_specs=[pl.BlockSpec((tm, D), lambda i, e: (i, 0)),
                  pl.BlockSpec((tm, D), lambda i, e: (i, 0)),
                  pl.BlockSpec((2 * P_HEADS, P_NKEYS, tm), lambda i, e: (0, 0, i)),
                  pl.BlockSpec((ec, D), lambda i, e: (e, 0)),
                  pl.BlockSpec((D, ec), lambda i, e: (0, e)),
                  pl.BlockSpec((None, 1, D), lambda i, e: (mod_row_of_batch(i // nt), 0, 5))],
        out_specs=pl.BlockSpec((tm, D), lambda i, e: (i, 0)),
        out_shape=jax.ShapeDtypeStruct((nb * t, D), F32),
        scratch_shapes=[pltpu.VMEM((P_HEADS, tm), F32), tab, tab,
                        pltpu.VMEM((2 * P_TOPK, tm), F32),
                        pltpu.VMEM((N_CAND, tm), F32),
                        pltpu.VMEM((ec, tm), BF16),
                        pltpu.VMEM((D, tm), F32)],
        compiler_params=_cp("parallel", "arbitrary"),
        name="peer_dense",
    )(x2, h2, st, u_bf, vt_bf, mod3)


def _peer(x2, norm_g, mod3, mod_row_of_batch, wq_bf, keys, u_bf, vt_bf, nb, t):
    h2, st = _peer_scores(x2, norm_g, mod3, mod_row_of_batch, wq_bf, keys, nb, t)
    return _peer_dense(x2, h2, st, u_bf, vt_bf, mod3, mod_row_of_batch, nb, t)


def kernel(x, c, ctx, c_ctx, norm_mix, norm_ffn, w_mod, b_mod, w_in, w_out, a_qnorm, a_knorm, a_sink, b_qnorm,
           b_knorm, b_rpb, r7_conv, r7_w0, r7_w2, r7_a0, r7_a2, r7_g2, r7_kk, r7_ka, r7_rk, r7_lnw, r7_lnb,
           peer_wq, peer_keys, peer_u, peer_v):
    nb, s, _ = x.shape
    lc = ctx.shape[1]
    depth = w_in.shape[0]
    assert nb < 16 and s % 256 == 0 and lc % 256 == 0
    rows = s // GRID_W

    cc = jnp.zeros((16, D), F32).at[:nb].set(c).at[nb].set(c_ctx)
    mod = _modulation(cc, w_mod, b_mod)
    rope_tabs = _rope_tables(s)
    lat_row = lambda b: b
    ctx_row = lambda b: nb

    x_lat = x.reshape(nb * s, D)
    x_ctx = ctx.reshape(nb * lc, D)
    scale = HD ** -0.5
    for l in range(depth):
        with_ctx = l < depth - 1
        mod3 = mod[l].reshape(16, 1, 6 * D)
        lp = {'r7_conv': r7_conv[l], 'r7_w0': r7_w0[l], 'r7_w2': r7_w2[l], 'r7_a0': r7_a0[l], 'r7_a2': r7_a2[l],
              'r7_g2': r7_g2[l], 'r7_kk': r7_kk[l], 'r7_ka': r7_ka[l], 'r7_rk': r7_rk[l], 'r7_lnw': r7_lnw[l],
              'r7_lnb': r7_lnb[l]}
        w_in_bf = w_in[l].astype(BF16)
        w_out_bf = w_out[l].astype(BF16)
        gain_a = jnp.concatenate([jnp.tile(a_qnorm[l] * scale, A_HEADS), jnp.tile(a_knorm[l], A_KV)]).reshape(1, 384)
        gain_b = jnp.concatenate([jnp.tile(b_qnorm[l] * scale, B_HEADS), jnp.tile(b_knorm[l], B_HEADS)]).reshape(1, 512)

        ab_lat, pc_lat = _inproj(x_lat, mod3, lat_row, norm_mix[l], w_in_bf, gain_a, gain_b, rope_tabs, nb, s)
        ab_ctx, pc_ctx = _inproj(x_ctx, mod3, ctx_row, norm_mix[l], w_in_bf, gain_a, gain_b, None, nb, lc)

        o_a = _attn_a(ab_lat, ab_ctx, a_sink[l], nb, s, lc)
        o_b = _attn_b(ab_lat, ab_ctx, _na_bias_tables(b_rpb[l], rows), nb, s, lc)

        prep_ctx = _r7prep(pc_ctx, lp, nb, lc)
        prep_lat = _r7prep(pc_lat, lp, nb, s)
        zero_state = jnp.zeros((2, nb, C_HEADS, HD, HD), F32)
        y_ctx, s_ctx = _scan(prep_ctx[:6], zero_state, nb, lc)
        y_lat, _ = _scan(prep_lat[:6], s_ctx, nb, s)

        x_lat = _outproj(x_lat, o_a, o_b, y_lat, prep_lat[7], prep_lat[6], lp, w_out_bf, mod3, lat_row, nb, s)
        wq_bf = peer_wq[l].astype(BF16)
        u_bf = peer_u[l].astype(BF16)
        vt_bf = peer_v[l].astype(BF16).T
        x_lat = _peer(x_lat, norm_ffn[l], mod3, lat_row, wq_bf, peer_keys[l], u_bf, vt_bf, nb, s)
        if with_ctx:
            o_ac, o_bc = _attn_ctx(ab_ctx, a_sink[l], nb, lc)
            x_ctx = _outproj(x_ctx, o_ac, o_bc, y_ctx, prep_ctx[7], prep_ctx[6], lp, w_out_bf, mod3, ctx_row, nb, lc)
            x_ctx = _peer(x_ctx, norm_ffn[l], mod3, ctx_row, wq_bf, peer_keys[l], u_bf, vt_bf, nb, lc)
    return x_lat.reshape(nb, s, D)
```

```python
import functools

import numpy as np
import jax
import jax.numpy as jnp
from jax import lax
from jax.experimental import pallas as pl
from jax.experimental.pallas import tpu as pltpu

F32 = jnp.float32
BF16 = jnp.bfloat16
HI = lax.Precision.HIGHEST

D = 1024
DEPTH = 2
GRID_W = 64
HD = 64
EPS = 1e-6
NEG = -1e30
A_HEADS, A_KV, A_BLOCK, A_WINDOW = 4, 2, 128, 128
B_HEADS, NA_ROWS, NA_COLS = 4, 8, 16
C_HEADS = 8
C_W = 512
C_IN = 1920
AB_W = 1280
IN_W = AB_W + C_IN
GN_EPS = 64e-5
DECAY_SCALE = 0.6065306597126334
ROPE_BASE = 10000.0
P_HEADS, P_NKEYS, P_QDIM, P_TOPK = 8, 128, 256, 16
P_EXPERTS = P_NKEYS * P_NKEYS
SCAN_C = 64
PREC_M = PREC_INV = PREC_SEQ = None
VMEM_LIMIT = 48 * 1024 * 1024


def _cp(*sem):
    return pltpu.CompilerParams(dimension_semantics=sem, vmem_limit_bytes=VMEM_LIMIT)


def _dot(a, b, prec=None):
    return jnp.dot(a, b, precision=prec, preferred_element_type=F32)


def _dot_nt(a, b, prec=None):
    return lax.dot_general(a, b, (((1,), (1,)), ((), ())), precision=prec, preferred_element_type=F32)


def _block_diag_ones(n):
    i = np.arange(n) // HD
    return jnp.asarray((i[:, None] == i[None, :]).astype(np.float32))


def _mod_kernel(c_ref, w_ref, b_ref, o_ref):
    c = c_ref[...]
    s = c * jax.nn.sigmoid(c)
    o_ref[...] = _dot(s, w_ref[...], HI) + b_ref[...]


def _modulation(cc, w_mod, b_mod):
    L, _, n = w_mod.shape
    tn = 512
    return pl.pallas_call(
        _mod_kernel,
        grid=(L, n // tn),
        in_specs=[pl.BlockSpec((16, D), lambda l, j: (0, 0)),
                  pl.BlockSpec((None, D, tn), lambda l, j: (l, 0, j)),
                  pl.BlockSpec((None, 1, tn), lambda l, j: (l, 0, j))],
        out_specs=pl.BlockSpec((None, 16, tn), lambda l, j: (l, 0, j)),
        out_shape=jax.ShapeDtypeStruct((L, 16, n), F32),
        compiler_params=_cp("parallel", "parallel"),
        name="modulation",
    )(cc, w_mod, b_mod.reshape(L, 1, n))


def _swap16(x):
    n = x.shape[-1]
    lane = lax.broadcasted_iota(jnp.int32, x.shape, 1)
    fwd = pltpu.roll(x, n - 16, 1)
    bwd = pltpu.roll(x, 16, 1)
    return jnp.where((lane % 32) < 16, fwd, bwd)


def _head_rms(x, bd, gain):
    ss = _dot(x * x, bd, HI)
    return x * lax.rsqrt(ss * (1.0 / HD) + EPS) * gain


def _inproj_kernel(*refs, rope):
    if rope:
        (x_ref, g_ref, sc_ref, sh_ref, w_ref, ga_ref, gb_ref, bda_ref, bdb_ref, cos_ref, sin_ref,
         ab_ref, c_ref) = refs
    else:
        (x_ref, g_ref, sc_ref, sh_ref, w_ref, ga_ref, gb_ref, bda_ref, bdb_ref, ab_ref, c_ref) = refs
    x = x_ref[...]
    y = x * lax.rsqrt(jnp.mean(x * x, axis=-1, keepdims=True) + EPS) * g_ref[...]
    h = y * (1.0 + sc_ref[...]) + sh_ref[...]
    acc = _dot(h.astype(BF16), w_ref[...])
    qa = _head_rms(acc[:, 0:384], bda_ref[...], ga_ref[...])
    if rope:
        qa = qa * cos_ref[...] + _swap16(qa) * sin_ref[...]
    qb = _head_rms(acc[:, 512:1024], bdb_ref[...], gb_ref[...])
    ab_ref[:, 0:384] = qa.astype(BF16)
    ab_ref[:, 384:512] = acc[:, 384:512].astype(BF16)
    ab_ref[:, 512:1024] = qb.astype(BF16)
    ab_ref[:, 1024:1280] = acc[:, 1024:1280].astype(BF16)
    c_ref[...] = acc[:, AB_W:IN_W]


def _inproj(x2, mod3, mod_row_of_batch, norm_g, w_in_bf, gain_a, gain_b, rope_tabs, nb, t):
    tm = 256
    nt = t // tm
    rope = rope_tabs is not None
    in_specs = [
        pl.BlockSpec((tm, D), lambda b, i: (b * nt + i, 0)),
        pl.BlockSpec((1, D), lambda b, i: (0, 0)),
        pl.BlockSpec((None, 1, D), lambda b, i: (mod_row_of_batch(b), 0, 1)),
        pl.BlockSpec((None, 1, D), lambda b, i: (mod_row_of_batch(b), 0, 0)),
        pl.BlockSpec((D, IN_W), lambda b, i: (0, 0)),
        pl.BlockSpec((1, 384), lambda b, i: (0, 0)),
        pl.BlockSpec((1, 512), lambda b, i: (0, 0)),
        pl.BlockSpec((384, 384), lambda b, i: (0, 0)),
        pl.BlockSpec((512, 512), lambda b, i: (0, 0)),
    ]
    args = [x2, norm_g.reshape(1, D), mod3, mod3, w_in_bf, gain_a, gain_b,
            _block_diag_ones(384), _block_diag_ones(512)]
    if rope:
        in_specs += [pl.BlockSpec((tm, 384), lambda b, i: (i, 0)),
                     pl.BlockSpec((tm, 384), lambda b, i: (i, 0))]
        args += list(rope_tabs)
    return pl.pallas_call(
        functools.partial(_inproj_kernel, rope=rope),
        grid=(nb, nt),
        in_specs=in_specs,
        out_specs=[pl.BlockSpec((tm, AB_W), lambda b, i: (b * nt + i, 0)),
                   pl.BlockSpec((tm, C_IN), lambda b, i: (b * nt + i, 0))],
        out_shape=[jax.ShapeDtypeStruct((nb * t, AB_W), BF16),
                   jax.ShapeDtypeStruct((nb * t, C_IN), F32)],
        compiler_params=_cp("parallel", "parallel"),
        name="inproj_rope" if rope else "inproj_ctx",
    )(*args)


def _rope_tables(s):
    tok = np.arange(s)
    inv = ROPE_BASE ** (-np.arange(0, 32, 2) / 32.0)
    ar = (tok // GRID_W)[:, None] * inv[None]
    ac = (tok % GRID_W)[:, None] * inv[None]
    cos = np.concatenate([np.cos(ar), np.cos(ar), np.cos(ac), np.cos(ac)], axis=1)
    sin = np.concatenate([-np.sin(ar), np.sin(ar), -np.sin(ac), np.sin(ac)], axis=1)
    return (jnp.asarray(np.tile(cos, (1, 6)), F32), jnp.asarray(np.tile(sin, (1, 6)), F32))


def _softmax_pv(s, v, sink=None):
    m = jnp.max(s, axis=-1, keepdims=True)
    if sink is not None:
        m = jnp.maximum(m, sink)
    p = jnp.exp(s - m)
    den = jnp.sum(p, axis=-1, keepdims=True)
    if sink is not None:
        den = den + jnp.exp(sink - m)
    return _dot(p.astype(BF16), v) / den


def _attn_a_kernel(q_ref, k0_ref, k1_ref, k2_ref, v0_ref, v1_ref, v2_ref, kc_ref, vc_ref, sink_ref, o_ref,
                   *, nblk):
    n = pl.program_id(1)
    q = q_ref[...]
    k_all = jnp.concatenate([k0_ref[...], k1_ref[...], k2_ref[...], kc_ref[...]], axis=0)
    v_all = jnp.concatenate([v0_ref[...], v1_ref[...], v2_ref[...], vc_ref[...]], axis=0)
    nk = 3 * A_BLOCK + kc_ref.shape[0]
    g = A_HEADS // A_KV
    row = lax.broadcasted_iota(jnp.int32, (g * A_BLOCK, nk), 0) % A_BLOCK
    col = lax.broadcasted_iota(jnp.int32, (g * A_BLOCK, nk), 1)
    rel = col - A_BLOCK - row
    ok = (jnp.abs(rel) <= A_WINDOW) & ((n > 0) | (col >= A_BLOCK)) & ((n < nblk - 1) | (col < 2 * A_BLOCK))
    ok = ok | (col >= 3 * A_BLOCK)
    for hk in range(A_KV):
        qg = jnp.concatenate([q[:, (hk * g + j) * HD:(hk * g + j + 1) * HD] for j in range(g)], axis=0)
        s = _dot_nt(qg, k_all[:, hk * HD:(hk + 1) * HD])
        s = jnp.where(ok, s, NEG)
        sink = jnp.concatenate(
            [jnp.broadcast_to(sink_ref[:, hk * g + j:hk * g + j + 1], (A_BLOCK, 1)) for j in range(g)], axis=0)
        o = _softmax_pv(s, v_all[:, hk * HD:(hk + 1) * HD], sink)
        for j in range(g):
            hq = hk * g + j
            o_ref[:, hq * HD:(hq + 1) * HD] = o[j * A_BLOCK:(j + 1) * A_BLOCK].astype(BF16)


def _attn_a(ab_lat, ab_ctx, sink, nb, s, lc):
    nblk = s // A_BLOCK

    def kv(col, d):
        return pl.BlockSpec((A_BLOCK, 128), lambda b, n: (b * nblk + jnp.clip(n + d, 0, nblk - 1), col))

    return pl.pallas_call(
        functools.partial(_attn_a_kernel, nblk=nblk),
        grid=(nb, nblk),
        in_specs=[pl.BlockSpec((A_BLOCK, 256), lambda b, n: (b * nblk + n, 0)),
                  kv(2, -1), kv(2, 0), kv(2, 1), kv(3, -1), kv(3, 0), kv(3, 1),
                  pl.BlockSpec((lc, 128), lambda b, n: (b, 2)),
                  pl.BlockSpec((lc, 128), lambda b, n: (b, 3)),
                  pl.BlockSpec((1, A_HEADS), lambda b, n: (0, 0))],
        out_specs=pl.BlockSpec((A_BLOCK, 256), lambda b, n: (b * nblk + n, 0)),
        out_shape=jax.ShapeDtypeStruct((nb * s, 256), BF16),
        compiler_params=_cp("parallel", "parallel"),
        name="attn_a",
    )(ab_lat, ab_lat, ab_lat, ab_lat, ab_lat, ab_lat, ab_lat, ab_ctx, ab_ctx, sink.reshape(1, A_HEADS))


def _attn_b_kernel(q_ref, k_ref, v_ref, kc_ref, vc_ref, bias_ref, o_ref, *, rows):
    r = pl.program_id(1)
    rs = jnp.clip(r - NA_ROWS // 2, 0, rows - NA_ROWS)
    start = pl.multiple_of(rs * GRID_W, GRID_W)
    nwin = NA_ROWS * GRID_W
    q = q_ref[...]
    kw = k_ref[pl.ds(start, nwin), :]
    vw = v_ref[pl.ds(start, nwin), :]
    kc = kc_ref[...]
    vc = vc_ref[...]
    for h in range(B_HEADS):
        sl = slice(h * HD, (h + 1) * HD)
        s_loc = _dot_nt(q[:, sl], kw[:, sl]) + bias_ref[h]
        s_ctx = _dot_nt(q[:, sl], kc[:, sl])
        m = jnp.maximum(jnp.max(s_loc, axis=-1, keepdims=True), jnp.max(s_ctx, axis=-1, keepdims=True))
        p_loc = jnp.exp(s_loc - m)
        p_ctx = jnp.exp(s_ctx - m)
        den = jnp.sum(p_loc, axis=-1, keepdims=True) + jnp.sum(p_ctx, axis=-1, keepdims=True)
        o = (_dot(p_loc.astype(BF16), vw[:, sl]) + _dot(p_ctx.astype(BF16), vc[:, sl])) / den
        o_ref[:, sl] = o.astype(BF16)


def _na_bias_tables(rpb, rows):
    kr = NA_ROWS
    qc = np.arange(GRID_W)
    win_start = np.clip(qc - NA_COLS // 2, 0, GRID_W - NA_COLS)
    kcol = np.arange(GRID_W)
    valid = (kcol[None, :] >= win_start[:, None]) & (kcol[None, :] < win_start[:, None] + NA_COLS)
    pad = GRID_W - NA_COLS
    rp = jnp.pad(rpb.astype(F32), ((0, 0), (0, 0), (pad, pad)))
    toe = jnp.stack([rp[:, :, GRID_W - 1 - q:2 * GRID_W - 1 - q] for q in range(GRID_W)], axis=2)
    toe = jnp.where(jnp.asarray(valid[None, None]), toe, NEG)
    tabs = [toe[:, off:off + kr].transpose(0, 2, 1, 3).reshape(rpb.shape[0], GRID_W, kr * GRID_W)
            for off in range(kr)]
    return jnp.stack(tabs, axis=0)


def _attn_b(ab_lat, ab_ctx, bias_tabs, nb, s, lc):
    rows = s // GRID_W

    def tab_index(b, r):
        return (jnp.clip(r - NA_ROWS // 2, 0, rows - NA_ROWS) - r + NA_ROWS - 1, 0, 0, 0)

    return pl.pallas_call(
        functools.partial(_attn_b_kernel, rows=rows),
        grid=(nb, rows),
        in_specs=[pl.BlockSpec((GRID_W, 256), lambda b, r: (b * rows + r, 2)),
                  pl.BlockSpec((s, 256), lambda b, r: (b, 3)),
                  pl.BlockSpec((s, 256), lambda b, r: (b, 4)),
                  pl.BlockSpec((lc, 256), lambda b, r: (b, 3)),
                  pl.BlockSpec((lc, 256), lambda b, r: (b, 4)),
                  pl.BlockSpec((None, B_HEADS, GRID_W, NA_ROWS * GRID_W), tab_index)],
        out_specs=pl.BlockSpec((GRID_W, 256), lambda b, r: (b * rows + r, 0)),
        out_shape=jax.ShapeDtypeStruct((nb * s, 256), BF16),
        compiler_params=_cp("parallel", "arbitrary"),
        name="attn_b",
    )(ab_lat, ab_lat, ab_lat, ab_ctx, ab_ctx, bias_tabs)


def _attn_ctx_kernel(ab_ref, sink_ref, oa_ref, ob_ref):
    lc = ab_ref.shape[0]
    g = A_HEADS // A_KV
    for hk in range(A_KV):
        qg = jnp.concatenate([ab_ref[:, (hk * g + j) * HD:(hk * g + j + 1) * HD] for j in range(g)], axis=0)
        s = _dot_nt(qg, ab_ref[:, 256 + hk * HD:256 + (hk + 1) * HD])
        sink = jnp.concatenate(
            [jnp.broadcast_to(sink_ref[:, hk * g + j:hk * g + j + 1], (lc, 1)) for j in range(g)], axis=0)
        o = _softmax_pv(s, ab_ref[:, 384 + hk * HD:384 + (hk + 1) * HD], sink)
        for j in range(g):
            hq = hk * g + j
            oa_ref[:, hq * HD:(hq + 1) * HD] = o[j * lc:(j + 1) * lc].astype(BF16)
    for h in range(B_HEADS):
        s = _dot_nt(ab_ref[:, 512 + h * HD:512 + (h + 1) * HD], ab_ref[:, 768 + h * HD:768 + (h + 1) * HD])
        o = _softmax_pv(s, ab_ref[:, 1024 + h * HD:1024 + (h + 1) * HD])
        ob_ref[:, h * HD:(h + 1) * HD] = o.astype(BF16)


def _attn_ctx(ab_ctx, sink, nb, lc):
    return pl.pallas_call(
        _attn_ctx_kernel,
        grid=(nb,),
        in_specs=[pl.BlockSpec((lc, AB_W), lambda b: (b, 0)),
                  pl.BlockSpec((1, A_HEADS), lambda b: (0, 0))],
        out_specs=[pl.BlockSpec((lc, 256), lambda b: (b, 0)),
                   pl.BlockSpec((lc, 256), lambda b: (b, 0))],
        out_shape=[jax.ShapeDtypeStruct((nb * lc, 256), BF16),
                   jax.ShapeDtypeStruct((nb * lc, 256), BF16)],
        compiler_params=_cp("parallel"),
        name="attn_ctx",
    )(ab_ctx, sink.reshape(1, A_HEADS))


def _r7prep_kernel(x_ref, prev_ref, next_ref, cw_ref, kkw_ref, ka_ref, rk_ref, w0_ref, w2_ref, a0_ref, a2_ref,
                   g2_ref, bd_ref,
                   r_ref, v_ref, kk_ref, lw_ref, beta_ref, kd_ref, g_ref, bonus_ref, pad_ref, *, nt):
    i = pl.program_id(1)
    tt = x_ref.shape[0]
    x = x_ref[...]
    pad_ref[8:8 + tt, :] = x
    pad_ref[7:8, :] = jnp.where(i > 0, prev_ref[7:8, :], 0.0)
    pad_ref[8 + tt:9 + tt, :] = jnp.where(i < nt - 1, next_ref[0:1, :], 0.0)
    xc = pad_ref[7:7 + tt, :] * cw_ref[0:1, :] + x * cw_ref[1:2, :] + pad_ref[9:9 + tt, :] * cw_ref[2:3, :]
    r = xc[:, 0:512]
    k = xc[:, 512:1024]
    v = xc[:, 1024:1536]
    wd = xc[:, 1536:1664]
    ad = xc[:, 1664:1792]
    gd = xc[:, 1792:1920]
    bd = bd_ref[...]
    kkh = k * kkw_ref[...]
    kk = kkh / jnp.maximum(jnp.sqrt(_dot(kkh * kkh, bd, HI)), 1e-12)
    zw = w0_ref[...] + _dot(jnp.tanh(wd), w2_ref[...], HI)
    za = a0_ref[...] + _dot(ad, a2_ref[...], HI)
    a = jax.nn.sigmoid(za)
    ka = ka_ref[...]
    kd0 = k * (1.0 + (a[:, 0:512] - 1.0) * ka)
    kd1 = k * (1.0 + (a[:, 512:1024] - 1.0) * ka)
    r_ref[...] = r
    v_ref[...] = v
    kk_ref[...] = kk
    lw_ref[...] = -DECAY_SCALE * jax.nn.sigmoid(zw)
    beta_ref[:, 0:512] = kk * a[:, 0:512]
    beta_ref[:, 512:1024] = kk * a[:, 512:1024]
    kd_ref[:, 0:512] = kd0
    kd_ref[:, 512:1024] = kd1
    g_ref[...] = _dot(jax.nn.sigmoid(gd), g2_ref[...], HI)
    bonus_ref[...] = _dot(r * (kd0 + kd1) * rk_ref[...], bd, HI) * v


def _bd2(w):
    z = jnp.zeros_like(w[0])
    return jnp.concatenate([jnp.concatenate([w[0], z], axis=1), jnp.concatenate([z, w[1]], axis=1)], axis=0)


def _r7prep(pc, lp, nb, t):
    tt = 256
    nt = t // tt
    nrow8 = t // 8

    def full(shape):
        return pl.BlockSpec(shape, lambda b, i: (0,) * len(shape))

    def rowspec(w):
        return pl.BlockSpec((tt, w), lambda b, i: (b * nt + i, 0))

    outs = [C_W, C_W, C_W, 2 * C_W, 2 * C_W, 2 * C_W, C_W, C_W]
    return pl.pallas_call(
        functools.partial(_r7prep_kernel, nt=nt),
        grid=(nb, nt),
        in_specs=[rowspec(C_IN),
                  pl.BlockSpec((8, C_IN), lambda b, i: (b * nrow8 + jnp.maximum(i * (tt // 8) - 1, 0), 0)),
                  pl.BlockSpec((8, C_IN), lambda b, i: (b * nrow8 + jnp.minimum((i + 1) * (tt // 8), nrow8 - 1), 0)),
                  full((3, C_IN)), full((1, C_W)), full((1, C_W)), full((1, C_W)),
                  full((1, 2 * C_W)), full((128, 2 * C_W)), full((1, 2 * C_W)), full((128, 2 * C_W)),
                  full((128, C_W)), full((C_W, C_W))],
        out_specs=[rowspec(w) for w in outs],
        out_shape=[jax.ShapeDtypeStruct((nb * t, w), F32) for w in outs],
        scratch_shapes=[pltpu.VMEM((tt + 16, C_IN), F32)],
        compiler_params=_cp("parallel", "parallel"),
        name="r7prep",
    )(pc, pc, pc, lp['r7_conv'], lp['r7_kk'].reshape(1, C_W), lp['r7_ka'].reshape(1, C_W),
      lp['r7_rk'].reshape(1, C_W), lp['r7_w0'].reshape(1, 2 * C_W), _bd2(lp['r7_w2']),
      lp['r7_a0'].reshape(1, 2 * C_W), _bd2(lp['r7_a2']), lp['r7_g2'], _block_diag_ones(C_W))


def _mm(a, b, prec):
    if prec is None:
        return _dot(a.astype(BF16), b.astype(BF16))
    return _dot(a, b, prec)


def _mm_nt(a, b, prec):
    if prec is None:
        return _dot_nt(a.astype(BF16), b.astype(BF16))
    return _dot_nt(a, b, prec)


def _scan_kernel(r_ref, v_ref, kk_ref, lw_ref, beta_ref, kd_ref, s0_ref, y_ref, sf_ref, st_ref, *, nc):
    d = pl.program_id(0)
    c = pl.program_id(2)
    cl = SCAN_C
    heads = range(C_HEADS)

    @pl.when(c == 0)
    def _():
        st_ref[...] = s0_ref[...]

    ti = lax.broadcasted_iota(jnp.int32, (cl, cl), 0)
    si = lax.broadcasted_iota(jnp.int32, (cl, cl), 1)
    rel = (si - ti) * (1 - 2 * d)
    incl = rel <= 0
    strict = rel < 0
    eye = (ti == si).astype(F32)

    lw = lw_ref[...]
    cum = _dot(incl.astype(F32), lw, HI)
    e_neg = jnp.exp(-cum)
    r_hat = r_ref[...] * jnp.exp(cum)
    a_hat = -kk_ref[...] * jnp.exp(cum - lw)
    b_til = beta_ref[...] * e_neg
    k_til = kd_ref[...] * e_neg
    wtot = jnp.exp(jnp.sum(lw, axis=0, keepdims=True))
    vv = v_ref[...]

    def hs(x, h):
        return x[:, h * HD:(h + 1) * HD]

    ar = [jnp.concatenate([hs(a_hat, h), hs(r_hat, h)], axis=0) for h in heads]
    bk = [jnp.concatenate([hs(b_til, h), hs(k_til, h)], axis=0) for h in heads]
    m = [_mm_nt(ar[h], bk[h], PREC_M) for h in heads]
    l_k = [jnp.where(strict, m[h][0:cl, cl:2 * cl], 0.0) for h in heads]
    r_b = [jnp.where(incl, m[h][cl:2 * cl, 0:cl], 0.0) for h in heads]
    r_k = [jnp.where(incl, m[h][cl:2 * cl, cl:2 * cl], 0.0) for h in heads]
    p = [jnp.where(strict, m[h][0:cl, 0:cl], 0.0) for h in heads]
    t_inv = [eye + p[h] for h in heads]
    n = 2
    while n < cl:
        p = [_mm(p[h], p[h], PREC_INV) for h in heads]
        t_inv = [t_inv[h] + _mm(t_inv[h], p[h], PREC_INV) for h in heads]
        n *= 2
    x1 = [_mm(l_k[h], hs(vv, h), PREC_M) for h in heads]
    y0 = [_mm(r_k[h], hs(vv, h), PREC_M) for h in heads]
    ua = [_mm(t_inv[h], jnp.concatenate([x1[h], hs(a_hat, h)], axis=1), PREC_INV) for h in heads]
    s0 = [st_ref[h] for h in heads]
    as0 = [_mm_nt(jnp.concatenate([ua[h][:, HD:2 * HD], hs(r_hat, h)], axis=0), s0[h], PREC_SEQ)
           for h in heads]
    u = [ua[h][:, 0:HD] + as0[h][0:cl] for h in heads]
    y = [y0[h] + as0[h][cl:2 * cl] + _mm(r_b[h], u[h], PREC_SEQ) for h in heads]
    for h in heads:
        y_ref[:, h * HD:(h + 1) * HD] = y[h]
    for h in heads:
        uv_t = jnp.concatenate([u[h], hs(vv, h)], axis=0).T
        st_ref[h] = (s0[h] + _mm(uv_t, bk[h], PREC_SEQ)) * hs(wtot, h)

    @pl.when(c == nc - 1)
    def _():
        sf_ref[...] = st_ref[...]


def _scan(prep, s0, nb, t):
    r, v, kk, lw, beta, kd = prep
    nc = t // SCAN_C

    def rows(d, b, c):
        return b * nc + c + d * (nc - 1 - 2 * c)

    shared = pl.BlockSpec((SCAN_C, C_W), lambda d, b, c: (rows(d, b, c), 0))
    perdir = pl.BlockSpec((SCAN_C, C_W), lambda d, b, c: (rows(d, b, c), d))
    st_spec = pl.BlockSpec((None, None, C_HEADS, HD, HD), lambda d, b, c: (d, b, 0, 0, 0))
    return pl.pallas_call(
        functools.partial(_scan_kernel, nc=nc),
        grid=(2, nb, nc),
        in_specs=[shared, shared, shared, perdir, perdir, perdir, st_spec],
        out_specs=[pl.BlockSpec((None, SCAN_C, C_W), lambda d, b, c: (d, rows(d, b, c), 0)), st_spec],
        out_shape=[jax.ShapeDtypeStruct((2, nb * t, C_W), F32),
                   jax.ShapeDtypeStruct((2, nb, C_HEADS, HD, HD), F32)],
        scratch_shapes=[pltpu.VMEM((C_HEADS, HD, HD), F32)],
        compiler_params=_cp("parallel", "parallel", "arbitrary"),
        name="r7scan",
    )(r, v, kk, lw, beta, kd, s0)


def _outproj_kernel(x_ref, oa_ref, ob_ref, y0_ref, y1_ref, bonus_ref, g_ref, lnw_ref, lnb_ref, bd_ref, w_ref,
                    gate_ref, o_ref):
    bd = bd_ref[...]
    y = y0_ref[...] + y1_ref[...]
    mu = _dot(y, bd, HI) * (1.0 / HD)
    yc = y - mu
    var = _dot(yc * yc, bd, HI) * (1.0 / HD)
    yn = yc * lax.rsqrt(var + GN_EPS) * lnw_ref[...] + lnb_ref[...]
    oc = ((yn + bonus_ref[...]) * g_ref[...]).astype(BF16)
    acc = (_dot(oa_ref[...], w_ref[0:256, :]) + _dot(ob_ref[...], w_ref[256:512, :])
           + _dot(oc, w_ref[512:1024, :]))
    o_ref[...] = x_ref[...] + gate_ref[...] * acc


def _outproj(x2, oa, ob, y, bonus, g, lp, w_out_bf, mod3, mod_row_of_batch, nb, t):
    tm = 256
    nt = t // tm

    def rowspec(w):
        return pl.BlockSpec((tm, w), lambda b, i: (b * nt + i, 0))

    def full(shape):
        return pl.BlockSpec(shape, lambda b, i: (0,) * len(shape))

    return pl.pallas_call(
        _outproj_kernel,
        grid=(nb, nt),
        in_specs=[rowspec(D), rowspec(256), rowspec(256),
                  pl.BlockSpec((None, tm, C_W), lambda b, i: (0, b * nt + i, 0)),
                  pl.BlockSpec((None, tm, C_W), lambda b, i: (1, b * nt + i, 0)),
                  rowspec(C_W), rowspec(C_W), full((1, C_W)), full((1, C_W)), full((C_W, C_W)), full((D, D)),
                  pl.BlockSpec((None, 1, D), lambda b, i: (mod_row_of_batch(b), 0, 2))],
        out_specs=rowspec(D),
        out_shape=jax.ShapeDtypeStruct((nb * t, D), F32),
        compiler_params=_cp("parallel", "parallel"),
        name="outproj",
    )(x2, oa, ob, y, y, bonus, g, lp['r7_lnw'].reshape(1, C_W), lp['r7_lnb'].reshape(1, C_W),
      _block_diag_ones(C_W), w_out_bf, mod3)


def _peer_score_kernel(x_ref, g_ref, sc_ref, sh_ref, wq_ref, keys_ref, h_ref, st_ref):
    x = x_ref[...]
    y = x * lax.rsqrt(jnp.mean(x * x, axis=-1, keepdims=True) + EPS) * g_ref[...]
    h = (y * (1.0 + sc_ref[...]) + sh_ref[...]).astype(BF16)
    h_ref[...] = h
    q = _dot(h, wq_ref[...])
    half = P_QDIM // 2
    for hp in range(2 * P_HEADS):
        st_ref[hp] = _dot_nt(keys_ref[hp], q[:, hp * half:(hp + 1) * half], HI)


def _peer_scores(x2, norm_g, mod3, mod_row_of_batch, wq_bf, keys, nb, t):
    tm = 256
    nt = t // tm
    n = nb * t
    return pl.pallas_call(
        _peer_score_kernel,
        grid=(nb, nt),
        in_specs=[pl.BlockSpec((tm, D), lambda b, i: (b * nt + i, 0)),
                  pl.BlockSpec((1, D), lambda b, i: (0, 0)),
                  pl.BlockSpec((None, 1, D), lambda b, i: (mod_row_of_batch(b), 0, 4)),
                  pl.BlockSpec((None, 1, D), lambda b, i: (mod_row_of_batch(b), 0, 3)),
                  pl.BlockSpec((D, P_HEADS * P_QDIM), lambda b, i: (0, 0)),
                  pl.BlockSpec((2 * P_HEADS, P_NKEYS, P_QDIM // 2), lambda b, i: (0, 0, 0))],
        out_specs=[pl.BlockSpec((tm, D), lambda b, i: (b * nt + i, 0)),
                   pl.BlockSpec((2 * P_HEADS, P_NKEYS, tm), lambda b, i: (0, 0, b * nt + i))],
        out_shape=[jax.ShapeDtypeStruct((n, D), BF16),
                   jax.ShapeDtypeStruct((2 * P_HEADS, P_NKEYS, n), F32)],
        compiler_params=_cp("parallel", "parallel"),
        name="peer_scores",
    )(x2, norm_g.reshape(1, D), mod3, mod3, wq_bf, keys.reshape(2 * P_HEADS, P_NKEYS, P_QDIM // 2))


def _batcher_pairs(n):
    pairs = []
    p = 1
    while p < n:
        k = p
        while k >= 1:
            for j in range(k % p, n - k, 2 * k):
                for i in range(min(k, n - j - k)):
                    if (i + j) // (2 * p) == (i + j + k) // (2 * p):
                        pairs.append((i + j, i + j + k))
            k //= 2
        p *= 2
    return pairs


_SORT16 = _batcher_pairs(P_TOPK)
_BITONIC16 = [(i, i + d) for d in (8, 4, 2, 1) for i in range(P_TOPK) if i & d == 0]


def _compare_exchange(x, pairs):
    for i, j in pairs:
        x[i], x[j] = jnp.maximum(x[i], x[j]), jnp.minimum(x[i], x[j])
    return x


def _top16_sorted(slabs):
    x = _compare_exchange(list(slabs), _SORT16)
    for shift in (4, 2, 1):
        y = [pltpu.roll(x[P_TOPK - 1 - k], shift, 0) for k in range(P_TOPK)]
        x = [jnp.maximum(x[k], y[k]) for k in range(P_TOPK)]
        x = _compare_exchange(x, _BITONIC16)
    return x


def _peer_gate_tables(st_ref, rk_ref, be_ref, cnt_ref, al_ref, top_ref):
    tm = st_ref.shape[-1]
    sub = lax.broadcasted_iota(jnp.int32, (8, tm), 0)
    ninf = jnp.full((8, tm), -jnp.inf, F32)
    for h in range(P_HEADS):
        for p in range(2):
            top = _top16_sorted([st_ref[2 * h + p, 8 * k:8 * k + 8, :] for k in range(P_TOPK)])
            for k in range(P_TOPK):
                top_ref[p * P_TOPK + k:p * P_TOPK + k + 1, :] = top[k][0:1]
        a16 = top_ref[0:P_TOPK, :]
        b16 = top_ref[P_TOPK:2 * P_TOPK, :]
        b8 = b16[0:8]
        cand = [a16[0:1] + b8, a16[0:1] + b16[8:16]]
        for p in range(2, 9):
            cand.append(jnp.where(sub < P_TOPK // p, a16[p - 1:p] + b8, -jnp.inf))
        cand.append(a16[8:16] + b16[0:1])
        best = _top16_sorted(cand + [ninf] * (P_TOPK - len(cand)))
        tau = best[P_TOPK - 1][0:1]
        z = jnp.ones_like(tau)
        for k in range(1, P_TOPK):
            z = z + jnp.exp(best[k][0:1] - best[0][0:1])
        s1 = st_ref[2 * h]
        s2 = st_ref[2 * h + 1]
        cnt = jnp.zeros_like(s1)
        rk = jnp.ones_like(s2)
        for q in range(P_TOPK):
            bq = b16[q:q + 1]
            theta = jnp.min(jnp.where(a16 + bq >= tau, a16, jnp.inf), axis=0, keepdims=True)
            cnt = cnt + jnp.where(s1 >= theta, 1.0, 0.0)
            rk = rk + jnp.where(bq > s2, 1.0, 0.0)
        cnt_ref[h] = cnt
        rk_ref[h] = rk.astype(BF16)
        al_ref[h] = jnp.exp(s1 - a16[0:1]) / z
        be_ref[h] = jnp.exp(s2 - b16[0:1]).astype(BF16)


def _peer_chunk(chunk, act_ref, rk_ref, be_ref, cnt_ref, al_ref, w_ref):
    nrow = w_ref.shape[0] // P_NKEYS
    for ii in range(nrow):
        i = chunk * nrow + ii
        wrow = None
        for h in range(P_HEADS):
            cnt = cnt_ref[h, pl.ds(i, 1), :].astype(BF16)
            al = al_ref[h, pl.ds(i, 1), :].astype(BF16)
            term = jnp.where(rk_ref[h] <= cnt, al * be_ref[h], 0.0)
            wrow = term if wrow is None else wrow + term
        rows = slice(ii * P_NKEYS, (ii + 1) * P_NKEYS)
        act = act_ref[rows, :]
        gl = 0.5 * act * (1.0 + lax.erf(act * (2.0 ** -0.5)))
        w_ref[rows, :] = wrow * gl.astype(BF16)


def _peer_dense_kernel(x_ref, h_ref, st_ref, u_ref, vt_ref, gate_ref, o_ref,
                       rk_ref, be_ref, cnt_ref, al_ref, top_ref, act0_ref, act1_ref, w_ref, acc_ref, *, ne):
    e = pl.program_id(1)

    @pl.when(e == 0)
    def _():
        act0_ref[...] = _dot_nt(u_ref[...], h_ref[...])
        _peer_gate_tables(st_ref, rk_ref, be_ref, cnt_ref, al_ref, top_ref)
        acc_ref[...] = jnp.zeros_like(acc_ref)

    def step(cur_ref, nxt_ref):
        nxt_ref[...] = _dot_nt(u_ref[...], h_ref[...])
        _peer_chunk(e - 1, cur_ref, rk_ref, be_ref, cnt_ref, al_ref, w_ref)
        acc_ref[...] += _dot(vt_ref[...], w_ref[...])

    @pl.when(e % 2 == 1)
    def _():
        step(act0_ref, act1_ref)

    @pl.when((e > 0) & (e % 2 == 0))
    def _():
        step(act1_ref, act0_ref)

    @pl.when(e == ne)
    def _():
        o_ref[...] = x_ref[...] + gate_ref[...] * acc_ref[...].T


def _peer_dense(x2, h2, st, u_bf, vt_bf, mod3, mod_row_of_batch, nb, t):
    tm = 256
    ec = 1024
    nt = t // tm
    ne = P_EXPERTS // ec
    return pl.pallas_call(
        functools.partial(_peer_dense_kernel, ne=ne),
        grid=(nb * nt, ne + 1),
        in_specs=[pl.BlockSpec((tm, D), lambda i, e: (i, 0)),
                  pl.BlockSpec((tm, D), lambda i, e: (i, 0)),
                  pl.BlockSpec((2 * P_HEADS, P_NKEYS, tm), lambda i, e: (0, 0, i)),
                  pl.BlockSpec((ec, D), lambda i, e: (jnp.minimum(e, ne - 1), 0)),
                  pl.BlockSpec((D, ec), lambda i, e: (0, jnp.maximum(e - 1, 0))),
                  pl.BlockSpec((None, 1, D), lambda i, e: (mod_row_of_batch(i // nt), 0, 5))],
        out_specs=pl.BlockSpec((tm, D), lambda i, e: (i, 0)),
        out_shape=jax.ShapeDtypeStruct((nb * t, D), F32),
        scratch_shapes=[pltpu.VMEM((P_HEADS, P_NKEYS, tm), BF16),
                        pltpu.VMEM((P_HEADS, P_NKEYS, tm), BF16),
                        pltpu.VMEM((P_HEADS, P_NKEYS, tm), F32),
                        pltpu.VMEM((P_HEADS, P_NKEYS, tm), F32),
                        pltpu.VMEM((2 * P_TOPK, tm), F32),
                        pltpu.VMEM((ec, tm), F32),
                        pltpu.VMEM((ec, tm), F32),
                        pltpu.VMEM((ec, tm), BF16),
                        pltpu.VMEM((D, tm), F32)],
        compiler_params=_cp("parallel", "arbitrary"),
        name="peer_dense",
    )(x2, h2, st, u_bf, vt_bf, mod3)


def _peer(x2, norm_g, mod3, mod_row_of_batch, wq_bf, keys, u_bf, vt_bf, nb, t):
    h2, st = _peer_scores(x2, norm_g, mod3, mod_row_of_batch, wq_bf, keys, nb, t)
    return _peer_dense(x2, h2, st, u_bf, vt_bf, mod3, mod_row_of_batch, nb, t)


def kernel(x, c, ctx, c_ctx, norm_mix, norm_ffn, w_mod, b_mod, w_in, w_out, a_qnorm, a_knorm, a_sink, b_qnorm,
           b_knorm, b_rpb, r7_conv, r7_w0, r7_w2, r7_a0, r7_a2, r7_g2, r7_kk, r7_ka, r7_rk, r7_lnw, r7_lnb,
           peer_wq, peer_keys, peer_u, peer_v):
    nb, s, _ = x.shape
    lc = ctx.shape[1]
    depth = w_in.shape[0]
    assert nb < 16 and s % 256 == 0 and lc % 256 == 0
    rows = s // GRID_W

    cc = jnp.zeros((16, D), F32).at[:nb].set(c).at[nb].set(c_ctx)
    mod = _modulation(cc, w_mod, b_mod)
    rope_tabs = _rope_tables(s)
    lat_row = lambda b: b
    ctx_row = lambda b: nb

    x_lat = x.reshape(nb * s, D)
    x_ctx = ctx.reshape(nb * lc, D)
    scale = HD ** -0.5
    for l in range(depth):
        with_ctx = l < depth - 1
        mod3 = mod[l].reshape(16, 1, 6 * D)
        lp = {'r7_conv': r7_conv[l], 'r7_w0': r7_w0[l], 'r7_w2': r7_w2[l], 'r7_a0': r7_a0[l], 'r7_a2': r7_a2[l],
              'r7_g2': r7_g2[l], 'r7_kk': r7_kk[l], 'r7_ka': r7_ka[l], 'r7_rk': r7_rk[l], 'r7_lnw': r7_lnw[l],
              'r7_lnb': r7_lnb[l]}
        w_in_bf = w_in[l].astype(BF16)
        w_out_bf = w_out[l].astype(BF16)
        gain_a = jnp.concatenate([jnp.tile(a_qnorm[l] * scale, A_HEADS), jnp.tile(a_knorm[l], A_KV)]).reshape(1, 384)
        gain_b = jnp.concatenate([jnp.tile(b_qnorm[l] * scale, B_HEADS), jnp.tile(b_knorm[l], B_HEADS)]).reshape(1, 512)

        ab_lat, pc_lat = _inproj(x_lat, mod3, lat_row, norm_mix[l], w_in_bf, gain_a, gain_b, rope_tabs, nb, s)
        ab_ctx, pc_ctx = _inproj(x_ctx, mod3, ctx_row, norm_mix[l], w_in_bf, gain_a, gain_b, None, nb, lc)

        o_a = _attn_a(ab_lat, ab_ctx, a_sink[l], nb, s, lc)
        o_b = _attn_b(ab_lat, ab_ctx, _na_bias_tables(b_rpb[l], rows), nb, s, lc)

        prep_ctx = _r7prep(pc_ctx, lp, nb, lc)
        prep_lat = _r7prep(pc_lat, lp, nb, s)
        zero_state = jnp.zeros((2, nb, C_HEADS, HD, HD), F32)
        y_ctx, s_ctx = _scan(prep_ctx[:6], zero_state, nb, lc)
        y_lat, _ = _scan(prep_lat[:6], s_ctx, nb, s)

        x_lat = _outproj(x_lat, o_a, o_b, y_lat, prep_lat[7], prep_lat[6], lp, w_out_bf, mod3, lat_row, nb, s)
        wq_bf = peer_wq[l].astype(BF16)
        u_bf = peer_u[l].astype(BF16)
        vt_bf = peer_v[l].astype(BF16).T
        x_lat = _peer(x_lat, norm_ffn[l], mod3, lat_row, wq_bf, peer_keys[l], u_bf, vt_bf, nb, s)
        if with_ctx:
            o_ac, o_bc = _attn_ctx(ab_ctx, a_sink[l], nb, lc)
            x_ctx = _outproj(x_ctx, o_ac, o_bc, y_ctx, prep_ctx[7], prep_ctx[6], lp, w_out_bf, mod3, ctx_row, nb, lc)
            x_ctx = _peer(x_ctx, norm_ffn[l], mod3, ctx_row, wq_bf, peer_keys[l], u_bf, vt_bf, nb, lc)
    return x_lat.reshape(nb, s, D)
```

```python
import functools

import numpy as np
import jax
import jax.numpy as jnp
from jax import lax
from jax.experimental import pallas as pl
from jax.experimental.pallas import tpu as pltpu

F32 = jnp.float32
BF16 = jnp.bfloat16
HI = lax.Precision.HIGHEST

D = 1024
DEPTH = 2
GRID_W = 64
HD = 64
EPS = 1e-6
NEG = -1e30
A_HEADS, A_KV, A_BLOCK, A_WINDOW = 4, 2, 128, 128
B_HEADS, NA_ROWS, NA_COLS = 4, 8, 16
C_HEADS = 8
C_W = 512
C_IN = 1920
AB_W = 1280
IN_W = AB_W + C_IN
GN_EPS = 64e-5
DECAY_SCALE = 0.6065306597126334
ROPE_BASE = 10000.0
P_HEADS, P_NKEYS, P_QDIM, P_TOPK = 8, 128, 256, 16
P_EXPERTS = P_NKEYS * P_NKEYS
SCAN_C = 64
PREC_M = PREC_INV = PREC_SEQ = None
VMEM_LIMIT = 48 * 1024 * 1024
PEER_VMEM_LIMIT = 56 * 1024 * 1024
PEER_EC = 1024


def _cp(*sem):
    return pltpu.CompilerParams(dimension_semantics=sem, vmem_limit_bytes=VMEM_LIMIT)


def _dot(a, b, prec=None):
    return jnp.dot(a, b, precision=prec, preferred_element_type=F32)


def _dot_nt(a, b, prec=None):
    return lax.dot_general(a, b, (((1,), (1,)), ((), ())), precision=prec, preferred_element_type=F32)


SEG_W = 256


def _block_diag_ones():
    i = np.arange(SEG_W) // HD
    return jnp.asarray((i[:, None] == i[None, :]).astype(np.float32), BF16)


def _split2(x):
    hi = x.astype(BF16)
    return hi, (x - hi.astype(F32)).astype(BF16)


def _segsum(x, bd):
    hi, lo = _split2(x)
    outs = []
    for g0 in range(0, x.shape[1], SEG_W):
        w = min(SEG_W, x.shape[1] - g0)
        outs.append(_dot(hi[:, g0:g0 + w], bd[0:w, 0:w]) + _dot(lo[:, g0:g0 + w], bd[0:w, 0:w]))
    return outs[0] if len(outs) == 1 else jnp.concatenate(outs, axis=1)


def _dot3(a, b, nt=False):
    f = _dot_nt if nt else _dot
    a_hi, a_lo = _split2(a)
    b_hi, b_lo = _split2(b)
    return f(a_hi, b_hi) + (f(a_lo, b_hi) + f(a_hi, b_lo))


def _mod_kernel(c_ref, w_ref, b_ref, o_ref):
    c = c_ref[...]
    s = c * jax.nn.sigmoid(c)
    o_ref[...] = _dot(s, w_ref[...], HI) + b_ref[...]


def _modulation(cc, w_mod, b_mod):
    L, _, n = w_mod.shape
    tn = 512
    return pl.pallas_call(
        _mod_kernel,
        grid=(L, n // tn),
        in_specs=[pl.BlockSpec((16, D), lambda l, j: (0, 0)),
                  pl.BlockSpec((None, D, tn), lambda l, j: (l, 0, j)),
                  pl.BlockSpec((None, 1, tn), lambda l, j: (l, 0, j))],
        out_specs=pl.BlockSpec((None, 16, tn), lambda l, j: (l, 0, j)),
        out_shape=jax.ShapeDtypeStruct((L, 16, n), F32),
        compiler_params=_cp("parallel", "parallel"),
        name="modulation",
    )(cc, w_mod, b_mod.reshape(L, 1, n))


def _swap16(x):
    n = x.shape[-1]
    lane = lax.broadcasted_iota(jnp.int32, x.shape, 1)
    fwd = pltpu.roll(x, n - 16, 1)
    bwd = pltpu.roll(x, 16, 1)
    return jnp.where((lane % 32) < 16, fwd, bwd)


def _head_rms(x, bd, gain):
    ss = _segsum(x * x, bd)
    return x * lax.rsqrt(ss * (1.0 / HD) + EPS) * gain


def _inproj_kernel(*refs, rope):
    if rope:
        (x_ref, g_ref, sc_ref, sh_ref, w_ref, ga_ref, gb_ref, bd_ref, cos_ref, sin_ref,
         ab_ref, c_ref) = refs
    else:
        (x_ref, g_ref, sc_ref, sh_ref, w_ref, ga_ref, gb_ref, bd_ref, ab_ref, c_ref) = refs
    x = x_ref[...]
    y = x * lax.rsqrt(jnp.mean(x * x, axis=-1, keepdims=True) + EPS) * g_ref[...]
    h = y * (1.0 + sc_ref[...]) + sh_ref[...]
    acc = _dot(h.astype(BF16), w_ref[...])
    bd = bd_ref[...]
    qa = _head_rms(acc[:, 0:384], bd, ga_ref[...])
    if rope:
        qa = qa * cos_ref[...] + _swap16(qa) * sin_ref[...]
    qb = _head_rms(acc[:, 512:1024], bd, gb_ref[...])
    ab_ref[:, 0:384] = qa.astype(BF16)
    ab_ref[:, 384:512] = acc[:, 384:512].astype(BF16)
    ab_ref[:, 512:1024] = qb.astype(BF16)
    ab_ref[:, 1024:1280] = acc[:, 1024:1280].astype(BF16)
    c_ref[...] = acc[:, AB_W:IN_W]


def _inproj(x2, mod3, mod_row_of_batch, norm_g, w_in_bf, gain_a, gain_b, rope_tabs, nb, t):
    tm = 256
    nt = t // tm
    rope = rope_tabs is not None
    in_specs = [
        pl.BlockSpec((tm, D), lambda b, i: (b * nt + i, 0)),
        pl.BlockSpec((1, D), lambda b, i: (0, 0)),
        pl.BlockSpec((None, 1, D), lambda b, i: (mod_row_of_batch(b), 0, 1)),
        pl.BlockSpec((None, 1, D), lambda b, i: (mod_row_of_batch(b), 0, 0)),
        pl.BlockSpec((D, IN_W), lambda b, i: (0, 0)),
        pl.BlockSpec((1, 384), lambda b, i: (0, 0)),
        pl.BlockSpec((1, 512), lambda b, i: (0, 0)),
        pl.BlockSpec((SEG_W, SEG_W), lambda b, i: (0, 0)),
    ]
    args = [x2, norm_g.reshape(1, D), mod3, mod3, w_in_bf, gain_a, gain_b,
            _block_diag_ones()]
    if rope:
        in_specs += [pl.BlockSpec((tm, 384), lambda b, i: (i, 0)),
                     pl.BlockSpec((tm, 384), lambda b, i: (i, 0))]
        args += list(rope_tabs)
    return pl.pallas_call(
        functools.partial(_inproj_kernel, rope=rope),
        grid=(nb, nt),
        in_specs=in_specs,
        out_specs=[pl.BlockSpec((tm, AB_W), lambda b, i: (b * nt + i, 0)),
                   pl.BlockSpec((tm, C_IN), lambda b, i: (b * nt + i, 0))],
        out_shape=[jax.ShapeDtypeStruct((nb * t, AB_W), BF16),
                   jax.ShapeDtypeStruct((nb * t, C_IN), F32)],
        compiler_params=_cp("parallel", "parallel"),
        name="inproj_rope" if rope else "inproj_ctx",
    )(*args)


def _rope_tables(s):
    tok = np.arange(s)
    inv = ROPE_BASE ** (-np.arange(0, 32, 2) / 32.0)
    ar = (tok // GRID_W)[:, None] * inv[None]
    ac = (tok % GRID_W)[:, None] * inv[None]
    cos = np.concatenate([np.cos(ar), np.cos(ar), np.cos(ac), np.cos(ac)], axis=1)
    sin = np.concatenate([-np.sin(ar), np.sin(ar), -np.sin(ac), np.sin(ac)], axis=1)
    return (jnp.asarray(np.tile(cos, (1, 6)), F32), jnp.asarray(np.tile(sin, (1, 6)), F32))


def _softmax_pv(s, v, sink=None):
    m = jnp.max(s, axis=-1, keepdims=True)
    if sink is not None:
        m = jnp.maximum(m, sink)
    p = jnp.exp(s - m)
    den = jnp.sum(p, axis=-1, keepdims=True)
    if sink is not None:
        den = den + jnp.exp(sink - m)
    return _dot(p.astype(BF16), v) / den


def _attn_a_kernel(q_ref, k0_ref, k1_ref, k2_ref, v0_ref, v1_ref, v2_ref, kc_ref, vc_ref, sink_ref, o_ref,
                   *, nblk):
    n = pl.program_id(1)
    q = q_ref[...]
    k_all = jnp.concatenate([k0_ref[...], k1_ref[...], k2_ref[...], kc_ref[...]], axis=0)
    v_all = jnp.concatenate([v0_ref[...], v1_ref[...], v2_ref[...], vc_ref[...]], axis=0)
    nk = 3 * A_BLOCK + kc_ref.shape[0]
    g = A_HEADS // A_KV
    row = lax.broadcasted_iota(jnp.int32, (g * A_BLOCK, nk), 0) % A_BLOCK
    col = lax.broadcasted_iota(jnp.int32, (g * A_BLOCK, nk), 1)
    rel = col - A_BLOCK - row
    ok = (jnp.abs(rel) <= A_WINDOW) & ((n > 0) | (col >= A_BLOCK)) & ((n < nblk - 1) | (col < 2 * A_BLOCK))
    ok = ok | (col >= 3 * A_BLOCK)
    for hk in range(A_KV):
        qg = jnp.concatenate([q[:, (hk * g + j) * HD:(hk * g + j + 1) * HD] for j in range(g)], axis=0)
        s = _dot_nt(qg, k_all[:, hk * HD:(hk + 1) * HD])
        s = jnp.where(ok, s, NEG)
        sink = jnp.concatenate(
            [jnp.broadcast_to(sink_ref[:, hk * g + j:hk * g + j + 1], (A_BLOCK, 1)) for j in range(g)], axis=0)
        o = _softmax_pv(s, v_all[:, hk * HD:(hk + 1) * HD], sink)
        for j in range(g):
            hq = hk * g + j
            o_ref[:, hq * HD:(hq + 1) * HD] = o[j * A_BLOCK:(j + 1) * A_BLOCK].astype(BF16)


def _attn_a(ab_lat, ab_ctx, sink, nb, s, lc):
    nblk = s // A_BLOCK

    def kv(col, d):
        return pl.BlockSpec((A_BLOCK, 128), lambda b, n: (b * nblk + jnp.clip(n + d, 0, nblk - 1), col))

    return pl.pallas_call(
        functools.partial(_attn_a_kernel, nblk=nblk),
        grid=(nb, nblk),
        in_specs=[pl.BlockSpec((A_BLOCK, 256), lambda b, n: (b * nblk + n, 0)),
                  kv(2, -1), kv(2, 0), kv(2, 1), kv(3, -1), kv(3, 0), kv(3, 1),
                  pl.BlockSpec((lc, 128), lambda b, n: (b, 2)),
                  pl.BlockSpec((lc, 128), lambda b, n: (b, 3)),
                  pl.BlockSpec((1, A_HEADS), lambda b, n: (0, 0))],
        out_specs=pl.BlockSpec((A_BLOCK, 256), lambda b, n: (b * nblk + n, 0)),
        out_shape=jax.ShapeDtypeStruct((nb * s, 256), BF16),
        compiler_params=_cp("parallel", "parallel"),
        name="attn_a",
    )(ab_lat, ab_lat, ab_lat, ab_lat, ab_lat, ab_lat, ab_lat, ab_ctx, ab_ctx, sink.reshape(1, A_HEADS))


def _attn_b_kernel(q_ref, k_ref, v_ref, kc_ref, vc_ref, bias_ref, o_ref, *, rows):
    r = pl.program_id(1)
    rs = jnp.clip(r - NA_ROWS // 2, 0, rows - NA_ROWS)
    start = pl.multiple_of(rs * GRID_W, GRID_W)
    nwin = NA_ROWS * GRID_W
    q = q_ref[...]
    kw = k_ref[pl.ds(start, nwin), :]
    vw = v_ref[pl.ds(start, nwin), :]
    kc = kc_ref[...]
    vc = vc_ref[...]
    for h in range(B_HEADS):
        sl = slice(h * HD, (h + 1) * HD)
        s_loc = _dot_nt(q[:, sl], kw[:, sl]) + bias_ref[h]
        s_ctx = _dot_nt(q[:, sl], kc[:, sl])
        m = jnp.maximum(jnp.max(s_loc, axis=-1, keepdims=True), jnp.max(s_ctx, axis=-1, keepdims=True))
        p_loc = jnp.exp(s_loc - m)
        p_ctx = jnp.exp(s_ctx - m)
        den = jnp.sum(p_loc, axis=-1, keepdims=True) + jnp.sum(p_ctx, axis=-1, keepdims=True)
        o = (_dot(p_loc.astype(BF16), vw[:, sl]) + _dot(p_ctx.astype(BF16), vc[:, sl])) / den
        o_ref[:, sl] = o.astype(BF16)


def _na_bias_tables(rpb, rows):
    kr = NA_ROWS
    qc = np.arange(GRID_W)
    win_start = np.clip(qc - NA_COLS // 2, 0, GRID_W - NA_COLS)
    kcol = np.arange(GRID_W)
    valid = (kcol[None, :] >= win_start[:, None]) & (kcol[None, :] < win_start[:, None] + NA_COLS)
    pad = GRID_W - NA_COLS
    rp = jnp.pad(rpb.astype(F32), ((0, 0), (0, 0), (pad, pad)))
    toe = jnp.stack([rp[:, :, GRID_W - 1 - q:2 * GRID_W - 1 - q] for q in range(GRID_W)], axis=2)
    toe = jnp.where(jnp.asarray(valid[None, None]), toe, NEG)
    tabs = [toe[:, off:off + kr].transpose(0, 2, 1, 3).reshape(rpb.shape[0], GRID_W, kr * GRID_W)
            for off in range(kr)]
    return jnp.stack(tabs, axis=0)


def _attn_b(ab_lat, ab_ctx, bias_tabs, nb, s, lc):
    rows = s // GRID_W

    def tab_index(b, r):
        return (jnp.clip(r - NA_ROWS // 2, 0, rows - NA_ROWS) - r + NA_ROWS - 1, 0, 0, 0)

    return pl.pallas_call(
        functools.partial(_attn_b_kernel, rows=rows),
        grid=(nb, rows),
        in_specs=[pl.BlockSpec((GRID_W, 256), lambda b, r: (b * rows + r, 2)),
                  pl.BlockSpec((s, 256), lambda b, r: (b, 3)),
                  pl.BlockSpec((s, 256), lambda b, r: (b, 4)),
                  pl.BlockSpec((lc, 256), lambda b, r: (b, 3)),
                  pl.BlockSpec((lc, 256), lambda b, r: (b, 4)),
                  pl.BlockSpec((None, B_HEADS, GRID_W, NA_ROWS * GRID_W), tab_index)],
        out_specs=pl.BlockSpec((GRID_W, 256), lambda b, r: (b * rows + r, 0)),
        out_shape=jax.ShapeDtypeStruct((nb * s, 256), BF16),
        compiler_params=_cp("parallel", "arbitrary"),
        name="attn_b",
    )(ab_lat, ab_lat, ab_lat, ab_ctx, ab_ctx, bias_tabs)


def _attn_ctx_kernel(ab_ref, sink_ref, oa_ref, ob_ref):
    lc = ab_ref.shape[0]
    g = A_HEADS // A_KV
    for hk in range(A_KV):
        qg = jnp.concatenate([ab_ref[:, (hk * g + j) * HD:(hk * g + j + 1) * HD] for j in range(g)], axis=0)
        s = _dot_nt(qg, ab_ref[:, 256 + hk * HD:256 + (hk + 1) * HD])
        sink = jnp.concatenate(
            [jnp.broadcast_to(sink_ref[:, hk * g + j:hk * g + j + 1], (lc, 1)) for j in range(g)], axis=0)
        o = _softmax_pv(s, ab_ref[:, 384 + hk * HD:384 + (hk + 1) * HD], sink)
        for j in range(g):
            hq = hk * g + j
            oa_ref[:, hq * HD:(hq + 1) * HD] = o[j * lc:(j + 1) * lc].astype(BF16)
    for h in range(B_HEADS):
        s = _dot_nt(ab_ref[:, 512 + h * HD:512 + (h + 1) * HD], ab_ref[:, 768 + h * HD:768 + (h + 1) * HD])
        o = _softmax_pv(s, ab_ref[:, 1024 + h * HD:1024 + (h + 1) * HD])
        ob_ref[:, h * HD:(h + 1) * HD] = o.astype(BF16)


def _attn_ctx(ab_ctx, sink, nb, lc):
    return pl.pallas_call(
        _attn_ctx_kernel,
        grid=(nb,),
        in_specs=[pl.BlockSpec((lc, AB_W), lambda b: (b, 0)),
                  pl.BlockSpec((1, A_HEADS), lambda b: (0, 0))],
        out_specs=[pl.BlockSpec((lc, 256), lambda b: (b, 0)),
                   pl.BlockSpec((lc, 256), lambda b: (b, 0))],
        out_shape=[jax.ShapeDtypeStruct((nb * lc, 256), BF16),
                   jax.ShapeDtypeStruct((nb * lc, 256), BF16)],
        compiler_params=_cp("parallel"),
        name="attn_ctx",
    )(ab_ctx, sink.reshape(1, A_HEADS))


def _r7prep_kernel(x_ref, prev_ref, next_ref, cw_ref, kkw_ref, ka_ref, rk_ref, w0_ref, w2_ref, a0_ref, a2_ref,
                   g2_ref, bd_ref,
                   r_ref, v_ref, kk_ref, lw_ref, beta_ref, kd_ref, g_ref, bonus_ref, pad_ref, *, nt):
    i = pl.program_id(1)
    tt = x_ref.shape[0]
    x = x_ref[...]
    pad_ref[8:8 + tt, :] = x
    pad_ref[7:8, :] = jnp.where(i > 0, prev_ref[7:8, :], 0.0)
    pad_ref[8 + tt:9 + tt, :] = jnp.where(i < nt - 1, next_ref[0:1, :], 0.0)
    xc = pad_ref[7:7 + tt, :] * cw_ref[0:1, :] + x * cw_ref[1:2, :] + pad_ref[9:9 + tt, :] * cw_ref[2:3, :]
    r = xc[:, 0:512]
    k = xc[:, 512:1024]
    v = xc[:, 1024:1536]
    wd = xc[:, 1536:1664]
    ad = xc[:, 1664:1792]
    gd = xc[:, 1792:1920]
    bd = bd_ref[...]
    kkh = k * kkw_ref[...]
    kk = kkh / jnp.maximum(jnp.sqrt(_segsum(kkh * kkh, bd)), 1e-12)
    zw = w0_ref[...] + _dot3(jnp.tanh(wd), w2_ref[...])
    za = a0_ref[...] + _dot3(ad, a2_ref[...])
    a = jax.nn.sigmoid(za)
    ka = ka_ref[...]
    kd0 = k * (1.0 + (a[:, 0:512] - 1.0) * ka)
    kd1 = k * (1.0 + (a[:, 512:1024] - 1.0) * ka)
    r_ref[...] = r
    v_ref[...] = v
    kk_ref[...] = kk
    lw_ref[...] = -DECAY_SCALE * jax.nn.sigmoid(zw)
    beta_ref[:, 0:512] = kk * a[:, 0:512]
    beta_ref[:, 512:1024] = kk * a[:, 512:1024]
    kd_ref[:, 0:512] = kd0
    kd_ref[:, 512:1024] = kd1
    g_ref[...] = _dot3(jax.nn.sigmoid(gd), g2_ref[...])
    bonus_ref[...] = _segsum(r * (kd0 + kd1) * rk_ref[...], bd) * v


def _bd2(w):
    z = jnp.zeros_like(w[0])
    return jnp.concatenate([jnp.concatenate([w[0], z], axis=1), jnp.concatenate([z, w[1]], axis=1)], axis=0)


def _r7prep(pc, lp, nb, t):
    tt = 256
    nt = t // tt
    nrow8 = t // 8

    def full(shape):
        return pl.BlockSpec(shape, lambda b, i: (0,) * len(shape))

    def rowspec(w):
        return pl.BlockSpec((tt, w), lambda b, i: (b * nt + i, 0))

    outs = [C_W, C_W, C_W, 2 * C_W, 2 * C_W, 2 * C_W, C_W, C_W]
    return pl.pallas_call(
        functools.partial(_r7prep_kernel, nt=nt),
        grid=(nb, nt),
        in_specs=[rowspec(C_IN),
                  pl.BlockSpec((8, C_IN), lambda b, i: (b * nrow8 + jnp.maximum(i * (tt // 8) - 1, 0), 0)),
                  pl.BlockSpec((8, C_IN), lambda b, i: (b * nrow8 + jnp.minimum((i + 1) * (tt // 8), nrow8 - 1), 0)),
                  full((3, C_IN)), full((1, C_W)), full((1, C_W)), full((1, C_W)),
                  full((1, 2 * C_W)), full((128, 2 * C_W)), full((1, 2 * C_W)), full((128, 2 * C_W)),
                  full((128, C_W)), full((SEG_W, SEG_W))],
        out_specs=[rowspec(w) for w in outs],
        out_shape=[jax.ShapeDtypeStruct((nb * t, w), F32) for w in outs],
        scratch_shapes=[pltpu.VMEM((tt + 16, C_IN), F32)],
        compiler_params=_cp("parallel", "parallel"),
        name="r7prep",
    )(pc, pc, pc, lp['r7_conv'], lp['r7_kk'].reshape(1, C_W), lp['r7_ka'].reshape(1, C_W),
      lp['r7_rk'].reshape(1, C_W), lp['r7_w0'].reshape(1, 2 * C_W), _bd2(lp['r7_w2']),
      lp['r7_a0'].reshape(1, 2 * C_W), _bd2(lp['r7_a2']), lp['r7_g2'], _block_diag_ones())


def _cumsum_rows(tri, x):
    t = tri.astype(BF16)
    hi = x.astype(BF16)
    rest = x - hi.astype(F32)
    mid = rest.astype(BF16)
    lo = (rest - mid.astype(F32)).astype(BF16)
    return _dot(t, hi) + (_dot(t, mid) + _dot(t, lo))


def _mm(a, b, prec):
    if prec is None:
        return _dot(a.astype(BF16), b.astype(BF16))
    return _dot(a, b, prec)


def _mm_nt(a, b, prec):
    if prec is None:
        return _dot_nt(a.astype(BF16), b.astype(BF16))
    return _dot_nt(a, b, prec)


def _scan_kernel(rf_ref, vf_ref, kkf_ref, lwf_ref, betaf_ref, kdf_ref,
                 rb_ref, vb_ref, kkb_ref, lwb_ref, betab_ref, kdb_ref, s0_ref,
                 yf_ref, yb_ref, sf_ref, st_ref, *, nc):
    c = pl.program_id(1)
    cl = SCAN_C

    @pl.when(c == 0)
    def _():
        st_ref[...] = s0_ref[...]

    ti = lax.broadcasted_iota(jnp.int32, (cl, cl), 0)
    si = lax.broadcasted_iota(jnp.int32, (cl, cl), 1)
    eye = (ti == si).astype(F32)
    incl = [si <= ti, si >= ti]
    strict = [si < ti, si > ti]

    def hs(x, h):
        return x[:, h * HD:(h + 1) * HD]

    pre = []
    for d, (r_ref, v_ref, kk_ref, lw_ref, beta_ref, kd_ref) in enumerate(
            [(rf_ref, vf_ref, kkf_ref, lwf_ref, betaf_ref, kdf_ref),
             (rb_ref, vb_ref, kkb_ref, lwb_ref, betab_ref, kdb_ref)]):
        lw = lw_ref[...]
        cum = _cumsum_rows(incl[d], lw)
        e_neg = jnp.exp(-cum)
        pre.append(dict(r_hat=r_ref[...] * jnp.exp(cum), a_hat=-kk_ref[...] * jnp.exp(cum - lw),
                        b_til=beta_ref[...] * e_neg, k_til=kd_ref[...] * e_neg,
                        wtot=jnp.exp(jnp.sum(lw, axis=0, keepdims=True)), vv=v_ref[...]))

    chains = [(d, h) for d in range(2) for h in range(C_HEADS)]
    n_ch = range(len(chains))

    def part(name, i):
        d, h = chains[i]
        return hs(pre[d][name], h)

    ar = [jnp.concatenate([part('a_hat', i), part('r_hat', i)], axis=0) for i in n_ch]
    bk = [jnp.concatenate([part('b_til', i), part('k_til', i)], axis=0) for i in n_ch]
    m = [_mm_nt(ar[i], bk[i], PREC_M) for i in n_ch]
    l_k = [jnp.where(strict[chains[i][0]], m[i][0:cl, cl:2 * cl], 0.0) for i in n_ch]
    r_b = [jnp.where(incl[chains[i][0]], m[i][cl:2 * cl, 0:cl], 0.0) for i in n_ch]
    r_k = [jnp.where(incl[chains[i][0]], m[i][cl:2 * cl, cl:2 * cl], 0.0) for i in n_ch]
    p = [jnp.where(strict[chains[i][0]], m[i][0:cl, 0:cl], 0.0) for i in n_ch]
    t_inv = [eye + p[i] for i in n_ch]
    n = 2
    while n < cl:
        p = [_mm(p[i], p[i], PREC_INV) for i in n_ch]
        t_inv = [t_inv[i] + _mm(t_inv[i], p[i], PREC_INV) for i in n_ch]
        n *= 2
    x1 = [_mm(l_k[i], part('vv', i), PREC_M) for i in n_ch]
    y0 = [_mm(r_k[i], part('vv', i), PREC_M) for i in n_ch]
    ua = [_mm(t_inv[i], jnp.concatenate([x1[i], part('a_hat', i)], axis=1), PREC_INV) for i in n_ch]
    s0 = [st_ref[d, h] for d, h in chains]
    as0 = [_mm_nt(jnp.concatenate([ua[i][:, HD:2 * HD], part('r_hat', i)], axis=0), s0[i], PREC_SEQ)
           for i in n_ch]
    u = [ua[i][:, 0:HD] + as0[i][0:cl] for i in n_ch]
    y = [y0[i] + as0[i][cl:2 * cl] + _mm(r_b[i], u[i], PREC_SEQ) for i in n_ch]
    for i in n_ch:
        d, h = chains[i]
        (yf_ref, yb_ref)[d][:, h * HD:(h + 1) * HD] = y[i]
    for i in n_ch:
        d, h = chains[i]
        uv_t = jnp.concatenate([u[i], part('vv', i)], axis=0).T
        st_ref[d, h] = (s0[i] + _mm(uv_t, bk[i], PREC_SEQ)) * part('wtot', i)

    @pl.when(c == nc - 1)
    def _():
        sf_ref[...] = st_ref[...]


def _scan(prep, s0, nb, t):
    r, v, kk, lw, beta, kd = prep
    nc = t // SCAN_C

    def rows(d, b, c):
        return b * nc + (nc - 1 - c if d else c)

    def specs(d):
        shared = pl.BlockSpec((SCAN_C, C_W), lambda b, c: (rows(d, b, c), 0))
        perdir = pl.BlockSpec((SCAN_C, C_W), lambda b, c: (rows(d, b, c), d))
        return [shared, shared, shared, perdir, perdir, perdir]

    st_spec = pl.BlockSpec((2, None, C_HEADS, HD, HD), lambda b, c: (0, b, 0, 0, 0))
    yf, yb, sf = pl.pallas_call(
        functools.partial(_scan_kernel, nc=nc),
        grid=(nb, nc),
        in_specs=specs(0) + specs(1) + [st_spec],
        out_specs=[pl.BlockSpec((SCAN_C, C_W), lambda b, c: (rows(0, b, c), 0)),
                   pl.BlockSpec((SCAN_C, C_W), lambda b, c: (rows(1, b, c), 0)), st_spec],
        out_shape=[jax.ShapeDtypeStruct((nb * t, C_W), F32), jax.ShapeDtypeStruct((nb * t, C_W), F32),
                   jax.ShapeDtypeStruct((2, nb, C_HEADS, HD, HD), F32)],
        scratch_shapes=[pltpu.VMEM((2, C_HEADS, HD, HD), F32)],
        compiler_params=_cp("parallel", "arbitrary"),
        name="r7scan",
    )(r, v, kk, lw, beta, kd, r, v, kk, lw, beta, kd, s0)
    return (yf, yb), sf


def _outproj_kernel(x_ref, oa_ref, ob_ref, y0_ref, y1_ref, bonus_ref, g_ref, lnw_ref, lnb_ref, bd_ref, w_ref,
                    gate_ref, o_ref):
    bd = bd_ref[...]
    y = y0_ref[...] + y1_ref[...]
    mu = _segsum(y, bd) * (1.0 / HD)
    yc = y - mu
    var = _segsum(yc * yc, bd) * (1.0 / HD)
    yn = yc * lax.rsqrt(var + GN_EPS) * lnw_ref[...] + lnb_ref[...]
    oc = ((yn + bonus_ref[...]) * g_ref[...]).astype(BF16)
    acc = (_dot(oa_ref[...], w_ref[0:256, :]) + _dot(ob_ref[...], w_ref[256:512, :])
           + _dot(oc, w_ref[512:1024, :]))
    o_ref[...] = x_ref[...] + gate_ref[...] * acc


def _outproj(x2, oa, ob, y, bonus, g, lp, w_out_bf, mod3, mod_row_of_batch, nb, t):
    tm = 256
    nt = t // tm

    def rowspec(w):
        return pl.BlockSpec((tm, w), lambda b, i: (b * nt + i, 0))

    def full(shape):
        return pl.BlockSpec(shape, lambda b, i: (0,) * len(shape))

    return pl.pallas_call(
        _outproj_kernel,
        grid=(nb, nt),
        in_specs=[rowspec(D), rowspec(256), rowspec(256),
                  rowspec(C_W), rowspec(C_W),
                  rowspec(C_W), rowspec(C_W), full((1, C_W)), full((1, C_W)), full((SEG_W, SEG_W)), full((D, D)),
                  pl.BlockSpec((None, 1, D), lambda b, i: (mod_row_of_batch(b), 0, 2))],
        out_specs=rowspec(D),
        out_shape=jax.ShapeDtypeStruct((nb * t, D), F32),
        compiler_params=_cp("parallel", "parallel"),
        name="outproj",
    )(x2, oa, ob, y[0], y[1], bonus, g, lp['r7_lnw'].reshape(1, C_W), lp['r7_lnb'].reshape(1, C_W),
      _block_diag_ones(), w_out_bf, mod3)


def _peer_score_kernel(x_ref, g_ref, sc_ref, sh_ref, wq_ref, keys_ref, h_ref, st_ref):
    x = x_ref[...]
    y = x * lax.rsqrt(jnp.mean(x * x, axis=-1, keepdims=True) + EPS) * g_ref[...]
    h = (y * (1.0 + sc_ref[...]) + sh_ref[...]).astype(BF16)
    h_ref[...] = h
    q = _dot(h, wq_ref[...])
    half = P_QDIM // 2
    for hp in range(2 * P_HEADS):
        st_ref[hp] = _dot3(keys_ref[hp], q[:, hp * half:(hp + 1) * half], nt=True)


def _peer_scores(x2, norm_g, mod3, mod_row_of_batch, wq_bf, keys, nb, t):
    tm = 256
    nt = t // tm
    n = nb * t
    return pl.pallas_call(
        _peer_score_kernel,
        grid=(nb, nt),
        in_specs=[pl.BlockSpec((tm, D), lambda b, i: (b * nt + i, 0)),
                  pl.BlockSpec((1, D), lambda b, i: (0, 0)),
                  pl.BlockSpec((None, 1, D), lambda b, i: (mod_row_of_batch(b), 0, 4)),
                  pl.BlockSpec((None, 1, D), lambda b, i: (mod_row_of_batch(b), 0, 3)),
                  pl.BlockSpec((D, P_HEADS * P_QDIM), lambda b, i: (0, 0)),
                  pl.BlockSpec((2 * P_HEADS, P_NKEYS, P_QDIM // 2), lambda b, i: (0, 0, 0))],
        out_specs=[pl.BlockSpec((tm, D), lambda b, i: (b * nt + i, 0)),
                   pl.BlockSpec((2 * P_HEADS, P_NKEYS, tm), lambda b, i: (0, 0, b * nt + i))],
        out_shape=[jax.ShapeDtypeStruct((n, D), BF16),
                   jax.ShapeDtypeStruct((2 * P_HEADS, P_NKEYS, n), F32)],
        compiler_params=_cp("parallel", "parallel"),
        name="peer_scores",
    )(x2, norm_g.reshape(1, D), mod3, mod3, wq_bf, keys.reshape(2 * P_HEADS, P_NKEYS, P_QDIM // 2))


def _batcher_pairs(n):
    pairs = []
    p = 1
    while p < n:
        k = p
        while k >= 1:
            for j in range(k % p, n - k, 2 * k):
                for i in range(min(k, n - j - k)):
                    if (i + j) // (2 * p) == (i + j + k) // (2 * p):
                        pairs.append((i + j, i + j + k))
            k //= 2
        p *= 2
    return pairs


_SORT16 = _batcher_pairs(P_TOPK)
_BITONIC16 = [(i, i + d) for d in (8, 4, 2, 1) for i in range(P_TOPK) if i & d == 0]


def _compare_exchange(x, pairs):
    for i, j in pairs:
        x[i], x[j] = jnp.maximum(x[i], x[j]), jnp.minimum(x[i], x[j])
    return x


def _top16_sorted(slabs):
    x = _compare_exchange(list(slabs), _SORT16)
    for shift in (4, 2, 1):
        y = [pltpu.roll(x[P_TOPK - 1 - k], shift, 0) for k in range(P_TOPK)]
        x = [jnp.maximum(x[k], y[k]) for k in range(P_TOPK)]
        x = _compare_exchange(x, _BITONIC16)
    return x


def _peer_gate_tables(st_ref, rk_ref, be_ref, cnt_ref, al_ref, top_ref):
    tm = st_ref.shape[-1]
    sub = lax.broadcasted_iota(jnp.int32, (8, tm), 0)
    ninf = jnp.full((8, tm), -jnp.inf, F32)
    for h in range(P_HEADS):
        for p in range(2):
            top = _top16_sorted([st_ref[2 * h + p, 8 * k:8 * k + 8, :] for k in range(P_TOPK)])
            for k in range(P_TOPK):
                top_ref[p * P_TOPK + k:p * P_TOPK + k + 1, :] = top[k][0:1]
        a16 = top_ref[0:P_TOPK, :]
        b16 = top_ref[P_TOPK:2 * P_TOPK, :]
        b8 = b16[0:8]
        cand = [a16[0:1] + b8, a16[0:1] + b16[8:16]]
        for p in range(2, 9):
            cand.append(jnp.where(sub < P_TOPK // p, a16[p - 1:p] + b8, -jnp.inf))
        cand.append(a16[8:16] + b16[0:1])
        best = _top16_sorted(cand + [ninf] * (P_TOPK - len(cand)))
        tau = best[P_TOPK - 1][0:1]
        z = jnp.ones_like(tau)
        for k in range(1, P_TOPK):
            z = z + jnp.exp(best[k][0:1] - best[0][0:1])
        s1 = st_ref[2 * h]
        s2 = st_ref[2 * h + 1]
        cnt = jnp.zeros_like(s1)
        rk = jnp.ones_like(s2)
        for q in range(P_TOPK):
            bq = b16[q:q + 1]
            theta = jnp.min(jnp.where(a16 + bq >= tau, a16, jnp.inf), axis=0, keepdims=True)
            cnt = cnt + jnp.where(s1 >= theta, 1.0, 0.0)
            rk = rk + jnp.where(bq > s2, 1.0, 0.0)
        cnt_ref[h] = cnt
        rk_ref[h] = rk.astype(BF16)
        al_ref[h] = jnp.exp(s1 - a16[0:1]) / z
        be_ref[h] = jnp.exp(s2 - b16[0:1]).astype(BF16)


def _peer_chunk(chunk, act_ref, rk_ref, be_ref, cnt_ref, al_ref, w_ref):
    nrow = w_ref.shape[0] // P_NKEYS
    for ii in range(nrow):
        i = chunk * nrow + ii
        wrow = None
        for h in range(P_HEADS):
            cnt = cnt_ref[h, pl.ds(i, 1), :].astype(BF16)
            al = al_ref[h, pl.ds(i, 1), :].astype(BF16)
            term = jnp.where(rk_ref[h] <= cnt, al * be_ref[h], 0.0)
            wrow = term if wrow is None else wrow + term
        rows = slice(ii * P_NKEYS, (ii + 1) * P_NKEYS)
        act = act_ref[rows, :]
        gl = 0.5 * act * (1.0 + lax.erf(act * (2.0 ** -0.5)))
        w_ref[rows, :] = wrow * gl.astype(BF16)


def _peer_dense_kernel(x_ref, h_ref, st_ref, u_ref, vt_ref, gate_ref, o_ref,
                       rk_ref, be_ref, cnt_ref, al_ref, top_ref, act0_ref, act1_ref, w_ref, acc_ref, *, ne):
    e = pl.program_id(1)

    @pl.when(e == 0)
    def _():
        act0_ref[...] = _dot_nt(u_ref[...], h_ref[...])
        _peer_gate_tables(st_ref, rk_ref, be_ref, cnt_ref, al_ref, top_ref)
        acc_ref[...] = jnp.zeros_like(acc_ref)

    def step(cur_ref, nxt_ref):
        nxt_ref[...] = _dot_nt(u_ref[...], h_ref[...])
        _peer_chunk(e - 1, cur_ref, rk_ref, be_ref, cnt_ref, al_ref, w_ref)
        acc_ref[...] += _dot(vt_ref[...], w_ref[...])

    @pl.when(e % 2 == 1)
    def _():
        step(act0_ref, act1_ref)

    @pl.when((e > 0) & (e % 2 == 0))
    def _():
        step(act1_ref, act0_ref)

    @pl.when(e == ne)
    def _():
        o_ref[...] = x_ref[...] + gate_ref[...] * acc_ref[...].T


def _peer_dense(x2, h2, st, u_bf, vt_bf, mod3, mod_row_of_batch, nb, t):
    tm = 512 if t % 512 == 0 else 256
    ec = PEER_EC
    nt = t // tm
    ne = P_EXPERTS // ec
    return pl.pallas_call(
        functools.partial(_peer_dense_kernel, ne=ne),
        grid=(nb * nt, ne + 1),
        in_specs=[pl.BlockSpec((tm, D), lambda i, e: (i, 0)),
                  pl.BlockSpec((tm, D), lambda i, e: (i, 0)),
                  pl.BlockSpec((2 * P_HEADS, P_NKEYS, tm), lambda i, e: (0, 0, i)),
                  pl.BlockSpec((ec, D), lambda i, e: (jnp.minimum(e, ne - 1), 0)),
                  pl.BlockSpec((None, D, ec), lambda i, e: (jnp.maximum(e - 1, 0), 0, 0)),
                  pl.BlockSpec((None, 1, D), lambda i, e: (mod_row_of_batch(i // nt), 0, 5))],
        out_specs=pl.BlockSpec((tm, D), lambda i, e: (i, 0)),
        out_shape=jax.ShapeDtypeStruct((nb * t, D), F32),
        scratch_shapes=[pltpu.VMEM((P_HEADS, P_NKEYS, tm), BF16),
                        pltpu.VMEM((P_HEADS, P_NKEYS, tm), BF16),
                        pltpu.VMEM((P_HEADS, P_NKEYS, tm), F32),
                        pltpu.VMEM((P_HEADS, P_NKEYS, tm), F32),
                        pltpu.VMEM((2 * P_TOPK, tm), F32),
                        pltpu.VMEM((ec, tm), F32),
                        pltpu.VMEM((ec, tm), F32),
                        pltpu.VMEM((ec, tm), BF16),
                        pltpu.VMEM((D, tm), F32)],
        compiler_params=pltpu.CompilerParams(dimension_semantics=("parallel", "arbitrary"),
                                             vmem_limit_bytes=PEER_VMEM_LIMIT),
        name="peer_dense",
    )(x2, h2, st, u_bf, vt_bf, mod3)


def _peer(x2, norm_g, mod3, mod_row_of_batch, wq_bf, keys, u_bf, vt_bf, nb, t):
    h2, st = _peer_scores(x2, norm_g, mod3, mod_row_of_batch, wq_bf, keys, nb, t)
    return _peer_dense(x2, h2, st, u_bf, vt_bf, mod3, mod_row_of_batch, nb, t)


def kernel(x, c, ctx, c_ctx, norm_mix, norm_ffn, w_mod, b_mod, w_in, w_out, a_qnorm, a_knorm, a_sink, b_qnorm,
           b_knorm, b_rpb, r7_conv, r7_w0, r7_w2, r7_a0, r7_a2, r7_g2, r7_kk, r7_ka, r7_rk, r7_lnw, r7_lnb,
           peer_wq, peer_keys, peer_u, peer_v):
    nb, s, _ = x.shape
    lc = ctx.shape[1]
    depth = w_in.shape[0]
    assert nb < 16 and s % 256 == 0 and lc % 256 == 0
    rows = s // GRID_W

    cc = jnp.zeros((16, D), F32).at[:nb].set(c).at[nb].set(c_ctx)
    mod = _modulation(cc, w_mod, b_mod)
    rope_tabs = _rope_tables(s)
    lat_row = lambda b: b
    ctx_row = lambda b: nb

    x_lat = x.reshape(nb * s, D)
    x_ctx = ctx.reshape(nb * lc, D)
    scale = HD ** -0.5
    for l in range(depth):
        with_ctx = l < depth - 1
        mod3 = mod[l].reshape(16, 1, 6 * D)
        lp = {'r7_conv': r7_conv[l], 'r7_w0': r7_w0[l], 'r7_w2': r7_w2[l], 'r7_a0': r7_a0[l], 'r7_a2': r7_a2[l],
              'r7_g2': r7_g2[l], 'r7_kk': r7_kk[l], 'r7_ka': r7_ka[l], 'r7_rk': r7_rk[l], 'r7_lnw': r7_lnw[l],
              'r7_lnb': r7_lnb[l]}
        w_in_bf = w_in[l].astype(BF16)
        w_out_bf = w_out[l].astype(BF16)
        gain_a = jnp.concatenate([jnp.tile(a_qnorm[l] * scale, A_HEADS), jnp.tile(a_knorm[l], A_KV)]).reshape(1, 384)
        gain_b = jnp.concatenate([jnp.tile(b_qnorm[l] * scale, B_HEADS), jnp.tile(b_knorm[l], B_HEADS)]).reshape(1, 512)

        ab_lat, pc_lat = _inproj(x_lat, mod3, lat_row, norm_mix[l], w_in_bf, gain_a, gain_b, rope_tabs, nb, s)
        ab_ctx, pc_ctx = _inproj(x_ctx, mod3, ctx_row, norm_mix[l], w_in_bf, gain_a, gain_b, None, nb, lc)

        o_a = _attn_a(ab_lat, ab_ctx, a_sink[l], nb, s, lc)
        o_b = _attn_b(ab_lat, ab_ctx, _na_bias_tables(b_rpb[l], rows), nb, s, lc)

        prep_ctx = _r7prep(pc_ctx, lp, nb, lc)
        prep_lat = _r7prep(pc_lat, lp, nb, s)
        zero_state = jnp.zeros((2, nb, C_HEADS, HD, HD), F32)
        y_ctx, s_ctx = _scan(prep_ctx[:6], zero_state, nb, lc)
        y_lat, _ = _scan(prep_lat[:6], s_ctx, nb, s)

        x_lat = _outproj(x_lat, o_a, o_b, y_lat, prep_lat[7], prep_lat[6], lp, w_out_bf, mod3, lat_row, nb, s)
        wq_bf = peer_wq[l].astype(BF16)
        u_bf = peer_u[l].astype(BF16)
        vt_bf = peer_v[l].astype(BF16).reshape(P_EXPERTS // PEER_EC, PEER_EC, D).transpose(0, 2, 1)
        x_lat = _peer(x_lat, norm_ffn[l], mod3, lat_row, wq_bf, peer_keys[l], u_bf, vt_bf, nb, s)
        if with_ctx:
            o_ac, o_bc = _attn_ctx(ab_ctx, a_sink[l], nb, lc)
            x_ctx = _outproj(x_ctx, o_ac, o_bc, y_ctx, prep_ctx[7], prep_ctx[6], lp, w_out_bf, mod3, ctx_row, nb, lc)
            x_ctx = _peer(x_ctx, norm_ffn[l], mod3, ctx_row, wq_bf, peer_keys[l], u_bf, vt_bf, nb, lc)
    return x_lat.reshape(nb, s, D)
```

```python
import functools

import numpy as np
import jax
import jax.numpy as jnp
from jax import lax
from jax.experimental import pallas as pl
from jax.experimental.pallas import tpu as pltpu

F32 = jnp.float32
BF16 = jnp.bfloat16
HI = lax.Precision.HIGHEST

D = 1024
DEPTH = 2
GRID_W = 64
HD = 64
LANES = 128
EPS = 1e-6
NEG = -1e30
A_HEADS, A_KV, A_BLOCK, A_WINDOW = 4, 2, 128, 128
B_HEADS, NA_ROWS, NA_COLS = 4, 8, 16
C_HEADS = 8
C_W = 512
C_IN = 1920
AB_W = 1280
IN_W = AB_W + C_IN
GN_EPS = 64e-5
DECAY_SCALE = 0.6065306597126334
ROPE_BASE = 10000.0
P_HEADS, P_NKEYS, P_QDIM, P_TOPK = 8, 128, 256, 16
P_EXPERTS = P_NKEYS * P_NKEYS
SCAN_C = 64
SCAN_NB = 2
PREC_M = PREC_INV = PREC_SEQ = None
VMEM_LIMIT = 48 * 1024 * 1024
PEER_VMEM_LIMIT = 56 * 1024 * 1024
PEER_EC = 1024


def _cp(*sem):
    return pltpu.CompilerParams(dimension_semantics=sem, vmem_limit_bytes=VMEM_LIMIT)


def _dot(a, b, prec=None):
    return jnp.dot(a, b, precision=prec, preferred_element_type=F32)


def _dot_nt(a, b, prec=None):
    return lax.dot_general(a, b, (((1,), (1,)), ((), ())), precision=prec, preferred_element_type=F32)


SEG_W = 256


def _block_diag_ones():
    i = np.arange(SEG_W) // HD
    return jnp.asarray((i[:, None] == i[None, :]).astype(np.float32), BF16)


def _split2(x):
    hi = x.astype(BF16)
    return hi, (x - hi.astype(F32)).astype(BF16)


def _segsum(x, bd):
    hi, lo = _split2(x)
    outs = []
    for g0 in range(0, x.shape[1], SEG_W):
        w = min(SEG_W, x.shape[1] - g0)
        outs.append(_dot(hi[:, g0:g0 + w], bd[0:w, 0:w]) + _dot(lo[:, g0:g0 + w], bd[0:w, 0:w]))
    return outs[0] if len(outs) == 1 else jnp.concatenate(outs, axis=1)


def _dot3(a, b, nt=False):
    f = _dot_nt if nt else _dot
    a_hi, a_lo = _split2(a)
    b_hi, b_lo = _split2(b)
    return f(a_hi, b_hi) + (f(a_lo, b_hi) + f(a_hi, b_lo))


def _mod_kernel(c_ref, w_ref, b_ref, o_ref):
    c = c_ref[...]
    s = c * jax.nn.sigmoid(c)
    o_ref[...] = _dot(s, w_ref[...], HI) + b_ref[...]


def _modulation(cc, w_mod, b_mod):
    L, _, n = w_mod.shape
    tn = 512
    return pl.pallas_call(
        _mod_kernel,
        grid=(L, n // tn),
        in_specs=[pl.BlockSpec((16, D), lambda l, j: (0, 0)),
                  pl.BlockSpec((None, D, tn), lambda l, j: (l, 0, j)),
                  pl.BlockSpec((None, 1, tn), lambda l, j: (l, 0, j))],
        out_specs=pl.BlockSpec((None, 16, tn), lambda l, j: (l, 0, j)),
        out_shape=jax.ShapeDtypeStruct((L, 16, n), F32),
        compiler_params=_cp("parallel", "parallel"),
        name="modulation",
    )(cc, w_mod, b_mod.reshape(L, 1, n))


def _swap16(x):
    n = x.shape[-1]
    lane = lax.broadcasted_iota(jnp.int32, x.shape, 1)
    fwd = pltpu.roll(x, n - 16, 1)
    bwd = pltpu.roll(x, 16, 1)
    return jnp.where((lane % 32) < 16, fwd, bwd)


def _head_rms(x, bd, gain):
    ss = _segsum(x * x, bd)
    return x * lax.rsqrt(ss * (1.0 / HD) + EPS) * gain


def _inproj_kernel(*refs, rope):
    if rope:
        (x_ref, g_ref, sc_ref, sh_ref, w_ref, ga_ref, gb_ref, bd_ref, cos_ref, sin_ref,
         ab_ref, c_ref) = refs
    else:
        (x_ref, g_ref, sc_ref, sh_ref, w_ref, ga_ref, gb_ref, bd_ref, ab_ref, c_ref) = refs
    x = x_ref[...]
    y = x * lax.rsqrt(jnp.mean(x * x, axis=-1, keepdims=True) + EPS) * g_ref[...]
    h = y * (1.0 + sc_ref[...]) + sh_ref[...]
    acc = _dot(h.astype(BF16), w_ref[...])
    bd = bd_ref[...]
    qa = _head_rms(acc[:, 0:384], bd, ga_ref[...])
    if rope:
        qa = qa * cos_ref[...] + _swap16(qa) * sin_ref[...]
    qb = _head_rms(acc[:, 512:1024], bd, gb_ref[...])
    ab_ref[:, 0:384] = qa.astype(BF16)
    ab_ref[:, 384:512] = acc[:, 384:512].astype(BF16)
    ab_ref[:, 512:1024] = qb.astype(BF16)
    ab_ref[:, 1024:1280] = acc[:, 1024:1280].astype(BF16)
    c_ref[...] = acc[:, AB_W:IN_W]


def _inproj(x2, mod3, mod_row_of_batch, norm_g, w_in_bf, gain_a, gain_b, rope_tabs, nb, t):
    tm = 256
    nt = t // tm
    rope = rope_tabs is not None
    in_specs = [
        pl.BlockSpec((tm, D), lambda b, i: (b * nt + i, 0)),
        pl.BlockSpec((1, D), lambda b, i: (0, 0)),
        pl.BlockSpec((None, 1, D), lambda b, i: (mod_row_of_batch(b), 0, 1)),
        pl.BlockSpec((None, 1, D), lambda b, i: (mod_row_of_batch(b), 0, 0)),
        pl.BlockSpec((D, IN_W), lambda b, i: (0, 0)),
        pl.BlockSpec((1, 384), lambda b, i: (0, 0)),
        pl.BlockSpec((1, 512), lambda b, i: (0, 0)),
        pl.BlockSpec((SEG_W, SEG_W), lambda b, i: (0, 0)),
    ]
    args = [x2, norm_g.reshape(1, D), mod3, mod3, w_in_bf, gain_a, gain_b,
            _block_diag_ones()]
    if rope:
        in_specs += [pl.BlockSpec((tm, 384), lambda b, i: (i, 0)),
                     pl.BlockSpec((tm, 384), lambda b, i: (i, 0))]
        args += list(rope_tabs)
    return pl.pallas_call(
        functools.partial(_inproj_kernel, rope=rope),
        grid=(nb, nt),
        in_specs=in_specs,
        out_specs=[pl.BlockSpec((tm, AB_W), lambda b, i: (b * nt + i, 0)),
                   pl.BlockSpec((tm, C_IN), lambda b, i: (b * nt + i, 0))],
        out_shape=[jax.ShapeDtypeStruct((nb * t, AB_W), BF16),
                   jax.ShapeDtypeStruct((nb * t, C_IN), F32)],
        compiler_params=_cp("parallel", "parallel"),
        name="inproj_rope" if rope else "inproj_ctx",
    )(*args)


def _rope_tables(s):
    tok = np.arange(s)
    inv = ROPE_BASE ** (-np.arange(0, 32, 2) / 32.0)
    ar = (tok // GRID_W)[:, None] * inv[None]
    ac = (tok % GRID_W)[:, None] * inv[None]
    cos = np.concatenate([np.cos(ar), np.cos(ar), np.cos(ac), np.cos(ac)], axis=1)
    sin = np.concatenate([-np.sin(ar), np.sin(ar), -np.sin(ac), np.sin(ac)], axis=1)
    return (jnp.asarray(np.tile(cos, (1, 6)), F32), jnp.asarray(np.tile(sin, (1, 6)), F32))


def _softmax_pv(s, v, sink=None):
    m = jnp.max(s, axis=-1, keepdims=True)
    if sink is not None:
        m = jnp.maximum(m, sink)
    p = jnp.exp(s - m)
    den = jnp.sum(p, axis=-1, keepdims=True)
    if sink is not None:
        den = den + jnp.exp(sink - m)
    return _dot(p.astype(BF16), v) / den


def _attn_a_kernel(q_ref, k0_ref, k1_ref, k2_ref, v0_ref, v1_ref, v2_ref, kc_ref, vc_ref, sink_ref, o_ref,
                   *, nblk):
    n = pl.program_id(1)
    q = q_ref[...]
    k_all = jnp.concatenate([k0_ref[...], k1_ref[...], k2_ref[...], kc_ref[...]], axis=0)
    v_all = jnp.concatenate([v0_ref[...], v1_ref[...], v2_ref[...], vc_ref[...]], axis=0)
    nk = 3 * A_BLOCK + kc_ref.shape[0]
    g = A_HEADS // A_KV
    row = lax.broadcasted_iota(jnp.int32, (g * A_BLOCK, nk), 0) % A_BLOCK
    col = lax.broadcasted_iota(jnp.int32, (g * A_BLOCK, nk), 1)
    rel = col - A_BLOCK - row
    ok = (jnp.abs(rel) <= A_WINDOW) & ((n > 0) | (col >= A_BLOCK)) & ((n < nblk - 1) | (col < 2 * A_BLOCK))
    ok = ok | (col >= 3 * A_BLOCK)
    for hk in range(A_KV):
        qg = jnp.concatenate([q[:, (hk * g + j) * HD:(hk * g + j + 1) * HD] for j in range(g)], axis=0)
        s = _dot_nt(qg, k_all[:, hk * HD:(hk + 1) * HD])
        s = jnp.where(ok, s, NEG)
        sink = jnp.concatenate(
            [jnp.broadcast_to(sink_ref[:, hk * g + j:hk * g + j + 1], (A_BLOCK, 1)) for j in range(g)], axis=0)
        o = _softmax_pv(s, v_all[:, hk * HD:(hk + 1) * HD], sink)
        for j in range(g):
            hq = hk * g + j
            o_ref[:, hq * HD:(hq + 1) * HD] = o[j * A_BLOCK:(j + 1) * A_BLOCK].astype(BF16)


def _attn_a(ab_lat, ab_ctx, sink, nb, s, lc):
    nblk = s // A_BLOCK

    def kv(col, d):
        return pl.BlockSpec((A_BLOCK, 128), lambda b, n: (b * nblk + jnp.clip(n + d, 0, nblk - 1), col))

    return pl.pallas_call(
        functools.partial(_attn_a_kernel, nblk=nblk),
        grid=(nb, nblk),
        in_specs=[pl.BlockSpec((A_BLOCK, 256), lambda b, n: (b * nblk + n, 0)),
                  kv(2, -1), kv(2, 0), kv(2, 1), kv(3, -1), kv(3, 0), kv(3, 1),
                  pl.BlockSpec((lc, 128), lambda b, n: (b, 2)),
                  pl.BlockSpec((lc, 128), lambda b, n: (b, 3)),
                  pl.BlockSpec((1, A_HEADS), lambda b, n: (0, 0))],
        out_specs=pl.BlockSpec((A_BLOCK, 256), lambda b, n: (b * nblk + n, 0)),
        out_shape=jax.ShapeDtypeStruct((nb * s, 256), BF16),
        compiler_params=_cp("parallel", "parallel"),
        name="attn_a",
    )(ab_lat, ab_lat, ab_lat, ab_lat, ab_lat, ab_lat, ab_lat, ab_ctx, ab_ctx, sink.reshape(1, A_HEADS))


def _attn_b_kernel(q_ref, k_ref, v_ref, kc_ref, vc_ref, bias_ref, o_ref, *, rows):
    r = pl.program_id(1)
    rs = jnp.clip(r - NA_ROWS // 2, 0, rows - NA_ROWS)
    start = pl.multiple_of(rs * GRID_W, GRID_W)
    nwin = NA_ROWS * GRID_W
    q = q_ref[...]
    kw = k_ref[pl.ds(start, nwin), :]
    vw = v_ref[pl.ds(start, nwin), :]
    kc = kc_ref[...]
    vc = vc_ref[...]
    for h in range(B_HEADS):
        sl = slice(h * HD, (h + 1) * HD)
        s_loc = _dot_nt(q[:, sl], kw[:, sl]) + bias_ref[h]
        s_ctx = _dot_nt(q[:, sl], kc[:, sl])
        m = jnp.maximum(jnp.max(s_loc, axis=-1, keepdims=True), jnp.max(s_ctx, axis=-1, keepdims=True))
        p_loc = jnp.exp(s_loc - m)
        p_ctx = jnp.exp(s_ctx - m)
        den = jnp.sum(p_loc, axis=-1, keepdims=True) + jnp.sum(p_ctx, axis=-1, keepdims=True)
        o = (_dot(p_loc.astype(BF16), vw[:, sl]) + _dot(p_ctx.astype(BF16), vc[:, sl])) / den
        o_ref[:, sl] = o.astype(BF16)


def _na_bias_tables(rpb, rows):
    kr = NA_ROWS
    qc = np.arange(GRID_W)
    win_start = np.clip(qc - NA_COLS // 2, 0, GRID_W - NA_COLS)
    kcol = np.arange(GRID_W)
    valid = (kcol[None, :] >= win_start[:, None]) & (kcol[None, :] < win_start[:, None] + NA_COLS)
    pad = GRID_W - NA_COLS
    rp = jnp.pad(rpb.astype(F32), ((0, 0), (0, 0), (pad, pad)))
    toe = jnp.stack([rp[:, :, GRID_W - 1 - q:2 * GRID_W - 1 - q] for q in range(GRID_W)], axis=2)
    toe = jnp.where(jnp.asarray(valid[None, None]), toe, NEG)
    tabs = [toe[:, off:off + kr].transpose(0, 2, 1, 3).reshape(rpb.shape[0], GRID_W, kr * GRID_W)
            for off in range(kr)]
    return jnp.stack(tabs, axis=0)


def _attn_b(ab_lat, ab_ctx, bias_tabs, nb, s, lc):
    rows = s // GRID_W

    def tab_index(b, r):
        return (jnp.clip(r - NA_ROWS // 2, 0, rows - NA_ROWS) - r + NA_ROWS - 1, 0, 0, 0)

    return pl.pallas_call(
        functools.partial(_attn_b_kernel, rows=rows),
        grid=(nb, rows),
        in_specs=[pl.BlockSpec((GRID_W, 256), lambda b, r: (b * rows + r, 2)),
                  pl.BlockSpec((s, 256), lambda b, r: (b, 3)),
                  pl.BlockSpec((s, 256), lambda b, r: (b, 4)),
                  pl.BlockSpec((lc, 256), lambda b, r: (b, 3)),
                  pl.BlockSpec((lc, 256), lambda b, r: (b, 4)),
                  pl.BlockSpec((None, B_HEADS, GRID_W, NA_ROWS * GRID_W), tab_index)],
        out_specs=pl.BlockSpec((GRID_W, 256), lambda b, r: (b * rows + r, 0)),
        out_shape=jax.ShapeDtypeStruct((nb * s, 256), BF16),
        compiler_params=_cp("parallel", "arbitrary"),
        name="attn_b",
    )(ab_lat, ab_lat, ab_lat, ab_ctx, ab_ctx, bias_tabs)


def _attn_ctx_kernel(ab_ref, sink_ref, oa_ref, ob_ref):
    lc = ab_ref.shape[0]
    g = A_HEADS // A_KV
    for hk in range(A_KV):
        qg = jnp.concatenate([ab_ref[:, (hk * g + j) * HD:(hk * g + j + 1) * HD] for j in range(g)], axis=0)
        s = _dot_nt(qg, ab_ref[:, 256 + hk * HD:256 + (hk + 1) * HD])
        sink = jnp.concatenate(
            [jnp.broadcast_to(sink_ref[:, hk * g + j:hk * g + j + 1], (lc, 1)) for j in range(g)], axis=0)
        o = _softmax_pv(s, ab_ref[:, 384 + hk * HD:384 + (hk + 1) * HD], sink)
        for j in range(g):
            hq = hk * g + j
            oa_ref[:, hq * HD:(hq + 1) * HD] = o[j * lc:(j + 1) * lc].astype(BF16)
    for h in range(B_HEADS):
        s = _dot_nt(ab_ref[:, 512 + h * HD:512 + (h + 1) * HD], ab_ref[:, 768 + h * HD:768 + (h + 1) * HD])
        o = _softmax_pv(s, ab_ref[:, 1024 + h * HD:1024 + (h + 1) * HD])
        ob_ref[:, h * HD:(h + 1) * HD] = o.astype(BF16)


def _attn_ctx(ab_ctx, sink, nb, lc):
    return pl.pallas_call(
        _attn_ctx_kernel,
        grid=(nb,),
        in_specs=[pl.BlockSpec((lc, AB_W), lambda b: (b, 0)),
                  pl.BlockSpec((1, A_HEADS), lambda b: (0, 0))],
        out_specs=[pl.BlockSpec((lc, 256), lambda b: (b, 0)),
                   pl.BlockSpec((lc, 256), lambda b: (b, 0))],
        out_shape=[jax.ShapeDtypeStruct((nb * lc, 256), BF16),
                   jax.ShapeDtypeStruct((nb * lc, 256), BF16)],
        compiler_params=_cp("parallel"),
        name="attn_ctx",
    )(ab_ctx, sink.reshape(1, A_HEADS))


def _r7prep_kernel(x_ref, prev_ref, next_ref, cw_ref, kkw_ref, ka_ref, rk_ref, w0_ref, w2_ref, a0_ref, a2_ref,
                   g2_ref, bd_ref,
                   r_ref, v_ref, kk_ref, lw_ref, beta_ref, kd_ref, g_ref, bonus_ref, pad_ref, *, nt):
    i = pl.program_id(1)
    tt = x_ref.shape[0]
    x = x_ref[...]
    pad_ref[8:8 + tt, :] = x
    pad_ref[7:8, :] = jnp.where(i > 0, prev_ref[7:8, :], 0.0)
    pad_ref[8 + tt:9 + tt, :] = jnp.where(i < nt - 1, next_ref[0:1, :], 0.0)
    xc = pad_ref[7:7 + tt, :] * cw_ref[0:1, :] + x * cw_ref[1:2, :] + pad_ref[9:9 + tt, :] * cw_ref[2:3, :]
    r = xc[:, 0:512]
    k = xc[:, 512:1024]
    v = xc[:, 1024:1536]
    wd = xc[:, 1536:1664]
    ad = xc[:, 1664:1792]
    gd = xc[:, 1792:1920]
    bd = bd_ref[...]
    kkh = k * kkw_ref[...]
    kk = kkh / jnp.maximum(jnp.sqrt(_segsum(kkh * kkh, bd)), 1e-12)
    zw = w0_ref[...] + _dot3(jnp.tanh(wd), w2_ref[...])
    za = a0_ref[...] + _dot3(ad, a2_ref[...])
    a = jax.nn.sigmoid(za)
    ka = ka_ref[...]
    kd0 = k * (1.0 + (a[:, 0:512] - 1.0) * ka)
    kd1 = k * (1.0 + (a[:, 512:1024] - 1.0) * ka)
    r_ref[...] = r
    v_ref[...] = v
    kk_ref[...] = kk
    lw_ref[...] = -DECAY_SCALE * jax.nn.sigmoid(zw)
    beta_ref[:, 0:512] = kk * a[:, 0:512]
    beta_ref[:, 512:1024] = kk * a[:, 512:1024]
    kd_ref[:, 0:512] = kd0
    kd_ref[:, 512:1024] = kd1
    g_ref[...] = _dot3(jax.nn.sigmoid(gd), g2_ref[...])
    bonus_ref[...] = _segsum(r * (kd0 + kd1) * rk_ref[...], bd) * v


def _bd2(w):
    z = jnp.zeros_like(w[0])
    return jnp.concatenate([jnp.concatenate([w[0], z], axis=1), jnp.concatenate([z, w[1]], axis=1)], axis=0)


def _r7prep(pc, lp, nb, t):
    tt = 256
    nt = t // tt
    nrow8 = t // 8

    def full(shape):
        return pl.BlockSpec(shape, lambda b, i: (0,) * len(shape))

    def rowspec(w):
        return pl.BlockSpec((tt, w), lambda b, i: (b * nt + i, 0))

    outs = [C_W, C_W, C_W, 2 * C_W, 2 * C_W, 2 * C_W, C_W, C_W]
    return pl.pallas_call(
        functools.partial(_r7prep_kernel, nt=nt),
        grid=(nb, nt),
        in_specs=[rowspec(C_IN),
                  pl.BlockSpec((8, C_IN), lambda b, i: (b * nrow8 + jnp.maximum(i * (tt // 8) - 1, 0), 0)),
                  pl.BlockSpec((8, C_IN), lambda b, i: (b * nrow8 + jnp.minimum((i + 1) * (tt // 8), nrow8 - 1), 0)),
                  full((3, C_IN)), full((1, C_W)), full((1, C_W)), full((1, C_W)),
                  full((1, 2 * C_W)), full((128, 2 * C_W)), full((1, 2 * C_W)), full((128, 2 * C_W)),
                  full((128, C_W)), full((SEG_W, SEG_W))],
        out_specs=[rowspec(w) for w in outs],
        out_shape=[jax.ShapeDtypeStruct((nb * t, w), F32) for w in outs],
        scratch_shapes=[pltpu.VMEM((tt + 16, C_IN), F32)],
        compiler_params=_cp("parallel", "parallel"),
        name="r7prep",
    )(pc, pc, pc, lp['r7_conv'], lp['r7_kk'].reshape(1, C_W), lp['r7_ka'].reshape(1, C_W),
      lp['r7_rk'].reshape(1, C_W), lp['r7_w0'].reshape(1, 2 * C_W), _bd2(lp['r7_w2']),
      lp['r7_a0'].reshape(1, 2 * C_W), _bd2(lp['r7_a2']), lp['r7_g2'], _block_diag_ones())


def _cumsum_rows(tri, x):
    t = tri.astype(BF16)
    hi = x.astype(BF16)
    rest = x - hi.astype(F32)
    mid = rest.astype(BF16)
    lo = (rest - mid.astype(F32)).astype(BF16)
    return _dot(t, hi) + (_dot(t, mid) + _dot(t, lo))


def _mm(a, b, prec):
    if prec is None:
        return _dot(a.astype(BF16), b.astype(BF16))
    return _dot(a, b, prec)


def _mm_nt(a, b, prec):
    if prec is None:
        return _dot_nt(a.astype(BF16), b.astype(BF16))
    return _dot_nt(a, b, prec)


def _scan_kernel(rf_ref, vf_ref, kkf_ref, lwf_ref, betaf_ref, kdf_ref,
                 rb_ref, vb_ref, kkb_ref, lwb_ref, betab_ref, kdb_ref, s0_ref,
                 yf_ref, yb_ref, sf_ref, st_ref, *, nc):
    c = pl.program_id(1)
    cl = SCAN_C

    @pl.when(c == 0)
    def _():
        st_ref[...] = s0_ref[...]

    ti = lax.broadcasted_iota(jnp.int32, (cl, cl), 0)
    si = lax.broadcasted_iota(jnp.int32, (cl, cl), 1)
    eye = (ti == si).astype(F32)
    incl = [si <= ti, si >= ti]
    strict = [si < ti, si > ti]

    def hs(x, h):
        return x[:, h * HD:(h + 1) * HD]

    pre = {}
    for bb in range(SCAN_NB):
        for d, (r_ref, v_ref, kk_ref, lw_ref, beta_ref, kd_ref) in enumerate(
                [(rf_ref, vf_ref, kkf_ref, lwf_ref, betaf_ref, kdf_ref),
                 (rb_ref, vb_ref, kkb_ref, lwb_ref, betab_ref, kdb_ref)]):
            lw = lw_ref[bb]
            cum = _cumsum_rows(incl[d], lw)
            e_neg = jnp.exp(-cum)
            pre[bb, d] = dict(r_hat=r_ref[bb] * jnp.exp(cum), a_hat=-kk_ref[bb] * jnp.exp(cum - lw),
                              b_til=beta_ref[bb] * e_neg, k_til=kd_ref[bb] * e_neg,
                              wtot=jnp.exp(jnp.sum(lw, axis=0, keepdims=True)), vv=v_ref[bb])

    chains = [(bb, d, h) for bb in range(SCAN_NB) for d in range(2) for h in range(C_HEADS)]
    n_ch = range(len(chains))

    def part(name, i):
        bb, d, h = chains[i]
        return hs(pre[bb, d][name], h)

    def dirn(i):
        return chains[i][1]

    ar = [jnp.concatenate([part('a_hat', i), part('r_hat', i)], axis=0) for i in n_ch]
    bk = [jnp.concatenate([part('b_til', i), part('k_til', i)], axis=0) for i in n_ch]
    m = [_mm_nt(ar[i], bk[i], PREC_M) for i in n_ch]
    l_k = [jnp.where(strict[dirn(i)], m[i][0:cl, cl:2 * cl], 0.0) for i in n_ch]
    r_b = [jnp.where(incl[dirn(i)], m[i][cl:2 * cl, 0:cl], 0.0) for i in n_ch]
    r_k = [jnp.where(incl[dirn(i)], m[i][cl:2 * cl, cl:2 * cl], 0.0) for i in n_ch]
    p = [jnp.where(strict[dirn(i)], m[i][0:cl, 0:cl], 0.0) for i in n_ch]
    t_inv = [eye + p[i] for i in n_ch]
    n = 2
    while n < cl:
        p = [_mm(p[i], p[i], PREC_INV) for i in n_ch]
        t_inv = [t_inv[i] + _mm(t_inv[i], p[i], PREC_INV) for i in n_ch]
        n *= 2
    x1 = [_mm(l_k[i], part('vv', i), PREC_M) for i in n_ch]
    y0 = [_mm(r_k[i], part('vv', i), PREC_M) for i in n_ch]
    ua = [_mm(t_inv[i], jnp.concatenate([x1[i], part('a_hat', i)], axis=1), PREC_INV) for i in n_ch]
    s0 = [st_ref[d, bb, h] for bb, d, h in chains]
    as0 = [_mm_nt(jnp.concatenate([ua[i][:, HD:2 * HD], part('r_hat', i)], axis=0), s0[i], PREC_SEQ)
           for i in n_ch]
    u = [ua[i][:, 0:HD] + as0[i][0:cl] for i in n_ch]
    y = [y0[i] + as0[i][cl:2 * cl] + _mm(r_b[i], u[i], PREC_SEQ) for i in n_ch]
    for i in n_ch:
        bb, d, h = chains[i]
        (yf_ref, yb_ref)[d][bb, :, h * HD:(h + 1) * HD] = y[i]
    for i in n_ch:
        bb, d, h = chains[i]
        uv_t = jnp.concatenate([u[i], part('vv', i)], axis=0).T
        st_ref[d, bb, h] = (s0[i] + _mm(uv_t, bk[i], PREC_SEQ)) * part('wtot', i)

    @pl.when(c == nc - 1)
    def _():
        sf_ref[...] = st_ref[...]


def _scan(prep, s0, nb, t):
    nc = t // SCAN_C
    assert nb % SCAN_NB == 0
    r, v, kk, lw, beta, kd = [a.reshape(nb, t, a.shape[-1]) for a in prep]

    def chunk(d, c):
        return nc - 1 - c if d else c

    def specs(d):
        shared = pl.BlockSpec((SCAN_NB, SCAN_C, C_W), lambda b, c: (b, chunk(d, c), 0))
        perdir = pl.BlockSpec((SCAN_NB, SCAN_C, C_W), lambda b, c: (b, chunk(d, c), d))
        return [shared, shared, shared, perdir, perdir, perdir]

    st_spec = pl.BlockSpec((2, SCAN_NB, C_HEADS, HD, HD), lambda b, c: (0, b, 0, 0, 0))
    yf, yb, sf = pl.pallas_call(
        functools.partial(_scan_kernel, nc=nc),
        grid=(nb // SCAN_NB, nc),
        in_specs=specs(0) + specs(1) + [st_spec],
        out_specs=[pl.BlockSpec((SCAN_NB, SCAN_C, C_W), lambda b, c: (b, chunk(0, c), 0)),
                   pl.BlockSpec((SCAN_NB, SCAN_C, C_W), lambda b, c: (b, chunk(1, c), 0)), st_spec],
        out_shape=[jax.ShapeDtypeStruct((nb, t, C_W), F32), jax.ShapeDtypeStruct((nb, t, C_W), F32),
                   jax.ShapeDtypeStruct((2, nb, C_HEADS, HD, HD), F32)],
        scratch_shapes=[pltpu.VMEM((2, SCAN_NB, C_HEADS, HD, HD), F32)],
        compiler_params=_cp("parallel", "arbitrary"),
        name="r7scan",
    )(r, v, kk, lw, beta, kd, r, v, kk, lw, beta, kd, s0)
    return (yf.reshape(nb * t, C_W), yb.reshape(nb * t, C_W)), sf


def _outproj_kernel(x_ref, oa_ref, ob_ref, y0_ref, y1_ref, bonus_ref, g_ref, lnw_ref, lnb_ref, bd_ref, w_ref,
                    gate_ref, o_ref):
    bd = bd_ref[...]
    y = y0_ref[...] + y1_ref[...]
    mu = _segsum(y, bd) * (1.0 / HD)
    yc = y - mu
    var = _segsum(yc * yc, bd) * (1.0 / HD)
    yn = yc * lax.rsqrt(var + GN_EPS) * lnw_ref[...] + lnb_ref[...]
    oc = ((yn + bonus_ref[...]) * g_ref[...]).astype(BF16)
    acc = (_dot(oa_ref[...], w_ref[0:256, :]) + _dot(ob_ref[...], w_ref[256:512, :])
           + _dot(oc, w_ref[512:1024, :]))
    o_ref[...] = x_ref[...] + gate_ref[...] * acc


def _outproj(x2, oa, ob, y, bonus, g, lp, w_out_bf, mod3, mod_row_of_batch, nb, t):
    tm = 256
    nt = t // tm

    def rowspec(w):
        return pl.BlockSpec((tm, w), lambda b, i: (b * nt + i, 0))

    def full(shape):
        return pl.BlockSpec(shape, lambda b, i: (0,) * len(shape))

    return pl.pallas_call(
        _outproj_kernel,
        grid=(nb, nt),
        in_specs=[rowspec(D), rowspec(256), rowspec(256),
                  rowspec(C_W), rowspec(C_W),
                  rowspec(C_W), rowspec(C_W), full((1, C_W)), full((1, C_W)), full((SEG_W, SEG_W)), full((D, D)),
                  pl.BlockSpec((None, 1, D), lambda b, i: (mod_row_of_batch(b), 0, 2))],
        out_specs=rowspec(D),
        out_shape=jax.ShapeDtypeStruct((nb * t, D), F32),
        compiler_params=_cp("parallel", "parallel"),
        name="outproj",
    )(x2, oa, ob, y[0], y[1], bonus, g, lp['r7_lnw'].reshape(1, C_W), lp['r7_lnb'].reshape(1, C_W),
      _block_diag_ones(), w_out_bf, mod3)


def _peer_score_kernel(x_ref, g_ref, sc_ref, sh_ref, wq_ref, keys_ref, h_ref, st_ref):
    x = x_ref[...]
    y = x * lax.rsqrt(jnp.mean(x * x, axis=-1, keepdims=True) + EPS) * g_ref[...]
    h = (y * (1.0 + sc_ref[...]) + sh_ref[...]).astype(BF16)
    h_ref[...] = h
    q = _dot(h, wq_ref[...])
    half = P_QDIM // 2
    for hp in range(2 * P_HEADS):
        st_ref[hp] = _dot3(keys_ref[hp], q[:, hp * half:(hp + 1) * half], nt=True)


def _peer_scores(x2, norm_g, mod3, mod_row_of_batch, wq_bf, keys, nb, t):
    tm = 256
    nt = t // tm
    n = nb * t
    return pl.pallas_call(
        _peer_score_kernel,
        grid=(nb, nt),
        in_specs=[pl.BlockSpec((tm, D), lambda b, i: (b * nt + i, 0)),
                  pl.BlockSpec((1, D), lambda b, i: (0, 0)),
                  pl.BlockSpec((None, 1, D), lambda b, i: (mod_row_of_batch(b), 0, 4)),
                  pl.BlockSpec((None, 1, D), lambda b, i: (mod_row_of_batch(b), 0, 3)),
                  pl.BlockSpec((D, P_HEADS * P_QDIM), lambda b, i: (0, 0)),
                  pl.BlockSpec((2 * P_HEADS, P_NKEYS, P_QDIM // 2), lambda b, i: (0, 0, 0))],
        out_specs=[pl.BlockSpec((tm, D), lambda b, i: (b * nt + i, 0)),
                   pl.BlockSpec((2 * P_HEADS, P_NKEYS, tm), lambda b, i: (0, 0, b * nt + i))],
        out_shape=[jax.ShapeDtypeStruct((n, D), BF16),
                   jax.ShapeDtypeStruct((2 * P_HEADS, P_NKEYS, n), F32)],
        compiler_params=_cp("parallel", "parallel"),
        name="peer_scores",
    )(x2, norm_g.reshape(1, D), mod3, mod3, wq_bf, keys.reshape(2 * P_HEADS, P_NKEYS, P_QDIM // 2))


def _batcher_pairs(n):
    pairs = []
    p = 1
    while p < n:
        k = p
        while k >= 1:
            for j in range(k % p, n - k, 2 * k):
                for i in range(min(k, n - j - k)):
                    if (i + j) // (2 * p) == (i + j + k) // (2 * p):
                        pairs.append((i + j, i + j + k))
            k //= 2
        p *= 2
    return pairs


_SORT16 = _batcher_pairs(P_TOPK)
_BITONIC16 = [(i, i + d) for d in (8, 4, 2, 1) for i in range(P_TOPK) if i & d == 0]


def _compare_exchange(x, pairs):
    for i, j in pairs:
        x[i], x[j] = jnp.maximum(x[i], x[j]), jnp.minimum(x[i], x[j])
    return x


def _top16_sorted(slabs):
    x = _compare_exchange(list(slabs), _SORT16)
    for shift in (4, 2, 1):
        y = [pltpu.roll(x[P_TOPK - 1 - k], shift, 0) for k in range(P_TOPK)]
        x = [jnp.maximum(x[k], y[k]) for k in range(P_TOPK)]
        x = _compare_exchange(x, _BITONIC16)
    return x


def _peer_gate_tables(st_ref, rk_ref, be_ref, cnt_ref, al_ref, top_ref):
    nblk = st_ref.shape[-1] // LANES
    sub = lax.broadcasted_iota(jnp.int32, (8, LANES), 0)
    ninf = jnp.full((8, LANES), -jnp.inf, F32)

    def block(it, carry):
        h = it // nblk
        lanes = pl.ds(pl.multiple_of((it % nblk) * LANES, LANES), LANES)
        for p in range(2):
            top = _top16_sorted([st_ref[2 * h + p, 8 * k:8 * k + 8, lanes] for k in range(P_TOPK)])
            for k in range(P_TOPK):
                top_ref[p * P_TOPK + k:p * P_TOPK + k + 1, lanes] = top[k][0:1]
        a16 = top_ref[0:P_TOPK, lanes]
        b16 = top_ref[P_TOPK:2 * P_TOPK, lanes]
        b8 = b16[0:8]
        cand = [a16[0:1] + b8, a16[0:1] + b16[8:16]]
        for p in range(2, 9):
            cand.append(jnp.where(sub < P_TOPK // p, a16[p - 1:p] + b8, -jnp.inf))
        cand.append(a16[8:16] + b16[0:1])
        best = _top16_sorted(cand + [ninf] * (P_TOPK - len(cand)))
        tau = best[P_TOPK - 1][0:1]
        z = jnp.ones_like(tau)
        for k in range(1, P_TOPK):
            z = z + jnp.exp(best[k][0:1] - best[0][0:1])
        s1 = st_ref[2 * h, :, lanes]
        s2 = st_ref[2 * h + 1, :, lanes]
        cnt = jnp.zeros_like(s1)
        rk = jnp.ones_like(s2)
        for q in range(P_TOPK):
            bq = b16[q:q + 1]
            theta = jnp.min(jnp.where(a16 + bq >= tau, a16, jnp.inf), axis=0, keepdims=True)
            cnt = jnp.where(s1 >= theta, q + 1.0, cnt)
            rk = jnp.where(bq > s2, q + 2.0, rk)
        cnt_ref[h, :, lanes] = cnt
        rk_ref[h, :, lanes] = rk.astype(BF16)
        al_ref[h, :, lanes] = jnp.exp(s1 - a16[0:1]) / z
        be_ref[h, :, lanes] = jnp.exp(s2 - b16[0:1]).astype(BF16)
        return carry

    lax.fori_loop(0, P_HEADS * nblk, block, 0)


def _peer_chunk(chunk, act_ref, rk_ref, be_ref, cnt_ref, al_ref, w_ref):
    nrow = w_ref.shape[0] // P_NKEYS
    for ii in range(nrow):
        i = chunk * nrow + ii
        wrow = None
        for h in range(P_HEADS):
            cnt = cnt_ref[h, pl.ds(i, 1), :].astype(BF16)
            al = al_ref[h, pl.ds(i, 1), :].astype(BF16)
            term = jnp.where(rk_ref[h] <= cnt, al * be_ref[h], 0.0)
            wrow = term if wrow is None else wrow + term
        rows = slice(ii * P_NKEYS, (ii + 1) * P_NKEYS)
        act = act_ref[rows, :]
        gl = 0.5 * act * (1.0 + lax.erf(act * (2.0 ** -0.5)))
        w_ref[rows, :] = wrow * gl.astype(BF16)


def _peer_dense_kernel(x_ref, h_ref, st_ref, u_ref, vt_ref, gate_ref, o_ref,
                       rk_ref, be_ref, cnt_ref, al_ref, top_ref, act0_ref, act1_ref, w_ref, acc_ref, *, ne):
    e = pl.program_id(1)

    @pl.when(e == 0)
    def _():
        act0_ref[...] = _dot_nt(u_ref[...], h_ref[...])
        _peer_gate_tables(st_ref, rk_ref, be_ref, cnt_ref, al_ref, top_ref)
        acc_ref[...] = jnp.zeros_like(acc_ref)

    def step(cur_ref, nxt_ref):
        nxt_ref[...] = _dot_nt(u_ref[...], h_ref[...])
        _peer_chunk(e - 1, cur_ref, rk_ref, be_ref, cnt_ref, al_ref, w_ref)
        acc_ref[...] += _dot(vt_ref[...], w_ref[...])

    @pl.when(e % 2 == 1)
    def _():
        step(act0_ref, act1_ref)

    @pl.when((e > 0) & (e % 2 == 0))
    def _():
        step(act1_ref, act0_ref)

    @pl.when(e == ne)
    def _():
        o_ref[...] = x_ref[...] + gate_ref[...] * acc_ref[...].T


def _peer_dense(x2, h2, st, u_bf, vt_bf, mod3, mod_row_of_batch, nb, t):
    tm = 512 if t % 512 == 0 else 256
    ec = PEER_EC
    nt = t // tm
    ne = P_EXPERTS // ec
    return pl.pallas_call(
        functools.partial(_peer_dense_kernel, ne=ne),
        grid=(nb * nt, ne + 1),
        in_specs=[pl.BlockSpec((tm, D), lambda i, e: (i, 0)),
                  pl.BlockSpec((tm, D), lambda i, e: (i, 0)),
                  pl.BlockSpec((2 * P_HEADS, P_NKEYS, tm), lambda i, e: (0, 0, i)),
                  pl.BlockSpec((ec, D), lambda i, e: (jnp.minimum(e, ne - 1), 0)),
                  pl.BlockSpec((None, D, ec), lambda i, e: (jnp.maximum(e - 1, 0), 0, 0)),
                  pl.BlockSpec((None, 1, D), lambda i, e: (mod_row_of_batch(i // nt), 0, 5))],
        out_specs=pl.BlockSpec((tm, D), lambda i, e: (i, 0)),
        out_shape=jax.ShapeDtypeStruct((nb * t, D), F32),
        scratch_shapes=[pltpu.VMEM((P_HEADS, P_NKEYS, tm), BF16),
                        pltpu.VMEM((P_HEADS, P_NKEYS, tm), BF16),
                        pltpu.VMEM((P_HEADS, P_NKEYS, tm), F32),
                        pltpu.VMEM((P_HEADS, P_NKEYS, tm), F32),
                        pltpu.VMEM((2 * P_TOPK, tm), F32),
                        pltpu.VMEM((ec, tm), F32),
                        pltpu.VMEM((ec, tm), F32),
                        pltpu.VMEM((ec, tm), BF16),
                        pltpu.VMEM((D, tm), F32)],
        compiler_params=pltpu.CompilerParams(dimension_semantics=("parallel", "arbitrary"),
                                             vmem_limit_bytes=PEER_VMEM_LIMIT),
        name="peer_dense",
    )(x2, h2, st, u_bf, vt_bf, mod3)


def _peer(x2, norm_g, mod3, mod_row_of_batch, wq_bf, keys, u_bf, vt_bf, nb, t):
    h2, st = _peer_scores(x2, norm_g, mod3, mod_row_of_batch, wq_bf, keys, nb, t)
    return _peer_dense(x2, h2, st, u_bf, vt_bf, mod3, mod_row_of_batch, nb, t)


def kernel(x, c, ctx, c_ctx, norm_mix, norm_ffn, w_mod, b_mod, w_in, w_out, a_qnorm, a_knorm, a_sink, b_qnorm,
           b_knorm, b_rpb, r7_conv, r7_w0, r7_w2, r7_a0, r7_a2, r7_g2, r7_kk, r7_ka, r7_rk, r7_lnw, r7_lnb,
           peer_wq, peer_keys, peer_u, peer_v):
    nb, s, _ = x.shape
    lc = ctx.shape[1]
    depth = w_in.shape[0]
    assert nb < 16 and nb % SCAN_NB == 0 and s % 512 == 0 and lc % 256 == 0
    rows = s // GRID_W

    cc = jnp.zeros((16, D), F32).at[:nb].set(c).at[nb].set(c_ctx)
    mod = _modulation(cc, w_mod, b_mod)
    rope_tabs = _rope_tables(s)
    lat_row = lambda b: b
    ctx_row = lambda b: nb

    x_lat = x.reshape(nb * s, D)
    x_ctx = ctx.reshape(nb * lc, D)
    scale = HD ** -0.5
    for l in range(depth):
        with_ctx = l < depth - 1
        mod3 = mod[l].reshape(16, 1, 6 * D)
        lp = {'r7_conv': r7_conv[l], 'r7_w0': r7_w0[l], 'r7_w2': r7_w2[l], 'r7_a0': r7_a0[l], 'r7_a2': r7_a2[l],
              'r7_g2': r7_g2[l], 'r7_kk': r7_kk[l], 'r7_ka': r7_ka[l], 'r7_rk': r7_rk[l], 'r7_lnw': r7_lnw[l],
              'r7_lnb': r7_lnb[l]}
        w_in_bf = w_in[l].astype(BF16)
        w_out_bf = w_out[l].astype(BF16)
        gain_a = jnp.concatenate([jnp.tile(a_qnorm[l] * scale, A_HEADS), jnp.tile(a_knorm[l], A_KV)]).reshape(1, 384)
        gain_b = jnp.concatenate([jnp.tile(b_qnorm[l] * scale, B_HEADS), jnp.tile(b_knorm[l], B_HEADS)]).reshape(1, 512)

        ab_lat, pc_lat = _inproj(x_lat, mod3, lat_row, norm_mix[l], w_in_bf, gain_a, gain_b, rope_tabs, nb, s)
        ab_ctx, pc_ctx = _inproj(x_ctx, mod3, ctx_row, norm_mix[l], w_in_bf, gain_a, gain_b, None, nb, lc)

        o_a = _attn_a(ab_lat, ab_ctx, a_sink[l], nb, s, lc)
        o_b = _attn_b(ab_lat, ab_ctx, _na_bias_tables(b_rpb[l], rows), nb, s, lc)

        prep_ctx = _r7prep(pc_ctx, lp, nb, lc)
        prep_lat = _r7prep(pc_lat, lp, nb, s)
        zero_state = jnp.zeros((2, nb, C_HEADS, HD, HD), F32)
        y_ctx, s_ctx = _scan(prep_ctx[:6], zero_state, nb, lc)
        y_lat, _ = _scan(prep_lat[:6], s_ctx, nb, s)

        x_lat = _outproj(x_lat, o_a, o_b, y_lat, prep_lat[7], prep_lat[6], lp, w_out_bf, mod3, lat_row, nb, s)
        wq_bf = peer_wq[l].astype(BF16)
        u_bf = peer_u[l].astype(BF16)
        vt_bf = peer_v[l].astype(BF16).reshape(P_EXPERTS // PEER_EC, PEER_EC, D).transpose(0, 2, 1)
        x_lat = _peer(x_lat, norm_ffn[l], mod3, lat_row, wq_bf, peer_keys[l], u_bf, vt_bf, nb, s)
        if with_ctx:
            o_ac, o_bc = _attn_ctx(ab_ctx, a_sink[l], nb, lc)
            x_ctx = _outproj(x_ctx, o_ac, o_bc, y_ctx, prep_ctx[7], prep_ctx[6], lp, w_out_bf, mod3, ctx_row, nb, lc)
            x_ctx = _peer(x_ctx, norm_ffn[l], mod3, ctx_row, wq_bf, peer_keys[l], u_bf, vt_bf, nb, lc)
    return x_lat.reshape(nb, s, D)
```

```python
import functools

import numpy as np
import jax
import jax.numpy as jnp
from jax import lax
from jax.experimental import pallas as pl
from jax.experimental.pallas import tpu as pltpu

F32 = jnp.float32
BF16 = jnp.bfloat16
HI = lax.Precision.HIGHEST

D = 1024
DEPTH = 2
GRID_W = 64
HD = 64
LANES = 128
EPS = 1e-6
NEG = -1e30
A_HEADS, A_KV, A_BLOCK, A_WINDOW = 4, 2, 128, 128
A_STEP_BLOCKS = 2
B_HEADS, NA_ROWS, NA_COLS = 4, 8, 16
NA_STEP_ROWS = 4
C_HEADS = 8
C_W = 512
C_IN = 1920
AB_W = 1280
IN_W = AB_W + C_IN
GN_EPS = 64e-5
DECAY_SCALE = 0.6065306597126334
ROPE_BASE = 10000.0
P_HEADS, P_NKEYS, P_QDIM, P_TOPK = 8, 128, 256, 16
P_EXPERTS = P_NKEYS * P_NKEYS
SCAN_C = 64
SCAN_NB = 2
PREC_M = PREC_INV = PREC_SEQ = None
VMEM_LIMIT = 48 * 1024 * 1024
PEER_VMEM_LIMIT = 56 * 1024 * 1024
PEER_EC = 1024


def _cp(*sem):
    return pltpu.CompilerParams(dimension_semantics=sem, vmem_limit_bytes=VMEM_LIMIT)


def _dot(a, b, prec=None):
    return jnp.dot(a, b, precision=prec, preferred_element_type=F32)


def _dot_nt(a, b, prec=None):
    return lax.dot_general(a, b, (((1,), (1,)), ((), ())), precision=prec, preferred_element_type=F32)


SEG_W = 256


def _block_diag_ones():
    i = np.arange(SEG_W) // HD
    return jnp.asarray((i[:, None] == i[None, :]).astype(np.float32), BF16)


def _split2(x):
    hi = x.astype(BF16)
    return hi, (x - hi.astype(F32)).astype(BF16)


def _segsum(x, bd):
    hi, lo = _split2(x)
    outs = []
    for g0 in range(0, x.shape[1], SEG_W):
        w = min(SEG_W, x.shape[1] - g0)
        outs.append(_dot(hi[:, g0:g0 + w], bd[0:w, 0:w]) + _dot(lo[:, g0:g0 + w], bd[0:w, 0:w]))
    return outs[0] if len(outs) == 1 else jnp.concatenate(outs, axis=1)


def _dot3(a, b, nt=False):
    f = _dot_nt if nt else _dot
    a_hi, a_lo = _split2(a)
    b_hi, b_lo = _split2(b)
    return f(a_hi, b_hi) + (f(a_lo, b_hi) + f(a_hi, b_lo))


def _mod_kernel(c_ref, w_ref, b_ref, o_ref):
    c = c_ref[...]
    s = c * jax.nn.sigmoid(c)
    o_ref[...] = _dot(s, w_ref[...], HI) + b_ref[...]


def _modulation(cc, w_mod, b_mod):
    L, _, n = w_mod.shape
    tn = 512
    return pl.pallas_call(
        _mod_kernel,
        grid=(L, n // tn),
        in_specs=[pl.BlockSpec((16, D), lambda l, j: (0, 0)),
                  pl.BlockSpec((None, D, tn), lambda l, j: (l, 0, j)),
                  pl.BlockSpec((None, 1, tn), lambda l, j: (l, 0, j))],
        out_specs=pl.BlockSpec((None, 16, tn), lambda l, j: (l, 0, j)),
        out_shape=jax.ShapeDtypeStruct((L, 16, n), F32),
        compiler_params=_cp("parallel", "parallel"),
        name="modulation",
    )(cc, w_mod, b_mod.reshape(L, 1, n))


def _swap16(x):
    n = x.shape[-1]
    lane = lax.broadcasted_iota(jnp.int32, x.shape, 1)
    fwd = pltpu.roll(x, n - 16, 1)
    bwd = pltpu.roll(x, 16, 1)
    return jnp.where((lane % 32) < 16, fwd, bwd)


def _head_rms(x, bd, gain):
    ss = _segsum(x * x, bd)
    return x * lax.rsqrt(ss * (1.0 / HD) + EPS) * gain


def _inproj_kernel(*refs, rope):
    if rope:
        (x_ref, g_ref, sc_ref, sh_ref, w_ref, ga_ref, gb_ref, bd_ref, cos_ref, sin_ref,
         ab_ref, c_ref) = refs
    else:
        (x_ref, g_ref, sc_ref, sh_ref, w_ref, ga_ref, gb_ref, bd_ref, ab_ref, c_ref) = refs
    x = x_ref[...]
    y = x * lax.rsqrt(jnp.mean(x * x, axis=-1, keepdims=True) + EPS) * g_ref[...]
    h = y * (1.0 + sc_ref[...]) + sh_ref[...]
    acc = _dot(h.astype(BF16), w_ref[...])
    bd = bd_ref[...]
    qa = _head_rms(acc[:, 0:384], bd, ga_ref[...])
    if rope:
        qa = qa * cos_ref[...] + _swap16(qa) * sin_ref[...]
    qb = _head_rms(acc[:, 512:1024], bd, gb_ref[...])
    ab_ref[:, 0:384] = qa.astype(BF16)
    ab_ref[:, 384:512] = acc[:, 384:512].astype(BF16)
    ab_ref[:, 512:1024] = qb.astype(BF16)
    ab_ref[:, 1024:1280] = acc[:, 1024:1280].astype(BF16)
    c_ref[...] = acc[:, AB_W:IN_W]


def _inproj(x2, mod3, mod_row_of_batch, norm_g, w_in_bf, gain_a, gain_b, rope_tabs, nb, t):
    tm = 256
    nt = t // tm
    rope = rope_tabs is not None
    in_specs = [
        pl.BlockSpec((tm, D), lambda b, i: (b * nt + i, 0)),
        pl.BlockSpec((1, D), lambda b, i: (0, 0)),
        pl.BlockSpec((None, 1, D), lambda b, i: (mod_row_of_batch(b), 0, 1)),
        pl.BlockSpec((None, 1, D), lambda b, i: (mod_row_of_batch(b), 0, 0)),
        pl.BlockSpec((D, IN_W), lambda b, i: (0, 0)),
        pl.BlockSpec((1, 384), lambda b, i: (0, 0)),
        pl.BlockSpec((1, 512), lambda b, i: (0, 0)),
        pl.BlockSpec((SEG_W, SEG_W), lambda b, i: (0, 0)),
    ]
    args = [x2, norm_g.reshape(1, D), mod3, mod3, w_in_bf, gain_a, gain_b,
            _block_diag_ones()]
    if rope:
        in_specs += [pl.BlockSpec((tm, 384), lambda b, i: (i, 0)),
                     pl.BlockSpec((tm, 384), lambda b, i: (i, 0))]
        args += list(rope_tabs)
    return pl.pallas_call(
        functools.partial(_inproj_kernel, rope=rope),
        grid=(nb, nt),
        in_specs=in_specs,
        out_specs=[pl.BlockSpec((tm, AB_W), lambda b, i: (b * nt + i, 0)),
                   pl.BlockSpec((tm, C_IN), lambda b, i: (b * nt + i, 0))],
        out_shape=[jax.ShapeDtypeStruct((nb * t, AB_W), BF16),
                   jax.ShapeDtypeStruct((nb * t, C_IN), F32)],
        compiler_params=_cp("parallel", "parallel"),
        name="inproj_rope" if rope else "inproj_ctx",
    )(*args)


def _rope_tables(s):
    tok = np.arange(s)
    inv = ROPE_BASE ** (-np.arange(0, 32, 2) / 32.0)
    ar = (tok // GRID_W)[:, None] * inv[None]
    ac = (tok % GRID_W)[:, None] * inv[None]
    cos = np.concatenate([np.cos(ar), np.cos(ar), np.cos(ac), np.cos(ac)], axis=1)
    sin = np.concatenate([-np.sin(ar), np.sin(ar), -np.sin(ac), np.sin(ac)], axis=1)
    return (jnp.asarray(np.tile(cos, (1, 6)), F32), jnp.asarray(np.tile(sin, (1, 6)), F32))


def _softmax_pv(s, v, sink=None):
    m = jnp.max(s, axis=-1, keepdims=True)
    if sink is not None:
        m = jnp.maximum(m, sink)
    p = jnp.exp(s - m)
    den = jnp.sum(p, axis=-1, keepdims=True)
    if sink is not None:
        den = den + jnp.exp(sink - m)
    return _dot(p.astype(BF16), v) / den


def _attn_a_kernel(q_ref, k0_ref, k1_ref, k2_ref, k3_ref, v0_ref, v1_ref, v2_ref, v3_ref, kc_ref, vc_ref,
                   sink_ref, o_ref, *, nblk):
    kb = [k0_ref[...], k1_ref[...], k2_ref[...], k3_ref[...]]
    vb = [v0_ref[...], v1_ref[...], v2_ref[...], v3_ref[...]]
    kc = kc_ref[...]
    vc = vc_ref[...]
    nk = 3 * A_BLOCK + kc.shape[0]
    g = A_HEADS // A_KV
    row = lax.broadcasted_iota(jnp.int32, (g * A_BLOCK, nk), 0) % A_BLOCK
    col = lax.broadcasted_iota(jnp.int32, (g * A_BLOCK, nk), 1)
    rel = col - A_BLOCK - row
    band = (jnp.abs(rel) <= A_WINDOW) | (col >= 3 * A_BLOCK)
    chains, ok, k_all, v_all = [], [], [], []
    for j in range(A_STEP_BLOCKS):
        n = pl.program_id(1) * A_STEP_BLOCKS + j
        ok.append(band & ((n > 0) | (col >= A_BLOCK)) & ((n < nblk - 1) | (col < 2 * A_BLOCK) | (col >= 3 * A_BLOCK)))
        k_all.append(jnp.concatenate(kb[j:j + 3] + [kc], axis=0))
        v_all.append(jnp.concatenate(vb[j:j + 3] + [vc], axis=0))
        chains += [(j, hk) for hk in range(A_KV)]

    def qg(j, hk):
        rows = slice(j * A_BLOCK, (j + 1) * A_BLOCK)
        return jnp.concatenate([q_ref[rows, (hk * g + i) * HD:(hk * g + i + 1) * HD] for i in range(g)], axis=0)

    sink = [jnp.concatenate([jnp.broadcast_to(sink_ref[:, hk * g + i:hk * g + i + 1], (A_BLOCK, 1))
                             for i in range(g)], axis=0) for hk in range(A_KV)]
    s = [jnp.where(ok[j], _dot_nt(qg(j, hk), k_all[j][:, hk * HD:(hk + 1) * HD]), NEG) for j, hk in chains]
    m = [jnp.maximum(jnp.max(s[i], axis=-1, keepdims=True), sink[hk]) for i, (j, hk) in enumerate(chains)]
    p = [jnp.exp(s[i] - m[i]) for i in range(len(chains))]
    den = [jnp.sum(p[i], axis=-1, keepdims=True) + jnp.exp(sink[hk] - m[i]) for i, (j, hk) in enumerate(chains)]
    o = [_dot(p[i].astype(BF16), v_all[j][:, hk * HD:(hk + 1) * HD]) / den[i] for i, (j, hk) in enumerate(chains)]
    for i, (j, hk) in enumerate(chains):
        for gi in range(g):
            hq = hk * g + gi
            o_ref[j * A_BLOCK:(j + 1) * A_BLOCK, hq * HD:(hq + 1) * HD] = (
                o[i][gi * A_BLOCK:(gi + 1) * A_BLOCK].astype(BF16))


def _attn_a(ab_lat, ab_ctx, sink, nb, s, lc):
    nblk = s // A_BLOCK
    steps = nblk // A_STEP_BLOCKS

    def kv(col, d):
        return pl.BlockSpec(
            (A_BLOCK, 128), lambda b, i: (b * nblk + jnp.clip(i * A_STEP_BLOCKS + d, 0, nblk - 1), col))

    tq = A_STEP_BLOCKS * A_BLOCK
    return pl.pallas_call(
        functools.partial(_attn_a_kernel, nblk=nblk),
        grid=(nb, steps),
        in_specs=[pl.BlockSpec((tq, 256), lambda b, i: (b * steps + i, 0)),
                  kv(2, -1), kv(2, 0), kv(2, 1), kv(2, 2), kv(3, -1), kv(3, 0), kv(3, 1), kv(3, 2),
                  pl.BlockSpec((lc, 128), lambda b, i: (b, 2)),
                  pl.BlockSpec((lc, 128), lambda b, i: (b, 3)),
                  pl.BlockSpec((1, A_HEADS), lambda b, i: (0, 0))],
        out_specs=pl.BlockSpec((tq, 256), lambda b, i: (b * steps + i, 0)),
        out_shape=jax.ShapeDtypeStruct((nb * s, 256), BF16),
        compiler_params=_cp("parallel", "parallel"),
        name="attn_a",
    )(ab_lat, *([ab_lat] * 8), ab_ctx, ab_ctx, sink.reshape(1, A_HEADS))


def _attn_b_kernel(q_ref, k_ref, v_ref, kc_ref, vc_ref, bias_ref, o_ref, *, rows):
    nwin = NA_ROWS * GRID_W
    kc = kc_ref[...]
    vc = vc_ref[...]
    q, kw, vw, tab = [], [], [], []
    for rr in range(NA_STEP_ROWS):
        r = pl.program_id(1) * NA_STEP_ROWS + rr
        rs = jnp.clip(r - NA_ROWS // 2, 0, rows - NA_ROWS)
        start = pl.multiple_of(rs * GRID_W, GRID_W)
        tab.append(rs - r + NA_ROWS - 1)
        q.append(q_ref[rr * GRID_W:(rr + 1) * GRID_W, :])
        kw.append(k_ref[pl.ds(start, nwin), :])
        vw.append(v_ref[pl.ds(start, nwin), :])
    chains = [(rr, h) for rr in range(NA_STEP_ROWS) for h in range(B_HEADS)]

    def hs(x, h):
        return x[:, h * HD:(h + 1) * HD]

    s_loc = [_dot_nt(hs(q[rr], h), hs(kw[rr], h)) + bias_ref[tab[rr], h] for rr, h in chains]
    s_ctx = [_dot_nt(hs(q[rr], h), hs(kc, h)) for rr, h in chains]
    m = [jnp.maximum(jnp.max(a, axis=-1, keepdims=True), jnp.max(b, axis=-1, keepdims=True))
         for a, b in zip(s_loc, s_ctx)]
    p_loc = [jnp.exp(a - mm) for a, mm in zip(s_loc, m)]
    p_ctx = [jnp.exp(b - mm) for b, mm in zip(s_ctx, m)]
    den = [jnp.sum(a, axis=-1, keepdims=True) + jnp.sum(b, axis=-1, keepdims=True) for a, b in zip(p_loc, p_ctx)]
    o = [(_dot(p_loc[i].astype(BF16), hs(vw[rr], h)) + _dot(p_ctx[i].astype(BF16), hs(vc, h))) / den[i]
         for i, (rr, h) in enumerate(chains)]
    for i, (rr, h) in enumerate(chains):
        o_ref[rr * GRID_W:(rr + 1) * GRID_W, h * HD:(h + 1) * HD] = o[i].astype(BF16)


def _na_bias_tables(rpb, rows):
    kr = NA_ROWS
    qc = np.arange(GRID_W)
    win_start = np.clip(qc - NA_COLS // 2, 0, GRID_W - NA_COLS)
    kcol = np.arange(GRID_W)
    valid = (kcol[None, :] >= win_start[:, None]) & (kcol[None, :] < win_start[:, None] + NA_COLS)
    pad = GRID_W - NA_COLS
    rp = jnp.pad(rpb.astype(F32), ((0, 0), (0, 0), (pad, pad)))
    toe = jnp.stack([rp[:, :, GRID_W - 1 - q:2 * GRID_W - 1 - q] for q in range(GRID_W)], axis=2)
    toe = jnp.where(jnp.asarray(valid[None, None]), toe, NEG)
    tabs = [toe[:, off:off + kr].transpose(0, 2, 1, 3).reshape(rpb.shape[0], GRID_W, kr * GRID_W)
            for off in range(kr)]
    return jnp.stack(tabs, axis=0)


def _attn_b(ab_lat, ab_ctx, bias_tabs, nb, s, lc):
    rows = s // GRID_W
    steps = rows // NA_STEP_ROWS
    tq = NA_STEP_ROWS * GRID_W
    return pl.pallas_call(
        functools.partial(_attn_b_kernel, rows=rows),
        grid=(nb, steps),
        in_specs=[pl.BlockSpec((tq, 256), lambda b, r: (b * steps + r, 2)),
                  pl.BlockSpec((s, 256), lambda b, r: (b, 3)),
                  pl.BlockSpec((s, 256), lambda b, r: (b, 4)),
                  pl.BlockSpec((lc, 256), lambda b, r: (b, 3)),
                  pl.BlockSpec((lc, 256), lambda b, r: (b, 4)),
                  pl.BlockSpec((NA_ROWS, B_HEADS, GRID_W, NA_ROWS * GRID_W), lambda b, r: (0, 0, 0, 0))],
        out_specs=pl.BlockSpec((tq, 256), lambda b, r: (b * steps + r, 0)),
        out_shape=jax.ShapeDtypeStruct((nb * s, 256), BF16),
        compiler_params=_cp("parallel", "arbitrary"),
        name="attn_b",
    )(ab_lat, ab_lat, ab_lat, ab_ctx, ab_ctx, bias_tabs)


def _attn_ctx_kernel(ab_ref, sink_ref, oa_ref, ob_ref):
    lc = ab_ref.shape[0]
    g = A_HEADS // A_KV
    for hk in range(A_KV):
        qg = jnp.concatenate([ab_ref[:, (hk * g + j) * HD:(hk * g + j + 1) * HD] for j in range(g)], axis=0)
        s = _dot_nt(qg, ab_ref[:, 256 + hk * HD:256 + (hk + 1) * HD])
        sink = jnp.concatenate(
            [jnp.broadcast_to(sink_ref[:, hk * g + j:hk * g + j + 1], (lc, 1)) for j in range(g)], axis=0)
        o = _softmax_pv(s, ab_ref[:, 384 + hk * HD:384 + (hk + 1) * HD], sink)
        for j in range(g):
            hq = hk * g + j
            oa_ref[:, hq * HD:(hq + 1) * HD] = o[j * lc:(j + 1) * lc].astype(BF16)
    for h in range(B_HEADS):
        s = _dot_nt(ab_ref[:, 512 + h * HD:512 + (h + 1) * HD], ab_ref[:, 768 + h * HD:768 + (h + 1) * HD])
        o = _softmax_pv(s, ab_ref[:, 1024 + h * HD:1024 + (h + 1) * HD])
        ob_ref[:, h * HD:(h + 1) * HD] = o.astype(BF16)


def _attn_ctx(ab_ctx, sink, nb, lc):
    return pl.pallas_call(
        _attn_ctx_kernel,
        grid=(nb,),
        in_specs=[pl.BlockSpec((lc, AB_W), lambda b: (b, 0)),
                  pl.BlockSpec((1, A_HEADS), lambda b: (0, 0))],
        out_specs=[pl.BlockSpec((lc, 256), lambda b: (b, 0)),
                   pl.BlockSpec((lc, 256), lambda b: (b, 0))],
        out_shape=[jax.ShapeDtypeStruct((nb * lc, 256), BF16),
                   jax.ShapeDtypeStruct((nb * lc, 256), BF16)],
        compiler_params=_cp("parallel"),
        name="attn_ctx",
    )(ab_ctx, sink.reshape(1, A_HEADS))


def _r7prep_kernel(x_ref, prev_ref, next_ref, cw_ref, kkw_ref, ka_ref, rk_ref, w0_ref, w2_ref, a0_ref, a2_ref,
                   g2_ref, bd_ref,
                   r_ref, v_ref, kk_ref, lw_ref, beta_ref, kd_ref, g_ref, bonus_ref, pad_ref, *, nt):
    i = pl.program_id(1)
    tt = x_ref.shape[0]
    x = x_ref[...]
    pad_ref[8:8 + tt, :] = x
    pad_ref[7:8, :] = jnp.where(i > 0, prev_ref[7:8, :], 0.0)
    pad_ref[8 + tt:9 + tt, :] = jnp.where(i < nt - 1, next_ref[0:1, :], 0.0)
    xc = pad_ref[7:7 + tt, :] * cw_ref[0:1, :] + x * cw_ref[1:2, :] + pad_ref[9:9 + tt, :] * cw_ref[2:3, :]
    r = xc[:, 0:512]
    k = xc[:, 512:1024]
    v = xc[:, 1024:1536]
    wd = xc[:, 1536:1664]
    ad = xc[:, 1664:1792]
    gd = xc[:, 1792:1920]
    bd = bd_ref[...]
    kkh = k * kkw_ref[...]
    kk = kkh / jnp.maximum(jnp.sqrt(_segsum(kkh * kkh, bd)), 1e-12)
    zw = w0_ref[...] + _dot3(jnp.tanh(wd), w2_ref[...])
    za = a0_ref[...] + _dot3(ad, a2_ref[...])
    a = jax.nn.sigmoid(za)
    ka = ka_ref[...]
    kd0 = k * (1.0 + (a[:, 0:512] - 1.0) * ka)
    kd1 = k * (1.0 + (a[:, 512:1024] - 1.0) * ka)
    r_ref[...] = r
    v_ref[...] = v
    kk_ref[...] = kk
    lw_ref[...] = -DECAY_SCALE * jax.nn.sigmoid(zw)
    beta_ref[:, 0:512] = kk * a[:, 0:512]
    beta_ref[:, 512:1024] = kk * a[:, 512:1024]
    kd_ref[:, 0:512] = kd0
    kd_ref[:, 512:1024] = kd1
    g_ref[...] = _dot3(jax.nn.sigmoid(gd), g2_ref[...])
    bonus_ref[...] = _segsum(r * (kd0 + kd1) * rk_ref[...], bd) * v


def _bd2(w):
    z = jnp.zeros_like(w[0])
    return jnp.concatenate([jnp.concatenate([w[0], z], axis=1), jnp.concatenate([z, w[1]], axis=1)], axis=0)


def _r7prep(pc, lp, nb, t):
    tt = 256
    nt = t // tt
    nrow8 = t // 8

    def full(shape):
        return pl.BlockSpec(shape, lambda b, i: (0,) * len(shape))

    def rowspec(w):
        return pl.BlockSpec((tt, w), lambda b, i: (b * nt + i, 0))

    outs = [C_W, C_W, C_W, 2 * C_W, 2 * C_W, 2 * C_W, C_W, C_W]
    return pl.pallas_call(
        functools.partial(_r7prep_kernel, nt=nt),
        grid=(nb, nt),
        in_specs=[rowspec(C_IN),
                  pl.BlockSpec((8, C_IN), lambda b, i: (b * nrow8 + jnp.maximum(i * (tt // 8) - 1, 0), 0)),
                  pl.BlockSpec((8, C_IN), lambda b, i: (b * nrow8 + jnp.minimum((i + 1) * (tt // 8), nrow8 - 1), 0)),
                  full((3, C_IN)), full((1, C_W)), full((1, C_W)), full((1, C_W)),
                  full((1, 2 * C_W)), full((128, 2 * C_W)), full((1, 2 * C_W)), full((128, 2 * C_W)),
                  full((128, C_W)), full((SEG_W, SEG_W))],
        out_specs=[rowspec(w) for w in outs],
        out_shape=[jax.ShapeDtypeStruct((nb * t, w), F32) for w in outs],
        scratch_shapes=[pltpu.VMEM((tt + 16, C_IN), F32)],
        compiler_params=_cp("parallel", "parallel"),
        name="r7prep",
    )(pc, pc, pc, lp['r7_conv'], lp['r7_kk'].reshape(1, C_W), lp['r7_ka'].reshape(1, C_W),
      lp['r7_rk'].reshape(1, C_W), lp['r7_w0'].reshape(1, 2 * C_W), _bd2(lp['r7_w2']),
      lp['r7_a0'].reshape(1, 2 * C_W), _bd2(lp['r7_a2']), lp['r7_g2'], _block_diag_ones())


def _cumsum_rows(tri, x):
    t = tri.astype(BF16)
    hi = x.astype(BF16)
    rest = x - hi.astype(F32)
    mid = rest.astype(BF16)
    lo = (rest - mid.astype(F32)).astype(BF16)
    return _dot(t, hi) + (_dot(t, mid) + _dot(t, lo))


def _mm(a, b, prec):
    if prec is None:
        return _dot(a.astype(BF16), b.astype(BF16))
    return _dot(a, b, prec)


def _mm_nt(a, b, prec):
    if prec is None:
        return _dot_nt(a.astype(BF16), b.astype(BF16))
    return _dot_nt(a, b, prec)


def _scan_kernel(rf_ref, vf_ref, kkf_ref, lwf_ref, betaf_ref, kdf_ref,
                 rb_ref, vb_ref, kkb_ref, lwb_ref, betab_ref, kdb_ref, s0_ref,
                 yf_ref, yb_ref, sf_ref, st_ref, *, nc):
    c = pl.program_id(1)
    cl = SCAN_C

    @pl.when(c == 0)
    def _():
        st_ref[...] = s0_ref[...]

    ti = lax.broadcasted_iota(jnp.int32, (cl, cl), 0)
    si = lax.broadcasted_iota(jnp.int32, (cl, cl), 1)
    eye = (ti == si).astype(F32)
    incl = [si <= ti, si >= ti]
    strict = [si < ti, si > ti]

    def hs(x, h):
        return x[:, h * HD:(h + 1) * HD]

    pre = {}
    for bb in range(SCAN_NB):
        for d, (r_ref, v_ref, kk_ref, lw_ref, beta_ref, kd_ref) in enumerate(
                [(rf_ref, vf_ref, kkf_ref, lwf_ref, betaf_ref, kdf_ref),
                 (rb_ref, vb_ref, kkb_ref, lwb_ref, betab_ref, kdb_ref)]):
            lw = lw_ref[bb]
            cum = _cumsum_rows(incl[d], lw)
            e_neg = jnp.exp(-cum)
            pre[bb, d] = dict(r_hat=r_ref[bb] * jnp.exp(cum), a_hat=-kk_ref[bb] * jnp.exp(cum - lw),
                              b_til=beta_ref[bb] * e_neg, k_til=kd_ref[bb] * e_neg,
                              wtot=jnp.exp(jnp.sum(lw, axis=0, keepdims=True)), vv=v_ref[bb])

    chains = [(bb, d, h) for bb in range(SCAN_NB) for d in range(2) for h in range(C_HEADS)]
    n_ch = range(len(chains))

    def part(name, i):
        bb, d, h = chains[i]
        return hs(pre[bb, d][name], h)

    def dirn(i):
        return chains[i][1]

    ar = [jnp.concatenate([part('a_hat', i), part('r_hat', i)], axis=0) for i in n_ch]
    bk = [jnp.concatenate([part('b_til', i), part('k_til', i)], axis=0) for i in n_ch]
    m = [_mm_nt(ar[i], bk[i], PREC_M) for i in n_ch]
    l_k = [jnp.where(strict[dirn(i)], m[i][0:cl, cl:2 * cl], 0.0) for i in n_ch]
    r_b = [jnp.where(incl[dirn(i)], m[i][cl:2 * cl, 0:cl], 0.0) for i in n_ch]
    r_k = [jnp.where(incl[dirn(i)], m[i][cl:2 * cl, cl:2 * cl], 0.0) for i in n_ch]
    p = [jnp.where(strict[dirn(i)], m[i][0:cl, 0:cl], 0.0) for i in n_ch]
    t_inv = [eye + p[i] for i in n_ch]
    n = 2
    while n < cl:
        p = [_mm(p[i], p[i], PREC_INV) for i in n_ch]
        t_inv = [t_inv[i] + _mm(t_inv[i], p[i], PREC_INV) for i in n_ch]
        n *= 2
    x1 = [_mm(l_k[i], part('vv', i), PREC_M) for i in n_ch]
    y0 = [_mm(r_k[i], part('vv', i), PREC_M) for i in n_ch]
    ua = [_mm(t_inv[i], jnp.concatenate([x1[i], part('a_hat', i)], axis=1), PREC_INV) for i in n_ch]
    s0 = [st_ref[d, bb, h] for bb, d, h in chains]
    as0 = [_mm_nt(jnp.concatenate([ua[i][:, HD:2 * HD], part('r_hat', i)], axis=0), s0[i], PREC_SEQ)
           for i in n_ch]
    u = [ua[i][:, 0:HD] + as0[i][0:cl] for i in n_ch]
    y = [y0[i] + as0[i][cl:2 * cl] + _mm(r_b[i], u[i], PREC_SEQ) for i in n_ch]
    for i in n_ch:
        bb, d, h = chains[i]
        (yf_ref, yb_ref)[d][bb, :, h * HD:(h + 1) * HD] = y[i]
    for i in n_ch:
        bb, d, h = chains[i]
        uv_t = jnp.concatenate([u[i], part('vv', i)], axis=0).T
        st_ref[d, bb, h] = (s0[i] + _mm(uv_t, bk[i], PREC_SEQ)) * part('wtot', i)

    @pl.when(c == nc - 1)
    def _():
        sf_ref[...] = st_ref[...]


def _scan(prep, s0, nb, t):
    nc = t // SCAN_C
    assert nb % SCAN_NB == 0
    r, v, kk, lw, beta, kd = [a.reshape(nb, t, a.shape[-1]) for a in prep]

    def chunk(d, c):
        return nc - 1 - c if d else c

    def specs(d):
        shared = pl.BlockSpec((SCAN_NB, SCAN_C, C_W), lambda b, c: (b, chunk(d, c), 0))
        perdir = pl.BlockSpec((SCAN_NB, SCAN_C, C_W), lambda b, c: (b, chunk(d, c), d))
        return [shared, shared, shared, perdir, perdir, perdir]

    st_spec = pl.BlockSpec((2, SCAN_NB, C_HEADS, HD, HD), lambda b, c: (0, b, 0, 0, 0))
    yf, yb, sf = pl.pallas_call(
        functools.partial(_scan_kernel, nc=nc),
        grid=(nb // SCAN_NB, nc),
        in_specs=specs(0) + specs(1) + [st_spec],
        out_specs=[pl.BlockSpec((SCAN_NB, SCAN_C, C_W), lambda b, c: (b, chunk(0, c), 0)),
                   pl.BlockSpec((SCAN_NB, SCAN_C, C_W), lambda b, c: (b, chunk(1, c), 0)), st_spec],
        out_shape=[jax.ShapeDtypeStruct((nb, t, C_W), F32), jax.ShapeDtypeStruct((nb, t, C_W), F32),
                   jax.ShapeDtypeStruct((2, nb, C_HEADS, HD, HD), F32)],
        scratch_shapes=[pltpu.VMEM((2, SCAN_NB, C_HEADS, HD, HD), F32)],
        compiler_params=_cp("parallel", "arbitrary"),
        name="r7scan",
    )(r, v, kk, lw, beta, kd, r, v, kk, lw, beta, kd, s0)
    return (yf.reshape(nb * t, C_W), yb.reshape(nb * t, C_W)), sf


def _outproj_kernel(x_ref, oa_ref, ob_ref, y0_ref, y1_ref, bonus_ref, g_ref, lnw_ref, lnb_ref, bd_ref, w_ref,
                    gate_ref, o_ref):
    bd = bd_ref[...]
    y = y0_ref[...] + y1_ref[...]
    mu = _segsum(y, bd) * (1.0 / HD)
    yc = y - mu
    var = _segsum(yc * yc, bd) * (1.0 / HD)
    yn = yc * lax.rsqrt(var + GN_EPS) * lnw_ref[...] + lnb_ref[...]
    oc = ((yn + bonus_ref[...]) * g_ref[...]).astype(BF16)
    acc = (_dot(oa_ref[...], w_ref[0:256, :]) + _dot(ob_ref[...], w_ref[256:512, :])
           + _dot(oc, w_ref[512:1024, :]))
    o_ref[...] = x_ref[...] + gate_ref[...] * acc


def _outproj(x2, oa, ob, y, bonus, g, lp, w_out_bf, mod3, mod_row_of_batch, nb, t):
    tm = 256
    nt = t // tm

    def rowspec(w):
        return pl.BlockSpec((tm, w), lambda b, i: (b * nt + i, 0))

    def full(shape):
        return pl.BlockSpec(shape, lambda b, i: (0,) * len(shape))

    return pl.pallas_call(
        _outproj_kernel,
        grid=(nb, nt),
        in_specs=[rowspec(D), rowspec(256), rowspec(256),
                  rowspec(C_W), rowspec(C_W),
                  rowspec(C_W), rowspec(C_W), full((1, C_W)), full((1, C_W)), full((SEG_W, SEG_W)), full((D, D)),
                  pl.BlockSpec((None, 1, D), lambda b, i: (mod_row_of_batch(b), 0, 2))],
        out_specs=rowspec(D),
        out_shape=jax.ShapeDtypeStruct((nb * t, D), F32),
        compiler_params=_cp("parallel", "parallel"),
        name="outproj",
    )(x2, oa, ob, y[0], y[1], bonus, g, lp['r7_lnw'].reshape(1, C_W), lp['r7_lnb'].reshape(1, C_W),
      _block_diag_ones(), w_out_bf, mod3)


def _peer_score_kernel(x_ref, g_ref, sc_ref, sh_ref, wq_ref, keys_ref, h_ref, st_ref):
    x = x_ref[...]
    y = x * lax.rsqrt(jnp.mean(x * x, axis=-1, keepdims=True) + EPS) * g_ref[...]
    h = (y * (1.0 + sc_ref[...]) + sh_ref[...]).astype(BF16)
    h_ref[...] = h
    q = _dot(h, wq_ref[...])
    half = P_QDIM // 2
    for hp in range(2 * P_HEADS):
        st_ref[hp] = _dot3(keys_ref[hp], q[:, hp * half:(hp + 1) * half], nt=True)


def _peer_scores(x2, norm_g, mod3, mod_row_of_batch, wq_bf, keys, nb, t):
    tm = 256
    nt = t // tm
    n = nb * t
    return pl.pallas_call(
        _peer_score_kernel,
        grid=(nb, nt),
        in_specs=[pl.BlockSpec((tm, D), lambda b, i: (b * nt + i, 0)),
                  pl.BlockSpec((1, D), lambda b, i: (0, 0)),
                  pl.BlockSpec((None, 1, D), lambda b, i: (mod_row_of_batch(b), 0, 4)),
                  pl.BlockSpec((None, 1, D), lambda b, i: (mod_row_of_batch(b), 0, 3)),
                  pl.BlockSpec((D, P_HEADS * P_QDIM), lambda b, i: (0, 0)),
                  pl.BlockSpec((2 * P_HEADS, P_NKEYS, P_QDIM // 2), lambda b, i: (0, 0, 0))],
        out_specs=[pl.BlockSpec((tm, D), lambda b, i: (b * nt + i, 0)),
                   pl.BlockSpec((2 * P_HEADS, P_NKEYS, tm), lambda b, i: (0, 0, b * nt + i))],
        out_shape=[jax.ShapeDtypeStruct((n, D), BF16),
                   jax.ShapeDtypeStruct((2 * P_HEADS, P_NKEYS, n), F32)],
        compiler_params=_cp("parallel", "parallel"),
        name="peer_scores",
    )(x2, norm_g.reshape(1, D), mod3, mod3, wq_bf, keys.reshape(2 * P_HEADS, P_NKEYS, P_QDIM // 2))


def _batcher_pairs(n):
    pairs = []
    p = 1
    while p < n:
        k = p
        while k >= 1:
            for j in range(k % p, n - k, 2 * k):
                for i in range(min(k, n - j - k)):
                    if (i + j) // (2 * p) == (i + j + k) // (2 * p):
                        pairs.append((i + j, i + j + k))
            k //= 2
        p *= 2
    return pairs


_SORT16 = _batcher_pairs(P_TOPK)
_BITONIC16 = [(i, i + d) for d in (8, 4, 2, 1) for i in range(P_TOPK) if i & d == 0]


def _compare_exchange(x, pairs):
    for i, j in pairs:
        x[i], x[j] = jnp.maximum(x[i], x[j]), jnp.minimum(x[i], x[j])
    return x


def _top16_sorted(slabs):
    x = _compare_exchange(list(slabs), _SORT16)
    for shift in (4, 2, 1):
        y = [pltpu.roll(x[P_TOPK - 1 - k], shift, 0) for k in range(P_TOPK)]
        x = [jnp.maximum(x[k], y[k]) for k in range(P_TOPK)]
        x = _compare_exchange(x, _BITONIC16)
    return x


def _peer_gate_tables(st_ref, rk_ref, be_ref, cnt_ref, al_ref, top_ref):
    nblk = st_ref.shape[-1] // LANES
    sub = lax.broadcasted_iota(jnp.int32, (8, LANES), 0)
    ninf = jnp.full((8, LANES), -jnp.inf, F32)

    def block(it, carry):
        h = it // nblk
        lanes = pl.ds(pl.multiple_of((it % nblk) * LANES, LANES), LANES)
        for p in range(2):
            top = _top16_sorted([st_ref[2 * h + p, 8 * k:8 * k + 8, lanes] for k in range(P_TOPK)])
            for k in range(P_TOPK):
                top_ref[p * P_TOPK + k:p * P_TOPK + k + 1, lanes] = top[k][0:1]
        a16 = top_ref[0:P_TOPK, lanes]
        b16 = top_ref[P_TOPK:2 * P_TOPK, lanes]
        b8 = b16[0:8]
        cand = [a16[0:1] + b8, a16[0:1] + b16[8:16]]
        for p in range(2, 9):
            cand.append(jnp.where(sub < P_TOPK // p, a16[p - 1:p] + b8, -jnp.inf))
        cand.append(a16[8:16] + b16[0:1])
        best = _top16_sorted(cand + [ninf] * (P_TOPK - len(cand)))
        tau = best[P_TOPK - 1][0:1]
        z = jnp.ones_like(tau)
        for k in range(1, P_TOPK):
            z = z + jnp.exp(best[k][0:1] - best[0][0:1])
        s1 = st_ref[2 * h, :, lanes]
        s2 = st_ref[2 * h + 1, :, lanes]
        cnt = jnp.zeros_like(s1)
        rk = jnp.ones_like(s2)
        for q in range(P_TOPK):
            bq = b16[q:q + 1]
            theta = jnp.min(jnp.where(a16 + bq >= tau, a16, jnp.inf), axis=0, keepdims=True)
            cnt = jnp.where(s1 >= theta, q + 1.0, cnt)
            rk = jnp.where(bq > s2, q + 2.0, rk)
        cnt_ref[h, :, lanes] = cnt
        rk_ref[h, :, lanes] = rk.astype(BF16)
        al_ref[h, :, lanes] = jnp.exp(s1 - a16[0:1]) / z
        be_ref[h, :, lanes] = jnp.exp(s2 - b16[0:1]).astype(BF16)
        return carry

    lax.fori_loop(0, P_HEADS * nblk, block, 0)


def _peer_chunk(chunk, act_ref, rk_ref, be_ref, cnt_ref, al_ref, w_ref):
    nrow = w_ref.shape[0] // P_NKEYS
    for ii in range(nrow):
        i = chunk * nrow + ii
        wrow = None
        for h in range(P_HEADS):
            cnt = cnt_ref[h, pl.ds(i, 1), :].astype(BF16)
            al = al_ref[h, pl.ds(i, 1), :].astype(BF16)
            term = jnp.where(rk_ref[h] <= cnt, al * be_ref[h], 0.0)
            wrow = term if wrow is None else wrow + term
        rows = slice(ii * P_NKEYS, (ii + 1) * P_NKEYS)
        act = act_ref[rows, :]
        gl = 0.5 * act * (1.0 + lax.erf(act * (2.0 ** -0.5)))
        w_ref[rows, :] = wrow * gl.astype(BF16)


def _peer_dense_kernel(x_ref, h_ref, st_ref, u_ref, vt_ref, gate_ref, o_ref,
                       rk_ref, be_ref, cnt_ref, al_ref, top_ref, act0_ref, act1_ref, w_ref, acc_ref, *, ne):
    e = pl.program_id(1)

    @pl.when(e == 0)
    def _():
        act0_ref[...] = _dot_nt(u_ref[...], h_ref[...])
        _peer_gate_tables(st_ref, rk_ref, be_ref, cnt_ref, al_ref, top_ref)
        acc_ref[...] = jnp.zeros_like(acc_ref)

    def step(cur_ref, nxt_ref):
        nxt_ref[...] = _dot_nt(u_ref[...], h_ref[...])
        _peer_chunk(e - 1, cur_ref, rk_ref, be_ref, cnt_ref, al_ref, w_ref)
        acc_ref[...] += _dot(vt_ref[...], w_ref[...])

    @pl.when(e % 2 == 1)
    def _():
        step(act0_ref, act1_ref)

    @pl.when((e > 0) & (e % 2 == 0))
    def _():
        step(act1_ref, act0_ref)

    @pl.when(e == ne)
    def _():
        o_ref[...] = x_ref[...] + gate_ref[...] * acc_ref[...].T


def _peer_dense(x2, h2, st, u_bf, vt_bf, mod3, mod_row_of_batch, nb, t):
    tm = 512 if t % 512 == 0 else 256
    ec = PEER_EC
    nt = t // tm
    ne = P_EXPERTS // ec
    return pl.pallas_call(
        functools.partial(_peer_dense_kernel, ne=ne),
        grid=(nb * nt, ne + 1),
        in_specs=[pl.BlockSpec((tm, D), lambda i, e: (i, 0)),
                  pl.BlockSpec((tm, D), lambda i, e: (i, 0)),
                  pl.BlockSpec((2 * P_HEADS, P_NKEYS, tm), lambda i, e: (0, 0, i)),
                  pl.BlockSpec((ec, D), lambda i, e: (jnp.minimum(e, ne - 1), 0)),
                  pl.BlockSpec((None, D, ec), lambda i, e: (jnp.maximum(e - 1, 0), 0, 0)),
                  pl.BlockSpec((None, 1, D), lambda i, e: (mod_row_of_batch(i // nt), 0, 5))],
        out_specs=pl.BlockSpec((tm, D), lambda i, e: (i, 0)),
        out_shape=jax.ShapeDtypeStruct((nb * t, D), F32),
        scratch_shapes=[pltpu.VMEM((P_HEADS, P_NKEYS, tm), BF16),
                        pltpu.VMEM((P_HEADS, P_NKEYS, tm), BF16),
                        pltpu.VMEM((P_HEADS, P_NKEYS, tm), F32),
                        pltpu.VMEM((P_HEADS, P_NKEYS, tm), F32),
                        pltpu.VMEM((2 * P_TOPK, tm), F32),
                        pltpu.VMEM((ec, tm), F32),
                        pltpu.VMEM((ec, tm), F32),
                        pltpu.VMEM((ec, tm), BF16),
                        pltpu.VMEM((D, tm), F32)],
        compiler_params=pltpu.CompilerParams(dimension_semantics=("parallel", "arbitrary"),
                                             vmem_limit_bytes=PEER_VMEM_LIMIT),
        name="peer_dense",
    )(x2, h2, st, u_bf, vt_bf, mod3)


def _peer(x2, norm_g, mod3, mod_row_of_batch, wq_bf, keys, u_bf, vt_bf, nb, t):
    h2, st = _peer_scores(x2, norm_g, mod3, mod_row_of_batch, wq_bf, keys, nb, t)
    return _peer_dense(x2, h2, st, u_bf, vt_bf, mod3, mod_row_of_batch, nb, t)


def kernel(x, c, ctx, c_ctx, norm_mix, norm_ffn, w_mod, b_mod, w_in, w_out, a_qnorm, a_knorm, a_sink, b_qnorm,
           b_knorm, b_rpb, r7_conv, r7_w0, r7_w2, r7_a0, r7_a2, r7_g2, r7_kk, r7_ka, r7_rk, r7_lnw, r7_lnb,
           peer_wq, peer_keys, peer_u, peer_v):
    nb, s, _ = x.shape
    lc = ctx.shape[1]
    depth = w_in.shape[0]
    assert nb < 16 and nb % SCAN_NB == 0 and s % 512 == 0 and lc % 256 == 0
    rows = s // GRID_W

    cc = jnp.zeros((16, D), F32).at[:nb].set(c).at[nb].set(c_ctx)
    mod = _modulation(cc, w_mod, b_mod)
    rope_tabs = _rope_tables(s)
    lat_row = lambda b: b
    ctx_row = lambda b: nb

    x_lat = x.reshape(nb * s, D)
    x_ctx = ctx.reshape(nb * lc, D)
    scale = HD ** -0.5
    for l in range(depth):
        with_ctx = l < depth - 1
        mod3 = mod[l].reshape(16, 1, 6 * D)
        lp = {'r7_conv': r7_conv[l], 'r7_w0': r7_w0[l], 'r7_w2': r7_w2[l], 'r7_a0': r7_a0[l], 'r7_a2': r7_a2[l],
              'r7_g2': r7_g2[l], 'r7_kk': r7_kk[l], 'r7_ka': r7_ka[l], 'r7_rk': r7_rk[l], 'r7_lnw': r7_lnw[l],
              'r7_lnb': r7_lnb[l]}
        w_in_bf = w_in[l].astype(BF16)
        w_out_bf = w_out[l].astype(BF16)
        gain_a = jnp.concatenate([jnp.tile(a_qnorm[l] * scale, A_HEADS), jnp.tile(a_knorm[l], A_KV)]).reshape(1, 384)
        gain_b = jnp.concatenate([jnp.tile(b_qnorm[l] * scale, B_HEADS), jnp.tile(b_knorm[l], B_HEADS)]).reshape(1, 512)

        ab_lat, pc_lat = _inproj(x_lat, mod3, lat_row, norm_mix[l], w_in_bf, gain_a, gain_b, rope_tabs, nb, s)
        ab_ctx, pc_ctx = _inproj(x_ctx, mod3, ctx_row, norm_mix[l], w_in_bf, gain_a, gain_b, None, nb, lc)

        o_a = _attn_a(ab_lat, ab_ctx, a_sink[l], nb, s, lc)
        o_b = _attn_b(ab_lat, ab_ctx, _na_bias_tables(b_rpb[l], rows), nb, s, lc)

        prep_ctx = _r7prep(pc_ctx, lp, nb, lc)
        prep_lat = _r7prep(pc_lat, lp, nb, s)
        zero_state = jnp.zeros((2, nb, C_HEADS, HD, HD), F32)
        y_ctx, s_ctx = _scan(prep_ctx[:6], zero_state, nb, lc)
        y_lat, _ = _scan(prep_lat[:6], s_ctx, nb, s)

        x_lat = _outproj(x_lat, o_a, o_b, y_lat, prep_lat[7], prep_lat[6], lp, w_out_bf, mod3, lat_row, nb, s)
        wq_bf = peer_wq[l].astype(BF16)
        u_bf = peer_u[l].astype(BF16)
        vt_bf = peer_v[l].astype(BF16).reshape(P_EXPERTS // PEER_EC, PEER_EC, D).transpose(0, 2, 1)
        x_lat = _peer(x_lat, norm_ffn[l], mod3, lat_row, wq_bf, peer_keys[l], u_bf, vt_bf, nb, s)
        if with_ctx:
            o_ac, o_bc = _attn_ctx(ab_ctx, a_sink[l], nb, lc)
            x_ctx = _outproj(x_ctx, o_ac, o_bc, y_ctx, prep_ctx[7], prep_ctx[6], lp, w_out_bf, mod3, ctx_row, nb, lc)
            x_ctx = _peer(x_ctx, norm_ffn[l], mod3, ctx_row, wq_bf, peer_keys[l], u_bf, vt_bf, nb, lc)
    return x_lat.reshape(nb, s, D)
```

```python
import functools

import numpy as np
import jax
import jax.numpy as jnp
from jax import lax
from jax.experimental import pallas as pl
from jax.experimental.pallas import tpu as pltpu

F32 = jnp.float32
BF16 = jnp.bfloat16
HI = lax.Precision.HIGHEST

D = 1024
DEPTH = 2
GRID_W = 64
HD = 64
LANES = 128
EPS = 1e-6
NEG = -1e30
A_HEADS, A_KV, A_BLOCK, A_WINDOW = 4, 2, 128, 128
A_STEP_BLOCKS = 2
B_HEADS, NA_ROWS, NA_COLS = 4, 8, 16
NA_STEP_ROWS = 4
C_HEADS = 8
C_W = 512
C_IN = 1920
AB_W = 1280
IN_W = AB_W + C_IN
GN_EPS = 64e-5
DECAY_SCALE = 0.6065306597126334
ROPE_BASE = 10000.0
P_HEADS, P_NKEYS, P_QDIM, P_TOPK = 8, 128, 256, 16
P_EXPERTS = P_NKEYS * P_NKEYS
SCAN_C = 64
SCAN_NB = 2
PREC_M = PREC_INV = PREC_SEQ = None
VMEM_LIMIT = 48 * 1024 * 1024
PEER_VMEM_LIMIT = 56 * 1024 * 1024
PEER_EC = 1024
PEER_HALF = 256


def _cp(*sem):
    return pltpu.CompilerParams(dimension_semantics=sem, vmem_limit_bytes=VMEM_LIMIT)


def _dot(a, b, prec=None):
    return jnp.dot(a, b, precision=prec, preferred_element_type=F32)


def _dot_nt(a, b, prec=None):
    return lax.dot_general(a, b, (((1,), (1,)), ((), ())), precision=prec, preferred_element_type=F32)


SEG_W = 256


def _block_diag_ones():
    i = np.arange(SEG_W) // HD
    return jnp.asarray((i[:, None] == i[None, :]).astype(np.float32), BF16)


def _split2(x):
    hi = x.astype(BF16)
    return hi, (x - hi.astype(F32)).astype(BF16)


def _segsum(x, bd):
    hi, lo = _split2(x)
    outs = []
    for g0 in range(0, x.shape[1], SEG_W):
        w = min(SEG_W, x.shape[1] - g0)
        outs.append(_dot(hi[:, g0:g0 + w], bd[0:w, 0:w]) + _dot(lo[:, g0:g0 + w], bd[0:w, 0:w]))
    return outs[0] if len(outs) == 1 else jnp.concatenate(outs, axis=1)


def _dot3(a, b, nt=False):
    f = _dot_nt if nt else _dot
    a_hi, a_lo = _split2(a)
    b_hi, b_lo = _split2(b)
    return f(a_hi, b_hi) + (f(a_lo, b_hi) + f(a_hi, b_lo))


def _mod_kernel(c_ref, w_ref, b_ref, o_ref):
    c = c_ref[...]
    s = c * jax.nn.sigmoid(c)
    o_ref[...] = _dot(s, w_ref[...], HI) + b_ref[...]


def _modulation(cc, w_mod, b_mod):
    L, _, n = w_mod.shape
    tn = 512
    return pl.pallas_call(
        _mod_kernel,
        grid=(L, n // tn),
        in_specs=[pl.BlockSpec((16, D), lambda l, j: (0, 0)),
                  pl.BlockSpec((None, D, tn), lambda l, j: (l, 0, j)),
                  pl.BlockSpec((None, 1, tn), lambda l, j: (l, 0, j))],
        out_specs=pl.BlockSpec((None, 16, tn), lambda l, j: (l, 0, j)),
        out_shape=jax.ShapeDtypeStruct((L, 16, n), F32),
        compiler_params=_cp("parallel", "parallel"),
        name="modulation",
    )(cc, w_mod, b_mod.reshape(L, 1, n))


def _swap16(x):
    n = x.shape[-1]
    lane = lax.broadcasted_iota(jnp.int32, x.shape, 1)
    fwd = pltpu.roll(x, n - 16, 1)
    bwd = pltpu.roll(x, 16, 1)
    return jnp.where((lane % 32) < 16, fwd, bwd)


def _head_rms(x, bd, gain):
    ss = _segsum(x * x, bd)
    return x * lax.rsqrt(ss * (1.0 / HD) + EPS) * gain


def _inproj_kernel(*refs, rope):
    if rope:
        (x_ref, g_ref, sc_ref, sh_ref, w_ref, ga_ref, gb_ref, bd_ref, cos_ref, sin_ref,
         ab_ref, c_ref) = refs
    else:
        (x_ref, g_ref, sc_ref, sh_ref, w_ref, ga_ref, gb_ref, bd_ref, ab_ref, c_ref) = refs
    x = x_ref[...]
    y = x * lax.rsqrt(jnp.mean(x * x, axis=-1, keepdims=True) + EPS) * g_ref[...]
    h = y * (1.0 + sc_ref[...]) + sh_ref[...]
    acc = _dot(h.astype(BF16), w_ref[...])
    bd = bd_ref[...]
    qa = _head_rms(acc[:, 0:384], bd, ga_ref[...])
    if rope:
        qa = qa * cos_ref[...] + _swap16(qa) * sin_ref[...]
    qb = _head_rms(acc[:, 512:1024], bd, gb_ref[...])
    ab_ref[:, 0:384] = qa.astype(BF16)
    ab_ref[:, 384:512] = acc[:, 384:512].astype(BF16)
    ab_ref[:, 512:1024] = qb.astype(BF16)
    ab_ref[:, 1024:1280] = acc[:, 1024:1280].astype(BF16)
    c_ref[...] = acc[:, AB_W:IN_W]


def _inproj(x2, mod3, mod_row_of_batch, norm_g, w_in_bf, gain_a, gain_b, rope_tabs, nb, t):
    tm = 256
    nt = t // tm
    rope = rope_tabs is not None
    in_specs = [
        pl.BlockSpec((tm, D), lambda b, i: (b * nt + i, 0)),
        pl.BlockSpec((1, D), lambda b, i: (0, 0)),
        pl.BlockSpec((None, 1, D), lambda b, i: (mod_row_of_batch(b), 0, 1)),
        pl.BlockSpec((None, 1, D), lambda b, i: (mod_row_of_batch(b), 0, 0)),
        pl.BlockSpec((D, IN_W), lambda b, i: (0, 0)),
        pl.BlockSpec((1, 384), lambda b, i: (0, 0)),
        pl.BlockSpec((1, 512), lambda b, i: (0, 0)),
        pl.BlockSpec((SEG_W, SEG_W), lambda b, i: (0, 0)),
    ]
    args = [x2, norm_g.reshape(1, D), mod3, mod3, w_in_bf, gain_a, gain_b,
            _block_diag_ones()]
    if rope:
        in_specs += [pl.BlockSpec((tm, 384), lambda b, i: (i, 0)),
                     pl.BlockSpec((tm, 384), lambda b, i: (i, 0))]
        args += list(rope_tabs)
    return pl.pallas_call(
        functools.partial(_inproj_kernel, rope=rope),
        grid=(nb, nt),
        in_specs=in_specs,
        out_specs=[pl.BlockSpec((tm, AB_W), lambda b, i: (b * nt + i, 0)),
                   pl.BlockSpec((tm, C_IN), lambda b, i: (b * nt + i, 0))],
        out_shape=[jax.ShapeDtypeStruct((nb * t, AB_W), BF16),
                   jax.ShapeDtypeStruct((nb * t, C_IN), F32)],
        compiler_params=_cp("parallel", "parallel"),
        name="inproj_rope" if rope else "inproj_ctx",
    )(*args)


def _rope_tables(s):
    tok = np.arange(s)
    inv = ROPE_BASE ** (-np.arange(0, 32, 2) / 32.0)
    ar = (tok // GRID_W)[:, None] * inv[None]
    ac = (tok % GRID_W)[:, None] * inv[None]
    cos = np.concatenate([np.cos(ar), np.cos(ar), np.cos(ac), np.cos(ac)], axis=1)
    sin = np.concatenate([-np.sin(ar), np.sin(ar), -np.sin(ac), np.sin(ac)], axis=1)
    return (jnp.asarray(np.tile(cos, (1, 6)), F32), jnp.asarray(np.tile(sin, (1, 6)), F32))


def _softmax_pv(s, v, sink=None):
    m = jnp.max(s, axis=-1, keepdims=True)
    if sink is not None:
        m = jnp.maximum(m, sink)
    p = jnp.exp(s - m)
    den = jnp.sum(p, axis=-1, keepdims=True)
    if sink is not None:
        den = den + jnp.exp(sink - m)
    return _dot(p.astype(BF16), v) / den


def _attn_a_kernel(q_ref, k0_ref, k1_ref, k2_ref, k3_ref, v0_ref, v1_ref, v2_ref, v3_ref, kc_ref, vc_ref,
                   sink_ref, o_ref, *, nblk):
    kb = [k0_ref[...], k1_ref[...], k2_ref[...], k3_ref[...]]
    vb = [v0_ref[...], v1_ref[...], v2_ref[...], v3_ref[...]]
    kc = kc_ref[...]
    vc = vc_ref[...]
    nk = 3 * A_BLOCK + kc.shape[0]
    g = A_HEADS // A_KV
    row = lax.broadcasted_iota(jnp.int32, (g * A_BLOCK, nk), 0) % A_BLOCK
    col = lax.broadcasted_iota(jnp.int32, (g * A_BLOCK, nk), 1)
    rel = col - A_BLOCK - row
    band = (jnp.abs(rel) <= A_WINDOW) | (col >= 3 * A_BLOCK)
    chains, ok, k_all, v_all = [], [], [], []
    for j in range(A_STEP_BLOCKS):
        n = pl.program_id(1) * A_STEP_BLOCKS + j
        ok.append(band & ((n > 0) | (col >= A_BLOCK)) & ((n < nblk - 1) | (col < 2 * A_BLOCK) | (col >= 3 * A_BLOCK)))
        k_all.append(jnp.concatenate(kb[j:j + 3] + [kc], axis=0))
        v_all.append(jnp.concatenate(vb[j:j + 3] + [vc], axis=0))
        chains += [(j, hk) for hk in range(A_KV)]

    def qg(j, hk):
        rows = slice(j * A_BLOCK, (j + 1) * A_BLOCK)
        return jnp.concatenate([q_ref[rows, (hk * g + i) * HD:(hk * g + i + 1) * HD] for i in range(g)], axis=0)

    sink = [jnp.concatenate([jnp.broadcast_to(sink_ref[:, hk * g + i:hk * g + i + 1], (A_BLOCK, 1))
                             for i in range(g)], axis=0) for hk in range(A_KV)]
    s = [jnp.where(ok[j], _dot_nt(qg(j, hk), k_all[j][:, hk * HD:(hk + 1) * HD]), NEG) for j, hk in chains]
    m = [jnp.maximum(jnp.max(s[i], axis=-1, keepdims=True), sink[hk]) for i, (j, hk) in enumerate(chains)]
    p = [jnp.exp(s[i] - m[i]) for i in range(len(chains))]
    den = [jnp.sum(p[i], axis=-1, keepdims=True) + jnp.exp(sink[hk] - m[i]) for i, (j, hk) in enumerate(chains)]
    o = [_dot(p[i].astype(BF16), v_all[j][:, hk * HD:(hk + 1) * HD]) / den[i] for i, (j, hk) in enumerate(chains)]
    for i, (j, hk) in enumerate(chains):
        for gi in range(g):
            hq = hk * g + gi
            o_ref[j * A_BLOCK:(j + 1) * A_BLOCK, hq * HD:(hq + 1) * HD] = (
                o[i][gi * A_BLOCK:(gi + 1) * A_BLOCK].astype(BF16))


def _attn_a(ab_lat, ab_ctx, sink, nb, s, lc):
    nblk = s // A_BLOCK
    steps = nblk // A_STEP_BLOCKS

    def kv(col, d):
        return pl.BlockSpec(
            (A_BLOCK, 128), lambda b, i: (b * nblk + jnp.clip(i * A_STEP_BLOCKS + d, 0, nblk - 1), col))

    tq = A_STEP_BLOCKS * A_BLOCK
    return pl.pallas_call(
        functools.partial(_attn_a_kernel, nblk=nblk),
        grid=(nb, steps),
        in_specs=[pl.BlockSpec((tq, 256), lambda b, i: (b * steps + i, 0)),
                  kv(2, -1), kv(2, 0), kv(2, 1), kv(2, 2), kv(3, -1), kv(3, 0), kv(3, 1), kv(3, 2),
                  pl.BlockSpec((lc, 128), lambda b, i: (b, 2)),
                  pl.BlockSpec((lc, 128), lambda b, i: (b, 3)),
                  pl.BlockSpec((1, A_HEADS), lambda b, i: (0, 0))],
        out_specs=pl.BlockSpec((tq, 256), lambda b, i: (b * steps + i, 0)),
        out_shape=jax.ShapeDtypeStruct((nb * s, 256), BF16),
        compiler_params=_cp("parallel", "parallel"),
        name="attn_a",
    )(ab_lat, *([ab_lat] * 8), ab_ctx, ab_ctx, sink.reshape(1, A_HEADS))


def _attn_b_kernel(q_ref, k_ref, v_ref, kc_ref, vc_ref, bias_ref, o_ref, *, rows):
    nwin = NA_ROWS * GRID_W
    kc = kc_ref[...]
    vc = vc_ref[...]
    q, kw, vw, tab = [], [], [], []
    for rr in range(NA_STEP_ROWS):
        r = pl.program_id(1) * NA_STEP_ROWS + rr
        rs = jnp.clip(r - NA_ROWS // 2, 0, rows - NA_ROWS)
        start = pl.multiple_of(rs * GRID_W, GRID_W)
        tab.append(rs - r + NA_ROWS - 1)
        q.append(q_ref[rr * GRID_W:(rr + 1) * GRID_W, :])
        kw.append(k_ref[pl.ds(start, nwin), :])
        vw.append(v_ref[pl.ds(start, nwin), :])
    chains = [(rr, h) for rr in range(NA_STEP_ROWS) for h in range(B_HEADS)]

    def hs(x, h):
        return x[:, h * HD:(h + 1) * HD]

    s_loc = [_dot_nt(hs(q[rr], h), hs(kw[rr], h)) + bias_ref[tab[rr], h] for rr, h in chains]
    s_ctx = [_dot_nt(hs(q[rr], h), hs(kc, h)) for rr, h in chains]
    m = [jnp.maximum(jnp.max(a, axis=-1, keepdims=True), jnp.max(b, axis=-1, keepdims=True))
         for a, b in zip(s_loc, s_ctx)]
    p_loc = [jnp.exp(a - mm) for a, mm in zip(s_loc, m)]
    p_ctx = [jnp.exp(b - mm) for b, mm in zip(s_ctx, m)]
    den = [jnp.sum(a, axis=-1, keepdims=True) + jnp.sum(b, axis=-1, keepdims=True) for a, b in zip(p_loc, p_ctx)]
    o = [(_dot(p_loc[i].astype(BF16), hs(vw[rr], h)) + _dot(p_ctx[i].astype(BF16), hs(vc, h))) / den[i]
         for i, (rr, h) in enumerate(chains)]
    for i, (rr, h) in enumerate(chains):
        o_ref[rr * GRID_W:(rr + 1) * GRID_W, h * HD:(h + 1) * HD] = o[i].astype(BF16)


def _na_bias_tables(rpb, rows):
    kr = NA_ROWS
    qc = np.arange(GRID_W)
    win_start = np.clip(qc - NA_COLS // 2, 0, GRID_W - NA_COLS)
    kcol = np.arange(GRID_W)
    valid = (kcol[None, :] >= win_start[:, None]) & (kcol[None, :] < win_start[:, None] + NA_COLS)
    pad = GRID_W - NA_COLS
    rp = jnp.pad(rpb.astype(F32), ((0, 0), (0, 0), (pad, pad)))
    toe = jnp.stack([rp[:, :, GRID_W - 1 - q:2 * GRID_W - 1 - q] for q in range(GRID_W)], axis=2)
    toe = jnp.where(jnp.asarray(valid[None, None]), toe, NEG)
    tabs = [toe[:, off:off + kr].transpose(0, 2, 1, 3).reshape(rpb.shape[0], GRID_W, kr * GRID_W)
            for off in range(kr)]
    return jnp.stack(tabs, axis=0)


def _attn_b(ab_lat, ab_ctx, bias_tabs, nb, s, lc):
    rows = s // GRID_W
    steps = rows // NA_STEP_ROWS
    tq = NA_STEP_ROWS * GRID_W
    return pl.pallas_call(
        functools.partial(_attn_b_kernel, rows=rows),
        grid=(nb, steps),
        in_specs=[pl.BlockSpec((tq, 256), lambda b, r: (b * steps + r, 2)),
                  pl.BlockSpec((s, 256), lambda b, r: (b, 3)),
                  pl.BlockSpec((s, 256), lambda b, r: (b, 4)),
                  pl.BlockSpec((lc, 256), lambda b, r: (b, 3)),
                  pl.BlockSpec((lc, 256), lambda b, r: (b, 4)),
                  pl.BlockSpec((NA_ROWS, B_HEADS, GRID_W, NA_ROWS * GRID_W), lambda b, r: (0, 0, 0, 0))],
        out_specs=pl.BlockSpec((tq, 256), lambda b, r: (b * steps + r, 0)),
        out_shape=jax.ShapeDtypeStruct((nb * s, 256), BF16),
        compiler_params=_cp("parallel", "arbitrary"),
        name="attn_b",
    )(ab_lat, ab_lat, ab_lat, ab_ctx, ab_ctx, bias_tabs)


def _attn_ctx_kernel(ab_ref, sink_ref, oa_ref, ob_ref):
    lc = ab_ref.shape[0]
    g = A_HEADS // A_KV
    for hk in range(A_KV):
        qg = jnp.concatenate([ab_ref[:, (hk * g + j) * HD:(hk * g + j + 1) * HD] for j in range(g)], axis=0)
        s = _dot_nt(qg, ab_ref[:, 256 + hk * HD:256 + (hk + 1) * HD])
        sink = jnp.concatenate(
            [jnp.broadcast_to(sink_ref[:, hk * g + j:hk * g + j + 1], (lc, 1)) for j in range(g)], axis=0)
        o = _softmax_pv(s, ab_ref[:, 384 + hk * HD:384 + (hk + 1) * HD], sink)
        for j in range(g):
            hq = hk * g + j
            oa_ref[:, hq * HD:(hq + 1) * HD] = o[j * lc:(j + 1) * lc].astype(BF16)
    for h in range(B_HEADS):
        s = _dot_nt(ab_ref[:, 512 + h * HD:512 + (h + 1) * HD], ab_ref[:, 768 + h * HD:768 + (h + 1) * HD])
        o = _softmax_pv(s, ab_ref[:, 1024 + h * HD:1024 + (h + 1) * HD])
        ob_ref[:, h * HD:(h + 1) * HD] = o.astype(BF16)


def _attn_ctx(ab_ctx, sink, nb, lc):
    return pl.pallas_call(
        _attn_ctx_kernel,
        grid=(nb,),
        in_specs=[pl.BlockSpec((lc, AB_W), lambda b: (b, 0)),
                  pl.BlockSpec((1, A_HEADS), lambda b: (0, 0))],
        out_specs=[pl.BlockSpec((lc, 256), lambda b: (b, 0)),
                   pl.BlockSpec((lc, 256), lambda b: (b, 0))],
        out_shape=[jax.ShapeDtypeStruct((nb * lc, 256), BF16),
                   jax.ShapeDtypeStruct((nb * lc, 256), BF16)],
        compiler_params=_cp("parallel"),
        name="attn_ctx",
    )(ab_ctx, sink.reshape(1, A_HEADS))


def _r7prep_kernel(x_ref, prev_ref, next_ref, cw_ref, kkw_ref, ka_ref, rk_ref, w0_ref, w2_ref, a0_ref, a2_ref,
                   g2_ref, bd_ref,
                   r_ref, v_ref, kk_ref, lw_ref, beta_ref, kd_ref, g_ref, bonus_ref, pad_ref, *, nt):
    i = pl.program_id(1)
    tt = x_ref.shape[0]
    x = x_ref[...]
    pad_ref[8:8 + tt, :] = x
    pad_ref[7:8, :] = jnp.where(i > 0, prev_ref[7:8, :], 0.0)
    pad_ref[8 + tt:9 + tt, :] = jnp.where(i < nt - 1, next_ref[0:1, :], 0.0)
    xc = pad_ref[7:7 + tt, :] * cw_ref[0:1, :] + x * cw_ref[1:2, :] + pad_ref[9:9 + tt, :] * cw_ref[2:3, :]
    r = xc[:, 0:512]
    k = xc[:, 512:1024]
    v = xc[:, 1024:1536]
    wd = xc[:, 1536:1664]
    ad = xc[:, 1664:1792]
    gd = xc[:, 1792:1920]
    bd = bd_ref[...]
    kkh = k * kkw_ref[...]
    kk = kkh / jnp.maximum(jnp.sqrt(_segsum(kkh * kkh, bd)), 1e-12)
    zw = w0_ref[...] + _dot3(jnp.tanh(wd), w2_ref[...])
    za = a0_ref[...] + _dot3(ad, a2_ref[...])
    a = jax.nn.sigmoid(za)
    ka = ka_ref[...]
    kd0 = k * (1.0 + (a[:, 0:512] - 1.0) * ka)
    kd1 = k * (1.0 + (a[:, 512:1024] - 1.0) * ka)
    r_ref[...] = r
    v_ref[...] = v
    kk_ref[...] = kk
    lw_ref[...] = -DECAY_SCALE * jax.nn.sigmoid(zw)
    beta_ref[:, 0:512] = kk * a[:, 0:512]
    beta_ref[:, 512:1024] = kk * a[:, 512:1024]
    kd_ref[:, 0:512] = kd0
    kd_ref[:, 512:1024] = kd1
    g_ref[...] = _dot3(jax.nn.sigmoid(gd), g2_ref[...])
    bonus_ref[...] = _segsum(r * (kd0 + kd1) * rk_ref[...], bd) * v


def _bd2(w):
    z = jnp.zeros_like(w[0])
    return jnp.concatenate([jnp.concatenate([w[0], z], axis=1), jnp.concatenate([z, w[1]], axis=1)], axis=0)


def _r7prep(pc, lp, nb, t):
    tt = 256
    nt = t // tt
    nrow8 = t // 8

    def full(shape):
        return pl.BlockSpec(shape, lambda b, i: (0,) * len(shape))

    def rowspec(w):
        return pl.BlockSpec((tt, w), lambda b, i: (b * nt + i, 0))

    outs = [C_W, C_W, C_W, 2 * C_W, 2 * C_W, 2 * C_W, C_W, C_W]
    return pl.pallas_call(
        functools.partial(_r7prep_kernel, nt=nt),
        grid=(nb, nt),
        in_specs=[rowspec(C_IN),
                  pl.BlockSpec((8, C_IN), lambda b, i: (b * nrow8 + jnp.maximum(i * (tt // 8) - 1, 0), 0)),
                  pl.BlockSpec((8, C_IN), lambda b, i: (b * nrow8 + jnp.minimum((i + 1) * (tt // 8), nrow8 - 1), 0)),
                  full((3, C_IN)), full((1, C_W)), full((1, C_W)), full((1, C_W)),
                  full((1, 2 * C_W)), full((128, 2 * C_W)), full((1, 2 * C_W)), full((128, 2 * C_W)),
                  full((128, C_W)), full((SEG_W, SEG_W))],
        out_specs=[rowspec(w) for w in outs],
        out_shape=[jax.ShapeDtypeStruct((nb * t, w), F32) for w in outs],
        scratch_shapes=[pltpu.VMEM((tt + 16, C_IN), F32)],
        compiler_params=_cp("parallel", "parallel"),
        name="r7prep",
    )(pc, pc, pc, lp['r7_conv'], lp['r7_kk'].reshape(1, C_W), lp['r7_ka'].reshape(1, C_W),
      lp['r7_rk'].reshape(1, C_W), lp['r7_w0'].reshape(1, 2 * C_W), _bd2(lp['r7_w2']),
      lp['r7_a0'].reshape(1, 2 * C_W), _bd2(lp['r7_a2']), lp['r7_g2'], _block_diag_ones())


def _cumsum_rows(tri, x):
    t = tri.astype(BF16)
    hi = x.astype(BF16)
    rest = x - hi.astype(F32)
    mid = rest.astype(BF16)
    lo = (rest - mid.astype(F32)).astype(BF16)
    return _dot(t, hi) + (_dot(t, mid) + _dot(t, lo))


def _mm(a, b, prec):
    if prec is None:
        return _dot(a.astype(BF16), b.astype(BF16))
    return _dot(a, b, prec)


def _mm_nt(a, b, prec):
    if prec is None:
        return _dot_nt(a.astype(BF16), b.astype(BF16))
    return _dot_nt(a, b, prec)


def _scan_kernel(rf_ref, vf_ref, kkf_ref, lwf_ref, betaf_ref, kdf_ref,
                 rb_ref, vb_ref, kkb_ref, lwb_ref, betab_ref, kdb_ref, s0_ref,
                 yf_ref, yb_ref, sf_ref, st_ref, *, nc):
    c = pl.program_id(1)
    cl = SCAN_C

    @pl.when(c == 0)
    def _():
        st_ref[...] = s0_ref[...]

    ti = lax.broadcasted_iota(jnp.int32, (cl, cl), 0)
    si = lax.broadcasted_iota(jnp.int32, (cl, cl), 1)
    eye = (ti == si).astype(F32)
    incl = [si <= ti, si >= ti]
    strict = [si < ti, si > ti]

    def hs(x, h):
        return x[:, h * HD:(h + 1) * HD]

    pre = {}
    for bb in range(SCAN_NB):
        for d, (r_ref, v_ref, kk_ref, lw_ref, beta_ref, kd_ref) in enumerate(
                [(rf_ref, vf_ref, kkf_ref, lwf_ref, betaf_ref, kdf_ref),
                 (rb_ref, vb_ref, kkb_ref, lwb_ref, betab_ref, kdb_ref)]):
            lw = lw_ref[bb]
            cum = _cumsum_rows(incl[d], lw)
            e_neg = jnp.exp(-cum)
            pre[bb, d] = dict(r_hat=r_ref[bb] * jnp.exp(cum), a_hat=-kk_ref[bb] * jnp.exp(cum - lw),
                              b_til=beta_ref[bb] * e_neg, k_til=kd_ref[bb] * e_neg,
                              wtot=jnp.exp(jnp.sum(lw, axis=0, keepdims=True)), vv=v_ref[bb])

    chains = [(bb, d, h) for bb in range(SCAN_NB) for d in range(2) for h in range(C_HEADS)]
    n_ch = range(len(chains))

    def part(name, i):
        bb, d, h = chains[i]
        return hs(pre[bb, d][name], h)

    def dirn(i):
        return chains[i][1]

    ar = [jnp.concatenate([part('a_hat', i), part('r_hat', i)], axis=0) for i in n_ch]
    bk = [jnp.concatenate([part('b_til', i), part('k_til', i)], axis=0) for i in n_ch]
    m = [_mm_nt(ar[i], bk[i], PREC_M) for i in n_ch]
    l_k = [jnp.where(strict[dirn(i)], m[i][0:cl, cl:2 * cl], 0.0) for i in n_ch]
    r_b = [jnp.where(incl[dirn(i)], m[i][cl:2 * cl, 0:cl], 0.0) for i in n_ch]
    r_k = [jnp.where(incl[dirn(i)], m[i][cl:2 * cl, cl:2 * cl], 0.0) for i in n_ch]
    p = [jnp.where(strict[dirn(i)], m[i][0:cl, 0:cl], 0.0) for i in n_ch]
    t_inv = [eye + p[i] for i in n_ch]
    n = 2
    while n < cl:
        p = [_mm(p[i], p[i], PREC_INV) for i in n_ch]
        t_inv = [t_inv[i] + _mm(t_inv[i], p[i], PREC_INV) for i in n_ch]
        n *= 2
    x1 = [_mm(l_k[i], part('vv', i), PREC_M) for i in n_ch]
    y0 = [_mm(r_k[i], part('vv', i), PREC_M) for i in n_ch]
    ua = [_mm(t_inv[i], jnp.concatenate([x1[i], part('a_hat', i)], axis=1), PREC_INV) for i in n_ch]
    s0 = [st_ref[d, bb, h] for bb, d, h in chains]
    as0 = [_mm_nt(jnp.concatenate([ua[i][:, HD:2 * HD], part('r_hat', i)], axis=0), s0[i], PREC_SEQ)
           for i in n_ch]
    u = [ua[i][:, 0:HD] + as0[i][0:cl] for i in n_ch]
    y = [y0[i] + as0[i][cl:2 * cl] + _mm(r_b[i], u[i], PREC_SEQ) for i in n_ch]
    for i in n_ch:
        bb, d, h = chains[i]
        (yf_ref, yb_ref)[d][bb, :, h * HD:(h + 1) * HD] = y[i]
    for i in n_ch:
        bb, d, h = chains[i]
        uv_t = jnp.concatenate([u[i], part('vv', i)], axis=0).T
        st_ref[d, bb, h] = (s0[i] + _mm(uv_t, bk[i], PREC_SEQ)) * part('wtot', i)

    @pl.when(c == nc - 1)
    def _():
        sf_ref[...] = st_ref[...]


def _scan(prep, s0, nb, t):
    nc = t // SCAN_C
    assert nb % SCAN_NB == 0
    r, v, kk, lw, beta, kd = [a.reshape(nb, t, a.shape[-1]) for a in prep]

    def chunk(d, c):
        return nc - 1 - c if d else c

    def specs(d):
        shared = pl.BlockSpec((SCAN_NB, SCAN_C, C_W), lambda b, c: (b, chunk(d, c), 0))
        perdir = pl.BlockSpec((SCAN_NB, SCAN_C, C_W), lambda b, c: (b, chunk(d, c), d))
        return [shared, shared, shared, perdir, perdir, perdir]

    st_spec = pl.BlockSpec((2, SCAN_NB, C_HEADS, HD, HD), lambda b, c: (0, b, 0, 0, 0))
    yf, yb, sf = pl.pallas_call(
        functools.partial(_scan_kernel, nc=nc),
        grid=(nb // SCAN_NB, nc),
        in_specs=specs(0) + specs(1) + [st_spec],
        out_specs=[pl.BlockSpec((SCAN_NB, SCAN_C, C_W), lambda b, c: (b, chunk(0, c), 0)),
                   pl.BlockSpec((SCAN_NB, SCAN_C, C_W), lambda b, c: (b, chunk(1, c), 0)), st_spec],
        out_shape=[jax.ShapeDtypeStruct((nb, t, C_W), F32), jax.ShapeDtypeStruct((nb, t, C_W), F32),
                   jax.ShapeDtypeStruct((2, nb, C_HEADS, HD, HD), F32)],
        scratch_shapes=[pltpu.VMEM((2, SCAN_NB, C_HEADS, HD, HD), F32)],
        compiler_params=_cp("parallel", "arbitrary"),
        name="r7scan",
    )(r, v, kk, lw, beta, kd, r, v, kk, lw, beta, kd, s0)
    return (yf.reshape(nb * t, C_W), yb.reshape(nb * t, C_W)), sf


def _outproj_kernel(x_ref, oa_ref, ob_ref, y0_ref, y1_ref, bonus_ref, g_ref, lnw_ref, lnb_ref, bd_ref, w_ref,
                    gate_ref, o_ref):
    bd = bd_ref[...]
    y = y0_ref[...] + y1_ref[...]
    mu = _segsum(y, bd) * (1.0 / HD)
    yc = y - mu
    var = _segsum(yc * yc, bd) * (1.0 / HD)
    yn = yc * lax.rsqrt(var + GN_EPS) * lnw_ref[...] + lnb_ref[...]
    oc = ((yn + bonus_ref[...]) * g_ref[...]).astype(BF16)
    acc = (_dot(oa_ref[...], w_ref[0:256, :]) + _dot(ob_ref[...], w_ref[256:512, :])
           + _dot(oc, w_ref[512:1024, :]))
    o_ref[...] = x_ref[...] + gate_ref[...] * acc


def _outproj(x2, oa, ob, y, bonus, g, lp, w_out_bf, mod3, mod_row_of_batch, nb, t):
    tm = 256
    nt = t // tm

    def rowspec(w):
        return pl.BlockSpec((tm, w), lambda b, i: (b * nt + i, 0))

    def full(shape):
        return pl.BlockSpec(shape, lambda b, i: (0,) * len(shape))

    return pl.pallas_call(
        _outproj_kernel,
        grid=(nb, nt),
        in_specs=[rowspec(D), rowspec(256), rowspec(256),
                  rowspec(C_W), rowspec(C_W),
                  rowspec(C_W), rowspec(C_W), full((1, C_W)), full((1, C_W)), full((SEG_W, SEG_W)), full((D, D)),
                  pl.BlockSpec((None, 1, D), lambda b, i: (mod_row_of_batch(b), 0, 2))],
        out_specs=rowspec(D),
        out_shape=jax.ShapeDtypeStruct((nb * t, D), F32),
        compiler_params=_cp("parallel", "parallel"),
        name="outproj",
    )(x2, oa, ob, y[0], y[1], bonus, g, lp['r7_lnw'].reshape(1, C_W), lp['r7_lnb'].reshape(1, C_W),
      _block_diag_ones(), w_out_bf, mod3)


def _peer_score_kernel(x_ref, g_ref, sc_ref, sh_ref, wq_ref, keys_ref, h_ref, st_ref):
    x = x_ref[...]
    y = x * lax.rsqrt(jnp.mean(x * x, axis=-1, keepdims=True) + EPS) * g_ref[...]
    h = (y * (1.0 + sc_ref[...]) + sh_ref[...]).astype(BF16)
    h_ref[...] = h
    q = _dot(h, wq_ref[...])
    half = P_QDIM // 2
    for hp in range(2 * P_HEADS):
        st_ref[hp] = _dot3(keys_ref[hp], q[:, hp * half:(hp + 1) * half], nt=True)


def _peer_scores(x2, norm_g, mod3, mod_row_of_batch, wq_bf, keys, nb, t):
    tm = 256
    nt = t // tm
    n = nb * t
    return pl.pallas_call(
        _peer_score_kernel,
        grid=(nb, nt),
        in_specs=[pl.BlockSpec((tm, D), lambda b, i: (b * nt + i, 0)),
                  pl.BlockSpec((1, D), lambda b, i: (0, 0)),
                  pl.BlockSpec((None, 1, D), lambda b, i: (mod_row_of_batch(b), 0, 4)),
                  pl.BlockSpec((None, 1, D), lambda b, i: (mod_row_of_batch(b), 0, 3)),
                  pl.BlockSpec((D, P_HEADS * P_QDIM), lambda b, i: (0, 0)),
                  pl.BlockSpec((2 * P_HEADS, P_NKEYS, P_QDIM // 2), lambda b, i: (0, 0, 0))],
        out_specs=[pl.BlockSpec((tm, D), lambda b, i: (b * nt + i, 0)),
                   pl.BlockSpec((2 * P_HEADS, P_NKEYS, tm), lambda b, i: (0, 0, b * nt + i))],
        out_shape=[jax.ShapeDtypeStruct((n, D), BF16),
                   jax.ShapeDtypeStruct((2 * P_HEADS, P_NKEYS, n), F32)],
        compiler_params=_cp("parallel", "parallel"),
        name="peer_scores",
    )(x2, norm_g.reshape(1, D), mod3, mod3, wq_bf, keys.reshape(2 * P_HEADS, P_NKEYS, P_QDIM // 2))


def _batcher_pairs(n):
    pairs = []
    p = 1
    while p < n:
        k = p
        while k >= 1:
            for j in range(k % p, n - k, 2 * k):
                for i in range(min(k, n - j - k)):
                    if (i + j) // (2 * p) == (i + j + k) // (2 * p):
                        pairs.append((i + j, i + j + k))
            k //= 2
        p *= 2
    return pairs


_SORT16 = _batcher_pairs(P_TOPK)
_BITONIC16 = [(i, i + d) for d in (8, 4, 2, 1) for i in range(P_TOPK) if i & d == 0]


def _compare_exchange(x, pairs):
    for i, j in pairs:
        x[i], x[j] = jnp.maximum(x[i], x[j]), jnp.minimum(x[i], x[j])
    return x


def _top16_sorted(slabs):
    x = _compare_exchange(list(slabs), _SORT16)
    for shift in (4, 2, 1):
        y = [pltpu.roll(x[P_TOPK - 1 - k], shift, 0) for k in range(P_TOPK)]
        x = [jnp.maximum(x[k], y[k]) for k in range(P_TOPK)]
        x = _compare_exchange(x, _BITONIC16)
    return x


def _peer_gate_tables(st_ref, rk_ref, be_ref, cnt_ref, al_ref, top_ref):
    nblk = st_ref.shape[-1] // LANES
    sub = lax.broadcasted_iota(jnp.int32, (8, LANES), 0)
    ninf = jnp.full((8, LANES), -jnp.inf, F32)

    def block(it, carry):
        h = it // nblk
        lb = it % nblk
        lanes = pl.ds(pl.multiple_of(lb * LANES, LANES), LANES)
        hf = lb // (PEER_HALF // LANES)
        hl = pl.ds(pl.multiple_of((lb % (PEER_HALF // LANES)) * LANES, LANES), LANES)
        for p in range(2):
            top = _top16_sorted([st_ref[2 * h + p, 8 * k:8 * k + 8, lanes] for k in range(P_TOPK)])
            for k in range(P_TOPK):
                top_ref[p * P_TOPK + k:p * P_TOPK + k + 1, lanes] = top[k][0:1]
        a16 = top_ref[0:P_TOPK, lanes]
        b16 = top_ref[P_TOPK:2 * P_TOPK, lanes]
        b8 = b16[0:8]
        cand = [a16[0:1] + b8, a16[0:1] + b16[8:16]]
        for p in range(2, 9):
            cand.append(jnp.where(sub < P_TOPK // p, a16[p - 1:p] + b8, -jnp.inf))
        cand.append(a16[8:16] + b16[0:1])
        best = _top16_sorted(cand + [ninf] * (P_TOPK - len(cand)))
        tau = best[P_TOPK - 1][0:1]
        z = jnp.ones_like(tau)
        for k in range(1, P_TOPK):
            z = z + jnp.exp(best[k][0:1] - best[0][0:1])
        s1 = st_ref[2 * h, :, lanes]
        s2 = st_ref[2 * h + 1, :, lanes]
        cnt = jnp.zeros_like(s1)
        rk = jnp.ones_like(s2)
        for q in range(P_TOPK):
            bq = b16[q:q + 1]
            theta = jnp.min(jnp.where(a16 + bq >= tau, a16, jnp.inf), axis=0, keepdims=True)
            cnt = jnp.where(s1 >= theta, q + 1.0, cnt)
            rk = jnp.where(bq > s2, q + 2.0, rk)
        cnt_ref[hf, h, :, hl] = cnt
        rk_ref[hf, h, :, hl] = rk.astype(BF16)
        al_ref[hf, h, :, hl] = jnp.exp(s1 - a16[0:1]) / z
        be_ref[hf, h, :, hl] = jnp.exp(s2 - b16[0:1]).astype(BF16)
        return carry

    lax.fori_loop(0, P_HEADS * nblk, block, 0)


def _peer_chunk(chunk, act_ref, rk_ref, be_ref, cnt_ref, al_ref, w_ref):
    nrow = w_ref.shape[0] // P_NKEYS
    for ii in range(nrow):
        i = chunk * nrow + ii
        wrow = None
        for h in range(P_HEADS):
            cnt = cnt_ref[h, pl.ds(i, 1), :].astype(BF16)
            al = al_ref[h, pl.ds(i, 1), :].astype(BF16)
            term = jnp.where(rk_ref[h] <= cnt, al * be_ref[h], 0.0)
            wrow = term if wrow is None else wrow + term
        rows = slice(ii * P_NKEYS, (ii + 1) * P_NKEYS)
        act = act_ref[rows, :]
        gl = 0.5 * act * (1.0 + lax.erf(act * (2.0 ** -0.5)))
        w_ref[rows, :] = wrow * gl.astype(BF16)


def _peer_dense_kernel(x_ref, h_ref, st_ref, u_ref, vt_ref, gate_ref, o_ref,
                       rk_ref, be_ref, cnt_ref, al_ref, top_ref, act0_ref, act1_ref, w_ref, acc_ref, *, ne):
    e = pl.program_id(1)
    halves = range(w_ref.shape[0])

    def rows(hf):
        return slice(hf * PEER_HALF, (hf + 1) * PEER_HALF)

    @pl.when(e == 0)
    def _():
        for hf in halves:
            act0_ref[hf] = _dot_nt(u_ref[...], h_ref[rows(hf), :])
        _peer_gate_tables(st_ref, rk_ref, be_ref, cnt_ref, al_ref, top_ref)
        acc_ref[...] = jnp.zeros_like(acc_ref)

    def step(cur_ref, nxt_ref):
        for hf in halves:
            nxt_ref[hf] = _dot_nt(u_ref[...], h_ref[rows(hf), :])
            _peer_chunk(e - 1, cur_ref.at[hf], rk_ref.at[hf], be_ref.at[hf], cnt_ref.at[hf], al_ref.at[hf],
                        w_ref.at[hf])
            acc_ref[hf] += _dot(vt_ref[...], w_ref[hf])

    @pl.when(e % 2 == 1)
    def _():
        step(act0_ref, act1_ref)

    @pl.when((e > 0) & (e % 2 == 0))
    def _():
        step(act1_ref, act0_ref)

    @pl.when(e == ne)
    def _():
        for hf in halves:
            o_ref[rows(hf), :] = x_ref[rows(hf), :] + gate_ref[...] * acc_ref[hf].T


def _peer_dense(x2, h2, st, u_bf, vt_bf, mod3, mod_row_of_batch, nb, t):
    tm = 512 if t % 512 == 0 else 256
    nh = tm // PEER_HALF
    ec = PEER_EC
    nt = t // tm
    ne = P_EXPERTS // ec
    return pl.pallas_call(
        functools.partial(_peer_dense_kernel, ne=ne),
        grid=(nb * nt, ne + 1),
        in_specs=[pl.BlockSpec((tm, D), lambda i, e: (i, 0)),
                  pl.BlockSpec((tm, D), lambda i, e: (i, 0)),
                  pl.BlockSpec((2 * P_HEADS, P_NKEYS, tm), lambda i, e: (0, 0, i)),
                  pl.BlockSpec((ec, D), lambda i, e: (jnp.minimum(e, ne - 1), 0)),
                  pl.BlockSpec((None, D, ec), lambda i, e: (jnp.maximum(e - 1, 0), 0, 0)),
                  pl.BlockSpec((None, 1, D), lambda i, e: (mod_row_of_batch(i // nt), 0, 5))],
        out_specs=pl.BlockSpec((tm, D), lambda i, e: (i, 0)),
        out_shape=jax.ShapeDtypeStruct((nb * t, D), F32),
        scratch_shapes=[pltpu.VMEM((nh, P_HEADS, P_NKEYS, PEER_HALF), BF16),
                        pltpu.VMEM((nh, P_HEADS, P_NKEYS, PEER_HALF), BF16),
                        pltpu.VMEM((nh, P_HEADS, P_NKEYS, PEER_HALF), F32),
                        pltpu.VMEM((nh, P_HEADS, P_NKEYS, PEER_HALF), F32),
                        pltpu.VMEM((2 * P_TOPK, tm), F32),
                        pltpu.VMEM((nh, ec, PEER_HALF), F32),
                        pltpu.VMEM((nh, ec, PEER_HALF), F32),
                        pltpu.VMEM((nh, ec, PEER_HALF), BF16),
                        pltpu.VMEM((nh, D, PEER_HALF), F32)],
        compiler_params=pltpu.CompilerParams(dimension_semantics=("parallel", "arbitrary"),
                                             vmem_limit_bytes=PEER_VMEM_LIMIT),
        name="peer_dense",
    )(x2, h2, st, u_bf, vt_bf, mod3)


def _peer(x2, norm_g, mod3, mod_row_of_batch, wq_bf, keys, u_bf, vt_bf, nb, t):
    h2, st = _peer_scores(x2, norm_g, mod3, mod_row_of_batch, wq_bf, keys, nb, t)
    return _peer_dense(x2, h2, st, u_bf, vt_bf, mod3, mod_row_of_batch, nb, t)


def kernel(x, c, ctx, c_ctx, norm_mix, norm_ffn, w_mod, b_mod, w_in, w_out, a_qnorm, a_knorm, a_sink, b_qnorm,
           b_knorm, b_rpb, r7_conv, r7_w0, r7_w2, r7_a0, r7_a2, r7_g2, r7_kk, r7_ka, r7_rk, r7_lnw, r7_lnb,
           peer_wq, peer_keys, peer_u, peer_v):
    nb, s, _ = x.shape
    lc = ctx.shape[1]
    depth = w_in.shape[0]
    assert nb < 16 and nb % SCAN_NB == 0 and s % 512 == 0 and lc % 256 == 0
    rows = s // GRID_W

    cc = jnp.zeros((16, D), F32).at[:nb].set(c).at[nb].set(c_ctx)
    mod = _modulation(cc, w_mod, b_mod)
    rope_tabs = _rope_tables(s)
    lat_row = lambda b: b
    ctx_row = lambda b: nb

    x_lat = x.reshape(nb * s, D)
    x_ctx = ctx.reshape(nb * lc, D)
    scale = HD ** -0.5
    for l in range(depth):
        with_ctx = l < depth - 1
        mod3 = mod[l].reshape(16, 1, 6 * D)
        lp = {'r7_conv': r7_conv[l], 'r7_w0': r7_w0[l], 'r7_w2': r7_w2[l], 'r7_a0': r7_a0[l], 'r7_a2': r7_a2[l],
              'r7_g2': r7_g2[l], 'r7_kk': r7_kk[l], 'r7_ka': r7_ka[l], 'r7_rk': r7_rk[l], 'r7_lnw': r7_lnw[l],
              'r7_lnb': r7_lnb[l]}
        w_in_bf = w_in[l].astype(BF16)
        w_out_bf = w_out[l].astype(BF16)
        gain_a = jnp.concatenate([jnp.tile(a_qnorm[l] * scale, A_HEADS), jnp.tile(a_knorm[l], A_KV)]).reshape(1, 384)
        gain_b = jnp.concatenate([jnp.tile(b_qnorm[l] * scale, B_HEADS), jnp.tile(b_knorm[l], B_HEADS)]).reshape(1, 512)

        ab_lat, pc_lat = _inproj(x_lat, mod3, lat_row, norm_mix[l], w_in_bf, gain_a, gain_b, rope_tabs, nb, s)
        ab_ctx, pc_ctx = _inproj(x_ctx, mod3, ctx_row, norm_mix[l], w_in_bf, gain_a, gain_b, None, nb, lc)

        o_a = _attn_a(ab_lat, ab_ctx, a_sink[l], nb, s, lc)
        o_b = _attn_b(ab_lat, ab_ctx, _na_bias_tables(b_rpb[l], rows), nb, s, lc)

        prep_ctx = _r7prep(pc_ctx, lp, nb, lc)
        prep_lat = _r7prep(pc_lat, lp, nb, s)
        zero_state = jnp.zeros((2, nb, C_HEADS, HD, HD), F32)
        y_ctx, s_ctx = _scan(prep_ctx[:6], zero_state, nb, lc)
        y_lat, _ = _scan(prep_lat[:6], s_ctx, nb, s)

        x_lat = _outproj(x_lat, o_a, o_b, y_lat, prep_lat[7], prep_lat[6], lp, w_out_bf, mod3, lat_row, nb, s)
        wq_bf = peer_wq[l].astype(BF16)
        u_bf = peer_u[l].astype(BF16)
        vt_bf = peer_v[l].astype(BF16).reshape(P_EXPERTS // PEER_EC, PEER_EC, D).transpose(0, 2, 1)
        x_lat = _peer(x_lat, norm_ffn[l], mod3, lat_row, wq_bf, peer_keys[l], u_bf, vt_bf, nb, s)
        if with_ctx:
            o_ac, o_bc = _attn_ctx(ab_ctx, a_sink[l], nb, lc)
            x_ctx = _outproj(x_ctx, o_ac, o_bc, y_ctx, prep_ctx[7], prep_ctx[6], lp, w_out_bf, mod3, ctx_row, nb, lc)
            x_ctx = _peer(x_ctx, norm_ffn[l], mod3, ctx_row, wq_bf, peer_keys[l], u_bf, vt_bf, nb, lc)
    return x_lat.reshape(nb, s, D)
```

```python
import functools

import numpy as np
import jax
import jax.numpy as jnp
from jax import lax
from jax.experimental import pallas as pl
from jax.experimental.pallas import tpu as pltpu

F32 = jnp.float32
BF16 = jnp.bfloat16
HI = lax.Precision.HIGHEST

D = 1024
DEPTH = 2
GRID_W = 64
HD = 64
LANES = 128
EPS = 1e-6
NEG = -1e30
A_HEADS, A_KV, A_BLOCK, A_WINDOW = 4, 2, 128, 128
A_STEP_BLOCKS = 2
B_HEADS, NA_ROWS, NA_COLS = 4, 8, 16
NA_STEP_ROWS = 8
C_HEADS = 8
C_W = 512
C_IN = 1920
AB_W = 1280
IN_W = AB_W + C_IN
GN_EPS = 64e-5
DECAY_SCALE = 0.6065306597126334
ROPE_BASE = 10000.0
P_HEADS, P_NKEYS, P_QDIM, P_TOPK = 8, 128, 256, 16
P_EXPERTS = P_NKEYS * P_NKEYS
SCAN_C = 64
SCAN_NB = 2
VMEM_LIMIT = 48 * 1024 * 1024
PEER_VMEM_LIMIT = 56 * 1024 * 1024
PEER_EC = 1024
PEER_HALF = 256


def _cp(*sem):
    return pltpu.CompilerParams(dimension_semantics=sem, vmem_limit_bytes=VMEM_LIMIT)


def _dot(a, b, prec=None):
    return jnp.dot(a, b, precision=prec, preferred_element_type=F32)


def _dot_nt(a, b, prec=None):
    return lax.dot_general(a, b, (((1,), (1,)), ((), ())), precision=prec, preferred_element_type=F32)


SEG_W = 256


def _block_diag_ones():
    i = np.arange(SEG_W) // HD
    return jnp.asarray((i[:, None] == i[None, :]).astype(np.float32), BF16)


def _split2(x):
    hi = x.astype(BF16)
    return hi, (x - hi.astype(F32)).astype(BF16)


def _segsum(x, bd):
    hi, lo = _split2(x)
    outs = []
    for g0 in range(0, x.shape[1], SEG_W):
        w = min(SEG_W, x.shape[1] - g0)
        outs.append(_dot(hi[:, g0:g0 + w], bd[0:w, 0:w]) + _dot(lo[:, g0:g0 + w], bd[0:w, 0:w]))
    return outs[0] if len(outs) == 1 else jnp.concatenate(outs, axis=1)


def _dot3(a, b, nt=False):
    f = _dot_nt if nt else _dot
    a_hi, a_lo = _split2(a)
    b_hi, b_lo = _split2(b)
    return f(a_hi, b_hi) + (f(a_lo, b_hi) + f(a_hi, b_lo))


def _mod_kernel(c_ref, w_ref, b_ref, o_ref):
    c = c_ref[...]
    s = c * jax.nn.sigmoid(c)
    o_ref[...] = _dot(s, w_ref[...], HI) + b_ref[...]


def _modulation(cc, w_mod, b_mod):
    L, _, n = w_mod.shape
    tn = 2048
    return pl.pallas_call(
        _mod_kernel,
        grid=(L, n // tn),
        in_specs=[pl.BlockSpec((16, D), lambda l, j: (0, 0)),
                  pl.BlockSpec((None, D, tn), lambda l, j: (l, 0, j)),
                  pl.BlockSpec((None, 1, tn), lambda l, j: (l, 0, j))],
        out_specs=pl.BlockSpec((None, 16, tn), lambda l, j: (l, 0, j)),
        out_shape=jax.ShapeDtypeStruct((L, 16, n), F32),
        compiler_params=_cp("parallel", "parallel"),
        name="modulation",
    )(cc, w_mod, b_mod.reshape(L, 1, n))


def _swap16(x):
    n = x.shape[-1]
    lane = lax.broadcasted_iota(jnp.int32, x.shape, 1)
    fwd = pltpu.roll(x, n - 16, 1)
    bwd = pltpu.roll(x, 16, 1)
    return jnp.where((lane % 32) < 16, fwd, bwd)


def _head_rms(x, bd, gain):
    ss = _segsum(x * x, bd)
    return x * lax.rsqrt(ss * (1.0 / HD) + EPS) * gain


def _inproj_kernel(*refs, rope):
    if rope:
        (x_ref, g_ref, sc_ref, sh_ref, w_ref, ga_ref, gb_ref, bd_ref, cos_ref, sin_ref,
         ab_ref, c_ref) = refs
    else:
        (x_ref, g_ref, sc_ref, sh_ref, w_ref, ga_ref, gb_ref, bd_ref, ab_ref, c_ref) = refs
    x = x_ref[...]
    y = x * lax.rsqrt(jnp.mean(x * x, axis=-1, keepdims=True) + EPS) * g_ref[...]
    h = y * (1.0 + sc_ref[...]) + sh_ref[...]
    acc = _dot(h.astype(BF16), w_ref[...])
    bd = bd_ref[...]
    qa = _head_rms(acc[:, 0:384], bd, ga_ref[...])
    if rope:
        qa = qa * cos_ref[...] + _swap16(qa) * sin_ref[...]
    qb = _head_rms(acc[:, 512:1024], bd, gb_ref[...])
    ab_ref[:, 0:384] = qa.astype(BF16)
    ab_ref[:, 384:512] = acc[:, 384:512].astype(BF16)
    ab_ref[:, 512:1024] = qb.astype(BF16)
    ab_ref[:, 1024:1280] = acc[:, 1024:1280].astype(BF16)
    c_ref[...] = acc[:, AB_W:IN_W]


def _inproj(x2, mod3, mod_row_of_batch, norm_g, w_in_bf, gain_a, gain_b, rope_tabs, nb, t):
    tm = 256
    nt = t // tm
    rope = rope_tabs is not None
    in_specs = [
        pl.BlockSpec((tm, D), lambda b, i: (b * nt + i, 0)),
        pl.BlockSpec((1, D), lambda b, i: (0, 0)),
        pl.BlockSpec((None, 1, D), lambda b, i: (mod_row_of_batch(b), 0, 1)),
        pl.BlockSpec((None, 1, D), lambda b, i: (mod_row_of_batch(b), 0, 0)),
        pl.BlockSpec((D, IN_W), lambda b, i: (0, 0)),
        pl.BlockSpec((1, 384), lambda b, i: (0, 0)),
        pl.BlockSpec((1, 512), lambda b, i: (0, 0)),
        pl.BlockSpec((SEG_W, SEG_W), lambda b, i: (0, 0)),
    ]
    args = [x2, norm_g.reshape(1, D), mod3, mod3, w_in_bf, gain_a, gain_b,
            _block_diag_ones()]
    if rope:
        in_specs += [pl.BlockSpec((tm, 384), lambda b, i: (i, 0)),
                     pl.BlockSpec((tm, 384), lambda b, i: (i, 0))]
        args += list(rope_tabs)
    return pl.pallas_call(
        functools.partial(_inproj_kernel, rope=rope),
        grid=(nb, nt),
        in_specs=in_specs,
        out_specs=[pl.BlockSpec((tm, AB_W), lambda b, i: (b * nt + i, 0)),
                   pl.BlockSpec((tm, C_IN), lambda b, i: (b * nt + i, 0))],
        out_shape=[jax.ShapeDtypeStruct((nb * t, AB_W), BF16),
                   jax.ShapeDtypeStruct((nb * t, C_IN), F32)],
        compiler_params=_cp("parallel", "parallel"),
        name="inproj_rope" if rope else "inproj_ctx",
    )(*args)


def _rope_tables(s):
    tok = np.arange(s)
    inv = ROPE_BASE ** (-np.arange(0, 32, 2) / 32.0)
    ar = (tok // GRID_W)[:, None] * inv[None]
    ac = (tok % GRID_W)[:, None] * inv[None]
    cos = np.concatenate([np.cos(ar), np.cos(ar), np.cos(ac), np.cos(ac)], axis=1)
    sin = np.concatenate([-np.sin(ar), np.sin(ar), -np.sin(ac), np.sin(ac)], axis=1)
    return (jnp.asarray(np.tile(cos, (1, 6)), F32), jnp.asarray(np.tile(sin, (1, 6)), F32))


def _softmax_pv(s, v, sink=None):
    m = jnp.max(s, axis=-1, keepdims=True)
    if sink is not None:
        m = jnp.maximum(m, sink)
    p = jnp.exp(s - m)
    den = jnp.sum(p, axis=-1, keepdims=True)
    if sink is not None:
        den = den + jnp.exp(sink - m)
    return _dot(p.astype(BF16), v) / den


def _attn_a_kernel(q_ref, k0_ref, k1_ref, k2_ref, k3_ref, v0_ref, v1_ref, v2_ref, v3_ref, kc_ref, vc_ref,
                   sink_ref, o_ref, *, nblk):
    kb = [k0_ref[...], k1_ref[...], k2_ref[...], k3_ref[...]]
    vb = [v0_ref[...], v1_ref[...], v2_ref[...], v3_ref[...]]
    kc = kc_ref[...]
    vc = vc_ref[...]
    nk = 3 * A_BLOCK + kc.shape[0]
    g = A_HEADS // A_KV
    row = lax.broadcasted_iota(jnp.int32, (g * A_BLOCK, nk), 0) % A_BLOCK
    col = lax.broadcasted_iota(jnp.int32, (g * A_BLOCK, nk), 1)
    rel = col - A_BLOCK - row
    band = (jnp.abs(rel) <= A_WINDOW) | (col >= 3 * A_BLOCK)
    chains, ok, k_all, v_all = [], [], [], []
    for j in range(A_STEP_BLOCKS):
        n = pl.program_id(1) * A_STEP_BLOCKS + j
        ok.append(band & ((n > 0) | (col >= A_BLOCK)) & ((n < nblk - 1) | (col < 2 * A_BLOCK) | (col >= 3 * A_BLOCK)))
        k_all.append(jnp.concatenate(kb[j:j + 3] + [kc], axis=0))
        v_all.append(jnp.concatenate(vb[j:j + 3] + [vc], axis=0))
        chains += [(j, hk) for hk in range(A_KV)]

    def qg(j, hk):
        rows = slice(j * A_BLOCK, (j + 1) * A_BLOCK)
        return jnp.concatenate([q_ref[rows, (hk * g + i) * HD:(hk * g + i + 1) * HD] for i in range(g)], axis=0)

    sink = [jnp.concatenate([jnp.broadcast_to(sink_ref[:, hk * g + i:hk * g + i + 1], (A_BLOCK, 1))
                             for i in range(g)], axis=0) for hk in range(A_KV)]
    s = [jnp.where(ok[j], _dot_nt(qg(j, hk), k_all[j][:, hk * HD:(hk + 1) * HD]), NEG) for j, hk in chains]
    m = [jnp.maximum(jnp.max(s[i], axis=-1, keepdims=True), sink[hk]) for i, (j, hk) in enumerate(chains)]
    p = [jnp.exp(s[i] - m[i]) for i in range(len(chains))]
    den = [jnp.sum(p[i], axis=-1, keepdims=True) + jnp.exp(sink[hk] - m[i]) for i, (j, hk) in enumerate(chains)]
    o = [_dot(p[i].astype(BF16), v_all[j][:, hk * HD:(hk + 1) * HD]) / den[i] for i, (j, hk) in enumerate(chains)]
    for i, (j, hk) in enumerate(chains):
        for gi in range(g):
            hq = hk * g + gi
            o_ref[j * A_BLOCK:(j + 1) * A_BLOCK, hq * HD:(hq + 1) * HD] = (
                o[i][gi * A_BLOCK:(gi + 1) * A_BLOCK].astype(BF16))


def _attn_a(ab_lat, ab_ctx, sink, nb, s, lc):
    nblk = s // A_BLOCK
    steps = nblk // A_STEP_BLOCKS

    def kv(col, d):
        return pl.BlockSpec(
            (A_BLOCK, 128), lambda b, i: (b * nblk + jnp.clip(i * A_STEP_BLOCKS + d, 0, nblk - 1), col))

    tq = A_STEP_BLOCKS * A_BLOCK
    return pl.pallas_call(
        functools.partial(_attn_a_kernel, nblk=nblk),
        grid=(nb, steps),
        in_specs=[pl.BlockSpec((tq, 256), lambda b, i: (b * steps + i, 0)),
                  kv(2, -1), kv(2, 0), kv(2, 1), kv(2, 2), kv(3, -1), kv(3, 0), kv(3, 1), kv(3, 2),
                  pl.BlockSpec((lc, 128), lambda b, i: (b, 2)),
                  pl.BlockSpec((lc, 128), lambda b, i: (b, 3)),
                  pl.BlockSpec((1, A_HEADS), lambda b, i: (0, 0))],
        out_specs=pl.BlockSpec((tq, 256), lambda b, i: (b * steps + i, 0)),
        out_shape=jax.ShapeDtypeStruct((nb * s, 256), BF16),
        compiler_params=_cp("parallel", "parallel"),
        name="attn_a",
    )(ab_lat, *([ab_lat] * 8), ab_ctx, ab_ctx, sink.reshape(1, A_HEADS))


def _attn_b_kernel(q_ref, k_ref, v_ref, kc_ref, vc_ref, bias_ref, o_ref, *, rows):
    nwin = NA_ROWS * GRID_W
    kc = kc_ref[...]
    vc = vc_ref[...]
    q, kw, vw, tab = [], [], [], []
    for rr in range(NA_STEP_ROWS):
        r = pl.program_id(1) * NA_STEP_ROWS + rr
        rs = jnp.clip(r - NA_ROWS // 2, 0, rows - NA_ROWS)
        start = pl.multiple_of(rs * GRID_W, GRID_W)
        tab.append(rs - r + NA_ROWS - 1)
        q.append(q_ref[rr * GRID_W:(rr + 1) * GRID_W, :])
        kw.append(k_ref[pl.ds(start, nwin), :])
        vw.append(v_ref[pl.ds(start, nwin), :])
    chains = [(rr, h) for rr in range(NA_STEP_ROWS) for h in range(B_HEADS)]

    def hs(x, h):
        return x[:, h * HD:(h + 1) * HD]

    s_loc = [_dot_nt(hs(q[rr], h), hs(kw[rr], h)) + bias_ref[tab[rr], h] for rr, h in chains]
    s_ctx = [_dot_nt(hs(q[rr], h), hs(kc, h)) for rr, h in chains]
    m = [jnp.maximum(jnp.max(a, axis=-1, keepdims=True), jnp.max(b, axis=-1, keepdims=True))
         for a, b in zip(s_loc, s_ctx)]
    p_loc = [jnp.exp(a - mm) for a, mm in zip(s_loc, m)]
    p_ctx = [jnp.exp(b - mm) for b, mm in zip(s_ctx, m)]
    den = [jnp.sum(a, axis=-1, keepdims=True) + jnp.sum(b, axis=-1, keepdims=True) for a, b in zip(p_loc, p_ctx)]
    o = [(_dot(p_loc[i].astype(BF16), hs(vw[rr], h)) + _dot(p_ctx[i].astype(BF16), hs(vc, h))) / den[i]
         for i, (rr, h) in enumerate(chains)]
    for i, (rr, h) in enumerate(chains):
        o_ref[rr * GRID_W:(rr + 1) * GRID_W, h * HD:(h + 1) * HD] = o[i].astype(BF16)


def _na_bias_tables(rpb, rows):
    kr = NA_ROWS
    qc = np.arange(GRID_W)
    win_start = np.clip(qc - NA_COLS // 2, 0, GRID_W - NA_COLS)
    kcol = np.arange(GRID_W)
    valid = (kcol[None, :] >= win_start[:, None]) & (kcol[None, :] < win_start[:, None] + NA_COLS)
    pad = GRID_W - NA_COLS
    rp = jnp.pad(rpb.astype(F32), ((0, 0), (0, 0), (pad, pad)))
    toe = jnp.stack([rp[:, :, GRID_W - 1 - q:2 * GRID_W - 1 - q] for q in range(GRID_W)], axis=2)
    toe = jnp.where(jnp.asarray(valid[None, None]), toe, NEG)
    tabs = [toe[:, off:off + kr].transpose(0, 2, 1, 3).reshape(rpb.shape[0], GRID_W, kr * GRID_W)
            for off in range(kr)]
    return jnp.stack(tabs, axis=0)


def _attn_b(ab_lat, ab_ctx, bias_tabs, nb, s, lc):
    rows = s // GRID_W
    steps = rows // NA_STEP_ROWS
    tq = NA_STEP_ROWS * GRID_W
    return pl.pallas_call(
        functools.partial(_attn_b_kernel, rows=rows),
        grid=(nb, steps),
        in_specs=[pl.BlockSpec((tq, 256), lambda b, r: (b * steps + r, 2)),
                  pl.BlockSpec((s, 256), lambda b, r: (b, 3)),
                  pl.BlockSpec((s, 256), lambda b, r: (b, 4)),
                  pl.BlockSpec((lc, 256), lambda b, r: (b, 3)),
                  pl.BlockSpec((lc, 256), lambda b, r: (b, 4)),
                  pl.BlockSpec((NA_ROWS, B_HEADS, GRID_W, NA_ROWS * GRID_W), lambda b, r: (0, 0, 0, 0))],
        out_specs=pl.BlockSpec((tq, 256), lambda b, r: (b * steps + r, 0)),
        out_shape=jax.ShapeDtypeStruct((nb * s, 256), BF16),
        compiler_params=_cp("parallel", "arbitrary"),
        name="attn_b",
    )(ab_lat, ab_lat, ab_lat, ab_ctx, ab_ctx, bias_tabs)


def _attn_ctx_kernel(ab_ref, sink_ref, oa_ref, ob_ref):
    lc = ab_ref.shape[0]
    g = A_HEADS // A_KV
    for hk in range(A_KV):
        qg = jnp.concatenate([ab_ref[:, (hk * g + j) * HD:(hk * g + j + 1) * HD] for j in range(g)], axis=0)
        s = _dot_nt(qg, ab_ref[:, 256 + hk * HD:256 + (hk + 1) * HD])
        sink = jnp.concatenate(
            [jnp.broadcast_to(sink_ref[:, hk * g + j:hk * g + j + 1], (lc, 1)) for j in range(g)], axis=0)
        o = _softmax_pv(s, ab_ref[:, 384 + hk * HD:384 + (hk + 1) * HD], sink)
        for j in range(g):
            hq = hk * g + j
            oa_ref[:, hq * HD:(hq + 1) * HD] = o[j * lc:(j + 1) * lc].astype(BF16)
    for h in range(B_HEADS):
        s = _dot_nt(ab_ref[:, 512 + h * HD:512 + (h + 1) * HD], ab_ref[:, 768 + h * HD:768 + (h + 1) * HD])
        o = _softmax_pv(s, ab_ref[:, 1024 + h * HD:1024 + (h + 1) * HD])
        ob_ref[:, h * HD:(h + 1) * HD] = o.astype(BF16)


def _attn_ctx(ab_ctx, sink, nb, lc):
    return pl.pallas_call(
        _attn_ctx_kernel,
        grid=(nb,),
        in_specs=[pl.BlockSpec((lc, AB_W), lambda b: (b, 0)),
                  pl.BlockSpec((1, A_HEADS), lambda b: (0, 0))],
        out_specs=[pl.BlockSpec((lc, 256), lambda b: (b, 0)),
                   pl.BlockSpec((lc, 256), lambda b: (b, 0))],
        out_shape=[jax.ShapeDtypeStruct((nb * lc, 256), BF16),
                   jax.ShapeDtypeStruct((nb * lc, 256), BF16)],
        compiler_params=_cp("parallel"),
        name="attn_ctx",
    )(ab_ctx, sink.reshape(1, A_HEADS))


def _r7prep_kernel(x_ref, prev_ref, next_ref, cw_ref, kkw_ref, ka_ref, rk_ref, w0_ref, w2_ref, a0_ref, a2_ref,
                   g2_ref, bd_ref,
                   r_ref, v_ref, kk_ref, lw_ref, beta_ref, kd_ref, g_ref, bonus_ref, pad_ref, *, nt):
    i = pl.program_id(1)
    tt = x_ref.shape[0]
    x = x_ref[...]
    pad_ref[8:8 + tt, :] = x
    pad_ref[7:8, :] = jnp.where(i > 0, prev_ref[7:8, :], 0.0)
    pad_ref[8 + tt:9 + tt, :] = jnp.where(i < nt - 1, next_ref[0:1, :], 0.0)
    xc = pad_ref[7:7 + tt, :] * cw_ref[0:1, :] + x * cw_ref[1:2, :] + pad_ref[9:9 + tt, :] * cw_ref[2:3, :]
    r = xc[:, 0:512]
    k = xc[:, 512:1024]
    v = xc[:, 1024:1536]
    wd = xc[:, 1536:1664]
    ad = xc[:, 1664:1792]
    gd = xc[:, 1792:1920]
    bd = bd_ref[...]
    kkh = k * kkw_ref[...]
    kk = kkh / jnp.maximum(jnp.sqrt(_segsum(kkh * kkh, bd)), 1e-12)
    zw = w0_ref[...] + _dot3(jnp.tanh(wd), w2_ref[...])
    za = a0_ref[...] + _dot3(ad, a2_ref[...])
    a = jax.nn.sigmoid(za)
    ka = ka_ref[...]
    kd0 = k * (1.0 + (a[:, 0:512] - 1.0) * ka)
    kd1 = k * (1.0 + (a[:, 512:1024] - 1.0) * ka)
    r_ref[...] = r
    v_ref[...] = v
    kk_ref[...] = kk
    lw_ref[...] = -DECAY_SCALE * jax.nn.sigmoid(zw)
    beta_ref[:, 0:512] = kk * a[:, 0:512]
    beta_ref[:, 512:1024] = kk * a[:, 512:1024]
    kd_ref[:, 0:512] = kd0
    kd_ref[:, 512:1024] = kd1
    g_ref[...] = _dot3(jax.nn.sigmoid(gd), g2_ref[...])
    bonus_ref[...] = _segsum(r * (kd0 + kd1) * rk_ref[...], bd) * v


def _bd2(w):
    z = jnp.zeros_like(w[0])
    return jnp.concatenate([jnp.concatenate([w[0], z], axis=1), jnp.concatenate([z, w[1]], axis=1)], axis=0)


def _r7prep(pc, lp, nb, t):
    tt = 256
    nt = t // tt
    nrow8 = t // 8

    def full(shape):
        return pl.BlockSpec(shape, lambda b, i: (0,) * len(shape))

    def rowspec(w):
        return pl.BlockSpec((tt, w), lambda b, i: (b * nt + i, 0))

    outs = [C_W, C_W, C_W, 2 * C_W, 2 * C_W, 2 * C_W, C_W, C_W]
    return pl.pallas_call(
        functools.partial(_r7prep_kernel, nt=nt),
        grid=(nb, nt),
        in_specs=[rowspec(C_IN),
                  pl.BlockSpec((8, C_IN), lambda b, i: (b * nrow8 + jnp.maximum(i * (tt // 8) - 1, 0), 0)),
                  pl.BlockSpec((8, C_IN), lambda b, i: (b * nrow8 + jnp.minimum((i + 1) * (tt // 8), nrow8 - 1), 0)),
                  full((3, C_IN)), full((1, C_W)), full((1, C_W)), full((1, C_W)),
                  full((1, 2 * C_W)), full((128, 2 * C_W)), full((1, 2 * C_W)), full((128, 2 * C_W)),
                  full((128, C_W)), full((SEG_W, SEG_W))],
        out_specs=[rowspec(w) for w in outs],
        out_shape=[jax.ShapeDtypeStruct((nb * t, w), F32) for w in outs],
        scratch_shapes=[pltpu.VMEM((tt + 16, C_IN), F32)],
        compiler_params=_cp("parallel", "parallel"),
        name="r7prep",
    )(pc, pc, pc, lp['r7_conv'], lp['r7_kk'].reshape(1, C_W), lp['r7_ka'].reshape(1, C_W),
      lp['r7_rk'].reshape(1, C_W), lp['r7_w0'].reshape(1, 2 * C_W), _bd2(lp['r7_w2']),
      lp['r7_a0'].reshape(1, 2 * C_W), _bd2(lp['r7_a2']), lp['r7_g2'], _block_diag_ones())


def _cumsum_rows(tri, x):
    t = tri.astype(BF16)
    hi = x.astype(BF16)
    rest = x - hi.astype(F32)
    mid = rest.astype(BF16)
    lo = (rest - mid.astype(F32)).astype(BF16)
    return _dot(t, hi) + (_dot(t, mid) + _dot(t, lo))


def _mm(a, b):
    return _dot(a.astype(BF16), b.astype(BF16))


def _mm_nt(a, b):
    return _dot_nt(a.astype(BF16), b.astype(BF16))


def _scan_kernel(rf_ref, vf_ref, kkf_ref, lwf_ref, betaf_ref, kdf_ref,
                 rb_ref, vb_ref, kkb_ref, lwb_ref, betab_ref, kdb_ref, s0_ref,
                 yf_ref, yb_ref, sf_ref, st_ref, *, nc):
    c = pl.program_id(1)
    cl = SCAN_C

    @pl.when(c == 0)
    def _():
        st_ref[...] = s0_ref[...]

    ti = lax.broadcasted_iota(jnp.int32, (cl, cl), 0)
    si = lax.broadcasted_iota(jnp.int32, (cl, cl), 1)
    eye = (ti == si).astype(F32)
    incl = [si <= ti, si >= ti]
    strict = [si < ti, si > ti]

    def hs(x, h):
        return x[:, h * HD:(h + 1) * HD]

    pre = {}
    for bb in range(SCAN_NB):
        for d, (r_ref, v_ref, kk_ref, lw_ref, beta_ref, kd_ref) in enumerate(
                [(rf_ref, vf_ref, kkf_ref, lwf_ref, betaf_ref, kdf_ref),
                 (rb_ref, vb_ref, kkb_ref, lwb_ref, betab_ref, kdb_ref)]):
            lw = lw_ref[bb]
            cum = _cumsum_rows(incl[d], lw)
            e_neg = jnp.exp(-cum)
            pre[bb, d] = dict(r_hat=r_ref[bb] * jnp.exp(cum), a_hat=-kk_ref[bb] * jnp.exp(cum - lw),
                              b_til=beta_ref[bb] * e_neg, k_til=kd_ref[bb] * e_neg,
                              wtot=jnp.exp(jnp.sum(lw, axis=0, keepdims=True)), vv=v_ref[bb])

    chains = [(bb, d, h) for bb in range(SCAN_NB) for d in range(2) for h in range(C_HEADS)]
    n_ch = range(len(chains))

    def part(name, i):
        bb, d, h = chains[i]
        return hs(pre[bb, d][name], h)

    def dirn(i):
        return chains[i][1]

    ar = [jnp.concatenate([part('a_hat', i), part('r_hat', i)], axis=0) for i in n_ch]
    bk = [jnp.concatenate([part('b_til', i), part('k_til', i)], axis=0) for i in n_ch]
    m = [_mm_nt(ar[i], bk[i]) for i in n_ch]
    l_k = [jnp.where(strict[dirn(i)], m[i][0:cl, cl:2 * cl], 0.0) for i in n_ch]
    r_b = [jnp.where(incl[dirn(i)], m[i][cl:2 * cl, 0:cl], 0.0) for i in n_ch]
    r_k = [jnp.where(incl[dirn(i)], m[i][cl:2 * cl, cl:2 * cl], 0.0) for i in n_ch]
    p = [jnp.where(strict[dirn(i)], m[i][0:cl, 0:cl], 0.0) for i in n_ch]
    t_inv = [eye + p[i] for i in n_ch]
    n = 2
    while n < cl:
        p = [_mm(p[i], p[i]) for i in n_ch]
        t_inv = [t_inv[i] + _mm(t_inv[i], p[i]) for i in n_ch]
        n *= 2
    x1 = [_mm(l_k[i], part('vv', i)) for i in n_ch]
    y0 = [_mm(r_k[i], part('vv', i)) for i in n_ch]
    ua = [_mm(t_inv[i], jnp.concatenate([x1[i], part('a_hat', i)], axis=1)) for i in n_ch]
    s0 = [st_ref[d, bb, h] for bb, d, h in chains]
    as0 = [_mm_nt(jnp.concatenate([ua[i][:, HD:2 * HD], part('r_hat', i)], axis=0), s0[i])
           for i in n_ch]
    u = [ua[i][:, 0:HD] + as0[i][0:cl] for i in n_ch]
    y = [y0[i] + as0[i][cl:2 * cl] + _mm(r_b[i], u[i]) for i in n_ch]
    for i in n_ch:
        bb, d, h = chains[i]
        (yf_ref, yb_ref)[d][bb, :, h * HD:(h + 1) * HD] = y[i]
    for i in n_ch:
        bb, d, h = chains[i]
        uv_t = jnp.concatenate([u[i], part('vv', i)], axis=0).T
        st_ref[d, bb, h] = (s0[i] + _mm(uv_t, bk[i])) * part('wtot', i)

    @pl.when(c == nc - 1)
    def _():
        sf_ref[...] = st_ref[...]


def _scan(prep, s0, nb, t):
    nc = t // SCAN_C
    assert nb % SCAN_NB == 0
    r, v, kk, lw, beta, kd = [a.reshape(nb, t, a.shape[-1]) for a in prep]

    def chunk(d, c):
        return nc - 1 - c if d else c

    def specs(d):
        shared = pl.BlockSpec((SCAN_NB, SCAN_C, C_W), lambda b, c: (b, chunk(d, c), 0))
        perdir = pl.BlockSpec((SCAN_NB, SCAN_C, C_W), lambda b, c: (b, chunk(d, c), d))
        return [shared, shared, shared, perdir, perdir, perdir]

    st_spec = pl.BlockSpec((2, SCAN_NB, C_HEADS, HD, HD), lambda b, c: (0, b, 0, 0, 0))
    yf, yb, sf = pl.pallas_call(
        functools.partial(_scan_kernel, nc=nc),
        grid=(nb // SCAN_NB, nc),
        in_specs=specs(0) + specs(1) + [st_spec],
        out_specs=[pl.BlockSpec((SCAN_NB, SCAN_C, C_W), lambda b, c: (b, chunk(0, c), 0)),
                   pl.BlockSpec((SCAN_NB, SCAN_C, C_W), lambda b, c: (b, chunk(1, c), 0)), st_spec],
        out_shape=[jax.ShapeDtypeStruct((nb, t, C_W), F32), jax.ShapeDtypeStruct((nb, t, C_W), F32),
                   jax.ShapeDtypeStruct((2, nb, C_HEADS, HD, HD), F32)],
        scratch_shapes=[pltpu.VMEM((2, SCAN_NB, C_HEADS, HD, HD), F32)],
        compiler_params=_cp("parallel", "arbitrary"),
        name="r7scan",
    )(r, v, kk, lw, beta, kd, r, v, kk, lw, beta, kd, s0)
    return (yf.reshape(nb * t, C_W), yb.reshape(nb * t, C_W)), sf


def _outproj_kernel(x_ref, oa_ref, ob_ref, y0_ref, y1_ref, bonus_ref, g_ref, lnw_ref, lnb_ref, bd_ref, w_ref,
                    gate_ref, o_ref):
    bd = bd_ref[...]
    y = y0_ref[...] + y1_ref[...]
    mu = _segsum(y, bd) * (1.0 / HD)
    yc = y - mu
    var = _segsum(yc * yc, bd) * (1.0 / HD)
    yn = yc * lax.rsqrt(var + GN_EPS) * lnw_ref[...] + lnb_ref[...]
    oc = ((yn + bonus_ref[...]) * g_ref[...]).astype(BF16)
    acc = (_dot(oa_ref[...], w_ref[0:256, :]) + _dot(ob_ref[...], w_ref[256:512, :])
           + _dot(oc, w_ref[512:1024, :]))
    o_ref[...] = x_ref[...] + gate_ref[...] * acc


def _outproj(x2, oa, ob, y, bonus, g, lp, w_out_bf, mod3, mod_row_of_batch, nb, t):
    tm = 256
    nt = t // tm

    def rowspec(w):
        return pl.BlockSpec((tm, w), lambda b, i: (b * nt + i, 0))

    def full(shape):
        return pl.BlockSpec(shape, lambda b, i: (0,) * len(shape))

    return pl.pallas_call(
        _outproj_kernel,
        grid=(nb, nt),
        in_specs=[rowspec(D), rowspec(256), rowspec(256),
                  rowspec(C_W), rowspec(C_W),
                  rowspec(C_W), rowspec(C_W), full((1, C_W)), full((1, C_W)), full((SEG_W, SEG_W)), full((D, D)),
                  pl.BlockSpec((None, 1, D), lambda b, i: (mod_row_of_batch(b), 0, 2))],
        out_specs=rowspec(D),
        out_shape=jax.ShapeDtypeStruct((nb * t, D), F32),
        compiler_params=_cp("parallel", "parallel"),
        name="outproj",
    )(x2, oa, ob, y[0], y[1], bonus, g, lp['r7_lnw'].reshape(1, C_W), lp['r7_lnb'].reshape(1, C_W),
      _block_diag_ones(), w_out_bf, mod3)


def _peer_score_kernel(x_ref, g_ref, sc_ref, sh_ref, wq_ref, keys_ref, h_ref, st_ref):
    x = x_ref[...]
    y = x * lax.rsqrt(jnp.mean(x * x, axis=-1, keepdims=True) + EPS) * g_ref[...]
    h = (y * (1.0 + sc_ref[...]) + sh_ref[...]).astype(BF16)
    h_ref[...] = h
    q = _dot(h, wq_ref[...])
    half = P_QDIM // 2
    for hp in range(2 * P_HEADS):
        st_ref[hp] = _dot3(keys_ref[hp], q[:, hp * half:(hp + 1) * half], nt=True)


def _peer_scores(x2, norm_g, mod3, mod_row_of_batch, wq_bf, keys, nb, t):
    tm = 256
    nt = t // tm
    n = nb * t
    return pl.pallas_call(
        _peer_score_kernel,
        grid=(nb, nt),
        in_specs=[pl.BlockSpec((tm, D), lambda b, i: (b * nt + i, 0)),
                  pl.BlockSpec((1, D), lambda b, i: (0, 0)),
                  pl.BlockSpec((None, 1, D), lambda b, i: (mod_row_of_batch(b), 0, 4)),
                  pl.BlockSpec((None, 1, D), lambda b, i: (mod_row_of_batch(b), 0, 3)),
                  pl.BlockSpec((D, P_HEADS * P_QDIM), lambda b, i: (0, 0)),
                  pl.BlockSpec((2 * P_HEADS, P_NKEYS, P_QDIM // 2), lambda b, i: (0, 0, 0))],
        out_specs=[pl.BlockSpec((tm, D), lambda b, i: (b * nt + i, 0)),
                   pl.BlockSpec((2 * P_HEADS, P_NKEYS, tm), lambda b, i: (0, 0, b * nt + i))],
        out_shape=[jax.ShapeDtypeStruct((n, D), BF16),
                   jax.ShapeDtypeStruct((2 * P_HEADS, P_NKEYS, n), F32)],
        compiler_params=_cp("parallel", "parallel"),
        name="peer_scores",
    )(x2, norm_g.reshape(1, D), mod3, mod3, wq_bf, keys.reshape(2 * P_HEADS, P_NKEYS, P_QDIM // 2))


def _batcher_pairs(n):
    pairs = []
    p = 1
    while p < n:
        k = p
        while k >= 1:
            for j in range(k % p, n - k, 2 * k):
                for i in range(min(k, n - j - k)):
                    if (i + j) // (2 * p) == (i + j + k) // (2 * p):
                        pairs.append((i + j, i + j + k))
            k //= 2
        p *= 2
    return pairs


_SORT16 = _batcher_pairs(P_TOPK)
_BITONIC16 = [(i, i + d) for d in (8, 4, 2, 1) for i in range(P_TOPK) if i & d == 0]


def _compare_exchange(x, pairs):
    for i, j in pairs:
        x[i], x[j] = jnp.maximum(x[i], x[j]), jnp.minimum(x[i], x[j])
    return x


def _top16_sorted(slabs):
    x = _compare_exchange(list(slabs), _SORT16)
    for shift in (4, 2, 1):
        y = [pltpu.roll(x[P_TOPK - 1 - k], shift, 0) for k in range(P_TOPK)]
        x = [jnp.maximum(x[k], y[k]) for k in range(P_TOPK)]
        x = _compare_exchange(x, _BITONIC16)
    return x


def _peer_gate_tables(st_ref, rk_ref, be_ref, cnt_ref, al_ref, top_ref):
    nblk = st_ref.shape[-1] // LANES
    sub = lax.broadcasted_iota(jnp.int32, (8, LANES), 0)
    ninf = jnp.full((8, LANES), -jnp.inf, F32)

    def block(it, carry):
        h = it // nblk
        lb = it % nblk
        lanes = pl.ds(pl.multiple_of(lb * LANES, LANES), LANES)
        hf = lb // (PEER_HALF // LANES)
        hl = pl.ds(pl.multiple_of((lb % (PEER_HALF // LANES)) * LANES, LANES), LANES)
        for p in range(2):
            top = _top16_sorted([st_ref[2 * h + p, 8 * k:8 * k + 8, lanes] for k in range(P_TOPK)])
            for k in range(P_TOPK):
                top_ref[p * P_TOPK + k:p * P_TOPK + k + 1, lanes] = top[k][0:1]
        a16 = top_ref[0:P_TOPK, lanes]
        b16 = top_ref[P_TOPK:2 * P_TOPK, lanes]
        b8 = b16[0:8]
        cand = [a16[0:1] + b8, a16[0:1] + b16[8:16]]
        for p in range(2, 9):
            cand.append(jnp.where(sub < P_TOPK // p, a16[p - 1:p] + b8, -jnp.inf))
        cand.append(a16[8:16] + b16[0:1])
        best = _top16_sorted(cand + [ninf] * (P_TOPK - len(cand)))
        tau = best[P_TOPK - 1][0:1]
        z = jnp.ones_like(tau)
        for k in range(1, P_TOPK):
            z = z + jnp.exp(best[k][0:1] - best[0][0:1])
        s1 = st_ref[2 * h, :, lanes]
        s2 = st_ref[2 * h + 1, :, lanes]
        cnt = jnp.zeros_like(s1)
        rk = jnp.ones_like(s2)
        for q in range(P_TOPK):
            bq = b16[q:q + 1]
            theta = jnp.min(jnp.where(a16 + bq >= tau, a16, jnp.inf), axis=0, keepdims=True)
            cnt = jnp.where(s1 >= theta, q + 1.0, cnt)
            rk = jnp.where(bq > s2, q + 2.0, rk)
        cnt_ref[hf, h, :, hl] = cnt
        rk_ref[hf, h, :, hl] = rk.astype(BF16)
        al_ref[hf, h, :, hl] = jnp.exp(s1 - a16[0:1]) / z
        be_ref[hf, h, :, hl] = jnp.exp(s2 - b16[0:1]).astype(BF16)
        return carry

    lax.fori_loop(0, P_HEADS * nblk, block, 0)


def _peer_chunk(chunk, act_ref, rk_ref, be_ref, cnt_ref, al_ref, w_ref):
    nrow = w_ref.shape[0] // P_NKEYS
    for ii in range(nrow):
        i = chunk * nrow + ii
        wrow = None
        for h in range(P_HEADS):
            cnt = cnt_ref[h, pl.ds(i, 1), :].astype(BF16)
            al = al_ref[h, pl.ds(i, 1), :].astype(BF16)
            term = jnp.where(rk_ref[h] <= cnt, al * be_ref[h], 0.0)
            wrow = term if wrow is None else wrow + term
        rows = slice(ii * P_NKEYS, (ii + 1) * P_NKEYS)
        act = act_ref[rows, :]
        gl = 0.5 * act * (1.0 + lax.erf(act * (2.0 ** -0.5)))
        w_ref[rows, :] = wrow * gl.astype(BF16)


def _peer_dense_kernel(x_ref, h_ref, st_ref, u_ref, vt_ref, gate_ref, o_ref,
                       rk_ref, be_ref, cnt_ref, al_ref, top_ref, act0_ref, act1_ref, w_ref, acc_ref, *, ne):
    e = pl.program_id(1)
    halves = range(w_ref.shape[0])

    def rows(hf):
        return slice(hf * PEER_HALF, (hf + 1) * PEER_HALF)

    @pl.when(e == 0)
    def _():
        for hf in halves:
            act0_ref[hf] = _dot_nt(u_ref[...], h_ref[rows(hf), :])
        _peer_gate_tables(st_ref, rk_ref, be_ref, cnt_ref, al_ref, top_ref)
        acc_ref[...] = jnp.zeros_like(acc_ref)

    def step(cur_ref, nxt_ref):
        for hf in halves:
            nxt_ref[hf] = _dot_nt(u_ref[...], h_ref[rows(hf), :])
            _peer_chunk(e - 1, cur_ref.at[hf], rk_ref.at[hf], be_ref.at[hf], cnt_ref.at[hf], al_ref.at[hf],
                        w_ref.at[hf])
            acc_ref[hf] += _dot(vt_ref[...], w_ref[hf])

    @pl.when(e % 2 == 1)
    def _():
        step(act0_ref, act1_ref)

    @pl.when((e > 0) & (e % 2 == 0))
    def _():
        step(act1_ref, act0_ref)

    @pl.when(e == ne)
    def _():
        for hf in halves:
            o_ref[rows(hf), :] = x_ref[rows(hf), :] + gate_ref[...] * acc_ref[hf].T


def _peer_dense(x2, h2, st, u_bf, vt_bf, mod3, mod_row_of_batch, nb, t):
    tm = 512 if t % 512 == 0 else 256
    nh = tm // PEER_HALF
    ec = PEER_EC
    nt = t // tm
    ne = P_EXPERTS // ec
    return pl.pallas_call(
        functools.partial(_peer_dense_kernel, ne=ne),
        grid=(nb * nt, ne + 1),
        in_specs=[pl.BlockSpec((tm, D), lambda i, e: (i, 0)),
                  pl.BlockSpec((tm, D), lambda i, e: (i, 0)),
                  pl.BlockSpec((2 * P_HEADS, P_NKEYS, tm), lambda i, e: (0, 0, i)),
                  pl.BlockSpec((ec, D), lambda i, e: (jnp.minimum(e, ne - 1), 0)),
                  pl.BlockSpec((None, D, ec), lambda i, e: (jnp.maximum(e - 1, 0), 0, 0)),
                  pl.BlockSpec((None, 1, D), lambda i, e: (mod_row_of_batch(i // nt), 0, 5))],
        out_specs=pl.BlockSpec((tm, D), lambda i, e: (i, 0)),
        out_shape=jax.ShapeDtypeStruct((nb * t, D), F32),
        scratch_shapes=[pltpu.VMEM((nh, P_HEADS, P_NKEYS, PEER_HALF), BF16),
                        pltpu.VMEM((nh, P_HEADS, P_NKEYS, PEER_HALF), BF16),
                        pltpu.VMEM((nh, P_HEADS, P_NKEYS, PEER_HALF), F32),
                        pltpu.VMEM((nh, P_HEADS, P_NKEYS, PEER_HALF), F32),
                        pltpu.VMEM((2 * P_TOPK, tm), F32),
                        pltpu.VMEM((nh, ec, PEER_HALF), F32),
                        pltpu.VMEM((nh, ec, PEER_HALF), F32),
                        pltpu.VMEM((nh, ec, PEER_HALF), BF16),
                        pltpu.VMEM((nh, D, PEER_HALF), F32)],
        compiler_params=pltpu.CompilerParams(dimension_semantics=("parallel", "arbitrary"),
                                             vmem_limit_bytes=PEER_VMEM_LIMIT),
        name="peer_dense",
    )(x2, h2, st, u_bf, vt_bf, mod3)


def _peer(x2, norm_g, mod3, mod_row_of_batch, wq_bf, keys, u_bf, vt_bf, nb, t):
    h2, st = _peer_scores(x2, norm_g, mod3, mod_row_of_batch, wq_bf, keys, nb, t)
    return _peer_dense(x2, h2, st, u_bf, vt_bf, mod3, mod_row_of_batch, nb, t)


def kernel(x, c, ctx, c_ctx, norm_mix, norm_ffn, w_mod, b_mod, w_in, w_out, a_qnorm, a_knorm, a_sink, b_qnorm,
           b_knorm, b_rpb, r7_conv, r7_w0, r7_w2, r7_a0, r7_a2, r7_g2, r7_kk, r7_ka, r7_rk, r7_lnw, r7_lnb,
           peer_wq, peer_keys, peer_u, peer_v):
    nb, s, _ = x.shape
    lc = ctx.shape[1]
    depth = w_in.shape[0]
    assert nb < 16 and nb % SCAN_NB == 0 and s % 512 == 0 and lc % 256 == 0
    rows = s // GRID_W

    cc = jnp.zeros((16, D), F32).at[:nb].set(c).at[nb].set(c_ctx)
    mod = _modulation(cc, w_mod, b_mod)
    rope_tabs = _rope_tables(s)
    lat_row = lambda b: b
    ctx_row = lambda b: nb

    x_lat = x.reshape(nb * s, D)
    x_ctx = ctx.reshape(nb * lc, D)
    scale = HD ** -0.5
    for l in range(depth):
        with_ctx = l < depth - 1
        mod3 = mod[l].reshape(16, 1, 6 * D)
        lp = {'r7_conv': r7_conv[l], 'r7_w0': r7_w0[l], 'r7_w2': r7_w2[l], 'r7_a0': r7_a0[l], 'r7_a2': r7_a2[l],
              'r7_g2': r7_g2[l], 'r7_kk': r7_kk[l], 'r7_ka': r7_ka[l], 'r7_rk': r7_rk[l], 'r7_lnw': r7_lnw[l],
              'r7_lnb': r7_lnb[l]}
        w_in_bf = w_in[l].astype(BF16)
        w_out_bf = w_out[l].astype(BF16)
        gain_a = jnp.concatenate([jnp.tile(a_qnorm[l] * scale, A_HEADS), jnp.tile(a_knorm[l], A_KV)]).reshape(1, 384)
        gain_b = jnp.concatenate([jnp.tile(b_qnorm[l] * scale, B_HEADS), jnp.tile(b_knorm[l], B_HEADS)]).reshape(1, 512)

        ab_lat, pc_lat = _inproj(x_lat, mod3, lat_row, norm_mix[l], w_in_bf, gain_a, gain_b, rope_tabs, nb, s)
        ab_ctx, pc_ctx = _inproj(x_ctx, mod3, ctx_row, norm_mix[l], w_in_bf, gain_a, gain_b, None, nb, lc)

        o_a = _attn_a(ab_lat, ab_ctx, a_sink[l], nb, s, lc)
        o_b = _attn_b(ab_lat, ab_ctx, _na_bias_tables(b_rpb[l], rows), nb, s, lc)

        prep_ctx = _r7prep(pc_ctx, lp, nb, lc)
        prep_lat = _r7prep(pc_lat, lp, nb, s)
        zero_state = jnp.zeros((2, nb, C_HEADS, HD, HD), F32)
        y_ctx, s_ctx = _scan(prep_ctx[:6], zero_state, nb, lc)
        y_lat, _ = _scan(prep_lat[:6], s_ctx, nb, s)

        x_lat = _outproj(x_lat, o_a, o_b, y_lat, prep_lat[7], prep_lat[6], lp, w_out_bf, mod3, lat_row, nb, s)
        wq_bf = peer_wq[l].astype(BF16)
        u_bf = peer_u[l].astype(BF16)
        vt_bf = peer_v[l].astype(BF16).reshape(P_EXPERTS // PEER_EC, PEER_EC, D).transpose(0, 2, 1)
        x_lat = _peer(x_lat, norm_ffn[l], mod3, lat_row, wq_bf, peer_keys[l], u_bf, vt_bf, nb, s)
        if with_ctx:
            o_ac, o_bc = _attn_ctx(ab_ctx, a_sink[l], nb, lc)
            x_ctx = _outproj(x_ctx, o_ac, o_bc, y_ctx, prep_ctx[7], prep_ctx[6], lp, w_out_bf, mod3, ctx_row, nb, lc)
            x_ctx = _peer(x_ctx, norm_ffn[l], mod3, ctx_row, wq_bf, peer_keys[l], u_bf, vt_bf, 1, nb * lc)
    return x_lat.reshape(nb, s, D)
```

```python
import functools

import numpy as np
import jax
import jax.numpy as jnp
from jax import lax
from jax.experimental import pallas as pl
from jax.experimental.pallas import tpu as pltpu

F32 = jnp.float32
BF16 = jnp.bfloat16
HI = lax.Precision.HIGHEST

D = 1024
DEPTH = 2
GRID_W = 64
HD = 64
LANES = 128
EPS = 1e-6
NEG = -1e30
A_HEADS, A_KV, A_BLOCK, A_WINDOW = 4, 2, 128, 128
A_STEP_BLOCKS = 2
B_HEADS, NA_ROWS, NA_COLS = 4, 8, 16
NA_STEP_ROWS = 8
C_HEADS = 8
C_W = 512
C_IN = 1920
AB_W = 1280
IN_W = AB_W + C_IN
GN_EPS = 64e-5
DECAY_SCALE = 0.6065306597126334
ROPE_BASE = 10000.0
P_HEADS, P_NKEYS, P_QDIM, P_TOPK = 8, 128, 256, 16
P_EXPERTS = P_NKEYS * P_NKEYS
SCAN_C = 64
SCAN_NB = 2
VMEM_LIMIT = 48 * 1024 * 1024
PEER_VMEM_LIMIT = 56 * 1024 * 1024
PEER_EC = 1024
PEER_HALF = 256


def _cp(*sem):
    return pltpu.CompilerParams(dimension_semantics=sem, vmem_limit_bytes=VMEM_LIMIT)


def _dot(a, b, prec=None):
    return jnp.dot(a, b, precision=prec, preferred_element_type=F32)


def _dot_nt(a, b, prec=None):
    return lax.dot_general(a, b, (((1,), (1,)), ((), ())), precision=prec, preferred_element_type=F32)


SEG_W = 256


def _block_diag_ones():
    i = np.arange(SEG_W) // HD
    return jnp.asarray((i[:, None] == i[None, :]).astype(np.float32), BF16)


def _split2(x):
    hi = x.astype(BF16)
    return hi, (x - hi.astype(F32)).astype(BF16)


def _segsum(x, bd):
    hi, lo = _split2(x)
    outs = []
    for g0 in range(0, x.shape[1], SEG_W):
        w = min(SEG_W, x.shape[1] - g0)
        outs.append(_dot(hi[:, g0:g0 + w], bd[0:w, 0:w]) + _dot(lo[:, g0:g0 + w], bd[0:w, 0:w]))
    return outs[0] if len(outs) == 1 else jnp.concatenate(outs, axis=1)


def _dot3(a, b, nt=False):
    f = _dot_nt if nt else _dot
    a_hi, a_lo = _split2(a)
    b_hi, b_lo = _split2(b)
    return f(a_hi, b_hi) + (f(a_lo, b_hi) + f(a_hi, b_lo))


def _mod_kernel(c_ref, w_ref, b_ref, o_ref):
    c = c_ref[...]
    s = c * jax.nn.sigmoid(c)
    o_ref[...] = _dot(s, w_ref[...], HI) + b_ref[...]


def _modulation(cc, w_mod, b_mod):
    L, _, n = w_mod.shape
    tn = 2048
    return pl.pallas_call(
        _mod_kernel,
        grid=(L, n // tn),
        in_specs=[pl.BlockSpec((16, D), lambda l, j: (0, 0)),
                  pl.BlockSpec((None, D, tn), lambda l, j: (l, 0, j)),
                  pl.BlockSpec((None, 1, tn), lambda l, j: (l, 0, j))],
        out_specs=pl.BlockSpec((None, 16, tn), lambda l, j: (l, 0, j)),
        out_shape=jax.ShapeDtypeStruct((L, 16, n), F32),
        compiler_params=_cp("parallel", "parallel"),
        name="modulation",
    )(cc, w_mod, b_mod.reshape(L, 1, n))


def _swap16(x):
    n = x.shape[-1]
    lane = lax.broadcasted_iota(jnp.int32, x.shape, 1)
    fwd = pltpu.roll(x, n - 16, 1)
    bwd = pltpu.roll(x, 16, 1)
    return jnp.where((lane % 32) < 16, fwd, bwd)


def _head_rms(x, bd, gain):
    ss = _segsum(x * x, bd)
    return x * lax.rsqrt(ss * (1.0 / HD) + EPS) * gain


def _inproj_kernel(*refs, rope):
    if rope:
        (x_ref, g_ref, sc_ref, sh_ref, w_ref, ga_ref, gb_ref, bd_ref, cos_ref, sin_ref,
         ab_ref, c_ref) = refs
    else:
        (x_ref, g_ref, sc_ref, sh_ref, w_ref, ga_ref, gb_ref, bd_ref, ab_ref, c_ref) = refs
    x = x_ref[...]
    y = x * lax.rsqrt(jnp.mean(x * x, axis=-1, keepdims=True) + EPS) * g_ref[...]
    h = y * (1.0 + sc_ref[...]) + sh_ref[...]
    acc = _dot(h.astype(BF16), w_ref[...])
    bd = bd_ref[...]
    qa = _head_rms(acc[:, 0:384], bd, ga_ref[...])
    if rope:
        qa = qa * cos_ref[...] + _swap16(qa) * sin_ref[...]
    qb = _head_rms(acc[:, 512:1024], bd, gb_ref[...])
    ab_ref[:, 0:384] = qa.astype(BF16)
    ab_ref[:, 384:512] = acc[:, 384:512].astype(BF16)
    ab_ref[:, 512:1024] = qb.astype(BF16)
    ab_ref[:, 1024:1280] = acc[:, 1024:1280].astype(BF16)
    c_ref[...] = acc[:, AB_W:IN_W]


def _inproj(x2, mod3, mod_row_of_batch, norm_g, w_in_bf, gain_a, gain_b, rope_tabs, nb, t):
    tm = 256
    nt = t // tm
    rope = rope_tabs is not None
    in_specs = [
        pl.BlockSpec((tm, D), lambda b, i: (b * nt + i, 0)),
        pl.BlockSpec((1, D), lambda b, i: (0, 0)),
        pl.BlockSpec((None, 1, D), lambda b, i: (mod_row_of_batch(b), 0, 1)),
        pl.BlockSpec((None, 1, D), lambda b, i: (mod_row_of_batch(b), 0, 0)),
        pl.BlockSpec((D, IN_W), lambda b, i: (0, 0)),
        pl.BlockSpec((1, 384), lambda b, i: (0, 0)),
        pl.BlockSpec((1, 512), lambda b, i: (0, 0)),
        pl.BlockSpec((SEG_W, SEG_W), lambda b, i: (0, 0)),
    ]
    args = [x2, norm_g.reshape(1, D), mod3, mod3, w_in_bf, gain_a, gain_b,
            _block_diag_ones()]
    if rope:
        in_specs += [pl.BlockSpec((tm, 384), lambda b, i: (i, 0)),
                     pl.BlockSpec((tm, 384), lambda b, i: (i, 0))]
        args += list(rope_tabs)
    return pl.pallas_call(
        functools.partial(_inproj_kernel, rope=rope),
        grid=(nb, nt),
        in_specs=in_specs,
        out_specs=[pl.BlockSpec((tm, AB_W), lambda b, i: (b * nt + i, 0)),
                   pl.BlockSpec((tm, C_IN), lambda b, i: (b * nt + i, 0))],
        out_shape=[jax.ShapeDtypeStruct((nb * t, AB_W), BF16),
                   jax.ShapeDtypeStruct((nb * t, C_IN), F32)],
        compiler_params=_cp("parallel", "parallel"),
        name="inproj_rope" if rope else "inproj_ctx",
    )(*args)


def _rope_tables(s):
    tok = np.arange(s)
    inv = ROPE_BASE ** (-np.arange(0, 32, 2) / 32.0)
    ar = (tok // GRID_W)[:, None] * inv[None]
    ac = (tok % GRID_W)[:, None] * inv[None]
    cos = np.concatenate([np.cos(ar), np.cos(ar), np.cos(ac), np.cos(ac)], axis=1)
    sin = np.concatenate([-np.sin(ar), np.sin(ar), -np.sin(ac), np.sin(ac)], axis=1)
    return (jnp.asarray(np.tile(cos, (1, 6)), F32), jnp.asarray(np.tile(sin, (1, 6)), F32))


def _softmax_pv(s, v, sink=None):
    m = jnp.max(s, axis=-1, keepdims=True)
    if sink is not None:
        m = jnp.maximum(m, sink)
    p = jnp.exp(s - m)
    den = jnp.sum(p, axis=-1, keepdims=True)
    if sink is not None:
        den = den + jnp.exp(sink - m)
    return _dot(p.astype(BF16), v) / den


def _attn_a_kernel(q_ref, k0_ref, k1_ref, k2_ref, k3_ref, v0_ref, v1_ref, v2_ref, v3_ref, kc_ref, vc_ref,
                   sink_ref, o_ref, *, nblk):
    kb = [k0_ref[...], k1_ref[...], k2_ref[...], k3_ref[...]]
    vb = [v0_ref[...], v1_ref[...], v2_ref[...], v3_ref[...]]
    kc = kc_ref[...]
    vc = vc_ref[...]
    nk = 3 * A_BLOCK + kc.shape[0]
    g = A_HEADS // A_KV
    row = lax.broadcasted_iota(jnp.int32, (g * A_BLOCK, nk), 0) % A_BLOCK
    col = lax.broadcasted_iota(jnp.int32, (g * A_BLOCK, nk), 1)
    rel = col - A_BLOCK - row
    band = (jnp.abs(rel) <= A_WINDOW) | (col >= 3 * A_BLOCK)
    chains, ok, k_all, v_all = [], [], [], []
    for j in range(A_STEP_BLOCKS):
        n = pl.program_id(1) * A_STEP_BLOCKS + j
        ok.append(band & ((n > 0) | (col >= A_BLOCK)) & ((n < nblk - 1) | (col < 2 * A_BLOCK) | (col >= 3 * A_BLOCK)))
        k_all.append(jnp.concatenate(kb[j:j + 3] + [kc], axis=0))
        v_all.append(jnp.concatenate(vb[j:j + 3] + [vc], axis=0))
        chains += [(j, hk) for hk in range(A_KV)]

    def qg(j, hk):
        rows = slice(j * A_BLOCK, (j + 1) * A_BLOCK)
        return jnp.concatenate([q_ref[rows, (hk * g + i) * HD:(hk * g + i + 1) * HD] for i in range(g)], axis=0)

    sink = [jnp.concatenate([jnp.broadcast_to(sink_ref[:, hk * g + i:hk * g + i + 1], (A_BLOCK, 1))
                             for i in range(g)], axis=0) for hk in range(A_KV)]
    s = [jnp.where(ok[j], _dot_nt(qg(j, hk), k_all[j][:, hk * HD:(hk + 1) * HD]), NEG) for j, hk in chains]
    m = [jnp.maximum(jnp.max(s[i], axis=-1, keepdims=True), sink[hk]) for i, (j, hk) in enumerate(chains)]
    p = [jnp.exp(s[i] - m[i]) for i in range(len(chains))]
    den = [jnp.sum(p[i], axis=-1, keepdims=True) + jnp.exp(sink[hk] - m[i]) for i, (j, hk) in enumerate(chains)]
    o = [_dot(p[i].astype(BF16), v_all[j][:, hk * HD:(hk + 1) * HD]) / den[i] for i, (j, hk) in enumerate(chains)]
    for i, (j, hk) in enumerate(chains):
        for gi in range(g):
            hq = hk * g + gi
            o_ref[j * A_BLOCK:(j + 1) * A_BLOCK, hq * HD:(hq + 1) * HD] = (
                o[i][gi * A_BLOCK:(gi + 1) * A_BLOCK].astype(BF16))


def _attn_a(ab_lat, ab_ctx, sink, nb, s, lc):
    nblk = s // A_BLOCK
    steps = nblk // A_STEP_BLOCKS

    def kv(col, d):
        return pl.BlockSpec(
            (A_BLOCK, 128), lambda b, i: (b * nblk + jnp.clip(i * A_STEP_BLOCKS + d, 0, nblk - 1), col))

    tq = A_STEP_BLOCKS * A_BLOCK
    return pl.pallas_call(
        functools.partial(_attn_a_kernel, nblk=nblk),
        grid=(nb, steps),
        in_specs=[pl.BlockSpec((tq, 256), lambda b, i: (b * steps + i, 0)),
                  kv(2, -1), kv(2, 0), kv(2, 1), kv(2, 2), kv(3, -1), kv(3, 0), kv(3, 1), kv(3, 2),
                  pl.BlockSpec((lc, 128), lambda b, i: (b, 2)),
                  pl.BlockSpec((lc, 128), lambda b, i: (b, 3)),
                  pl.BlockSpec((1, A_HEADS), lambda b, i: (0, 0))],
        out_specs=pl.BlockSpec((tq, 256), lambda b, i: (b * steps + i, 0)),
        out_shape=jax.ShapeDtypeStruct((nb * s, 256), BF16),
        compiler_params=_cp("parallel", "parallel"),
        name="attn_a",
    )(ab_lat, *([ab_lat] * 8), ab_ctx, ab_ctx, sink.reshape(1, A_HEADS))


def _attn_b_kernel(q_ref, k_ref, v_ref, kc_ref, vc_ref, bias_ref, o_ref, *, rows):
    nwin = NA_ROWS * GRID_W
    kc = kc_ref[...]
    vc = vc_ref[...]
    q, kw, vw, tab = [], [], [], []
    for rr in range(NA_STEP_ROWS):
        r = pl.program_id(1) * NA_STEP_ROWS + rr
        rs = jnp.clip(r - NA_ROWS // 2, 0, rows - NA_ROWS)
        start = pl.multiple_of(rs * GRID_W, GRID_W)
        tab.append(rs - r + NA_ROWS - 1)
        q.append(q_ref[rr * GRID_W:(rr + 1) * GRID_W, :])
        kw.append(k_ref[pl.ds(start, nwin), :])
        vw.append(v_ref[pl.ds(start, nwin), :])
    chains = [(rr, h) for rr in range(NA_STEP_ROWS) for h in range(B_HEADS)]

    def hs(x, h):
        return x[:, h * HD:(h + 1) * HD]

    s_loc = [_dot_nt(hs(q[rr], h), hs(kw[rr], h)) + bias_ref[tab[rr], h] for rr, h in chains]
    s_ctx = [_dot_nt(hs(q[rr], h), hs(kc, h)) for rr, h in chains]
    m = [jnp.maximum(jnp.max(a, axis=-1, keepdims=True), jnp.max(b, axis=-1, keepdims=True))
         for a, b in zip(s_loc, s_ctx)]
    p_loc = [jnp.exp(a - mm) for a, mm in zip(s_loc, m)]
    p_ctx = [jnp.exp(b - mm) for b, mm in zip(s_ctx, m)]
    den = [jnp.sum(a, axis=-1, keepdims=True) + jnp.sum(b, axis=-1, keepdims=True) for a, b in zip(p_loc, p_ctx)]
    o = [(_dot(p_loc[i].astype(BF16), hs(vw[rr], h)) + _dot(p_ctx[i].astype(BF16), hs(vc, h))) / den[i]
         for i, (rr, h) in enumerate(chains)]
    for i, (rr, h) in enumerate(chains):
        o_ref[rr * GRID_W:(rr + 1) * GRID_W, h * HD:(h + 1) * HD] = o[i].astype(BF16)


def _na_bias_tables(rpb, rows):
    kr = NA_ROWS
    qc = np.arange(GRID_W)
    win_start = np.clip(qc - NA_COLS // 2, 0, GRID_W - NA_COLS)
    kcol = np.arange(GRID_W)
    valid = (kcol[None, :] >= win_start[:, None]) & (kcol[None, :] < win_start[:, None] + NA_COLS)
    pad = GRID_W - NA_COLS
    rp = jnp.pad(rpb.astype(F32), ((0, 0), (0, 0), (pad, pad)))
    toe = jnp.stack([rp[:, :, GRID_W - 1 - q:2 * GRID_W - 1 - q] for q in range(GRID_W)], axis=2)
    toe = jnp.where(jnp.asarray(valid[None, None]), toe, NEG)
    tabs = [toe[:, off:off + kr].transpose(0, 2, 1, 3).reshape(rpb.shape[0], GRID_W, kr * GRID_W)
            for off in range(kr)]
    return jnp.stack(tabs, axis=0)


def _attn_b(ab_lat, ab_ctx, bias_tabs, nb, s, lc):
    rows = s // GRID_W
    steps = rows // NA_STEP_ROWS
    tq = NA_STEP_ROWS * GRID_W
    return pl.pallas_call(
        functools.partial(_attn_b_kernel, rows=rows),
        grid=(nb, steps),
        in_specs=[pl.BlockSpec((tq, 256), lambda b, r: (b * steps + r, 2)),
                  pl.BlockSpec((s, 256), lambda b, r: (b, 3)),
                  pl.BlockSpec((s, 256), lambda b, r: (b, 4)),
                  pl.BlockSpec((lc, 256), lambda b, r: (b, 3)),
                  pl.BlockSpec((lc, 256), lambda b, r: (b, 4)),
                  pl.BlockSpec((NA_ROWS, B_HEADS, GRID_W, NA_ROWS * GRID_W), lambda b, r: (0, 0, 0, 0))],
        out_specs=pl.BlockSpec((tq, 256), lambda b, r: (b * steps + r, 0)),
        out_shape=jax.ShapeDtypeStruct((nb * s, 256), BF16),
        compiler_params=_cp("parallel", "arbitrary"),
        name="attn_b",
    )(ab_lat, ab_lat, ab_lat, ab_ctx, ab_ctx, bias_tabs)


def _attn_ctx_kernel(ab_ref, sink_ref, oa_ref, ob_ref):
    lc = ab_ref.shape[0]
    g = A_HEADS // A_KV
    for hk in range(A_KV):
        qg = jnp.concatenate([ab_ref[:, (hk * g + j) * HD:(hk * g + j + 1) * HD] for j in range(g)], axis=0)
        s = _dot_nt(qg, ab_ref[:, 256 + hk * HD:256 + (hk + 1) * HD])
        sink = jnp.concatenate(
            [jnp.broadcast_to(sink_ref[:, hk * g + j:hk * g + j + 1], (lc, 1)) for j in range(g)], axis=0)
        o = _softmax_pv(s, ab_ref[:, 384 + hk * HD:384 + (hk + 1) * HD], sink)
        for j in range(g):
            hq = hk * g + j
            oa_ref[:, hq * HD:(hq + 1) * HD] = o[j * lc:(j + 1) * lc].astype(BF16)
    for h in range(B_HEADS):
        s = _dot_nt(ab_ref[:, 512 + h * HD:512 + (h + 1) * HD], ab_ref[:, 768 + h * HD:768 + (h + 1) * HD])
        o = _softmax_pv(s, ab_ref[:, 1024 + h * HD:1024 + (h + 1) * HD])
        ob_ref[:, h * HD:(h + 1) * HD] = o.astype(BF16)


def _attn_ctx(ab_ctx, sink, nb, lc):
    return pl.pallas_call(
        _attn_ctx_kernel,
        grid=(nb,),
        in_specs=[pl.BlockSpec((lc, AB_W), lambda b: (b, 0)),
                  pl.BlockSpec((1, A_HEADS), lambda b: (0, 0))],
        out_specs=[pl.BlockSpec((lc, 256), lambda b: (b, 0)),
                   pl.BlockSpec((lc, 256), lambda b: (b, 0))],
        out_shape=[jax.ShapeDtypeStruct((nb * lc, 256), BF16),
                   jax.ShapeDtypeStruct((nb * lc, 256), BF16)],
        compiler_params=_cp("parallel"),
        name="attn_ctx",
    )(ab_ctx, sink.reshape(1, A_HEADS))


def _r7prep_kernel(x_ref, prev_ref, next_ref, cw_ref, kkw_ref, ka_ref, rk_ref, w0_ref, w2_ref, a0_ref, a2_ref,
                   g2_ref, bd_ref,
                   r_ref, v_ref, kk_ref, lw_ref, beta_ref, kd_ref, g_ref, bonus_ref, *, nt):
    i = pl.program_id(1)
    tt = x_ref.shape[0]
    x = x_ref[...]
    row = lax.broadcasted_iota(jnp.int32, x.shape, 0)
    before = jnp.where(i > 0, prev_ref[7:8, :], 0.0)
    after = jnp.where(i < nt - 1, next_ref[0:1, :], 0.0)
    x_m1 = jnp.where(row == 0, before, pltpu.roll(x, 1, 0))
    x_p1 = jnp.where(row == tt - 1, after, pltpu.roll(x, tt - 1, 0))
    xc = x_m1 * cw_ref[0:1, :] + x * cw_ref[1:2, :] + x_p1 * cw_ref[2:3, :]
    r = xc[:, 0:512]
    k = xc[:, 512:1024]
    v = xc[:, 1024:1536]
    wd = xc[:, 1536:1664]
    ad = xc[:, 1664:1792]
    gd = xc[:, 1792:1920]
    bd = bd_ref[...]
    kkh = k * kkw_ref[...]
    kk = kkh / jnp.maximum(jnp.sqrt(_segsum(kkh * kkh, bd)), 1e-12)
    zw = w0_ref[...] + _dot3(jnp.tanh(wd), w2_ref[...])
    za = a0_ref[...] + _dot3(ad, a2_ref[...])
    a = jax.nn.sigmoid(za)
    ka = ka_ref[...]
    kd0 = k * (1.0 + (a[:, 0:512] - 1.0) * ka)
    kd1 = k * (1.0 + (a[:, 512:1024] - 1.0) * ka)
    r_ref[...] = r.astype(BF16)
    v_ref[...] = v.astype(BF16)
    kk_ref[...] = kk.astype(BF16)
    lw_ref[...] = -DECAY_SCALE * jax.nn.sigmoid(zw)
    beta_ref[:, 0:512] = (kk * a[:, 0:512]).astype(BF16)
    beta_ref[:, 512:1024] = (kk * a[:, 512:1024]).astype(BF16)
    kd_ref[:, 0:512] = kd0.astype(BF16)
    kd_ref[:, 512:1024] = kd1.astype(BF16)
    g_ref[...] = _dot3(jax.nn.sigmoid(gd), g2_ref[...])
    bonus_ref[...] = _segsum(r * (kd0 + kd1) * rk_ref[...], bd) * v


def _bd2(w):
    z = jnp.zeros_like(w[0])
    return jnp.concatenate([jnp.concatenate([w[0], z], axis=1), jnp.concatenate([z, w[1]], axis=1)], axis=0)


def _r7prep(pc, lp, nb, t):
    tt = 256
    nt = t // tt
    nrow8 = t // 8

    def full(shape):
        return pl.BlockSpec(shape, lambda b, i: (0,) * len(shape))

    def rowspec(w):
        return pl.BlockSpec((tt, w), lambda b, i: (b * nt + i, 0))

    outs = [C_W, C_W, C_W, 2 * C_W, 2 * C_W, 2 * C_W, C_W, C_W]
    dtypes = [BF16, BF16, BF16, F32, BF16, BF16, F32, F32]
    return pl.pallas_call(
        functools.partial(_r7prep_kernel, nt=nt),
        grid=(nb, nt),
        in_specs=[rowspec(C_IN),
                  pl.BlockSpec((8, C_IN), lambda b, i: (b * nrow8 + jnp.maximum(i * (tt // 8) - 1, 0), 0)),
                  pl.BlockSpec((8, C_IN), lambda b, i: (b * nrow8 + jnp.minimum((i + 1) * (tt // 8), nrow8 - 1), 0)),
                  full((3, C_IN)), full((1, C_W)), full((1, C_W)), full((1, C_W)),
                  full((1, 2 * C_W)), full((128, 2 * C_W)), full((1, 2 * C_W)), full((128, 2 * C_W)),
                  full((128, C_W)), full((SEG_W, SEG_W))],
        out_specs=[rowspec(w) for w in outs],
        out_shape=[jax.ShapeDtypeStruct((nb * t, w), dt) for w, dt in zip(outs, dtypes)],
        compiler_params=_cp("parallel", "parallel"),
        name="r7prep",
    )(pc, pc, pc, lp['r7_conv'], lp['r7_kk'].reshape(1, C_W), lp['r7_ka'].reshape(1, C_W),
      lp['r7_rk'].reshape(1, C_W), lp['r7_w0'].reshape(1, 2 * C_W), _bd2(lp['r7_w2']),
      lp['r7_a0'].reshape(1, 2 * C_W), _bd2(lp['r7_a2']), lp['r7_g2'], _block_diag_ones())


def _cumsum_rows(tri, x):
    t = tri.astype(BF16)
    hi = x.astype(BF16)
    rest = x - hi.astype(F32)
    mid = rest.astype(BF16)
    lo = (rest - mid.astype(F32)).astype(BF16)
    return _dot(t, hi) + (_dot(t, mid) + _dot(t, lo))


def _mm(a, b):
    return _dot(a.astype(BF16), b.astype(BF16))


def _mm_nt(a, b):
    return _dot_nt(a.astype(BF16), b.astype(BF16))


def _scan_kernel(rf_ref, vf_ref, kkf_ref, lwf_ref, betaf_ref, kdf_ref,
                 rb_ref, vb_ref, kkb_ref, lwb_ref, betab_ref, kdb_ref, s0_ref,
                 yf_ref, yb_ref, sf_ref, st_ref, *, nc):
    c = pl.program_id(1)
    cl = SCAN_C

    @pl.when(c == 0)
    def _():
        st_ref[...] = s0_ref[...]

    ti = lax.broadcasted_iota(jnp.int32, (cl, cl), 0)
    si = lax.broadcasted_iota(jnp.int32, (cl, cl), 1)
    eye = (ti == si).astype(F32)
    incl = [si <= ti, si >= ti]
    strict = [si < ti, si > ti]

    def hs(x, h):
        return x[:, h * HD:(h + 1) * HD]

    pre = {}
    for bb in range(SCAN_NB):
        for d, (r_ref, v_ref, kk_ref, lw_ref, beta_ref, kd_ref) in enumerate(
                [(rf_ref, vf_ref, kkf_ref, lwf_ref, betaf_ref, kdf_ref),
                 (rb_ref, vb_ref, kkb_ref, lwb_ref, betab_ref, kdb_ref)]):
            lw = lw_ref[bb]
            cum = _cumsum_rows(incl[d], lw)
            e_neg = jnp.exp(-cum)
            pre[bb, d] = dict(r_hat=r_ref[bb] * jnp.exp(cum), a_hat=-kk_ref[bb] * jnp.exp(cum - lw),
                              b_til=beta_ref[bb] * e_neg, k_til=kd_ref[bb] * e_neg,
                              wtot=jnp.exp(jnp.sum(lw, axis=0, keepdims=True)), vv=v_ref[bb])

    chains = [(bb, d, h) for bb in range(SCAN_NB) for d in range(2) for h in range(C_HEADS)]
    n_ch = range(len(chains))

    def part(name, i):
        bb, d, h = chains[i]
        return hs(pre[bb, d][name], h)

    def dirn(i):
        return chains[i][1]

    ar = [jnp.concatenate([part('a_hat', i), part('r_hat', i)], axis=0) for i in n_ch]
    bk = [jnp.concatenate([part('b_til', i), part('k_til', i)], axis=0) for i in n_ch]
    m = [_mm_nt(ar[i], bk[i]) for i in n_ch]
    l_k = [jnp.where(strict[dirn(i)], m[i][0:cl, cl:2 * cl], 0.0) for i in n_ch]
    r_b = [jnp.where(incl[dirn(i)], m[i][cl:2 * cl, 0:cl], 0.0) for i in n_ch]
    r_k = [jnp.where(incl[dirn(i)], m[i][cl:2 * cl, cl:2 * cl], 0.0) for i in n_ch]
    p = [jnp.where(strict[dirn(i)], m[i][0:cl, 0:cl], 0.0) for i in n_ch]
    t_inv = [eye + p[i] for i in n_ch]
    n = 2
    while n < cl:
        p = [_mm(p[i], p[i]) for i in n_ch]
        t_inv = [t_inv[i] + _mm(t_inv[i], p[i]) for i in n_ch]
        n *= 2
    x1 = [_mm(l_k[i], part('vv', i)) for i in n_ch]
    y0 = [_mm(r_k[i], part('vv', i)) for i in n_ch]
    ua = [_mm(t_inv[i], jnp.concatenate([x1[i], part('a_hat', i)], axis=1)) for i in n_ch]
    s0 = [st_ref[d, bb, h] for bb, d, h in chains]
    as0 = [_mm_nt(jnp.concatenate([ua[i][:, HD:2 * HD], part('r_hat', i)], axis=0), s0[i])
           for i in n_ch]
    u = [ua[i][:, 0:HD] + as0[i][0:cl] for i in n_ch]
    y = [y0[i] + as0[i][cl:2 * cl] + _mm(r_b[i], u[i]) for i in n_ch]
    for i in n_ch:
        bb, d, h = chains[i]
        (yf_ref, yb_ref)[d][bb, :, h * HD:(h + 1) * HD] = y[i]
    for i in n_ch:
        bb, d, h = chains[i]
        uv_t = jnp.concatenate([u[i], part('vv', i)], axis=0).T
        st_ref[d, bb, h] = (s0[i] + _mm(uv_t, bk[i])) * part('wtot', i)

    @pl.when(c == nc - 1)
    def _():
        sf_ref[...] = st_ref[...]


def _scan(prep, s0, nb, t):
    nc = t // SCAN_C
    assert nb % SCAN_NB == 0
    r, v, kk, lw, beta, kd = [a.reshape(nb, t, a.shape[-1]) for a in prep]

    def chunk(d, c):
        return nc - 1 - c if d else c

    def specs(d):
        shared = pl.BlockSpec((SCAN_NB, SCAN_C, C_W), lambda b, c: (b, chunk(d, c), 0))
        perdir = pl.BlockSpec((SCAN_NB, SCAN_C, C_W), lambda b, c: (b, chunk(d, c), d))
        return [shared, shared, shared, perdir, perdir, perdir]

    st_spec = pl.BlockSpec((2, SCAN_NB, C_HEADS, HD, HD), lambda b, c: (0, b, 0, 0, 0))
    yf, yb, sf = pl.pallas_call(
        functools.partial(_scan_kernel, nc=nc),
        grid=(nb // SCAN_NB, nc),
        in_specs=specs(0) + specs(1) + [st_spec],
        out_specs=[pl.BlockSpec((SCAN_NB, SCAN_C, C_W), lambda b, c: (b, chunk(0, c), 0)),
                   pl.BlockSpec((SCAN_NB, SCAN_C, C_W), lambda b, c: (b, chunk(1, c), 0)), st_spec],
        out_shape=[jax.ShapeDtypeStruct((nb, t, C_W), F32), jax.ShapeDtypeStruct((nb, t, C_W), F32),
                   jax.ShapeDtypeStruct((2, nb, C_HEADS, HD, HD), F32)],
        scratch_shapes=[pltpu.VMEM((2, SCAN_NB, C_HEADS, HD, HD), F32)],
        compiler_params=_cp("parallel", "arbitrary"),
        name="r7scan",
    )(r, v, kk, lw, beta, kd, r, v, kk, lw, beta, kd, s0)
    return (yf.reshape(nb * t, C_W), yb.reshape(nb * t, C_W)), sf


def _outproj_kernel(x_ref, oa_ref, ob_ref, y0_ref, y1_ref, bonus_ref, g_ref, lnw_ref, lnb_ref, bd_ref, w_ref,
                    gate_ref, o_ref):
    bd = bd_ref[...]
    y = y0_ref[...] + y1_ref[...]
    mu = _segsum(y, bd) * (1.0 / HD)
    yc = y - mu
    var = _segsum(yc * yc, bd) * (1.0 / HD)
    yn = yc * lax.rsqrt(var + GN_EPS) * lnw_ref[...] + lnb_ref[...]
    oc = ((yn + bonus_ref[...]) * g_ref[...]).astype(BF16)
    acc = (_dot(oa_ref[...], w_ref[0:256, :]) + _dot(ob_ref[...], w_ref[256:512, :])
           + _dot(oc, w_ref[512:1024, :]))
    o_ref[...] = x_ref[...] + gate_ref[...] * acc


def _outproj(x2, oa, ob, y, bonus, g, lp, w_out_bf, mod3, mod_row_of_batch, nb, t):
    tm = 256
    nt = t // tm

    def rowspec(w):
        return pl.BlockSpec((tm, w), lambda b, i: (b * nt + i, 0))

    def full(shape):
        return pl.BlockSpec(shape, lambda b, i: (0,) * len(shape))

    return pl.pallas_call(
        _outproj_kernel,
        grid=(nb, nt),
        in_specs=[rowspec(D), rowspec(256), rowspec(256),
                  rowspec(C_W), rowspec(C_W),
                  rowspec(C_W), rowspec(C_W), full((1, C_W)), full((1, C_W)), full((SEG_W, SEG_W)), full((D, D)),
                  pl.BlockSpec((None, 1, D), lambda b, i: (mod_row_of_batch(b), 0, 2))],
        out_specs=rowspec(D),
        out_shape=jax.ShapeDtypeStruct((nb * t, D), F32),
        compiler_params=_cp("parallel", "parallel"),
        name="outproj",
    )(x2, oa, ob, y[0], y[1], bonus, g, lp['r7_lnw'].reshape(1, C_W), lp['r7_lnb'].reshape(1, C_W),
      _block_diag_ones(), w_out_bf, mod3)


def _peer_score_kernel(x_ref, g_ref, sc_ref, sh_ref, wq_ref, keys_ref, h_ref, st_ref):
    x = x_ref[...]
    y = x * lax.rsqrt(jnp.mean(x * x, axis=-1, keepdims=True) + EPS) * g_ref[...]
    h = (y * (1.0 + sc_ref[...]) + sh_ref[...]).astype(BF16)
    h_ref[...] = h
    q = _dot(h, wq_ref[...])
    half = P_QDIM // 2
    for hp in range(2 * P_HEADS):
        st_ref[hp] = _dot3(keys_ref[hp], q[:, hp * half:(hp + 1) * half], nt=True)


def _peer_scores(x2, norm_g, mod3, mod_row_of_batch, wq_bf, keys, nb, t):
    tm = 256
    nt = t // tm
    n = nb * t
    return pl.pallas_call(
        _peer_score_kernel,
        grid=(nb, nt),
        in_specs=[pl.BlockSpec((tm, D), lambda b, i: (b * nt + i, 0)),
                  pl.BlockSpec((1, D), lambda b, i: (0, 0)),
                  pl.BlockSpec((None, 1, D), lambda b, i: (mod_row_of_batch(b), 0, 4)),
                  pl.BlockSpec((None, 1, D), lambda b, i: (mod_row_of_batch(b), 0, 3)),
                  pl.BlockSpec((D, P_HEADS * P_QDIM), lambda b, i: (0, 0)),
                  pl.BlockSpec((2 * P_HEADS, P_NKEYS, P_QDIM // 2), lambda b, i: (0, 0, 0))],
        out_specs=[pl.BlockSpec((tm, D), lambda b, i: (b * nt + i, 0)),
                   pl.BlockSpec((2 * P_HEADS, P_NKEYS, tm), lambda b, i: (0, 0, b * nt + i))],
        out_shape=[jax.ShapeDtypeStruct((n, D), BF16),
                   jax.ShapeDtypeStruct((2 * P_HEADS, P_NKEYS, n), F32)],
        compiler_params=_cp("parallel", "parallel"),
        name="peer_scores",
    )(x2, norm_g.reshape(1, D), mod3, mod3, wq_bf, keys.reshape(2 * P_HEADS, P_NKEYS, P_QDIM // 2))


def _batcher_pairs(n):
    pairs = []
    p = 1
    while p < n:
        k = p
        while k >= 1:
            for j in range(k % p, n - k, 2 * k):
                for i in range(min(k, n - j - k)):
                    if (i + j) // (2 * p) == (i + j + k) // (2 * p):
                        pairs.append((i + j, i + j + k))
            k //= 2
        p *= 2
    return pairs


_SORT16 = _batcher_pairs(P_TOPK)
_BITONIC16 = [(i, i + d) for d in (8, 4, 2, 1) for i in range(P_TOPK) if i & d == 0]


def _compare_exchange(x, pairs):
    for i, j in pairs:
        x[i], x[j] = jnp.maximum(x[i], x[j]), jnp.minimum(x[i], x[j])
    return x


def _top16_sorted(slabs):
    x = _compare_exchange(list(slabs), _SORT16)
    for shift in (4, 2, 1):
        y = [pltpu.roll(x[P_TOPK - 1 - k], shift, 0) for k in range(P_TOPK)]
        x = [jnp.maximum(x[k], y[k]) for k in range(P_TOPK)]
        x = _compare_exchange(x, _BITONIC16)
    return x


def _peer_gate_tables(st_ref, rk_ref, be_ref, cnt_ref, al_ref, top_ref):
    nblk = st_ref.shape[-1] // LANES
    sub = lax.broadcasted_iota(jnp.int32, (8, LANES), 0)
    ninf = jnp.full((8, LANES), -jnp.inf, F32)

    def block(it, carry):
        h = it // nblk
        lb = it % nblk
        lanes = pl.ds(pl.multiple_of(lb * LANES, LANES), LANES)
        hf = lb // (PEER_HALF // LANES)
        hl = pl.ds(pl.multiple_of((lb % (PEER_HALF // LANES)) * LANES, LANES), LANES)
        for p in range(2):
            top = _top16_sorted([st_ref[2 * h + p, 8 * k:8 * k + 8, lanes] for k in range(P_TOPK)])
            for k in range(P_TOPK):
                top_ref[p * P_TOPK + k:p * P_TOPK + k + 1, lanes] = top[k][0:1]
        a16 = top_ref[0:P_TOPK, lanes]
        b16 = top_ref[P_TOPK:2 * P_TOPK, lanes]
        b8 = b16[0:8]
        cand = [a16[0:1] + b8, a16[0:1] + b16[8:16]]
        for p in range(2, 9):
            cand.append(jnp.where(sub < P_TOPK // p, a16[p - 1:p] + b8, -jnp.inf))
        cand.append(a16[8:16] + b16[0:1])
        best = _top16_sorted(cand + [ninf] * (P_TOPK - len(cand)))
        tau = best[P_TOPK - 1][0:1]
        z = jnp.ones_like(tau)
        for k in range(1, P_TOPK):
            z = z + jnp.exp(best[k][0:1] - best[0][0:1])
        s1 = st_ref[2 * h, :, lanes]
        s2 = st_ref[2 * h + 1, :, lanes]
        cnt = jnp.zeros_like(s1)
        rk = jnp.ones_like(s2)
        for q in range(P_TOPK):
            bq = b16[q:q + 1]
            theta = jnp.min(jnp.where(a16 + bq >= tau, a16, jnp.inf), axis=0, keepdims=True)
            cnt = jnp.where(s1 >= theta, q + 1.0, cnt)
            rk = jnp.where(bq > s2, q + 2.0, rk)
        cnt_ref[hf, h, :, hl] = cnt
        rk_ref[hf, h, :, hl] = rk.astype(BF16)
        al_ref[hf, h, :, hl] = jnp.exp(s1 - a16[0:1]) / z
        be_ref[hf, h, :, hl] = jnp.exp(s2 - b16[0:1]).astype(BF16)
        return carry

    lax.fori_loop(0, P_HEADS * nblk, block, 0)


def _peer_chunk(chunk, act_ref, rk_ref, be_ref, cnt_ref, al_ref, w_ref):
    nrow = w_ref.shape[0] // P_NKEYS
    for ii in range(nrow):
        i = chunk * nrow + ii
        wrow = None
        for h in range(P_HEADS):
            cnt = cnt_ref[h, pl.ds(i, 1), :].astype(BF16)
            al = al_ref[h, pl.ds(i, 1), :].astype(BF16)
            term = jnp.where(rk_ref[h] <= cnt, al * be_ref[h], 0.0)
            wrow = term if wrow is None else wrow + term
        rows = slice(ii * P_NKEYS, (ii + 1) * P_NKEYS)
        act = act_ref[rows, :]
        gl = 0.5 * act * (1.0 + lax.erf(act * (2.0 ** -0.5)))
        w_ref[rows, :] = wrow * gl.astype(BF16)


def _peer_dense_kernel(x_ref, h_ref, st_ref, u_ref, vt_ref, gate_ref, o_ref,
                       rk_ref, be_ref, cnt_ref, al_ref, top_ref, act0_ref, act1_ref, w_ref, acc_ref, *, ne):
    e = pl.program_id(1)
    halves = range(w_ref.shape[0])

    def rows(hf):
        return slice(hf * PEER_HALF, (hf + 1) * PEER_HALF)

    @pl.when(e == 0)
    def _():
        for hf in halves:
            act0_ref[hf] = _dot_nt(u_ref[...], h_ref[rows(hf), :])
        _peer_gate_tables(st_ref, rk_ref, be_ref, cnt_ref, al_ref, top_ref)
        acc_ref[...] = jnp.zeros_like(acc_ref)

    def step(cur_ref, nxt_ref):
        for hf in halves:
            nxt_ref[hf] = _dot_nt(u_ref[...], h_ref[rows(hf), :])
            _peer_chunk(e - 1, cur_ref.at[hf], rk_ref.at[hf], be_ref.at[hf], cnt_ref.at[hf], al_ref.at[hf],
                        w_ref.at[hf])
            acc_ref[hf] += _dot(vt_ref[...], w_ref[hf])

    @pl.when(e % 2 == 1)
    def _():
        step(act0_ref, act1_ref)

    @pl.when((e > 0) & (e % 2 == 0))
    def _():
        step(act1_ref, act0_ref)

    @pl.when(e == ne)
    def _():
        for hf in halves:
            o_ref[rows(hf), :] = x_ref[rows(hf), :] + gate_ref[...] * acc_ref[hf].T


def _peer_dense(x2, h2, st, u_bf, vt_bf, mod3, mod_row_of_batch, nb, t):
    tm = 512 if t % 512 == 0 else 256
    nh = tm // PEER_HALF
    ec = PEER_EC
    nt = t // tm
    ne = P_EXPERTS // ec
    return pl.pallas_call(
        functools.partial(_peer_dense_kernel, ne=ne),
        grid=(nb * nt, ne + 1),
        in_specs=[pl.BlockSpec((tm, D), lambda i, e: (i, 0)),
                  pl.BlockSpec((tm, D), lambda i, e: (i, 0)),
                  pl.BlockSpec((2 * P_HEADS, P_NKEYS, tm), lambda i, e: (0, 0, i)),
                  pl.BlockSpec((ec, D), lambda i, e: (jnp.minimum(e, ne - 1), 0)),
                  pl.BlockSpec((None, D, ec), lambda i, e: (jnp.maximum(e - 1, 0), 0, 0)),
                  pl.BlockSpec((None, 1, D), lambda i, e: (mod_row_of_batch(i // nt), 0, 5))],
        out_specs=pl.BlockSpec((tm, D), lambda i, e: (i, 0)),
        out_shape=jax.ShapeDtypeStruct((nb * t, D), F32),
        scratch_shapes=[pltpu.VMEM((nh, P_HEADS, P_NKEYS, PEER_HALF), BF16),
                        pltpu.VMEM((nh, P_HEADS, P_NKEYS, PEER_HALF), BF16),
                        pltpu.VMEM((nh, P_HEADS, P_NKEYS, PEER_HALF), F32),
                        pltpu.VMEM((nh, P_HEADS, P_NKEYS, PEER_HALF), F32),
                        pltpu.VMEM((2 * P_TOPK, tm), F32),
                        pltpu.VMEM((nh, ec, PEER_HALF), F32),
                        pltpu.VMEM((nh, ec, PEER_HALF), F32),
                        pltpu.VMEM((nh, ec, PEER_HALF), BF16),
                        pltpu.VMEM((nh, D, PEER_HALF), F32)],
        compiler_params=pltpu.CompilerParams(dimension_semantics=("parallel", "arbitrary"),
                                             vmem_limit_bytes=PEER_VMEM_LIMIT),
        name="peer_dense",
    )(x2, h2, st, u_bf, vt_bf, mod3)


def _peer(x2, norm_g, mod3, mod_row_of_batch, wq_bf, keys, u_bf, vt_bf, nb, t):
    h2, st = _peer_scores(x2, norm_g, mod3, mod_row_of_batch, wq_bf, keys, nb, t)
    return _peer_dense(x2, h2, st, u_bf, vt_bf, mod3, mod_row_of_batch, nb, t)


def kernel(x, c, ctx, c_ctx, norm_mix, norm_ffn, w_mod, b_mod, w_in, w_out, a_qnorm, a_knorm, a_sink, b_qnorm,
           b_knorm, b_rpb, r7_conv, r7_w0, r7_w2, r7_a0, r7_a2, r7_g2, r7_kk, r7_ka, r7_rk, r7_lnw, r7_lnb,
           peer_wq, peer_keys, peer_u, peer_v):
    nb, s, _ = x.shape
    lc = ctx.shape[1]
    depth = w_in.shape[0]
    assert nb < 16 and nb % SCAN_NB == 0 and s % 512 == 0 and lc % 256 == 0
    rows = s // GRID_W

    cc = jnp.zeros((16, D), F32).at[:nb].set(c).at[nb].set(c_ctx)
    mod = _modulation(cc, w_mod, b_mod)
    rope_tabs = _rope_tables(s)
    lat_row = lambda b: b
    ctx_row = lambda b: nb

    x_lat = x.reshape(nb * s, D)
    x_ctx = ctx.reshape(nb * lc, D)
    scale = HD ** -0.5
    for l in range(depth):
        with_ctx = l < depth - 1
        mod3 = mod[l].reshape(16, 1, 6 * D)
        lp = {'r7_conv': r7_conv[l], 'r7_w0': r7_w0[l], 'r7_w2': r7_w2[l], 'r7_a0': r7_a0[l], 'r7_a2': r7_a2[l],
              'r7_g2': r7_g2[l], 'r7_kk': r7_kk[l], 'r7_ka': r7_ka[l], 'r7_rk': r7_rk[l], 'r7_lnw': r7_lnw[l],
              'r7_lnb': r7_lnb[l]}
        w_in_bf = w_in[l].astype(BF16)
        w_out_bf = w_out[l].astype(BF16)
        gain_a = jnp.concatenate([jnp.tile(a_qnorm[l] * scale, A_HEADS), jnp.tile(a_knorm[l], A_KV)]).reshape(1, 384)
        gain_b = jnp.concatenate([jnp.tile(b_qnorm[l] * scale, B_HEADS), jnp.tile(b_knorm[l], B_HEADS)]).reshape(1, 512)

        ab_lat, pc_lat = _inproj(x_lat, mod3, lat_row, norm_mix[l], w_in_bf, gain_a, gain_b, rope_tabs, nb, s)
        ab_ctx, pc_ctx = _inproj(x_ctx, mod3, ctx_row, norm_mix[l], w_in_bf, gain_a, gain_b, None, nb, lc)

        o_a = _attn_a(ab_lat, ab_ctx, a_sink[l], nb, s, lc)
        o_b = _attn_b(ab_lat, ab_ctx, _na_bias_tables(b_rpb[l], rows), nb, s, lc)

        prep_ctx = _r7prep(pc_ctx, lp, nb, lc)
        prep_lat = _r7prep(pc_lat, lp, nb, s)
        zero_state = jnp.zeros((2, nb, C_HEADS, HD, HD), F32)
        y_ctx, s_ctx = _scan(prep_ctx[:6], zero_state, nb, lc)
        y_lat, _ = _scan(prep_lat[:6], s_ctx, nb, s)

        x_lat = _outproj(x_lat, o_a, o_b, y_lat, prep_lat[7], prep_lat[6], lp, w_out_bf, mod3, lat_row, nb, s)
        wq_bf = peer_wq[l].astype(BF16)
        u_bf = peer_u[l].astype(BF16)
        vt_bf = peer_v[l].astype(BF16).reshape(P_EXPERTS // PEER_EC, PEER_EC, D).transpose(0, 2, 1)
        x_lat = _peer(x_lat, norm_ffn[l], mod3, lat_row, wq_bf, peer_keys[l], u_bf, vt_bf, nb, s)
        if with_ctx:
            o_ac, o_bc = _attn_ctx(ab_ctx, a_sink[l], nb, lc)
            x_ctx = _outproj(x_ctx, o_ac, o_bc, y_ctx, prep_ctx[7], prep_ctx[6], lp, w_out_bf, mod3, ctx_row, nb, lc)
            x_ctx = _peer(x_ctx, norm_ffn[l], mod3, ctx_row, wq_bf, peer_keys[l], u_bf, vt_bf, 1, nb * lc)
    return x_lat.reshape(nb, s, D)
```

```python
import functools

import numpy as np
import jax
import jax.numpy as jnp
from jax import lax
from jax.experimental import pallas as pl
from jax.experimental.pallas import tpu as pltpu

F32 = jnp.float32
BF16 = jnp.bfloat16
HI = lax.Precision.HIGHEST

D = 1024
DEPTH = 2
GRID_W = 64
HD = 64
LANES = 128
EPS = 1e-6
NEG = -1e30
A_HEADS, A_KV, A_BLOCK, A_WINDOW = 4, 2, 128, 128
A_STEP_BLOCKS = 2
B_HEADS, NA_ROWS, NA_COLS = 4, 8, 16
NA_STEP_ROWS = 8
C_HEADS = 8
C_W = 512
C_IN = 1920
AB_W = 1280
IN_W = AB_W + C_IN
GN_EPS = 64e-5
DECAY_SCALE = 0.6065306597126334
ROPE_BASE = 10000.0
P_HEADS, P_NKEYS, P_QDIM, P_TOPK = 8, 128, 256, 16
P_EXPERTS = P_NKEYS * P_NKEYS
SCAN_C = 64
SCAN_NB = 2
VMEM_LIMIT = 48 * 1024 * 1024
PEER_VMEM_LIMIT = 56 * 1024 * 1024
PEER_EC = 1024
PEER_HALF = 256


def _row_tile(t):
    return 512 if t % 512 == 0 else 256


def _cp(*sem):
    return pltpu.CompilerParams(dimension_semantics=sem, vmem_limit_bytes=VMEM_LIMIT)


def _dot(a, b, prec=None):
    return jnp.dot(a, b, precision=prec, preferred_element_type=F32)


def _dot_nt(a, b, prec=None):
    return lax.dot_general(a, b, (((1,), (1,)), ((), ())), precision=prec, preferred_element_type=F32)


SEG_W = 256


def _block_diag_ones():
    i = np.arange(SEG_W) // HD
    return jnp.asarray((i[:, None] == i[None, :]).astype(np.float32), BF16)


def _split2(x):
    hi = x.astype(BF16)
    return hi, (x - hi.astype(F32)).astype(BF16)


def _segsum(x, bd):
    hi, lo = _split2(x)
    outs = []
    for g0 in range(0, x.shape[1], SEG_W):
        w = min(SEG_W, x.shape[1] - g0)
        outs.append(_dot(hi[:, g0:g0 + w], bd[0:w, 0:w]) + _dot(lo[:, g0:g0 + w], bd[0:w, 0:w]))
    return outs[0] if len(outs) == 1 else jnp.concatenate(outs, axis=1)


def _dot3(a, b, nt=False):
    f = _dot_nt if nt else _dot
    a_hi, a_lo = _split2(a)
    b_hi, b_lo = _split2(b)
    return f(a_hi, b_hi) + (f(a_lo, b_hi) + f(a_hi, b_lo))


def _mod_kernel(c_ref, w_ref, b_ref, o_ref):
    c = c_ref[...]
    s = c * jax.nn.sigmoid(c)
    o_ref[...] = _dot(s, w_ref[...], HI) + b_ref[...]


def _modulation(cc, w_mod, b_mod):
    L, _, n = w_mod.shape
    tn = 2048
    return pl.pallas_call(
        _mod_kernel,
        grid=(L, n // tn),
        in_specs=[pl.BlockSpec((16, D), lambda l, j: (0, 0)),
                  pl.BlockSpec((None, D, tn), lambda l, j: (l, 0, j)),
                  pl.BlockSpec((None, 1, tn), lambda l, j: (l, 0, j))],
        out_specs=pl.BlockSpec((None, 16, tn), lambda l, j: (l, 0, j)),
        out_shape=jax.ShapeDtypeStruct((L, 16, n), F32),
        compiler_params=_cp("parallel", "parallel"),
        name="modulation",
    )(cc, w_mod, b_mod.reshape(L, 1, n))


def _swap16(x):
    n = x.shape[-1]
    lane = lax.broadcasted_iota(jnp.int32, x.shape, 1)
    fwd = pltpu.roll(x, n - 16, 1)
    bwd = pltpu.roll(x, 16, 1)
    return jnp.where((lane % 32) < 16, fwd, bwd)


def _head_rms(x, bd, gain):
    ss = _segsum(x * x, bd)
    return x * lax.rsqrt(ss * (1.0 / HD) + EPS) * gain


def _inproj_kernel(*refs, rope):
    if rope:
        (x_ref, g_ref, sc_ref, sh_ref, w_ref, ga_ref, gb_ref, bd_ref, cos_ref, sin_ref,
         ab_ref, c_ref) = refs
    else:
        (x_ref, g_ref, sc_ref, sh_ref, w_ref, ga_ref, gb_ref, bd_ref, ab_ref, c_ref) = refs
    x = x_ref[...]
    y = x * lax.rsqrt(jnp.mean(x * x, axis=-1, keepdims=True) + EPS) * g_ref[...]
    h = y * (1.0 + sc_ref[...]) + sh_ref[...]
    acc = _dot(h.astype(BF16), w_ref[...])
    bd = bd_ref[...]
    qa = _head_rms(acc[:, 0:384], bd, ga_ref[...])
    if rope:
        qa = qa * cos_ref[...] + _swap16(qa) * sin_ref[...]
    qb = _head_rms(acc[:, 512:1024], bd, gb_ref[...])
    ab_ref[:, 0:384] = qa.astype(BF16)
    ab_ref[:, 384:512] = acc[:, 384:512].astype(BF16)
    ab_ref[:, 512:1024] = qb.astype(BF16)
    ab_ref[:, 1024:1280] = acc[:, 1024:1280].astype(BF16)
    c_ref[...] = acc[:, AB_W:IN_W]


def _inproj(x2, mod3, mod_row_of_batch, norm_g, w_in_bf, gain_a, gain_b, rope_tabs, nb, t):
    tm = _row_tile(t)
    nt = t // tm
    rope = rope_tabs is not None
    in_specs = [
        pl.BlockSpec((tm, D), lambda b, i: (b * nt + i, 0)),
        pl.BlockSpec((1, D), lambda b, i: (0, 0)),
        pl.BlockSpec((None, 1, D), lambda b, i: (mod_row_of_batch(b), 0, 1)),
        pl.BlockSpec((None, 1, D), lambda b, i: (mod_row_of_batch(b), 0, 0)),
        pl.BlockSpec((D, IN_W), lambda b, i: (0, 0)),
        pl.BlockSpec((1, 384), lambda b, i: (0, 0)),
        pl.BlockSpec((1, 512), lambda b, i: (0, 0)),
        pl.BlockSpec((SEG_W, SEG_W), lambda b, i: (0, 0)),
    ]
    args = [x2, norm_g.reshape(1, D), mod3, mod3, w_in_bf, gain_a, gain_b,
            _block_diag_ones()]
    if rope:
        in_specs += [pl.BlockSpec((tm, 384), lambda b, i: (i, 0)),
                     pl.BlockSpec((tm, 384), lambda b, i: (i, 0))]
        args += list(rope_tabs)
    return pl.pallas_call(
        functools.partial(_inproj_kernel, rope=rope),
        grid=(nb, nt),
        in_specs=in_specs,
        out_specs=[pl.BlockSpec((tm, AB_W), lambda b, i: (b * nt + i, 0)),
                   pl.BlockSpec((tm, C_IN), lambda b, i: (b * nt + i, 0))],
        out_shape=[jax.ShapeDtypeStruct((nb * t, AB_W), BF16),
                   jax.ShapeDtypeStruct((nb * t, C_IN), F32)],
        compiler_params=_cp("parallel", "parallel"),
        name="inproj_rope" if rope else "inproj_ctx",
    )(*args)


def _rope_tables(s):
    tok = np.arange(s)
    inv = ROPE_BASE ** (-np.arange(0, 32, 2) / 32.0)
    ar = (tok // GRID_W)[:, None] * inv[None]
    ac = (tok % GRID_W)[:, None] * inv[None]
    cos = np.concatenate([np.cos(ar), np.cos(ar), np.cos(ac), np.cos(ac)], axis=1)
    sin = np.concatenate([-np.sin(ar), np.sin(ar), -np.sin(ac), np.sin(ac)], axis=1)
    return (jnp.asarray(np.tile(cos, (1, 6)), F32), jnp.asarray(np.tile(sin, (1, 6)), F32))


def _softmax_pv(s, v, sink=None):
    m = jnp.max(s, axis=-1, keepdims=True)
    if sink is not None:
        m = jnp.maximum(m, sink)
    p = jnp.exp(s - m)
    den = jnp.sum(p, axis=-1, keepdims=True)
    if sink is not None:
        den = den + jnp.exp(sink - m)
    return _dot(p.astype(BF16), v) / den


def _attn_a_kernel(q_ref, k0_ref, k1_ref, k2_ref, k3_ref, v0_ref, v1_ref, v2_ref, v3_ref, kc_ref, vc_ref,
                   sink_ref, o_ref, *, nblk):
    kb = [k0_ref[...], k1_ref[...], k2_ref[...], k3_ref[...]]
    vb = [v0_ref[...], v1_ref[...], v2_ref[...], v3_ref[...]]
    kc = kc_ref[...]
    vc = vc_ref[...]
    nk = 3 * A_BLOCK + kc.shape[0]
    g = A_HEADS // A_KV
    row = lax.broadcasted_iota(jnp.int32, (g * A_BLOCK, nk), 0) % A_BLOCK
    col = lax.broadcasted_iota(jnp.int32, (g * A_BLOCK, nk), 1)
    rel = col - A_BLOCK - row
    band = (jnp.abs(rel) <= A_WINDOW) | (col >= 3 * A_BLOCK)
    chains, ok, k_all, v_all = [], [], [], []
    for j in range(A_STEP_BLOCKS):
        n = pl.program_id(1) * A_STEP_BLOCKS + j
        ok.append(band & ((n > 0) | (col >= A_BLOCK)) & ((n < nblk - 1) | (col < 2 * A_BLOCK) | (col >= 3 * A_BLOCK)))
        k_all.append(jnp.concatenate(kb[j:j + 3] + [kc], axis=0))
        v_all.append(jnp.concatenate(vb[j:j + 3] + [vc], axis=0))
        chains += [(j, hk) for hk in range(A_KV)]

    def qg(j, hk):
        rows = slice(j * A_BLOCK, (j + 1) * A_BLOCK)
        return jnp.concatenate([q_ref[rows, (hk * g + i) * HD:(hk * g + i + 1) * HD] for i in range(g)], axis=0)

    sink = [jnp.concatenate([jnp.broadcast_to(sink_ref[:, hk * g + i:hk * g + i + 1], (A_BLOCK, 1))
                             for i in range(g)], axis=0) for hk in range(A_KV)]
    s = [jnp.where(ok[j], _dot_nt(qg(j, hk), k_all[j][:, hk * HD:(hk + 1) * HD]), NEG) for j, hk in chains]
    m = [jnp.maximum(jnp.max(s[i], axis=-1, keepdims=True), sink[hk]) for i, (j, hk) in enumerate(chains)]
    p = [jnp.exp(s[i] - m[i]) for i in range(len(chains))]
    den = [jnp.sum(p[i], axis=-1, keepdims=True) + jnp.exp(sink[hk] - m[i]) for i, (j, hk) in enumerate(chains)]
    o = [_dot(p[i].astype(BF16), v_all[j][:, hk * HD:(hk + 1) * HD]) / den[i] for i, (j, hk) in enumerate(chains)]
    for i, (j, hk) in enumerate(chains):
        for gi in range(g):
            hq = hk * g + gi
            o_ref[j * A_BLOCK:(j + 1) * A_BLOCK, hq * HD:(hq + 1) * HD] = (
                o[i][gi * A_BLOCK:(gi + 1) * A_BLOCK].astype(BF16))


def _attn_a(ab_lat, ab_ctx, sink, nb, s, lc):
    nblk = s // A_BLOCK
    steps = nblk // A_STEP_BLOCKS

    def kv(col, d):
        return pl.BlockSpec(
            (A_BLOCK, 128), lambda b, i: (b * nblk + jnp.clip(i * A_STEP_BLOCKS + d, 0, nblk - 1), col))

    tq = A_STEP_BLOCKS * A_BLOCK
    return pl.pallas_call(
        functools.partial(_attn_a_kernel, nblk=nblk),
        grid=(nb, steps),
        in_specs=[pl.BlockSpec((tq, 256), lambda b, i: (b * steps + i, 0)),
                  kv(2, -1), kv(2, 0), kv(2, 1), kv(2, 2), kv(3, -1), kv(3, 0), kv(3, 1), kv(3, 2),
                  pl.BlockSpec((lc, 128), lambda b, i: (b, 2)),
                  pl.BlockSpec((lc, 128), lambda b, i: (b, 3)),
                  pl.BlockSpec((1, A_HEADS), lambda b, i: (0, 0))],
        out_specs=pl.BlockSpec((tq, 256), lambda b, i: (b * steps + i, 0)),
        out_shape=jax.ShapeDtypeStruct((nb * s, 256), BF16),
        compiler_params=_cp("parallel", "parallel"),
        name="attn_a",
    )(ab_lat, *([ab_lat] * 8), ab_ctx, ab_ctx, sink.reshape(1, A_HEADS))


def _attn_b_kernel(q_ref, k_ref, v_ref, kc_ref, vc_ref, bias_ref, o_ref, *, rows):
    nwin = NA_ROWS * GRID_W
    kc = kc_ref[...]
    vc = vc_ref[...]
    q, kw, vw, tab = [], [], [], []
    for rr in range(NA_STEP_ROWS):
        r = pl.program_id(1) * NA_STEP_ROWS + rr
        rs = jnp.clip(r - NA_ROWS // 2, 0, rows - NA_ROWS)
        start = pl.multiple_of(rs * GRID_W, GRID_W)
        tab.append(rs - r + NA_ROWS - 1)
        q.append(q_ref[rr * GRID_W:(rr + 1) * GRID_W, :])
        kw.append(k_ref[pl.ds(start, nwin), :])
        vw.append(v_ref[pl.ds(start, nwin), :])
    chains = [(rr, h) for rr in range(NA_STEP_ROWS) for h in range(B_HEADS)]

    def hs(x, h):
        return x[:, h * HD:(h + 1) * HD]

    s_loc = [_dot_nt(hs(q[rr], h), hs(kw[rr], h)) + bias_ref[tab[rr], h] for rr, h in chains]
    s_ctx = [_dot_nt(hs(q[rr], h), hs(kc, h)) for rr, h in chains]
    m = [jnp.maximum(jnp.max(a, axis=-1, keepdims=True), jnp.max(b, axis=-1, keepdims=True))
         for a, b in zip(s_loc, s_ctx)]
    p_loc = [jnp.exp(a - mm) for a, mm in zip(s_loc, m)]
    p_ctx = [jnp.exp(b - mm) for b, mm in zip(s_ctx, m)]
    den = [jnp.sum(a, axis=-1, keepdims=True) + jnp.sum(b, axis=-1, keepdims=True) for a, b in zip(p_loc, p_ctx)]
    o = [(_dot(p_loc[i].astype(BF16), hs(vw[rr], h)) + _dot(p_ctx[i].astype(BF16), hs(vc, h))) / den[i]
         for i, (rr, h) in enumerate(chains)]
    for i, (rr, h) in enumerate(chains):
        o_ref[rr * GRID_W:(rr + 1) * GRID_W, h * HD:(h + 1) * HD] = o[i].astype(BF16)


def _na_bias_tables(rpb, rows):
    kr = NA_ROWS
    qc = np.arange(GRID_W)
    win_start = np.clip(qc - NA_COLS // 2, 0, GRID_W - NA_COLS)
    kcol = np.arange(GRID_W)
    valid = (kcol[None, :] >= win_start[:, None]) & (kcol[None, :] < win_start[:, None] + NA_COLS)
    pad = GRID_W - NA_COLS
    rp = jnp.pad(rpb.astype(F32), ((0, 0), (0, 0), (pad, pad)))
    toe = jnp.stack([rp[:, :, GRID_W - 1 - q:2 * GRID_W - 1 - q] for q in range(GRID_W)], axis=2)
    toe = jnp.where(jnp.asarray(valid[None, None]), toe, NEG)
    tabs = [toe[:, off:off + kr].transpose(0, 2, 1, 3).reshape(rpb.shape[0], GRID_W, kr * GRID_W)
            for off in range(kr)]
    return jnp.stack(tabs, axis=0)


def _attn_b(ab_lat, ab_ctx, bias_tabs, nb, s, lc):
    rows = s // GRID_W
    steps = rows // NA_STEP_ROWS
    tq = NA_STEP_ROWS * GRID_W
    return pl.pallas_call(
        functools.partial(_attn_b_kernel, rows=rows),
        grid=(nb, steps),
        in_specs=[pl.BlockSpec((tq, 256), lambda b, r: (b * steps + r, 2)),
                  pl.BlockSpec((s, 256), lambda b, r: (b, 3)),
                  pl.BlockSpec((s, 256), lambda b, r: (b, 4)),
                  pl.BlockSpec((lc, 256), lambda b, r: (b, 3)),
                  pl.BlockSpec((lc, 256), lambda b, r: (b, 4)),
                  pl.BlockSpec((NA_ROWS, B_HEADS, GRID_W, NA_ROWS * GRID_W), lambda b, r: (0, 0, 0, 0))],
        out_specs=pl.BlockSpec((tq, 256), lambda b, r: (b * steps + r, 0)),
        out_shape=jax.ShapeDtypeStruct((nb * s, 256), BF16),
        compiler_params=_cp("parallel", "arbitrary"),
        name="attn_b",
    )(ab_lat, ab_lat, ab_lat, ab_ctx, ab_ctx, bias_tabs)


def _attn_ctx_kernel(ab_ref, sink_ref, oa_ref, ob_ref):
    lc = ab_ref.shape[0]
    g = A_HEADS // A_KV
    for hk in range(A_KV):
        qg = jnp.concatenate([ab_ref[:, (hk * g + j) * HD:(hk * g + j + 1) * HD] for j in range(g)], axis=0)
        s = _dot_nt(qg, ab_ref[:, 256 + hk * HD:256 + (hk + 1) * HD])
        sink = jnp.concatenate(
            [jnp.broadcast_to(sink_ref[:, hk * g + j:hk * g + j + 1], (lc, 1)) for j in range(g)], axis=0)
        o = _softmax_pv(s, ab_ref[:, 384 + hk * HD:384 + (hk + 1) * HD], sink)
        for j in range(g):
            hq = hk * g + j
            oa_ref[:, hq * HD:(hq + 1) * HD] = o[j * lc:(j + 1) * lc].astype(BF16)
    for h in range(B_HEADS):
        s = _dot_nt(ab_ref[:, 512 + h * HD:512 + (h + 1) * HD], ab_ref[:, 768 + h * HD:768 + (h + 1) * HD])
        o = _softmax_pv(s, ab_ref[:, 1024 + h * HD:1024 + (h + 1) * HD])
        ob_ref[:, h * HD:(h + 1) * HD] = o.astype(BF16)


def _attn_ctx(ab_ctx, sink, nb, lc):
    return pl.pallas_call(
        _attn_ctx_kernel,
        grid=(nb,),
        in_specs=[pl.BlockSpec((lc, AB_W), lambda b: (b, 0)),
                  pl.BlockSpec((1, A_HEADS), lambda b: (0, 0))],
        out_specs=[pl.BlockSpec((lc, 256), lambda b: (b, 0)),
                   pl.BlockSpec((lc, 256), lambda b: (b, 0))],
        out_shape=[jax.ShapeDtypeStruct((nb * lc, 256), BF16),
                   jax.ShapeDtypeStruct((nb * lc, 256), BF16)],
        compiler_params=_cp("parallel"),
        name="attn_ctx",
    )(ab_ctx, sink.reshape(1, A_HEADS))


def _r7prep_kernel(x_ref, prev_ref, next_ref, cw_ref, kkw_ref, ka_ref, rk_ref, w0_ref, w2_ref, a0_ref, a2_ref,
                   g2_ref, bd_ref,
                   r_ref, v_ref, kk_ref, lw_ref, beta_ref, kd_ref, g_ref, bonus_ref, *, nt):
    i = pl.program_id(1)
    tt = x_ref.shape[0]
    x = x_ref[...]
    row = lax.broadcasted_iota(jnp.int32, x.shape, 0)
    before = jnp.where(i > 0, prev_ref[7:8, :], 0.0)
    after = jnp.where(i < nt - 1, next_ref[0:1, :], 0.0)
    x_m1 = jnp.where(row == 0, before, pltpu.roll(x, 1, 0))
    x_p1 = jnp.where(row == tt - 1, after, pltpu.roll(x, tt - 1, 0))
    xc = x_m1 * cw_ref[0:1, :] + x * cw_ref[1:2, :] + x_p1 * cw_ref[2:3, :]
    r = xc[:, 0:512]
    k = xc[:, 512:1024]
    v = xc[:, 1024:1536]
    wd = xc[:, 1536:1664]
    ad = xc[:, 1664:1792]
    gd = xc[:, 1792:1920]
    bd = bd_ref[...]
    kkh = k * kkw_ref[...]
    kk = kkh / jnp.maximum(jnp.sqrt(_segsum(kkh * kkh, bd)), 1e-12)
    zw = w0_ref[...] + _dot3(jnp.tanh(wd), w2_ref[...])
    za = a0_ref[...] + _dot3(ad, a2_ref[...])
    a = jax.nn.sigmoid(za)
    ka = ka_ref[...]
    kd0 = k * (1.0 + (a[:, 0:512] - 1.0) * ka)
    kd1 = k * (1.0 + (a[:, 512:1024] - 1.0) * ka)
    r_ref[...] = r.astype(BF16)
    v_ref[...] = v.astype(BF16)
    kk_ref[...] = kk.astype(BF16)
    lw_ref[...] = -DECAY_SCALE * jax.nn.sigmoid(zw)
    beta_ref[:, 0:512] = (kk * a[:, 0:512]).astype(BF16)
    beta_ref[:, 512:1024] = (kk * a[:, 512:1024]).astype(BF16)
    kd_ref[:, 0:512] = kd0.astype(BF16)
    kd_ref[:, 512:1024] = kd1.astype(BF16)
    g_ref[...] = _dot3(jax.nn.sigmoid(gd), g2_ref[...])
    bonus_ref[...] = _segsum(r * (kd0 + kd1) * rk_ref[...], bd) * v


def _bd2(w):
    z = jnp.zeros_like(w[0])
    return jnp.concatenate([jnp.concatenate([w[0], z], axis=1), jnp.concatenate([z, w[1]], axis=1)], axis=0)


def _r7prep(pc, lp, nb, t):
    tt = _row_tile(t)
    nt = t // tt
    nrow8 = t // 8

    def full(shape):
        return pl.BlockSpec(shape, lambda b, i: (0,) * len(shape))

    def rowspec(w):
        return pl.BlockSpec((tt, w), lambda b, i: (b * nt + i, 0))

    outs = [C_W, C_W, C_W, 2 * C_W, 2 * C_W, 2 * C_W, C_W, C_W]
    dtypes = [BF16, BF16, BF16, F32, BF16, BF16, F32, F32]
    return pl.pallas_call(
        functools.partial(_r7prep_kernel, nt=nt),
        grid=(nb, nt),
        in_specs=[rowspec(C_IN),
                  pl.BlockSpec((8, C_IN), lambda b, i: (b * nrow8 + jnp.maximum(i * (tt // 8) - 1, 0), 0)),
                  pl.BlockSpec((8, C_IN), lambda b, i: (b * nrow8 + jnp.minimum((i + 1) * (tt // 8), nrow8 - 1), 0)),
                  full((3, C_IN)), full((1, C_W)), full((1, C_W)), full((1, C_W)),
                  full((1, 2 * C_W)), full((128, 2 * C_W)), full((1, 2 * C_W)), full((128, 2 * C_W)),
                  full((128, C_W)), full((SEG_W, SEG_W))],
        out_specs=[rowspec(w) for w in outs],
        out_shape=[jax.ShapeDtypeStruct((nb * t, w), dt) for w, dt in zip(outs, dtypes)],
        compiler_params=_cp("parallel", "parallel"),
        name="r7prep",
    )(pc, pc, pc, lp['r7_conv'], lp['r7_kk'].reshape(1, C_W), lp['r7_ka'].reshape(1, C_W),
      lp['r7_rk'].reshape(1, C_W), lp['r7_w0'].reshape(1, 2 * C_W), _bd2(lp['r7_w2']),
      lp['r7_a0'].reshape(1, 2 * C_W), _bd2(lp['r7_a2']), lp['r7_g2'], _block_diag_ones())


def _cumsum_rows(tri, x):
    t = tri.astype(BF16)
    hi = x.astype(BF16)
    rest = x - hi.astype(F32)
    mid = rest.astype(BF16)
    lo = (rest - mid.astype(F32)).astype(BF16)
    return _dot(t, hi) + (_dot(t, mid) + _dot(t, lo))


def _mm(a, b):
    return _dot(a.astype(BF16), b.astype(BF16))


def _mm_nt(a, b):
    return _dot_nt(a.astype(BF16), b.astype(BF16))


def _scan_kernel(rf_ref, vf_ref, kkf_ref, lwf_ref, betaf_ref, kdf_ref,
                 rb_ref, vb_ref, kkb_ref, lwb_ref, betab_ref, kdb_ref, s0_ref,
                 yf_ref, yb_ref, sf_ref, st_ref, *, nc):
    c = pl.program_id(1)
    cl = SCAN_C

    @pl.when(c == 0)
    def _():
        st_ref[...] = s0_ref[...]

    ti = lax.broadcasted_iota(jnp.int32, (cl, cl), 0)
    si = lax.broadcasted_iota(jnp.int32, (cl, cl), 1)
    eye = (ti == si).astype(F32)
    incl = [si <= ti, si >= ti]
    strict = [si < ti, si > ti]

    def hs(x, h):
        return x[:, h * HD:(h + 1) * HD]

    pre = {}
    for bb in range(SCAN_NB):
        for d, (r_ref, v_ref, kk_ref, lw_ref, beta_ref, kd_ref) in enumerate(
                [(rf_ref, vf_ref, kkf_ref, lwf_ref, betaf_ref, kdf_ref),
                 (rb_ref, vb_ref, kkb_ref, lwb_ref, betab_ref, kdb_ref)]):
            lw = lw_ref[bb]
            cum = _cumsum_rows(incl[d], lw)
            e_neg = jnp.exp(-cum)
            pre[bb, d] = dict(r_hat=r_ref[bb] * jnp.exp(cum), a_hat=-kk_ref[bb] * jnp.exp(cum - lw),
                              b_til=beta_ref[bb] * e_neg, k_til=kd_ref[bb] * e_neg,
                              wtot=jnp.exp(jnp.sum(lw, axis=0, keepdims=True)), vv=v_ref[bb])

    chains = [(bb, d, h) for bb in range(SCAN_NB) for d in range(2) for h in range(C_HEADS)]
    n_ch = range(len(chains))

    def part(name, i):
        bb, d, h = chains[i]
        return hs(pre[bb, d][name], h)

    def dirn(i):
        return chains[i][1]

    ar = [jnp.concatenate([part('a_hat', i), part('r_hat', i)], axis=0) for i in n_ch]
    bk = [jnp.concatenate([part('b_til', i), part('k_til', i)], axis=0) for i in n_ch]
    m = [_mm_nt(ar[i], bk[i]) for i in n_ch]
    l_k = [jnp.where(strict[dirn(i)], m[i][0:cl, cl:2 * cl], 0.0) for i in n_ch]
    r_b = [jnp.where(incl[dirn(i)], m[i][cl:2 * cl, 0:cl], 0.0) for i in n_ch]
    r_k = [jnp.where(incl[dirn(i)], m[i][cl:2 * cl, cl:2 * cl], 0.0) for i in n_ch]
    p = [jnp.where(strict[dirn(i)], m[i][0:cl, 0:cl], 0.0) for i in n_ch]
    t_inv = [eye + p[i] for i in n_ch]
    n = 2
    while n < cl:
        p = [_mm(p[i], p[i]) for i in n_ch]
        t_inv = [t_inv[i] + _mm(t_inv[i], p[i]) for i in n_ch]
        n *= 2
    x1 = [_mm(l_k[i], part('vv', i)) for i in n_ch]
    y0 = [_mm(r_k[i], part('vv', i)) for i in n_ch]
    ua = [_mm(t_inv[i], jnp.concatenate([x1[i], part('a_hat', i)], axis=1)) for i in n_ch]
    s0 = [st_ref[d, bb, h] for bb, d, h in chains]
    as0 = [_mm_nt(jnp.concatenate([ua[i][:, HD:2 * HD], part('r_hat', i)], axis=0), s0[i])
           for i in n_ch]
    u = [ua[i][:, 0:HD] + as0[i][0:cl] for i in n_ch]
    y = [y0[i] + as0[i][cl:2 * cl] + _mm(r_b[i], u[i]) for i in n_ch]
    for i in n_ch:
        bb, d, h = chains[i]
        (yf_ref, yb_ref)[d][bb, :, h * HD:(h + 1) * HD] = y[i]
    for i in n_ch:
        bb, d, h = chains[i]
        uv_t = jnp.concatenate([u[i], part('vv', i)], axis=0).T
        st_ref[d, bb, h] = (s0[i] + _mm(uv_t, bk[i])) * part('wtot', i)

    @pl.when(c == nc - 1)
    def _():
        sf_ref[...] = st_ref[...]


def _scan(prep, s0, nb, t):
    nc = t // SCAN_C
    assert nb % SCAN_NB == 0
    r, v, kk, lw, beta, kd = [a.reshape(nb, t, a.shape[-1]) for a in prep]

    def chunk(d, c):
        return nc - 1 - c if d else c

    def specs(d):
        shared = pl.BlockSpec((SCAN_NB, SCAN_C, C_W), lambda b, c: (b, chunk(d, c), 0))
        perdir = pl.BlockSpec((SCAN_NB, SCAN_C, C_W), lambda b, c: (b, chunk(d, c), d))
        return [shared, shared, shared, perdir, perdir, perdir]

    st_spec = pl.BlockSpec((2, SCAN_NB, C_HEADS, HD, HD), lambda b, c: (0, b, 0, 0, 0))
    yf, yb, sf = pl.pallas_call(
        functools.partial(_scan_kernel, nc=nc),
        grid=(nb // SCAN_NB, nc),
        in_specs=specs(0) + specs(1) + [st_spec],
        out_specs=[pl.BlockSpec((SCAN_NB, SCAN_C, C_W), lambda b, c: (b, chunk(0, c), 0)),
                   pl.BlockSpec((SCAN_NB, SCAN_C, C_W), lambda b, c: (b, chunk(1, c), 0)), st_spec],
        out_shape=[jax.ShapeDtypeStruct((nb, t, C_W), F32), jax.ShapeDtypeStruct((nb, t, C_W), F32),
                   jax.ShapeDtypeStruct((2, nb, C_HEADS, HD, HD), F32)],
        scratch_shapes=[pltpu.VMEM((2, SCAN_NB, C_HEADS, HD, HD), F32)],
        compiler_params=_cp("parallel", "arbitrary"),
        name="r7scan",
    )(r, v, kk, lw, beta, kd, r, v, kk, lw, beta, kd, s0)
    return (yf.reshape(nb * t, C_W), yb.reshape(nb * t, C_W)), sf


def _outproj_kernel(x_ref, oa_ref, ob_ref, y0_ref, y1_ref, bonus_ref, g_ref, lnw_ref, lnb_ref, bd_ref, w_ref,
                    gate_ref, o_ref):
    bd = bd_ref[...]
    y = y0_ref[...] + y1_ref[...]
    mu = _segsum(y, bd) * (1.0 / HD)
    yc = y - mu
    var = _segsum(yc * yc, bd) * (1.0 / HD)
    yn = yc * lax.rsqrt(var + GN_EPS) * lnw_ref[...] + lnb_ref[...]
    oc = ((yn + bonus_ref[...]) * g_ref[...]).astype(BF16)
    acc = (_dot(oa_ref[...], w_ref[0:256, :]) + _dot(ob_ref[...], w_ref[256:512, :])
           + _dot(oc, w_ref[512:1024, :]))
    o_ref[...] = x_ref[...] + gate_ref[...] * acc


def _outproj(x2, oa, ob, y, bonus, g, lp, w_out_bf, mod3, mod_row_of_batch, nb, t):
    tm = _row_tile(t)
    nt = t // tm

    def rowspec(w):
        return pl.BlockSpec((tm, w), lambda b, i: (b * nt + i, 0))

    def full(shape):
        return pl.BlockSpec(shape, lambda b, i: (0,) * len(shape))

    return pl.pallas_call(
        _outproj_kernel,
        grid=(nb, nt),
        in_specs=[rowspec(D), rowspec(256), rowspec(256),
                  rowspec(C_W), rowspec(C_W),
                  rowspec(C_W), rowspec(C_W), full((1, C_W)), full((1, C_W)), full((SEG_W, SEG_W)), full((D, D)),
                  pl.BlockSpec((None, 1, D), lambda b, i: (mod_row_of_batch(b), 0, 2))],
        out_specs=rowspec(D),
        out_shape=jax.ShapeDtypeStruct((nb * t, D), F32),
        compiler_params=_cp("parallel", "parallel"),
        name="outproj",
    )(x2, oa, ob, y[0], y[1], bonus, g, lp['r7_lnw'].reshape(1, C_W), lp['r7_lnb'].reshape(1, C_W),
      _block_diag_ones(), w_out_bf, mod3)


def _peer_score_kernel(x_ref, g_ref, sc_ref, sh_ref, wq_ref, keys_ref, h_ref, st_ref):
    x = x_ref[...]
    y = x * lax.rsqrt(jnp.mean(x * x, axis=-1, keepdims=True) + EPS) * g_ref[...]
    h = (y * (1.0 + sc_ref[...]) + sh_ref[...]).astype(BF16)
    h_ref[...] = h
    q = _dot(h, wq_ref[...])
    half = P_QDIM // 2
    for hp in range(2 * P_HEADS):
        st_ref[hp] = _dot3(keys_ref[hp], q[:, hp * half:(hp + 1) * half], nt=True)


def _peer_scores(x2, norm_g, mod3, mod_row_of_batch, wq_bf, keys, nb, t):
    tm = _row_tile(t)
    nt = t // tm
    n = nb * t
    return pl.pallas_call(
        _peer_score_kernel,
        grid=(nb, nt),
        in_specs=[pl.BlockSpec((tm, D), lambda b, i: (b * nt + i, 0)),
                  pl.BlockSpec((1, D), lambda b, i: (0, 0)),
                  pl.BlockSpec((None, 1, D), lambda b, i: (mod_row_of_batch(b), 0, 4)),
                  pl.BlockSpec((None, 1, D), lambda b, i: (mod_row_of_batch(b), 0, 3)),
                  pl.BlockSpec((D, P_HEADS * P_QDIM), lambda b, i: (0, 0)),
                  pl.BlockSpec((2 * P_HEADS, P_NKEYS, P_QDIM // 2), lambda b, i: (0, 0, 0))],
        out_specs=[pl.BlockSpec((tm, D), lambda b, i: (b * nt + i, 0)),
                   pl.BlockSpec((2 * P_HEADS, P_NKEYS, tm), lambda b, i: (0, 0, b * nt + i))],
        out_shape=[jax.ShapeDtypeStruct((n, D), BF16),
                   jax.ShapeDtypeStruct((2 * P_HEADS, P_NKEYS, n), F32)],
        compiler_params=_cp("parallel", "parallel"),
        name="peer_scores",
    )(x2, norm_g.reshape(1, D), mod3, mod3, wq_bf, keys.reshape(2 * P_HEADS, P_NKEYS, P_QDIM // 2))


def _batcher_pairs(n):
    pairs = []
    p = 1
    while p < n:
        k = p
        while k >= 1:
            for j in range(k % p, n - k, 2 * k):
                for i in range(min(k, n - j - k)):
                    if (i + j) // (2 * p) == (i + j + k) // (2 * p):
                        pairs.append((i + j, i + j + k))
            k //= 2
        p *= 2
    return pairs


_SORT16 = _batcher_pairs(P_TOPK)
_BITONIC16 = [(i, i + d) for d in (8, 4, 2, 1) for i in range(P_TOPK) if i & d == 0]


def _compare_exchange(x, pairs):
    for i, j in pairs:
        x[i], x[j] = jnp.maximum(x[i], x[j]), jnp.minimum(x[i], x[j])
    return x


def _top16_sorted(slabs):
    x = _compare_exchange(list(slabs), _SORT16)
    for shift in (4, 2, 1):
        y = [pltpu.roll(x[P_TOPK - 1 - k], shift, 0) for k in range(P_TOPK)]
        x = [jnp.maximum(x[k], y[k]) for k in range(P_TOPK)]
        x = _compare_exchange(x, _BITONIC16)
    return x


def _peer_gate_tables(st_ref, rk_ref, be_ref, cnt_ref, al_ref, top_ref):
    nblk = st_ref.shape[-1] // LANES
    sub = lax.broadcasted_iota(jnp.int32, (8, LANES), 0)
    ninf = jnp.full((8, LANES), -jnp.inf, F32)

    def block(it, carry):
        h = it // nblk
        lb = it % nblk
        lanes = pl.ds(pl.multiple_of(lb * LANES, LANES), LANES)
        hf = lb // (PEER_HALF // LANES)
        hl = pl.ds(pl.multiple_of((lb % (PEER_HALF // LANES)) * LANES, LANES), LANES)
        for p in range(2):
            top = _top16_sorted([st_ref[2 * h + p, 8 * k:8 * k + 8, lanes] for k in range(P_TOPK)])
            for k in range(P_TOPK):
                top_ref[p * P_TOPK + k:p * P_TOPK + k + 1, lanes] = top[k][0:1]
        a16 = top_ref[0:P_TOPK, lanes]
        b16 = top_ref[P_TOPK:2 * P_TOPK, lanes]
        b8 = b16[0:8]
        cand = [a16[0:1] + b8, a16[0:1] + b16[8:16]]
        for p in range(2, 9):
            cand.append(jnp.where(sub < P_TOPK // p, a16[p - 1:p] + b8, -jnp.inf))
        cand.append(a16[8:16] + b16[0:1])
        best = _top16_sorted(cand + [ninf] * (P_TOPK - len(cand)))
        tau = best[P_TOPK - 1][0:1]
        z = jnp.ones_like(tau)
        for k in range(1, P_TOPK):
            z = z + jnp.exp(best[k][0:1] - best[0][0:1])
        s1 = st_ref[2 * h, :, lanes]
        s2 = st_ref[2 * h + 1, :, lanes]
        cnt = jnp.zeros_like(s1)
        rk = jnp.ones_like(s2)
        for q in range(P_TOPK):
            bq = b16[q:q + 1]
            theta = jnp.min(jnp.where(a16 + bq >= tau, a16, jnp.inf), axis=0, keepdims=True)
            cnt = jnp.where(s1 >= theta, q + 1.0, cnt)
            rk = jnp.where(bq > s2, q + 2.0, rk)
        cnt_ref[hf, h, :, hl] = cnt
        rk_ref[hf, h, :, hl] = rk.astype(BF16)
        al_ref[hf, h, :, hl] = jnp.exp(s1 - a16[0:1]) / z
        be_ref[hf, h, :, hl] = jnp.exp(s2 - b16[0:1]).astype(BF16)
        return carry

    lax.fori_loop(0, P_HEADS * nblk, block, 0)


def _peer_chunk(chunk, act_ref, rk_ref, be_ref, cnt_ref, al_ref, w_ref):
    nrow = w_ref.shape[0] // P_NKEYS
    for ii in range(nrow):
        i = chunk * nrow + ii
        wrow = None
        for h in range(P_HEADS):
            cnt = cnt_ref[h, pl.ds(i, 1), :].astype(BF16)
            al = al_ref[h, pl.ds(i, 1), :].astype(BF16)
            term = jnp.where(rk_ref[h] <= cnt, al * be_ref[h], 0.0)
            wrow = term if wrow is None else wrow + term
        rows = slice(ii * P_NKEYS, (ii + 1) * P_NKEYS)
        act = act_ref[rows, :]
        gl = 0.5 * act * (1.0 + lax.erf(act * (2.0 ** -0.5)))
        w_ref[rows, :] = wrow * gl.astype(BF16)


def _peer_dense_kernel(x_ref, h_ref, st_ref, u_ref, vt_ref, gate_ref, o_ref,
                       rk_ref, be_ref, cnt_ref, al_ref, top_ref, act0_ref, act1_ref, w_ref, acc_ref, *, ne):
    e = pl.program_id(1)
    halves = range(w_ref.shape[0])

    def rows(hf):
        return slice(hf * PEER_HALF, (hf + 1) * PEER_HALF)

    @pl.when(e == 0)
    def _():
        for hf in halves:
            act0_ref[hf] = _dot_nt(u_ref[...], h_ref[rows(hf), :])
        _peer_gate_tables(st_ref, rk_ref, be_ref, cnt_ref, al_ref, top_ref)
        acc_ref[...] = jnp.zeros_like(acc_ref)

    def step(cur_ref, nxt_ref):
        for hf in halves:
            nxt_ref[hf] = _dot_nt(u_ref[...], h_ref[rows(hf), :])
            _peer_chunk(e - 1, cur_ref.at[hf], rk_ref.at[hf], be_ref.at[hf], cnt_ref.at[hf], al_ref.at[hf],
                        w_ref.at[hf])
            acc_ref[hf] += _dot(vt_ref[...], w_ref[hf])

    @pl.when(e % 2 == 1)
    def _():
        step(act0_ref, act1_ref)

    @pl.when((e > 0) & (e % 2 == 0))
    def _():
        step(act1_ref, act0_ref)

    @pl.when(e == ne)
    def _():
        for hf in halves:
            o_ref[rows(hf), :] = x_ref[rows(hf), :] + gate_ref[...] * acc_ref[hf].T


def _peer_dense(x2, h2, st, u_bf, vt_bf, mod3, mod_row_of_batch, nb, t):
    tm = _row_tile(t)
    nh = tm // PEER_HALF
    ec = PEER_EC
    nt = t // tm
    ne = P_EXPERTS // ec
    return pl.pallas_call(
        functools.partial(_peer_dense_kernel, ne=ne),
        grid=(nb * nt, ne + 1),
        in_specs=[pl.BlockSpec((tm, D), lambda i, e: (i, 0)),
                  pl.BlockSpec((tm, D), lambda i, e: (i, 0)),
                  pl.BlockSpec((2 * P_HEADS, P_NKEYS, tm), lambda i, e: (0, 0, i)),
                  pl.BlockSpec((ec, D), lambda i, e: (jnp.minimum(e, ne - 1), 0)),
                  pl.BlockSpec((None, D, ec), lambda i, e: (jnp.maximum(e - 1, 0), 0, 0)),
                  pl.BlockSpec((None, 1, D), lambda i, e: (mod_row_of_batch(i // nt), 0, 5))],
        out_specs=pl.BlockSpec((tm, D), lambda i, e: (i, 0)),
        out_shape=jax.ShapeDtypeStruct((nb * t, D), F32),
        scratch_shapes=[pltpu.VMEM((nh, P_HEADS, P_NKEYS, PEER_HALF), BF16),
                        pltpu.VMEM((nh, P_HEADS, P_NKEYS, PEER_HALF), BF16),
                        pltpu.VMEM((nh, P_HEADS, P_NKEYS, PEER_HALF), F32),
                        pltpu.VMEM((nh, P_HEADS, P_NKEYS, PEER_HALF), F32),
                        pltpu.VMEM((2 * P_TOPK, tm), F32),
                        pltpu.VMEM((nh, ec, PEER_HALF), F32),
                        pltpu.VMEM((nh, ec, PEER_HALF), F32),
                        pltpu.VMEM((nh, ec, PEER_HALF), BF16),
                        pltpu.VMEM((nh, D, PEER_HALF), F32)],
        compiler_params=pltpu.CompilerParams(dimension_semantics=("parallel", "arbitrary"),
                                             vmem_limit_bytes=PEER_VMEM_LIMIT),
        name="peer_dense",
    )(x2, h2, st, u_bf, vt_bf, mod3)


def _peer(x2, norm_g, mod3, mod_row_of_batch, wq_bf, keys, u_bf, vt_bf, nb, t):
    h2, st = _peer_scores(x2, norm_g, mod3, mod_row_of_batch, wq_bf, keys, nb, t)
    return _peer_dense(x2, h2, st, u_bf, vt_bf, mod3, mod_row_of_batch, nb, t)


def kernel(x, c, ctx, c_ctx, norm_mix, norm_ffn, w_mod, b_mod, w_in, w_out, a_qnorm, a_knorm, a_sink, b_qnorm,
           b_knorm, b_rpb, r7_conv, r7_w0, r7_w2, r7_a0, r7_a2, r7_g2, r7_kk, r7_ka, r7_rk, r7_lnw, r7_lnb,
           peer_wq, peer_keys, peer_u, peer_v):
    nb, s, _ = x.shape
    lc = ctx.shape[1]
    depth = w_in.shape[0]
    assert nb < 16 and nb % SCAN_NB == 0 and s % 512 == 0 and lc % 256 == 0
    rows = s // GRID_W

    cc = jnp.zeros((16, D), F32).at[:nb].set(c).at[nb].set(c_ctx)
    mod = _modulation(cc, w_mod, b_mod)
    rope_tabs = _rope_tables(s)
    lat_row = lambda b: b
    ctx_row = lambda b: nb

    x_lat = x.reshape(nb * s, D)
    x_ctx = ctx.reshape(nb * lc, D)
    scale = HD ** -0.5
    for l in range(depth):
        with_ctx = l < depth - 1
        mod3 = mod[l].reshape(16, 1, 6 * D)
        lp = {'r7_conv': r7_conv[l], 'r7_w0': r7_w0[l], 'r7_w2': r7_w2[l], 'r7_a0': r7_a0[l], 'r7_a2': r7_a2[l],
              'r7_g2': r7_g2[l], 'r7_kk': r7_kk[l], 'r7_ka': r7_ka[l], 'r7_rk': r7_rk[l], 'r7_lnw': r7_lnw[l],
              'r7_lnb': r7_lnb[l]}
        w_in_bf = w_in[l].astype(BF16)
        w_out_bf = w_out[l].astype(BF16)
        gain_a = jnp.concatenate([jnp.tile(a_qnorm[l] * scale, A_HEADS), jnp.tile(a_knorm[l], A_KV)]).reshape(1, 384)
        gain_b = jnp.concatenate([jnp.tile(b_qnorm[l] * scale, B_HEADS), jnp.tile(b_knorm[l], B_HEADS)]).reshape(1, 512)

        ab_lat, pc_lat = _inproj(x_lat, mod3, lat_row, norm_mix[l], w_in_bf, gain_a, gain_b, rope_tabs, nb, s)
        ab_ctx, pc_ctx = _inproj(x_ctx, mod3, ctx_row, norm_mix[l], w_in_bf, gain_a, gain_b, None, 1, nb * lc)

        o_a = _attn_a(ab_lat, ab_ctx, a_sink[l], nb, s, lc)
        o_b = _attn_b(ab_lat, ab_ctx, _na_bias_tables(b_rpb[l], rows), nb, s, lc)

        prep_ctx = _r7prep(pc_ctx, lp, nb, lc)
        prep_lat = _r7prep(pc_lat, lp, nb, s)
        zero_state = jnp.zeros((2, nb, C_HEADS, HD, HD), F32)
        y_ctx, s_ctx = _scan(prep_ctx[:6], zero_state, nb, lc)
        y_lat, _ = _scan(prep_lat[:6], s_ctx, nb, s)

        x_lat = _outproj(x_lat, o_a, o_b, y_lat, prep_lat[7], prep_lat[6], lp, w_out_bf, mod3, lat_row, nb, s)
        wq_bf = peer_wq[l].astype(BF16)
        u_bf = peer_u[l].astype(BF16)
        vt_bf = peer_v[l].astype(BF16).reshape(P_EXPERTS // PEER_EC, PEER_EC, D).transpose(0, 2, 1)
        x_lat = _peer(x_lat, norm_ffn[l], mod3, lat_row, wq_bf, peer_keys[l], u_bf, vt_bf, nb, s)
        if with_ctx:
            o_ac, o_bc = _attn_ctx(ab_ctx, a_sink[l], nb, lc)
            x_ctx = _outproj(x_ctx, o_ac, o_bc, y_ctx, prep_ctx[7], prep_ctx[6], lp, w_out_bf, mod3, ctx_row,
                             1, nb * lc)
            x_ctx = _peer(x_ctx, norm_ffn[l], mod3, ctx_row, wq_bf, peer_keys[l], u_bf, vt_bf, 1, nb * lc)
    return x_lat.reshape(nb, s, D)
```

```python
import functools

import numpy as np
import jax
import jax.numpy as jnp
from jax import lax
from jax.experimental import pallas as pl
from jax.experimental.pallas import tpu as pltpu

F32 = jnp.float32
BF16 = jnp.bfloat16
HI = lax.Precision.HIGHEST

D = 1024
GRID_W = 64
HD = 64
LANES = 128
EPS = 1e-6
NEG = -1e30
A_HEADS, A_KV, A_BLOCK, A_WINDOW = 4, 2, 128, 128
A_STEP_BLOCKS = 2
B_HEADS, NA_ROWS, NA_COLS = 4, 8, 16
NA_STEP_ROWS = 8
C_HEADS = 8
C_W = 512
C_IN = 1920
AB_W = 1280
IN_W = AB_W + C_IN
GN_EPS = 64e-5
DECAY_SCALE = 0.6065306597126334
ROPE_BASE = 10000.0
P_HEADS, P_NKEYS, P_QDIM, P_TOPK = 8, 128, 256, 16
P_EXPERTS = P_NKEYS * P_NKEYS
SCAN_C = 64
SCAN_NB = 2
VMEM_LIMIT = 48 * 1024 * 1024
PEER_VMEM_LIMIT = 56 * 1024 * 1024
PEER_EC = 1024
PEER_HALF = 256


def _row_tile(t):
    return 512 if t % 512 == 0 else 256


def _cp(*sem):
    return pltpu.CompilerParams(dimension_semantics=sem, vmem_limit_bytes=VMEM_LIMIT)


def _dot(a, b, prec=None):
    return jnp.dot(a, b, precision=prec, preferred_element_type=F32)


def _dot_nt(a, b, prec=None):
    return lax.dot_general(a, b, (((1,), (1,)), ((), ())), precision=prec, preferred_element_type=F32)


SEG_W = 256


def _block_diag_ones():
    i = np.arange(SEG_W) // HD
    return jnp.asarray((i[:, None] == i[None, :]).astype(np.float32), BF16)


def _split2(x):
    hi = x.astype(BF16)
    return hi, (x - hi.astype(F32)).astype(BF16)


def _segsum(x, bd):
    hi, lo = _split2(x)
    outs = []
    for g0 in range(0, x.shape[1], SEG_W):
        w = min(SEG_W, x.shape[1] - g0)
        outs.append(_dot(hi[:, g0:g0 + w], bd[0:w, 0:w]) + _dot(lo[:, g0:g0 + w], bd[0:w, 0:w]))
    return outs[0] if len(outs) == 1 else jnp.concatenate(outs, axis=1)


def _dot3(a, b, nt=False):
    f = _dot_nt if nt else _dot
    a_hi, a_lo = _split2(a)
    b_hi, b_lo = _split2(b)
    return f(a_hi, b_hi) + (f(a_lo, b_hi) + f(a_hi, b_lo))


def _mod_kernel(c_ref, w_ref, b_ref, o_ref):
    c = c_ref[...]
    s = c * jax.nn.sigmoid(c)
    o_ref[...] = _dot(s, w_ref[...], HI) + b_ref[...]


def _modulation(cc, w_mod, b_mod):
    L, _, n = w_mod.shape
    tn = 2048
    return pl.pallas_call(
        _mod_kernel,
        grid=(L, n // tn),
        in_specs=[pl.BlockSpec((16, D), lambda l, j: (0, 0)),
                  pl.BlockSpec((None, D, tn), lambda l, j: (l, 0, j)),
                  pl.BlockSpec((None, 1, tn), lambda l, j: (l, 0, j))],
        out_specs=pl.BlockSpec((None, 16, tn), lambda l, j: (l, 0, j)),
        out_shape=jax.ShapeDtypeStruct((L, 16, n), F32),
        compiler_params=_cp("parallel", "parallel"),
        name="modulation",
    )(cc, w_mod, b_mod.reshape(L, 1, n))


def _swap16(x):
    n = x.shape[-1]
    lane = lax.broadcasted_iota(jnp.int32, x.shape, 1)
    fwd = pltpu.roll(x, n - 16, 1)
    bwd = pltpu.roll(x, 16, 1)
    return jnp.where((lane % 32) < 16, fwd, bwd)


def _head_rms(x, bd, gain):
    ss = _segsum(x * x, bd)
    return x * lax.rsqrt(ss * (1.0 / HD) + EPS) * gain


def _inproj_kernel(*refs, rope):
    if rope:
        (x_ref, g_ref, sc_ref, sh_ref, w_ref, ga_ref, gb_ref, bd_ref, cos_ref, sin_ref,
         ab_ref, c_ref) = refs
    else:
        (x_ref, g_ref, sc_ref, sh_ref, w_ref, ga_ref, gb_ref, bd_ref, ab_ref, c_ref) = refs
    x = x_ref[...]
    y = x * lax.rsqrt(jnp.mean(x * x, axis=-1, keepdims=True) + EPS) * g_ref[...]
    h = y * (1.0 + sc_ref[...]) + sh_ref[...]
    acc = _dot(h.astype(BF16), w_ref[...])
    bd = bd_ref[...]
    qa = _head_rms(acc[:, 0:384], bd, ga_ref[...])
    if rope:
        qa = qa * cos_ref[...] + _swap16(qa) * sin_ref[...]
    qb = _head_rms(acc[:, 512:1024], bd, gb_ref[...])
    ab_ref[:, 0:384] = qa.astype(BF16)
    ab_ref[:, 384:512] = acc[:, 384:512].astype(BF16)
    ab_ref[:, 512:1024] = qb.astype(BF16)
    ab_ref[:, 1024:1280] = acc[:, 1024:1280].astype(BF16)
    c_ref[...] = acc[:, AB_W:IN_W]


def _inproj(x2, mod3, mod_row_of_batch, norm_g, w_in_bf, gain_a, gain_b, rope_tabs, nb, t):
    tm = _row_tile(t)
    nt = t // tm
    rope = rope_tabs is not None
    in_specs = [
        pl.BlockSpec((tm, D), lambda b, i: (b * nt + i, 0)),
        pl.BlockSpec((1, D), lambda b, i: (0, 0)),
        pl.BlockSpec((None, 1, D), lambda b, i: (mod_row_of_batch(b), 0, 1)),
        pl.BlockSpec((None, 1, D), lambda b, i: (mod_row_of_batch(b), 0, 0)),
        pl.BlockSpec((D, IN_W), lambda b, i: (0, 0)),
        pl.BlockSpec((1, 384), lambda b, i: (0, 0)),
        pl.BlockSpec((1, 512), lambda b, i: (0, 0)),
        pl.BlockSpec((SEG_W, SEG_W), lambda b, i: (0, 0)),
    ]
    args = [x2, norm_g.reshape(1, D), mod3, mod3, w_in_bf, gain_a, gain_b,
            _block_diag_ones()]
    if rope:
        in_specs += [pl.BlockSpec((tm, 384), lambda b, i: (i, 0)),
                     pl.BlockSpec((tm, 384), lambda b, i: (i, 0))]
        args += list(rope_tabs)
    return pl.pallas_call(
        functools.partial(_inproj_kernel, rope=rope),
        grid=(nb, nt),
        in_specs=in_specs,
        out_specs=[pl.BlockSpec((tm, AB_W), lambda b, i: (b * nt + i, 0)),
                   pl.BlockSpec((tm, C_IN), lambda b, i: (b * nt + i, 0))],
        out_shape=[jax.ShapeDtypeStruct((nb * t, AB_W), BF16),
                   jax.ShapeDtypeStruct((nb * t, C_IN), F32)],
        compiler_params=_cp("parallel", "parallel"),
        name="inproj_rope" if rope else "inproj_ctx",
    )(*args)


def _rope_tables(s):
    tok = np.arange(s)
    inv = ROPE_BASE ** (-np.arange(0, 32, 2) / 32.0)
    ar = (tok // GRID_W)[:, None] * inv[None]
    ac = (tok % GRID_W)[:, None] * inv[None]
    cos = np.concatenate([np.cos(ar), np.cos(ar), np.cos(ac), np.cos(ac)], axis=1)
    sin = np.concatenate([-np.sin(ar), np.sin(ar), -np.sin(ac), np.sin(ac)], axis=1)
    return (jnp.asarray(np.tile(cos, (1, 6)), F32), jnp.asarray(np.tile(sin, (1, 6)), F32))


def _softmax_pv(s, v, sink=None):
    m = jnp.max(s, axis=-1, keepdims=True)
    if sink is not None:
        m = jnp.maximum(m, sink)
    p = jnp.exp(s - m)
    den = jnp.sum(p, axis=-1, keepdims=True)
    if sink is not None:
        den = den + jnp.exp(sink - m)
    return _dot(p.astype(BF16), v) / den


def _attn_a_kernel(q_ref, k0_ref, k1_ref, k2_ref, k3_ref, v0_ref, v1_ref, v2_ref, v3_ref, kc_ref, vc_ref,
                   sink_ref, o_ref, *, nblk):
    kb = [k0_ref[...], k1_ref[...], k2_ref[...], k3_ref[...]]
    vb = [v0_ref[...], v1_ref[...], v2_ref[...], v3_ref[...]]
    kc = kc_ref[...]
    vc = vc_ref[...]
    nk = 3 * A_BLOCK + kc.shape[0]
    g = A_HEADS // A_KV
    row = lax.broadcasted_iota(jnp.int32, (g * A_BLOCK, nk), 0) % A_BLOCK
    col = lax.broadcasted_iota(jnp.int32, (g * A_BLOCK, nk), 1)
    rel = col - A_BLOCK - row
    band = (jnp.abs(rel) <= A_WINDOW) | (col >= 3 * A_BLOCK)
    chains, ok, k_all, v_all = [], [], [], []
    for j in range(A_STEP_BLOCKS):
        n = pl.program_id(1) * A_STEP_BLOCKS + j
        ok.append(band & ((n > 0) | (col >= A_BLOCK)) & ((n < nblk - 1) | (col < 2 * A_BLOCK) | (col >= 3 * A_BLOCK)))
        k_all.append(jnp.concatenate(kb[j:j + 3] + [kc], axis=0))
        v_all.append(jnp.concatenate(vb[j:j + 3] + [vc], axis=0))
        chains += [(j, hk) for hk in range(A_KV)]

    def qg(j, hk):
        rows = slice(j * A_BLOCK, (j + 1) * A_BLOCK)
        return jnp.concatenate([q_ref[rows, (hk * g + i) * HD:(hk * g + i + 1) * HD] for i in range(g)], axis=0)

    sink = [jnp.concatenate([jnp.broadcast_to(sink_ref[:, hk * g + i:hk * g + i + 1], (A_BLOCK, 1))
                             for i in range(g)], axis=0) for hk in range(A_KV)]
    s = [jnp.where(ok[j], _dot_nt(qg(j, hk), k_all[j][:, hk * HD:(hk + 1) * HD]), NEG) for j, hk in chains]
    m = [jnp.maximum(jnp.max(s[i], axis=-1, keepdims=True), sink[hk]) for i, (j, hk) in enumerate(chains)]
    p = [jnp.exp(s[i] - m[i]) for i in range(len(chains))]
    den = [jnp.sum(p[i], axis=-1, keepdims=True) + jnp.exp(sink[hk] - m[i]) for i, (j, hk) in enumerate(chains)]
    o = [_dot(p[i].astype(BF16), v_all[j][:, hk * HD:(hk + 1) * HD]) / den[i] for i, (j, hk) in enumerate(chains)]
    for i, (j, hk) in enumerate(chains):
        for gi in range(g):
            hq = hk * g + gi
            o_ref[j * A_BLOCK:(j + 1) * A_BLOCK, hq * HD:(hq + 1) * HD] = (
                o[i][gi * A_BLOCK:(gi + 1) * A_BLOCK].astype(BF16))


def _attn_a(ab_lat, ab_ctx, sink, nb, s, lc):
    nblk = s // A_BLOCK
    steps = nblk // A_STEP_BLOCKS

    def kv(col, d):
        return pl.BlockSpec(
            (A_BLOCK, 128), lambda b, i: (b * nblk + jnp.clip(i * A_STEP_BLOCKS + d, 0, nblk - 1), col))

    tq = A_STEP_BLOCKS * A_BLOCK
    return pl.pallas_call(
        functools.partial(_attn_a_kernel, nblk=nblk),
        grid=(nb, steps),
        in_specs=[pl.BlockSpec((tq, 256), lambda b, i: (b * steps + i, 0)),
                  kv(2, -1), kv(2, 0), kv(2, 1), kv(2, 2), kv(3, -1), kv(3, 0), kv(3, 1), kv(3, 2),
                  pl.BlockSpec((lc, 128), lambda b, i: (b, 2)),
                  pl.BlockSpec((lc, 128), lambda b, i: (b, 3)),
                  pl.BlockSpec((1, A_HEADS), lambda b, i: (0, 0))],
        out_specs=pl.BlockSpec((tq, 256), lambda b, i: (b * steps + i, 0)),
        out_shape=jax.ShapeDtypeStruct((nb * s, 256), BF16),
        compiler_params=_cp("parallel", "parallel"),
        name="attn_a",
    )(ab_lat, *([ab_lat] * 8), ab_ctx, ab_ctx, sink.reshape(1, A_HEADS))


def _attn_b_kernel(q_ref, k_ref, v_ref, kc_ref, vc_ref, bias_ref, o_ref, *, rows):
    nwin = NA_ROWS * GRID_W
    kc = kc_ref[...]
    vc = vc_ref[...]
    q, kw, vw, tab = [], [], [], []
    for rr in range(NA_STEP_ROWS):
        r = pl.program_id(1) * NA_STEP_ROWS + rr
        rs = jnp.clip(r - NA_ROWS // 2, 0, rows - NA_ROWS)
        start = pl.multiple_of(rs * GRID_W, GRID_W)
        tab.append(rs - r + NA_ROWS - 1)
        q.append(q_ref[rr * GRID_W:(rr + 1) * GRID_W, :])
        kw.append(k_ref[pl.ds(start, nwin), :])
        vw.append(v_ref[pl.ds(start, nwin), :])
    chains = [(rr, h) for rr in range(NA_STEP_ROWS) for h in range(B_HEADS)]

    def hs(x, h):
        return x[:, h * HD:(h + 1) * HD]

    s_loc = [_dot_nt(hs(q[rr], h), hs(kw[rr], h)) + bias_ref[tab[rr], h] for rr, h in chains]
    s_ctx = [_dot_nt(hs(q[rr], h), hs(kc, h)) for rr, h in chains]
    m = [jnp.maximum(jnp.max(a, axis=-1, keepdims=True), jnp.max(b, axis=-1, keepdims=True))
         for a, b in zip(s_loc, s_ctx)]
    p_loc = [jnp.exp(a - mm) for a, mm in zip(s_loc, m)]
    p_ctx = [jnp.exp(b - mm) for b, mm in zip(s_ctx, m)]
    den = [jnp.sum(a, axis=-1, keepdims=True) + jnp.sum(b, axis=-1, keepdims=True) for a, b in zip(p_loc, p_ctx)]
    o = [(_dot(p_loc[i].astype(BF16), hs(vw[rr], h)) + _dot(p_ctx[i].astype(BF16), hs(vc, h))) / den[i]
         for i, (rr, h) in enumerate(chains)]
    for i, (rr, h) in enumerate(chains):
        o_ref[rr * GRID_W:(rr + 1) * GRID_W, h * HD:(h + 1) * HD] = o[i].astype(BF16)


def _na_bias_tables(rpb, rows):
    kr = NA_ROWS
    qc = np.arange(GRID_W)
    win_start = np.clip(qc - NA_COLS // 2, 0, GRID_W - NA_COLS)
    kcol = np.arange(GRID_W)
    valid = (kcol[None, :] >= win_start[:, None]) & (kcol[None, :] < win_start[:, None] + NA_COLS)
    pad = GRID_W - NA_COLS
    rp = jnp.pad(rpb.astype(F32), ((0, 0), (0, 0), (pad, pad)))
    toe = jnp.stack([rp[:, :, GRID_W - 1 - q:2 * GRID_W - 1 - q] for q in range(GRID_W)], axis=2)
    toe = jnp.where(jnp.asarray(valid[None, None]), toe, NEG)
    tabs = [toe[:, off:off + kr].transpose(0, 2, 1, 3).reshape(rpb.shape[0], GRID_W, kr * GRID_W)
            for off in range(kr)]
    return jnp.stack(tabs, axis=0)


def _attn_b(ab_lat, ab_ctx, bias_tabs, nb, s, lc):
    rows = s // GRID_W
    steps = rows // NA_STEP_ROWS
    tq = NA_STEP_ROWS * GRID_W
    return pl.pallas_call(
        functools.partial(_attn_b_kernel, rows=rows),
        grid=(nb, steps),
        in_specs=[pl.BlockSpec((tq, 256), lambda b, r: (b * steps + r, 2)),
                  pl.BlockSpec((s, 256), lambda b, r: (b, 3)),
                  pl.BlockSpec((s, 256), lambda b, r: (b, 4)),
                  pl.BlockSpec((lc, 256), lambda b, r: (b, 3)),
                  pl.BlockSpec((lc, 256), lambda b, r: (b, 4)),
                  pl.BlockSpec((NA_ROWS, B_HEADS, GRID_W, NA_ROWS * GRID_W), lambda b, r: (0, 0, 0, 0))],
        out_specs=pl.BlockSpec((tq, 256), lambda b, r: (b * steps + r, 0)),
        out_shape=jax.ShapeDtypeStruct((nb * s, 256), BF16),
        compiler_params=_cp("parallel", "arbitrary"),
        name="attn_b",
    )(ab_lat, ab_lat, ab_lat, ab_ctx, ab_ctx, bias_tabs)


def _attn_ctx_kernel(ab_ref, sink_ref, oa_ref, ob_ref):
    lc = ab_ref.shape[0]
    g = A_HEADS // A_KV
    for hk in range(A_KV):
        qg = jnp.concatenate([ab_ref[:, (hk * g + j) * HD:(hk * g + j + 1) * HD] for j in range(g)], axis=0)
        s = _dot_nt(qg, ab_ref[:, 256 + hk * HD:256 + (hk + 1) * HD])
        sink = jnp.concatenate(
            [jnp.broadcast_to(sink_ref[:, hk * g + j:hk * g + j + 1], (lc, 1)) for j in range(g)], axis=0)
        o = _softmax_pv(s, ab_ref[:, 384 + hk * HD:384 + (hk + 1) * HD], sink)
        for j in range(g):
            hq = hk * g + j
            oa_ref[:, hq * HD:(hq + 1) * HD] = o[j * lc:(j + 1) * lc].astype(BF16)
    for h in range(B_HEADS):
        s = _dot_nt(ab_ref[:, 512 + h * HD:512 + (h + 1) * HD], ab_ref[:, 768 + h * HD:768 + (h + 1) * HD])
        o = _softmax_pv(s, ab_ref[:, 1024 + h * HD:1024 + (h + 1) * HD])
        ob_ref[:, h * HD:(h + 1) * HD] = o.astype(BF16)


def _attn_ctx(ab_ctx, sink, nb, lc):
    return pl.pallas_call(
        _attn_ctx_kernel,
        grid=(nb,),
        in_specs=[pl.BlockSpec((lc, AB_W), lambda b: (b, 0)),
                  pl.BlockSpec((1, A_HEADS), lambda b: (0, 0))],
        out_specs=[pl.BlockSpec((lc, 256), lambda b: (b, 0)),
                   pl.BlockSpec((lc, 256), lambda b: (b, 0))],
        out_shape=[jax.ShapeDtypeStruct((nb * lc, 256), BF16),
                   jax.ShapeDtypeStruct((nb * lc, 256), BF16)],
        compiler_params=_cp("parallel"),
        name="attn_ctx",
    )(ab_ctx, sink.reshape(1, A_HEADS))


def _r7prep_kernel(x_ref, prev_ref, next_ref, cw_ref, kkw_ref, ka_ref, rk_ref, w0_ref, w2_ref, a0_ref, a2_ref,
                   g2_ref, bd_ref,
                   r_ref, v_ref, kk_ref, lw_ref, beta_ref, kd_ref, g_ref, bonus_ref, *, nt):
    i = pl.program_id(1)
    tt = x_ref.shape[0]
    x = x_ref[...]
    row = lax.broadcasted_iota(jnp.int32, x.shape, 0)
    before = jnp.where(i > 0, prev_ref[7:8, :], 0.0)
    after = jnp.where(i < nt - 1, next_ref[0:1, :], 0.0)
    x_m1 = jnp.where(row == 0, before, pltpu.roll(x, 1, 0))
    x_p1 = jnp.where(row == tt - 1, after, pltpu.roll(x, tt - 1, 0))
    xc = x_m1 * cw_ref[0:1, :] + x * cw_ref[1:2, :] + x_p1 * cw_ref[2:3, :]
    r = xc[:, 0:512]
    k = xc[:, 512:1024]
    v = xc[:, 1024:1536]
    wd = xc[:, 1536:1664]
    ad = xc[:, 1664:1792]
    gd = xc[:, 1792:1920]
    bd = bd_ref[...]
    kkh = k * kkw_ref[...]
    kk = kkh / jnp.maximum(jnp.sqrt(_segsum(kkh * kkh, bd)), 1e-12)
    zw = w0_ref[...] + _dot3(jnp.tanh(wd), w2_ref[...])
    za = a0_ref[...] + _dot3(ad, a2_ref[...])
    a = jax.nn.sigmoid(za)
    ka = ka_ref[...]
    kd0 = k * (1.0 + (a[:, 0:512] - 1.0) * ka)
    kd1 = k * (1.0 + (a[:, 512:1024] - 1.0) * ka)
    r_ref[...] = r.astype(BF16)
    v_ref[...] = v.astype(BF16)
    kk_ref[...] = kk.astype(BF16)
    lw_ref[...] = -DECAY_SCALE * jax.nn.sigmoid(zw)
    beta_ref[:, 0:512] = (kk * a[:, 0:512]).astype(BF16)
    beta_ref[:, 512:1024] = (kk * a[:, 512:1024]).astype(BF16)
    kd_ref[:, 0:512] = kd0.astype(BF16)
    kd_ref[:, 512:1024] = kd1.astype(BF16)
    g_ref[...] = _dot3(jax.nn.sigmoid(gd), g2_ref[...])
    bonus_ref[...] = _segsum(r * (kd0 + kd1) * rk_ref[...], bd) * v


def _bd2(w):
    z = jnp.zeros_like(w[0])
    return jnp.concatenate([jnp.concatenate([w[0], z], axis=1), jnp.concatenate([z, w[1]], axis=1)], axis=0)


def _r7prep(pc, lp, nb, t):
    tt = _row_tile(t)
    nt = t // tt
    nrow8 = t // 8

    def full(shape):
        return pl.BlockSpec(shape, lambda b, i: (0,) * len(shape))

    def rowspec(w):
        return pl.BlockSpec((tt, w), lambda b, i: (b * nt + i, 0))

    outs = [C_W, C_W, C_W, 2 * C_W, 2 * C_W, 2 * C_W, C_W, C_W]
    dtypes = [BF16, BF16, BF16, F32, BF16, BF16, F32, F32]
    return pl.pallas_call(
        functools.partial(_r7prep_kernel, nt=nt),
        grid=(nb, nt),
        in_specs=[rowspec(C_IN),
                  pl.BlockSpec((8, C_IN), lambda b, i: (b * nrow8 + jnp.maximum(i * (tt // 8) - 1, 0), 0)),
                  pl.BlockSpec((8, C_IN), lambda b, i: (b * nrow8 + jnp.minimum((i + 1) * (tt // 8), nrow8 - 1), 0)),
                  full((3, C_IN)), full((1, C_W)), full((1, C_W)), full((1, C_W)),
                  full((1, 2 * C_W)), full((128, 2 * C_W)), full((1, 2 * C_W)), full((128, 2 * C_W)),
                  full((128, C_W)), full((SEG_W, SEG_W))],
        out_specs=[rowspec(w) for w in outs],
        out_shape=[jax.ShapeDtypeStruct((nb * t, w), dt) for w, dt in zip(outs, dtypes)],
        compiler_params=_cp("parallel", "parallel"),
        name="r7prep",
    )(pc, pc, pc, lp['r7_conv'], lp['r7_kk'].reshape(1, C_W), lp['r7_ka'].reshape(1, C_W),
      lp['r7_rk'].reshape(1, C_W), lp['r7_w0'].reshape(1, 2 * C_W), _bd2(lp['r7_w2']),
      lp['r7_a0'].reshape(1, 2 * C_W), _bd2(lp['r7_a2']), lp['r7_g2'], _block_diag_ones())


def _cumsum_rows(tri, x):
    t = tri.astype(BF16)
    hi = x.astype(BF16)
    rest = x - hi.astype(F32)
    mid = rest.astype(BF16)
    lo = (rest - mid.astype(F32)).astype(BF16)
    return _dot(t, hi) + (_dot(t, mid) + _dot(t, lo))


def _mm(a, b):
    return _dot(a.astype(BF16), b.astype(BF16))


def _mm_nt(a, b):
    return _dot_nt(a.astype(BF16), b.astype(BF16))


def _scan_kernel(rf_ref, vf_ref, kkf_ref, lwf_ref, betaf_ref, kdf_ref,
                 rb_ref, vb_ref, kkb_ref, lwb_ref, betab_ref, kdb_ref, s0_ref,
                 yf_ref, yb_ref, sf_ref, st_ref, *, nc):
    c = pl.program_id(1)
    cl = SCAN_C

    @pl.when(c == 0)
    def _():
        st_ref[...] = s0_ref[...]

    ti = lax.broadcasted_iota(jnp.int32, (cl, cl), 0)
    si = lax.broadcasted_iota(jnp.int32, (cl, cl), 1)
    eye = (ti == si).astype(F32)
    incl = [si <= ti, si >= ti]
    strict = [si < ti, si > ti]

    def hs(x, h):
        return x[:, h * HD:(h + 1) * HD]

    pre = {}
    for bb in range(SCAN_NB):
        for d, (r_ref, v_ref, kk_ref, lw_ref, beta_ref, kd_ref) in enumerate(
                [(rf_ref, vf_ref, kkf_ref, lwf_ref, betaf_ref, kdf_ref),
                 (rb_ref, vb_ref, kkb_ref, lwb_ref, betab_ref, kdb_ref)]):
            lw = lw_ref[bb]
            cum = _cumsum_rows(incl[d], lw)
            e_neg = jnp.exp(-cum)
            pre[bb, d] = dict(r_hat=r_ref[bb] * jnp.exp(cum), a_hat=-kk_ref[bb] * jnp.exp(cum - lw),
                              b_til=beta_ref[bb] * e_neg, k_til=kd_ref[bb] * e_neg,
                              wtot=jnp.exp(jnp.sum(lw, axis=0, keepdims=True)), vv=v_ref[bb])

    chains = [(bb, d, h) for bb in range(SCAN_NB) for d in range(2) for h in range(C_HEADS)]
    n_ch = range(len(chains))

    def part(name, i):
        bb, d, h = chains[i]
        return hs(pre[bb, d][name], h)

    def dirn(i):
        return chains[i][1]

    ar = [jnp.concatenate([part('a_hat', i), part('r_hat', i)], axis=0) for i in n_ch]
    bk = [jnp.concatenate([part('b_til', i), part('k_til', i)], axis=0) for i in n_ch]
    m = [_mm_nt(ar[i], bk[i]) for i in n_ch]
    l_k = [jnp.where(strict[dirn(i)], m[i][0:cl, cl:2 * cl], 0.0) for i in n_ch]
    r_b = [jnp.where(incl[dirn(i)], m[i][cl:2 * cl, 0:cl], 0.0) for i in n_ch]
    r_k = [jnp.where(incl[dirn(i)], m[i][cl:2 * cl, cl:2 * cl], 0.0) for i in n_ch]
    l_mat = [jnp.where(strict[dirn(i)], m[i][0:cl, 0:cl], 0.0) for i in n_ch]
    halves_of = lambda b: ((ti // (2 * b)) == (si // (2 * b))) & ((ti // b) != (si // b))
    t_inv = [eye + jnp.where(halves_of(1), l_mat[i], 0.0) for i in n_ch]
    b = 2
    while b < cl:
        couple = halves_of(b)
        nt_ = [_mm(jnp.where(couple, l_mat[i], 0.0), t_inv[i]) for i in n_ch]
        t_inv = [t_inv[i] + _mm(t_inv[i], nt_[i]) for i in n_ch]
        b *= 2
    x1 = [_mm(l_k[i], part('vv', i)) for i in n_ch]
    y0 = [_mm(r_k[i], part('vv', i)) for i in n_ch]
    ua = [_mm(t_inv[i], jnp.concatenate([x1[i], part('a_hat', i)], axis=1)) for i in n_ch]
    s0 = [st_ref[d, bb, h] for bb, d, h in chains]
    as0 = [_mm_nt(jnp.concatenate([ua[i][:, HD:2 * HD], part('r_hat', i)], axis=0), s0[i])
           for i in n_ch]
    u = [ua[i][:, 0:HD] + as0[i][0:cl] for i in n_ch]
    y = [y0[i] + as0[i][cl:2 * cl] + _mm(r_b[i], u[i]) for i in n_ch]
    for i in n_ch:
        bb, d, h = chains[i]
        (yf_ref, yb_ref)[d][bb, :, h * HD:(h + 1) * HD] = y[i]
    for i in n_ch:
        bb, d, h = chains[i]
        uv_t = jnp.concatenate([u[i], part('vv', i)], axis=0).T
        st_ref[d, bb, h] = (s0[i] + _mm(uv_t, bk[i])) * part('wtot', i)

    @pl.when(c == nc - 1)
    def _():
        sf_ref[...] = st_ref[...]


def _scan(prep, s0, nb, t):
    nc = t // SCAN_C
    assert nb % SCAN_NB == 0
    r, v, kk, lw, beta, kd = [a.reshape(nb, t, a.shape[-1]) for a in prep]

    def chunk(d, c):
        return nc - 1 - c if d else c

    def specs(d):
        shared = pl.BlockSpec((SCAN_NB, SCAN_C, C_W), lambda b, c: (b, chunk(d, c), 0))
        perdir = pl.BlockSpec((SCAN_NB, SCAN_C, C_W), lambda b, c: (b, chunk(d, c), d))
        return [shared, shared, shared, perdir, perdir, perdir]

    st_spec = pl.BlockSpec((2, SCAN_NB, C_HEADS, HD, HD), lambda b, c: (0, b, 0, 0, 0))
    yf, yb, sf = pl.pallas_call(
        functools.partial(_scan_kernel, nc=nc),
        grid=(nb // SCAN_NB, nc),
        in_specs=specs(0) + specs(1) + [st_spec],
        out_specs=[pl.BlockSpec((SCAN_NB, SCAN_C, C_W), lambda b, c: (b, chunk(0, c), 0)),
                   pl.BlockSpec((SCAN_NB, SCAN_C, C_W), lambda b, c: (b, chunk(1, c), 0)), st_spec],
        out_shape=[jax.ShapeDtypeStruct((nb, t, C_W), F32), jax.ShapeDtypeStruct((nb, t, C_W), F32),
                   jax.ShapeDtypeStruct((2, nb, C_HEADS, HD, HD), F32)],
        scratch_shapes=[pltpu.VMEM((2, SCAN_NB, C_HEADS, HD, HD), F32)],
        compiler_params=_cp("parallel", "arbitrary"),
        name="r7scan",
    )(r, v, kk, lw, beta, kd, r, v, kk, lw, beta, kd, s0)
    return (yf.reshape(nb * t, C_W), yb.reshape(nb * t, C_W)), sf


def _outproj_kernel(x_ref, oa_ref, ob_ref, y0_ref, y1_ref, bonus_ref, g_ref, lnw_ref, lnb_ref, bd_ref, w_ref,
                    gate_ref, o_ref):
    bd = bd_ref[...]
    y = y0_ref[...] + y1_ref[...]
    mu = _segsum(y, bd) * (1.0 / HD)
    yc = y - mu
    var = _segsum(yc * yc, bd) * (1.0 / HD)
    yn = yc * lax.rsqrt(var + GN_EPS) * lnw_ref[...] + lnb_ref[...]
    oc = ((yn + bonus_ref[...]) * g_ref[...]).astype(BF16)
    acc = (_dot(oa_ref[...], w_ref[0:256, :]) + _dot(ob_ref[...], w_ref[256:512, :])
           + _dot(oc, w_ref[512:1024, :]))
    o_ref[...] = x_ref[...] + gate_ref[...] * acc


def _outproj(x2, oa, ob, y, bonus, g, lp, w_out_bf, mod3, mod_row_of_batch, nb, t):
    tm = _row_tile(t)
    nt = t // tm

    def rowspec(w):
        return pl.BlockSpec((tm, w), lambda b, i: (b * nt + i, 0))

    def full(shape):
        return pl.BlockSpec(shape, lambda b, i: (0,) * len(shape))

    return pl.pallas_call(
        _outproj_kernel,
        grid=(nb, nt),
        in_specs=[rowspec(D), rowspec(256), rowspec(256),
                  rowspec(C_W), rowspec(C_W),
                  rowspec(C_W), rowspec(C_W), full((1, C_W)), full((1, C_W)), full((SEG_W, SEG_W)), full((D, D)),
                  pl.BlockSpec((None, 1, D), lambda b, i: (mod_row_of_batch(b), 0, 2))],
        out_specs=rowspec(D),
        out_shape=jax.ShapeDtypeStruct((nb * t, D), F32),
        compiler_params=_cp("parallel", "parallel"),
        name="outproj",
    )(x2, oa, ob, y[0], y[1], bonus, g, lp['r7_lnw'].reshape(1, C_W), lp['r7_lnb'].reshape(1, C_W),
      _block_diag_ones(), w_out_bf, mod3)


def _peer_score_kernel(x_ref, g_ref, sc_ref, sh_ref, wq_ref, keys_ref, h_ref, st_ref):
    x = x_ref[...]
    y = x * lax.rsqrt(jnp.mean(x * x, axis=-1, keepdims=True) + EPS) * g_ref[...]
    h = (y * (1.0 + sc_ref[...]) + sh_ref[...]).astype(BF16)
    h_ref[...] = h
    q = _dot(h, wq_ref[...])
    half = P_QDIM // 2
    for hp in range(2 * P_HEADS):
        st_ref[hp] = _dot3(keys_ref[hp], q[:, hp * half:(hp + 1) * half], nt=True)


def _peer_scores(x2, norm_g, mod3, mod_row_of_batch, wq_bf, keys, nb, t):
    tm = _row_tile(t)
    nt = t // tm
    n = nb * t
    return pl.pallas_call(
        _peer_score_kernel,
        grid=(nb, nt),
        in_specs=[pl.BlockSpec((tm, D), lambda b, i: (b * nt + i, 0)),
                  pl.BlockSpec((1, D), lambda b, i: (0, 0)),
                  pl.BlockSpec((None, 1, D), lambda b, i: (mod_row_of_batch(b), 0, 4)),
                  pl.BlockSpec((None, 1, D), lambda b, i: (mod_row_of_batch(b), 0, 3)),
                  pl.BlockSpec((D, P_HEADS * P_QDIM), lambda b, i: (0, 0)),
                  pl.BlockSpec((2 * P_HEADS, P_NKEYS, P_QDIM // 2), lambda b, i: (0, 0, 0))],
        out_specs=[pl.BlockSpec((tm, D), lambda b, i: (b * nt + i, 0)),
                   pl.BlockSpec((2 * P_HEADS, P_NKEYS, tm), lambda b, i: (0, 0, b * nt + i))],
        out_shape=[jax.ShapeDtypeStruct((n, D), BF16),
                   jax.ShapeDtypeStruct((2 * P_HEADS, P_NKEYS, n), F32)],
        compiler_params=_cp("parallel", "parallel"),
        name="peer_scores",
    )(x2, norm_g.reshape(1, D), mod3, mod3, wq_bf, keys.reshape(2 * P_HEADS, P_NKEYS, P_QDIM // 2))


def _batcher_pairs(n):
    pairs = []
    p = 1
    while p < n:
        k = p
        while k >= 1:
            for j in range(k % p, n - k, 2 * k):
                for i in range(min(k, n - j - k)):
                    if (i + j) // (2 * p) == (i + j + k) // (2 * p):
                        pairs.append((i + j, i + j + k))
            k //= 2
        p *= 2
    return pairs


_SORT16 = _batcher_pairs(P_TOPK)
_BITONIC16 = [(i, i + d) for d in (8, 4, 2, 1) for i in range(P_TOPK) if i & d == 0]


def _compare_exchange(x, pairs):
    for i, j in pairs:
        x[i], x[j] = jnp.maximum(x[i], x[j]), jnp.minimum(x[i], x[j])
    return x


def _top16_sorted(slabs):
    x = _compare_exchange(list(slabs), _SORT16)
    for shift in (4, 2, 1):
        y = [pltpu.roll(x[P_TOPK - 1 - k], shift, 0) for k in range(P_TOPK)]
        x = [jnp.maximum(x[k], y[k]) for k in range(P_TOPK)]
        x = _compare_exchange(x, _BITONIC16)
    return x


def _peer_gate_tables(st_ref, rk_ref, be_ref, cnt_ref, al_ref, top_ref):
    nblk = st_ref.shape[-1] // LANES
    sub = lax.broadcasted_iota(jnp.int32, (8, LANES), 0)
    ninf = jnp.full((8, LANES), -jnp.inf, F32)

    def block(it, carry):
        h = it // nblk
        lb = it % nblk
        lanes = pl.ds(pl.multiple_of(lb * LANES, LANES), LANES)
        hf = lb // (PEER_HALF // LANES)
        hl = pl.ds(pl.multiple_of((lb % (PEER_HALF // LANES)) * LANES, LANES), LANES)
        for p in range(2):
            top = _top16_sorted([st_ref[2 * h + p, 8 * k:8 * k + 8, lanes] for k in range(P_TOPK)])
            for k in range(P_TOPK):
                top_ref[p * P_TOPK + k:p * P_TOPK + k + 1, lanes] = top[k][0:1]
        a16 = top_ref[0:P_TOPK, lanes]
        b16 = top_ref[P_TOPK:2 * P_TOPK, lanes]
        b8 = b16[0:8]
        cand = [a16[0:1] + b8, a16[0:1] + b16[8:16]]
        for p in range(2, 9):
            cand.append(jnp.where(sub < P_TOPK // p, a16[p - 1:p] + b8, -jnp.inf))
        cand.append(a16[8:16] + b16[0:1])
        best = _top16_sorted(cand + [ninf] * (P_TOPK - len(cand)))
        tau = best[P_TOPK - 1][0:1]
        z = jnp.ones_like(tau)
        for k in range(1, P_TOPK):
            z = z + jnp.exp(best[k][0:1] - best[0][0:1])
        s1 = st_ref[2 * h, :, lanes]
        s2 = st_ref[2 * h + 1, :, lanes]
        cnt = jnp.zeros_like(s1)
        rk = jnp.ones_like(s2)
        for q in range(P_TOPK):
            bq = b16[q:q + 1]
            theta = jnp.min(jnp.where(a16 + bq >= tau, a16, jnp.inf), axis=0, keepdims=True)
            cnt = jnp.where(s1 >= theta, q + 1.0, cnt)
            rk = jnp.where(bq > s2, q + 2.0, rk)
        cnt_ref[hf, h, :, hl] = cnt
        rk_ref[hf, h, :, hl] = rk.astype(BF16)
        al_ref[hf, h, :, hl] = jnp.exp(s1 - a16[0:1]) / z
        be_ref[hf, h, :, hl] = jnp.exp(s2 - b16[0:1]).astype(BF16)
        return carry

    lax.fori_loop(0, P_HEADS * nblk, block, 0)


def _peer_chunk(chunk, act_ref, rk_ref, be_ref, cnt_ref, al_ref, w_ref):
    nrow = w_ref.shape[0] // P_NKEYS
    for ii in range(nrow):
        i = chunk * nrow + ii
        wrow = None
        for h in range(P_HEADS):
            cnt = cnt_ref[h, pl.ds(i, 1), :].astype(BF16)
            al = al_ref[h, pl.ds(i, 1), :].astype(BF16)
            term = jnp.where(rk_ref[h] <= cnt, al * be_ref[h], 0.0)
            wrow = term if wrow is None else wrow + term
        rows = slice(ii * P_NKEYS, (ii + 1) * P_NKEYS)
        act = act_ref[rows, :]
        gl = 0.5 * act * (1.0 + lax.erf(act * (2.0 ** -0.5)))
        w_ref[rows, :] = wrow * gl.astype(BF16)


def _peer_dense_kernel(x_ref, h_ref, st_ref, u_ref, vt_ref, gate_ref, o_ref,
                       rk_ref, be_ref, cnt_ref, al_ref, top_ref, act0_ref, act1_ref, w_ref, acc_ref, *, ne):
    e = pl.program_id(1)
    halves = range(w_ref.shape[0])

    def rows(hf):
        return slice(hf * PEER_HALF, (hf + 1) * PEER_HALF)

    @pl.when(e == 0)
    def _():
        for hf in halves:
            act0_ref[hf] = _dot_nt(u_ref[...], h_ref[rows(hf), :])
        _peer_gate_tables(st_ref, rk_ref, be_ref, cnt_ref, al_ref, top_ref)
        acc_ref[...] = jnp.zeros_like(acc_ref)

    def step(cur_ref, nxt_ref):
        for hf in halves:
            nxt_ref[hf] = _dot_nt(u_ref[...], h_ref[rows(hf), :])
            _peer_chunk(e - 1, cur_ref.at[hf], rk_ref.at[hf], be_ref.at[hf], cnt_ref.at[hf], al_ref.at[hf],
                        w_ref.at[hf])
            acc_ref[hf] += _dot(vt_ref[...], w_ref[hf])

    @pl.when(e % 2 == 1)
    def _():
        step(act0_ref, act1_ref)

    @pl.when((e > 0) & (e % 2 == 0))
    def _():
        step(act1_ref, act0_ref)

    @pl.when(e == ne)
    def _():
        for hf in halves:
            o_ref[rows(hf), :] = x_ref[rows(hf), :] + gate_ref[...] * acc_ref[hf].T


def _peer_dense(x2, h2, st, u_bf, vt_bf, mod3, mod_row_of_batch, nb, t):
    tm = _row_tile(t)
    nh = tm // PEER_HALF
    ec = PEER_EC
    nt = t // tm
    ne = P_EXPERTS // ec
    return pl.pallas_call(
        functools.partial(_peer_dense_kernel, ne=ne),
        grid=(nb * nt, ne + 1),
        in_specs=[pl.BlockSpec((tm, D), lambda i, e: (i, 0)),
                  pl.BlockSpec((tm, D), lambda i, e: (i, 0)),
                  pl.BlockSpec((2 * P_HEADS, P_NKEYS, tm), lambda i, e: (0, 0, i)),
                  pl.BlockSpec((ec, D), lambda i, e: (jnp.minimum(e, ne - 1), 0)),
                  pl.BlockSpec((None, D, ec), lambda i, e: (jnp.maximum(e - 1, 0), 0, 0)),
                  pl.BlockSpec((None, 1, D), lambda i, e: (mod_row_of_batch(i // nt), 0, 5))],
        out_specs=pl.BlockSpec((tm, D), lambda i, e: (i, 0)),
        out_shape=jax.ShapeDtypeStruct((nb * t, D), F32),
        scratch_shapes=[pltpu.VMEM((nh, P_HEADS, P_NKEYS, PEER_HALF), BF16),
                        pltpu.VMEM((nh, P_HEADS, P_NKEYS, PEER_HALF), BF16),
                        pltpu.VMEM((nh, P_HEADS, P_NKEYS, PEER_HALF), F32),
                        pltpu.VMEM((nh, P_HEADS, P_NKEYS, PEER_HALF), F32),
                        pltpu.VMEM((2 * P_TOPK, tm), F32),
                        pltpu.VMEM((nh, ec, PEER_HALF), F32),
                        pltpu.VMEM((nh, ec, PEER_HALF), F32),
                        pltpu.VMEM((nh, ec, PEER_HALF), BF16),
                        pltpu.VMEM((nh, D, PEER_HALF), F32)],
        compiler_params=pltpu.CompilerParams(dimension_semantics=("parallel", "arbitrary"),
                                             vmem_limit_bytes=PEER_VMEM_LIMIT),
        name="peer_dense",
    )(x2, h2, st, u_bf, vt_bf, mod3)


def _peer(x2, norm_g, mod3, mod_row_of_batch, wq_bf, keys, u_bf, vt_bf, nb, t):
    h2, st = _peer_scores(x2, norm_g, mod3, mod_row_of_batch, wq_bf, keys, nb, t)
    return _peer_dense(x2, h2, st, u_bf, vt_bf, mod3, mod_row_of_batch, nb, t)


def kernel(x, c, ctx, c_ctx, norm_mix, norm_ffn, w_mod, b_mod, w_in, w_out, a_qnorm, a_knorm, a_sink, b_qnorm,
           b_knorm, b_rpb, r7_conv, r7_w0, r7_w2, r7_a0, r7_a2, r7_g2, r7_kk, r7_ka, r7_rk, r7_lnw, r7_lnb,
           peer_wq, peer_keys, peer_u, peer_v):
    nb, s, _ = x.shape
    lc = ctx.shape[1]
    depth = w_in.shape[0]
    assert nb < 16 and nb % SCAN_NB == 0 and s % 512 == 0 and lc % 256 == 0
    rows = s // GRID_W

    cc = jnp.zeros((16, D), F32).at[:nb].set(c).at[nb].set(c_ctx)
    mod = _modulation(cc, w_mod, b_mod)
    rope_tabs = _rope_tables(s)
    lat_row = lambda b: b
    ctx_row = lambda b: nb

    x_lat = x.reshape(nb * s, D)
    x_ctx = ctx.reshape(nb * lc, D)
    scale = HD ** -0.5
    for l in range(depth):
        with_ctx = l < depth - 1
        mod3 = mod[l].reshape(16, 1, 6 * D)
        lp = {'r7_conv': r7_conv[l], 'r7_w0': r7_w0[l], 'r7_w2': r7_w2[l], 'r7_a0': r7_a0[l], 'r7_a2': r7_a2[l],
              'r7_g2': r7_g2[l], 'r7_kk': r7_kk[l], 'r7_ka': r7_ka[l], 'r7_rk': r7_rk[l], 'r7_lnw': r7_lnw[l],
              'r7_lnb': r7_lnb[l]}
        w_in_bf = w_in[l].astype(BF16)
        w_out_bf = w_out[l].astype(BF16)
        gain_a = jnp.concatenate([jnp.tile(a_qnorm[l] * scale, A_HEADS), jnp.tile(a_knorm[l], A_KV)]).reshape(1, 384)
        gain_b = jnp.concatenate([jnp.tile(b_qnorm[l] * scale, B_HEADS), jnp.tile(b_knorm[l], B_HEADS)]).reshape(1, 512)

        ab_lat, pc_lat = _inproj(x_lat, mod3, lat_row, norm_mix[l], w_in_bf, gain_a, gain_b, rope_tabs, nb, s)
        ab_ctx, pc_ctx = _inproj(x_ctx, mod3, ctx_row, norm_mix[l], w_in_bf, gain_a, gain_b, None, 1, nb * lc)

        o_a = _attn_a(ab_lat, ab_ctx, a_sink[l], nb, s, lc)
        o_b = _attn_b(ab_lat, ab_ctx, _na_bias_tables(b_rpb[l], rows), nb, s, lc)

        prep_ctx = _r7prep(pc_ctx, lp, nb, lc)
        prep_lat = _r7prep(pc_lat, lp, nb, s)
        zero_state = jnp.zeros((2, nb, C_HEADS, HD, HD), F32)
        y_ctx, s_ctx = _scan(prep_ctx[:6], zero_state, nb, lc)
        y_lat, _ = _scan(prep_lat[:6], s_ctx, nb, s)

        x_lat = _outproj(x_lat, o_a, o_b, y_lat, prep_lat[7], prep_lat[6], lp, w_out_bf, mod3, lat_row, nb, s)
        wq_bf = peer_wq[l].astype(BF16)
        u_bf = peer_u[l].astype(BF16)
        vt_bf = peer_v[l].astype(BF16).reshape(P_EXPERTS // PEER_EC, PEER_EC, D).transpose(0, 2, 1)
        x_lat = _peer(x_lat, norm_ffn[l], mod3, lat_row, wq_bf, peer_keys[l], u_bf, vt_bf, nb, s)
        if with_ctx:
            o_ac, o_bc = _attn_ctx(ab_ctx, a_sink[l], nb, lc)
            x_ctx = _outproj(x_ctx, o_ac, o_bc, y_ctx, prep_ctx[7], prep_ctx[6], lp, w_out_bf, mod3, ctx_row,
                             1, nb * lc)
            x_ctx = _peer(x_ctx, norm_ffn[l], mod3, ctx_row, wq_bf, peer_keys[l], u_bf, vt_bf, 1, nb * lc)
    return x_lat.reshape(nb, s, D)
```

```python
import functools

import numpy as np
import jax
import jax.numpy as jnp
from jax import lax
from jax.experimental import pallas as pl
from jax.experimental.pallas import tpu as pltpu

F32 = jnp.float32
BF16 = jnp.bfloat16
HI = lax.Precision.HIGHEST

D = 1024
GRID_W = 64
HD = 64
LANES = 128
EPS = 1e-6
NEG = -1e30
A_HEADS, A_KV, A_BLOCK, A_WINDOW = 4, 2, 128, 128
A_STEP_BLOCKS = 2
B_HEADS, NA_ROWS, NA_COLS = 4, 8, 16
NA_STEP_ROWS = 8
C_HEADS = 8
C_W = 512
C_IN = 1920
AB_W = 1280
IN_W = AB_W + C_IN
GN_EPS = 64e-5
DECAY_SCALE = 0.6065306597126334
ROPE_BASE = 10000.0
P_HEADS, P_NKEYS, P_QDIM, P_TOPK = 8, 128, 256, 16
P_EXPERTS = P_NKEYS * P_NKEYS
SCAN_C = 64
SCAN_NB = 2
VMEM_LIMIT = 48 * 1024 * 1024
PEER_VMEM_LIMIT = 60 * 1024 * 1024
PEER_EC = 2048
PEER_HALF = 256


def _row_tile(t):
    return 512 if t % 512 == 0 else 256


def _cp(*sem):
    return pltpu.CompilerParams(dimension_semantics=sem, vmem_limit_bytes=VMEM_LIMIT)


def _dot(a, b, prec=None):
    return jnp.dot(a, b, precision=prec, preferred_element_type=F32)


def _dot_nt(a, b, prec=None):
    return lax.dot_general(a, b, (((1,), (1,)), ((), ())), precision=prec, preferred_element_type=F32)


SEG_W = 256


def _block_diag_ones():
    i = np.arange(SEG_W) // HD
    return jnp.asarray((i[:, None] == i[None, :]).astype(np.float32), BF16)


def _split2(x):
    hi = x.astype(BF16)
    return hi, (x - hi.astype(F32)).astype(BF16)


def _segsum(x, bd):
    hi, lo = _split2(x)
    outs = []
    for g0 in range(0, x.shape[1], SEG_W):
        w = min(SEG_W, x.shape[1] - g0)
        outs.append(_dot(hi[:, g0:g0 + w], bd[0:w, 0:w]) + _dot(lo[:, g0:g0 + w], bd[0:w, 0:w]))
    return outs[0] if len(outs) == 1 else jnp.concatenate(outs, axis=1)


def _dot3(a, b, nt=False):
    f = _dot_nt if nt else _dot
    a_hi, a_lo = _split2(a)
    b_hi, b_lo = _split2(b)
    return f(a_hi, b_hi) + (f(a_lo, b_hi) + f(a_hi, b_lo))


def _mod_kernel(c_ref, w_ref, b_ref, o_ref):
    c = c_ref[...]
    s = c * jax.nn.sigmoid(c)
    o_ref[...] = _dot(s, w_ref[...], HI) + b_ref[...]


def _modulation(cc, w_mod, b_mod):
    L, _, n = w_mod.shape
    tn = 2048
    return pl.pallas_call(
        _mod_kernel,
        grid=(L, n // tn),
        in_specs=[pl.BlockSpec((16, D), lambda l, j: (0, 0)),
                  pl.BlockSpec((None, D, tn), lambda l, j: (l, 0, j)),
                  pl.BlockSpec((None, 1, tn), lambda l, j: (l, 0, j))],
        out_specs=pl.BlockSpec((None, 16, tn), lambda l, j: (l, 0, j)),
        out_shape=jax.ShapeDtypeStruct((L, 16, n), F32),
        compiler_params=_cp("parallel", "parallel"),
        name="modulation",
    )(cc, w_mod, b_mod.reshape(L, 1, n))


def _swap16(x):
    n = x.shape[-1]
    lane = lax.broadcasted_iota(jnp.int32, x.shape, 1)
    fwd = pltpu.roll(x, n - 16, 1)
    bwd = pltpu.roll(x, 16, 1)
    return jnp.where((lane % 32) < 16, fwd, bwd)


def _head_rms(x, bd, gain):
    ss = _segsum(x * x, bd)
    return x * lax.rsqrt(ss * (1.0 / HD) + EPS) * gain


def _inproj_kernel(*refs, rope):
    if rope:
        (x_ref, g_ref, sc_ref, sh_ref, w_ref, ga_ref, gb_ref, bd_ref, cos_ref, sin_ref,
         ab_ref, c_ref) = refs
    else:
        (x_ref, g_ref, sc_ref, sh_ref, w_ref, ga_ref, gb_ref, bd_ref, ab_ref, c_ref) = refs
    x = x_ref[...]
    y = x * lax.rsqrt(jnp.mean(x * x, axis=-1, keepdims=True) + EPS) * g_ref[...]
    h = y * (1.0 + sc_ref[...]) + sh_ref[...]
    acc = _dot(h.astype(BF16), w_ref[...])
    bd = bd_ref[...]
    qa = _head_rms(acc[:, 0:384], bd, ga_ref[...])
    if rope:
        qa = qa * cos_ref[...] + _swap16(qa) * sin_ref[...]
    qb = _head_rms(acc[:, 512:1024], bd, gb_ref[...])
    ab_ref[:, 0:384] = qa.astype(BF16)
    ab_ref[:, 384:512] = acc[:, 384:512].astype(BF16)
    ab_ref[:, 512:1024] = qb.astype(BF16)
    ab_ref[:, 1024:1280] = acc[:, 1024:1280].astype(BF16)
    c_ref[...] = acc[:, AB_W:IN_W]


def _inproj(x2, mod3, mod_row_of_batch, norm_g, w_in_bf, gain_a, gain_b, rope_tabs, nb, t):
    tm = _row_tile(t)
    nt = t // tm
    rope = rope_tabs is not None
    in_specs = [
        pl.BlockSpec((tm, D), lambda b, i: (b * nt + i, 0)),
        pl.BlockSpec((1, D), lambda b, i: (0, 0)),
        pl.BlockSpec((None, 1, D), lambda b, i: (mod_row_of_batch(b), 0, 1)),
        pl.BlockSpec((None, 1, D), lambda b, i: (mod_row_of_batch(b), 0, 0)),
        pl.BlockSpec((D, IN_W), lambda b, i: (0, 0)),
        pl.BlockSpec((1, 384), lambda b, i: (0, 0)),
        pl.BlockSpec((1, 512), lambda b, i: (0, 0)),
        pl.BlockSpec((SEG_W, SEG_W), lambda b, i: (0, 0)),
    ]
    args = [x2, norm_g.reshape(1, D), mod3, mod3, w_in_bf, gain_a, gain_b,
            _block_diag_ones()]
    if rope:
        in_specs += [pl.BlockSpec((tm, 384), lambda b, i: (i, 0)),
                     pl.BlockSpec((tm, 384), lambda b, i: (i, 0))]
        args += list(rope_tabs)
    return pl.pallas_call(
        functools.partial(_inproj_kernel, rope=rope),
        grid=(nb, nt),
        in_specs=in_specs,
        out_specs=[pl.BlockSpec((tm, AB_W), lambda b, i: (b * nt + i, 0)),
                   pl.BlockSpec((tm, C_IN), lambda b, i: (b * nt + i, 0))],
        out_shape=[jax.ShapeDtypeStruct((nb * t, AB_W), BF16),
                   jax.ShapeDtypeStruct((nb * t, C_IN), F32)],
        compiler_params=_cp("parallel", "parallel"),
        name="inproj_rope" if rope else "inproj_ctx",
    )(*args)


def _rope_tables(s):
    tok = np.arange(s)
    inv = ROPE_BASE ** (-np.arange(0, 32, 2) / 32.0)
    ar = (tok // GRID_W)[:, None] * inv[None]
    ac = (tok % GRID_W)[:, None] * inv[None]
    cos = np.concatenate([np.cos(ar), np.cos(ar), np.cos(ac), np.cos(ac)], axis=1)
    sin = np.concatenate([-np.sin(ar), np.sin(ar), -np.sin(ac), np.sin(ac)], axis=1)
    return (jnp.asarray(np.tile(cos, (1, 6)), F32), jnp.asarray(np.tile(sin, (1, 6)), F32))


def _softmax_pv(s, v, sink=None):
    m = jnp.max(s, axis=-1, keepdims=True)
    if sink is not None:
        m = jnp.maximum(m, sink)
    p = jnp.exp(s - m)
    den = jnp.sum(p, axis=-1, keepdims=True)
    if sink is not None:
        den = den + jnp.exp(sink - m)
    return _dot(p.astype(BF16), v) / den


def _attn_a_kernel(q_ref, k0_ref, k1_ref, k2_ref, k3_ref, v0_ref, v1_ref, v2_ref, v3_ref, kc_ref, vc_ref,
                   sink_ref, o_ref, *, nblk):
    kb = [k0_ref[...], k1_ref[...], k2_ref[...], k3_ref[...]]
    vb = [v0_ref[...], v1_ref[...], v2_ref[...], v3_ref[...]]
    kc = kc_ref[...]
    vc = vc_ref[...]
    nk = 3 * A_BLOCK + kc.shape[0]
    g = A_HEADS // A_KV
    row = lax.broadcasted_iota(jnp.int32, (g * A_BLOCK, nk), 0) % A_BLOCK
    col = lax.broadcasted_iota(jnp.int32, (g * A_BLOCK, nk), 1)
    rel = col - A_BLOCK - row
    band = (jnp.abs(rel) <= A_WINDOW) | (col >= 3 * A_BLOCK)
    chains, ok, k_all, v_all = [], [], [], []
    for j in range(A_STEP_BLOCKS):
        n = pl.program_id(1) * A_STEP_BLOCKS + j
        ok.append(band & ((n > 0) | (col >= A_BLOCK)) & ((n < nblk - 1) | (col < 2 * A_BLOCK) | (col >= 3 * A_BLOCK)))
        k_all.append(jnp.concatenate(kb[j:j + 3] + [kc], axis=0))
        v_all.append(jnp.concatenate(vb[j:j + 3] + [vc], axis=0))
        chains += [(j, hk) for hk in range(A_KV)]

    def qg(j, hk):
        rows = slice(j * A_BLOCK, (j + 1) * A_BLOCK)
        return jnp.concatenate([q_ref[rows, (hk * g + i) * HD:(hk * g + i + 1) * HD] for i in range(g)], axis=0)

    sink = [jnp.concatenate([jnp.broadcast_to(sink_ref[:, hk * g + i:hk * g + i + 1], (A_BLOCK, 1))
                             for i in range(g)], axis=0) for hk in range(A_KV)]
    s = [jnp.where(ok[j], _dot_nt(qg(j, hk), k_all[j][:, hk * HD:(hk + 1) * HD]), NEG) for j, hk in chains]
    m = [jnp.maximum(jnp.max(s[i], axis=-1, keepdims=True), sink[hk]) for i, (j, hk) in enumerate(chains)]
    p = [jnp.exp(s[i] - m[i]) for i in range(len(chains))]
    den = [jnp.sum(p[i], axis=-1, keepdims=True) + jnp.exp(sink[hk] - m[i]) for i, (j, hk) in enumerate(chains)]
    o = [_dot(p[i].astype(BF16), v_all[j][:, hk * HD:(hk + 1) * HD]) / den[i] for i, (j, hk) in enumerate(chains)]
    for i, (j, hk) in enumerate(chains):
        for gi in range(g):
            hq = hk * g + gi
            o_ref[j * A_BLOCK:(j + 1) * A_BLOCK, hq * HD:(hq + 1) * HD] = (
                o[i][gi * A_BLOCK:(gi + 1) * A_BLOCK].astype(BF16))


def _attn_a(ab_lat, ab_ctx, sink, nb, s, lc):
    nblk = s // A_BLOCK
    steps = nblk // A_STEP_BLOCKS

    def kv(col, d):
        return pl.BlockSpec(
            (A_BLOCK, 128), lambda b, i: (b * nblk + jnp.clip(i * A_STEP_BLOCKS + d, 0, nblk - 1), col))

    tq = A_STEP_BLOCKS * A_BLOCK
    return pl.pallas_call(
        functools.partial(_attn_a_kernel, nblk=nblk),
        grid=(nb, steps),
        in_specs=[pl.BlockSpec((tq, 256), lambda b, i: (b * steps + i, 0)),
                  kv(2, -1), kv(2, 0), kv(2, 1), kv(2, 2), kv(3, -1), kv(3, 0), kv(3, 1), kv(3, 2),
                  pl.BlockSpec((lc, 128), lambda b, i: (b, 2)),
                  pl.BlockSpec((lc, 128), lambda b, i: (b, 3)),
                  pl.BlockSpec((1, A_HEADS), lambda b, i: (0, 0))],
        out_specs=pl.BlockSpec((tq, 256), lambda b, i: (b * steps + i, 0)),
        out_shape=jax.ShapeDtypeStruct((nb * s, 256), BF16),
        compiler_params=_cp("parallel", "parallel"),
        name="attn_a",
    )(ab_lat, *([ab_lat] * 8), ab_ctx, ab_ctx, sink.reshape(1, A_HEADS))


def _attn_b_kernel(q_ref, k_ref, v_ref, kc_ref, vc_ref, bias_ref, o_ref, *, rows):
    nwin = NA_ROWS * GRID_W
    kc = kc_ref[...]
    vc = vc_ref[...]
    q, kw, vw, tab = [], [], [], []
    for rr in range(NA_STEP_ROWS):
        r = pl.program_id(1) * NA_STEP_ROWS + rr
        rs = jnp.clip(r - NA_ROWS // 2, 0, rows - NA_ROWS)
        start = pl.multiple_of(rs * GRID_W, GRID_W)
        tab.append(rs - r + NA_ROWS - 1)
        q.append(q_ref[rr * GRID_W:(rr + 1) * GRID_W, :])
        kw.append(k_ref[pl.ds(start, nwin), :])
        vw.append(v_ref[pl.ds(start, nwin), :])
    chains = [(rr, h) for rr in range(NA_STEP_ROWS) for h in range(B_HEADS)]

    def hs(x, h):
        return x[:, h * HD:(h + 1) * HD]

    s_loc = [_dot_nt(hs(q[rr], h), hs(kw[rr], h)) + bias_ref[tab[rr], h] for rr, h in chains]
    s_ctx = [_dot_nt(hs(q[rr], h), hs(kc, h)) for rr, h in chains]
    m = [jnp.maximum(jnp.max(a, axis=-1, keepdims=True), jnp.max(b, axis=-1, keepdims=True))
         for a, b in zip(s_loc, s_ctx)]
    p_loc = [jnp.exp(a - mm) for a, mm in zip(s_loc, m)]
    p_ctx = [jnp.exp(b - mm) for b, mm in zip(s_ctx, m)]
    den = [jnp.sum(a, axis=-1, keepdims=True) + jnp.sum(b, axis=-1, keepdims=True) for a, b in zip(p_loc, p_ctx)]
    o = [(_dot(p_loc[i].astype(BF16), hs(vw[rr], h)) + _dot(p_ctx[i].astype(BF16), hs(vc, h))) / den[i]
         for i, (rr, h) in enumerate(chains)]
    for i, (rr, h) in enumerate(chains):
        o_ref[rr * GRID_W:(rr + 1) * GRID_W, h * HD:(h + 1) * HD] = o[i].astype(BF16)


def _na_bias_tables(rpb, rows):
    kr = NA_ROWS
    qc = np.arange(GRID_W)
    win_start = np.clip(qc - NA_COLS // 2, 0, GRID_W - NA_COLS)
    kcol = np.arange(GRID_W)
    valid = (kcol[None, :] >= win_start[:, None]) & (kcol[None, :] < win_start[:, None] + NA_COLS)
    pad = GRID_W - NA_COLS
    rp = jnp.pad(rpb.astype(F32), ((0, 0), (0, 0), (pad, pad)))
    toe = jnp.stack([rp[:, :, GRID_W - 1 - q:2 * GRID_W - 1 - q] for q in range(GRID_W)], axis=2)
    toe = jnp.where(jnp.asarray(valid[None, None]), toe, NEG)
    tabs = [toe[:, off:off + kr].transpose(0, 2, 1, 3).reshape(rpb.shape[0], GRID_W, kr * GRID_W)
            for off in range(kr)]
    return jnp.stack(tabs, axis=0)


def _attn_b(ab_lat, ab_ctx, bias_tabs, nb, s, lc):
    rows = s // GRID_W
    steps = rows // NA_STEP_ROWS
    tq = NA_STEP_ROWS * GRID_W
    return pl.pallas_call(
        functools.partial(_attn_b_kernel, rows=rows),
        grid=(nb, steps),
        in_specs=[pl.BlockSpec((tq, 256), lambda b, r: (b * steps + r, 2)),
                  pl.BlockSpec((s, 256), lambda b, r: (b, 3)),
                  pl.BlockSpec((s, 256), lambda b, r: (b, 4)),
                  pl.BlockSpec((lc, 256), lambda b, r: (b, 3)),
                  pl.BlockSpec((lc, 256), lambda b, r: (b, 4)),
                  pl.BlockSpec((NA_ROWS, B_HEADS, GRID_W, NA_ROWS * GRID_W), lambda b, r: (0, 0, 0, 0))],
        out_specs=pl.BlockSpec((tq, 256), lambda b, r: (b * steps + r, 0)),
        out_shape=jax.ShapeDtypeStruct((nb * s, 256), BF16),
        compiler_params=_cp("parallel", "arbitrary"),
        name="attn_b",
    )(ab_lat, ab_lat, ab_lat, ab_ctx, ab_ctx, bias_tabs)


def _attn_ctx_kernel(ab_ref, sink_ref, oa_ref, ob_ref):
    lc = ab_ref.shape[0]
    g = A_HEADS // A_KV
    for hk in range(A_KV):
        qg = jnp.concatenate([ab_ref[:, (hk * g + j) * HD:(hk * g + j + 1) * HD] for j in range(g)], axis=0)
        s = _dot_nt(qg, ab_ref[:, 256 + hk * HD:256 + (hk + 1) * HD])
        sink = jnp.concatenate(
            [jnp.broadcast_to(sink_ref[:, hk * g + j:hk * g + j + 1], (lc, 1)) for j in range(g)], axis=0)
        o = _softmax_pv(s, ab_ref[:, 384 + hk * HD:384 + (hk + 1) * HD], sink)
        for j in range(g):
            hq = hk * g + j
            oa_ref[:, hq * HD:(hq + 1) * HD] = o[j * lc:(j + 1) * lc].astype(BF16)
    for h in range(B_HEADS):
        s = _dot_nt(ab_ref[:, 512 + h * HD:512 + (h + 1) * HD], ab_ref[:, 768 + h * HD:768 + (h + 1) * HD])
        o = _softmax_pv(s, ab_ref[:, 1024 + h * HD:1024 + (h + 1) * HD])
        ob_ref[:, h * HD:(h + 1) * HD] = o.astype(BF16)


def _attn_ctx(ab_ctx, sink, nb, lc):
    return pl.pallas_call(
        _attn_ctx_kernel,
        grid=(nb,),
        in_specs=[pl.BlockSpec((lc, AB_W), lambda b: (b, 0)),
                  pl.BlockSpec((1, A_HEADS), lambda b: (0, 0))],
        out_specs=[pl.BlockSpec((lc, 256), lambda b: (b, 0)),
                   pl.BlockSpec((lc, 256), lambda b: (b, 0))],
        out_shape=[jax.ShapeDtypeStruct((nb * lc, 256), BF16),
                   jax.ShapeDtypeStruct((nb * lc, 256), BF16)],
        compiler_params=_cp("parallel"),
        name="attn_ctx",
    )(ab_ctx, sink.reshape(1, A_HEADS))


def _r7prep_kernel(x_ref, prev_ref, next_ref, cw_ref, kkw_ref, ka_ref, rk_ref, w0_ref, w2_ref, a0_ref, a2_ref,
                   g2_ref, bd_ref,
                   r_ref, v_ref, kk_ref, lw_ref, beta_ref, kd_ref, g_ref, bonus_ref, *, nt):
    i = pl.program_id(1)
    tt = x_ref.shape[0]
    x = x_ref[...]
    row = lax.broadcasted_iota(jnp.int32, x.shape, 0)
    before = jnp.where(i > 0, prev_ref[7:8, :], 0.0)
    after = jnp.where(i < nt - 1, next_ref[0:1, :], 0.0)
    x_m1 = jnp.where(row == 0, before, pltpu.roll(x, 1, 0))
    x_p1 = jnp.where(row == tt - 1, after, pltpu.roll(x, tt - 1, 0))
    xc = x_m1 * cw_ref[0:1, :] + x * cw_ref[1:2, :] + x_p1 * cw_ref[2:3, :]
    r = xc[:, 0:512]
    k = xc[:, 512:1024]
    v = xc[:, 1024:1536]
    wd = xc[:, 1536:1664]
    ad = xc[:, 1664:1792]
    gd = xc[:, 1792:1920]
    bd = bd_ref[...]
    kkh = k * kkw_ref[...]
    kk = kkh / jnp.maximum(jnp.sqrt(_segsum(kkh * kkh, bd)), 1e-12)
    zw = w0_ref[...] + _dot3(jnp.tanh(wd), w2_ref[...])
    za = a0_ref[...] + _dot3(ad, a2_ref[...])
    a = jax.nn.sigmoid(za)
    ka = ka_ref[...]
    kd0 = k * (1.0 + (a[:, 0:512] - 1.0) * ka)
    kd1 = k * (1.0 + (a[:, 512:1024] - 1.0) * ka)
    r_ref[...] = r.astype(BF16)
    v_ref[...] = v.astype(BF16)
    kk_ref[...] = kk.astype(BF16)
    lw_ref[...] = -DECAY_SCALE * jax.nn.sigmoid(zw)
    beta_ref[:, 0:512] = (kk * a[:, 0:512]).astype(BF16)
    beta_ref[:, 512:1024] = (kk * a[:, 512:1024]).astype(BF16)
    kd_ref[:, 0:512] = kd0.astype(BF16)
    kd_ref[:, 512:1024] = kd1.astype(BF16)
    g_ref[...] = _dot3(jax.nn.sigmoid(gd), g2_ref[...])
    bonus_ref[...] = _segsum(r * (kd0 + kd1) * rk_ref[...], bd) * v


def _bd2(w):
    z = jnp.zeros_like(w[0])
    return jnp.concatenate([jnp.concatenate([w[0], z], axis=1), jnp.concatenate([z, w[1]], axis=1)], axis=0)


def _r7prep(pc, lp, nb, t):
    tt = _row_tile(t)
    nt = t // tt
    nrow8 = t // 8

    def full(shape):
        return pl.BlockSpec(shape, lambda b, i: (0,) * len(shape))

    def rowspec(w):
        return pl.BlockSpec((tt, w), lambda b, i: (b * nt + i, 0))

    outs = [C_W, C_W, C_W, 2 * C_W, 2 * C_W, 2 * C_W, C_W, C_W]
    dtypes = [BF16, BF16, BF16, F32, BF16, BF16, F32, F32]
    return pl.pallas_call(
        functools.partial(_r7prep_kernel, nt=nt),
        grid=(nb, nt),
        in_specs=[rowspec(C_IN),
                  pl.BlockSpec((8, C_IN), lambda b, i: (b * nrow8 + jnp.maximum(i * (tt // 8) - 1, 0), 0)),
                  pl.BlockSpec((8, C_IN), lambda b, i: (b * nrow8 + jnp.minimum((i + 1) * (tt // 8), nrow8 - 1), 0)),
                  full((3, C_IN)), full((1, C_W)), full((1, C_W)), full((1, C_W)),
                  full((1, 2 * C_W)), full((128, 2 * C_W)), full((1, 2 * C_W)), full((128, 2 * C_W)),
                  full((128, C_W)), full((SEG_W, SEG_W))],
        out_specs=[rowspec(w) for w in outs],
        out_shape=[jax.ShapeDtypeStruct((nb * t, w), dt) for w, dt in zip(outs, dtypes)],
        compiler_params=_cp("parallel", "parallel"),
        name="r7prep",
    )(pc, pc, pc, lp['r7_conv'], lp['r7_kk'].reshape(1, C_W), lp['r7_ka'].reshape(1, C_W),
      lp['r7_rk'].reshape(1, C_W), lp['r7_w0'].reshape(1, 2 * C_W), _bd2(lp['r7_w2']),
      lp['r7_a0'].reshape(1, 2 * C_W), _bd2(lp['r7_a2']), lp['r7_g2'], _block_diag_ones())


def _cumsum_rows(tri, x):
    t = tri.astype(BF16)
    hi = x.astype(BF16)
    rest = x - hi.astype(F32)
    mid = rest.astype(BF16)
    lo = (rest - mid.astype(F32)).astype(BF16)
    return _dot(t, hi) + (_dot(t, mid) + _dot(t, lo))


def _mm(a, b):
    return _dot(a.astype(BF16), b.astype(BF16))


def _mm_nt(a, b):
    return _dot_nt(a.astype(BF16), b.astype(BF16))


def _scan_kernel(rf_ref, vf_ref, kkf_ref, lwf_ref, betaf_ref, kdf_ref,
                 rb_ref, vb_ref, kkb_ref, lwb_ref, betab_ref, kdb_ref, s0_ref,
                 yf_ref, yb_ref, sf_ref, st_ref, *, nc):
    c = pl.program_id(1)
    cl = SCAN_C

    @pl.when(c == 0)
    def _():
        st_ref[...] = s0_ref[...]

    ti = lax.broadcasted_iota(jnp.int32, (cl, cl), 0)
    si = lax.broadcasted_iota(jnp.int32, (cl, cl), 1)
    eye = (ti == si).astype(F32)
    incl = [si <= ti, si >= ti]
    strict = [si < ti, si > ti]

    def hs(x, h):
        return x[:, h * HD:(h + 1) * HD]

    pre = {}
    for bb in range(SCAN_NB):
        for d, (r_ref, v_ref, kk_ref, lw_ref, beta_ref, kd_ref) in enumerate(
                [(rf_ref, vf_ref, kkf_ref, lwf_ref, betaf_ref, kdf_ref),
                 (rb_ref, vb_ref, kkb_ref, lwb_ref, betab_ref, kdb_ref)]):
            lw = lw_ref[bb]
            cum = _cumsum_rows(incl[d], lw)
            e_neg = jnp.exp(-cum)
            pre[bb, d] = dict(r_hat=r_ref[bb] * jnp.exp(cum), a_hat=-kk_ref[bb] * jnp.exp(cum - lw),
                              b_til=beta_ref[bb] * e_neg, k_til=kd_ref[bb] * e_neg,
                              wtot=jnp.exp(jnp.sum(lw, axis=0, keepdims=True)), vv=v_ref[bb])

    chains = [(bb, d, h) for bb in range(SCAN_NB) for d in range(2) for h in range(C_HEADS)]
    n_ch = range(len(chains))

    def part(name, i):
        bb, d, h = chains[i]
        return hs(pre[bb, d][name], h)

    def dirn(i):
        return chains[i][1]

    ar = [jnp.concatenate([part('a_hat', i), part('r_hat', i)], axis=0) for i in n_ch]
    bk = [jnp.concatenate([part('b_til', i), part('k_til', i)], axis=0) for i in n_ch]
    m = [_mm_nt(ar[i], bk[i]) for i in n_ch]
    l_k = [jnp.where(strict[dirn(i)], m[i][0:cl, cl:2 * cl], 0.0) for i in n_ch]
    r_b = [jnp.where(incl[dirn(i)], m[i][cl:2 * cl, 0:cl], 0.0) for i in n_ch]
    r_k = [jnp.where(incl[dirn(i)], m[i][cl:2 * cl, cl:2 * cl], 0.0) for i in n_ch]
    l_mat = [jnp.where(strict[dirn(i)], m[i][0:cl, 0:cl], 0.0) for i in n_ch]
    halves_of = lambda b: ((ti // (2 * b)) == (si // (2 * b))) & ((ti // b) != (si // b))
    t_inv = [eye + jnp.where(halves_of(1), l_mat[i], 0.0) for i in n_ch]
    b = 2
    while b < cl:
        couple = halves_of(b)
        nt_ = [_mm(jnp.where(couple, l_mat[i], 0.0), t_inv[i]) for i in n_ch]
        t_inv = [t_inv[i] + _mm(t_inv[i], nt_[i]) for i in n_ch]
        b *= 2
    x1 = [_mm(l_k[i], part('vv', i)) for i in n_ch]
    y0 = [_mm(r_k[i], part('vv', i)) for i in n_ch]
    ua = [_mm(t_inv[i], jnp.concatenate([x1[i], part('a_hat', i)], axis=1)) for i in n_ch]
    s0 = [st_ref[d, bb, h] for bb, d, h in chains]
    as0 = [_mm_nt(jnp.concatenate([ua[i][:, HD:2 * HD], part('r_hat', i)], axis=0), s0[i])
           for i in n_ch]
    u = [ua[i][:, 0:HD] + as0[i][0:cl] for i in n_ch]
    y = [y0[i] + as0[i][cl:2 * cl] + _mm(r_b[i], u[i]) for i in n_ch]
    for i in n_ch:
        bb, d, h = chains[i]
        (yf_ref, yb_ref)[d][bb, :, h * HD:(h + 1) * HD] = y[i]
    for i in n_ch:
        bb, d, h = chains[i]
        uv_t = jnp.concatenate([u[i], part('vv', i)], axis=0).T
        st_ref[d, bb, h] = (s0[i] + _mm(uv_t, bk[i])) * part('wtot', i)

    @pl.when(c == nc - 1)
    def _():
        sf_ref[...] = st_ref[...]


def _scan(prep, s0, nb, t):
    nc = t // SCAN_C
    assert nb % SCAN_NB == 0
    r, v, kk, lw, beta, kd = [a.reshape(nb, t, a.shape[-1]) for a in prep]

    def chunk(d, c):
        return nc - 1 - c if d else c

    def specs(d):
        shared = pl.BlockSpec((SCAN_NB, SCAN_C, C_W), lambda b, c: (b, chunk(d, c), 0))
        perdir = pl.BlockSpec((SCAN_NB, SCAN_C, C_W), lambda b, c: (b, chunk(d, c), d))
        return [shared, shared, shared, perdir, perdir, perdir]

    st_spec = pl.BlockSpec((2, SCAN_NB, C_HEADS, HD, HD), lambda b, c: (0, b, 0, 0, 0))
    yf, yb, sf = pl.pallas_call(
        functools.partial(_scan_kernel, nc=nc),
        grid=(nb // SCAN_NB, nc),
        in_specs=specs(0) + specs(1) + [st_spec],
        out_specs=[pl.BlockSpec((SCAN_NB, SCAN_C, C_W), lambda b, c: (b, chunk(0, c), 0)),
                   pl.BlockSpec((SCAN_NB, SCAN_C, C_W), lambda b, c: (b, chunk(1, c), 0)), st_spec],
        out_shape=[jax.ShapeDtypeStruct((nb, t, C_W), F32), jax.ShapeDtypeStruct((nb, t, C_W), F32),
                   jax.ShapeDtypeStruct((2, nb, C_HEADS, HD, HD), F32)],
        scratch_shapes=[pltpu.VMEM((2, SCAN_NB, C_HEADS, HD, HD), F32)],
        compiler_params=_cp("parallel", "arbitrary"),
        name="r7scan",
    )(r, v, kk, lw, beta, kd, r, v, kk, lw, beta, kd, s0)
    return (yf.reshape(nb * t, C_W), yb.reshape(nb * t, C_W)), sf


def _outproj_kernel(x_ref, oa_ref, ob_ref, y0_ref, y1_ref, bonus_ref, g_ref, lnw_ref, lnb_ref, bd_ref, w_ref,
                    gate_ref, o_ref):
    bd = bd_ref[...]
    y = y0_ref[...] + y1_ref[...]
    mu = _segsum(y, bd) * (1.0 / HD)
    yc = y - mu
    var = _segsum(yc * yc, bd) * (1.0 / HD)
    yn = yc * lax.rsqrt(var + GN_EPS) * lnw_ref[...] + lnb_ref[...]
    oc = ((yn + bonus_ref[...]) * g_ref[...]).astype(BF16)
    acc = (_dot(oa_ref[...], w_ref[0:256, :]) + _dot(ob_ref[...], w_ref[256:512, :])
           + _dot(oc, w_ref[512:1024, :]))
    o_ref[...] = x_ref[...] + gate_ref[...] * acc


def _outproj(x2, oa, ob, y, bonus, g, lp, w_out_bf, mod3, mod_row_of_batch, nb, t):
    tm = _row_tile(t)
    nt = t // tm

    def rowspec(w):
        return pl.BlockSpec((tm, w), lambda b, i: (b * nt + i, 0))

    def full(shape):
        return pl.BlockSpec(shape, lambda b, i: (0,) * len(shape))

    return pl.pallas_call(
        _outproj_kernel,
        grid=(nb, nt),
        in_specs=[rowspec(D), rowspec(256), rowspec(256),
                  rowspec(C_W), rowspec(C_W),
                  rowspec(C_W), rowspec(C_W), full((1, C_W)), full((1, C_W)), full((SEG_W, SEG_W)), full((D, D)),
                  pl.BlockSpec((None, 1, D), lambda b, i: (mod_row_of_batch(b), 0, 2))],
        out_specs=rowspec(D),
        out_shape=jax.ShapeDtypeStruct((nb * t, D), F32),
        compiler_params=_cp("parallel", "parallel"),
        name="outproj",
    )(x2, oa, ob, y[0], y[1], bonus, g, lp['r7_lnw'].reshape(1, C_W), lp['r7_lnb'].reshape(1, C_W),
      _block_diag_ones(), w_out_bf, mod3)


def _peer_score_kernel(x_ref, g_ref, sc_ref, sh_ref, wq_ref, keys_ref, h_ref, st_ref):
    x = x_ref[...]
    y = x * lax.rsqrt(jnp.mean(x * x, axis=-1, keepdims=True) + EPS) * g_ref[...]
    h = (y * (1.0 + sc_ref[...]) + sh_ref[...]).astype(BF16)
    h_ref[...] = h
    q = _dot(h, wq_ref[...])
    half = P_QDIM // 2
    for hp in range(2 * P_HEADS):
        st_ref[hp] = _dot3(keys_ref[hp], q[:, hp * half:(hp + 1) * half], nt=True)


def _peer_scores(x2, norm_g, mod3, mod_row_of_batch, wq_bf, keys, nb, t):
    tm = _row_tile(t)
    nt = t // tm
    n = nb * t
    return pl.pallas_call(
        _peer_score_kernel,
        grid=(nb, nt),
        in_specs=[pl.BlockSpec((tm, D), lambda b, i: (b * nt + i, 0)),
                  pl.BlockSpec((1, D), lambda b, i: (0, 0)),
                  pl.BlockSpec((None, 1, D), lambda b, i: (mod_row_of_batch(b), 0, 4)),
                  pl.BlockSpec((None, 1, D), lambda b, i: (mod_row_of_batch(b), 0, 3)),
                  pl.BlockSpec((D, P_HEADS * P_QDIM), lambda b, i: (0, 0)),
                  pl.BlockSpec((2 * P_HEADS, P_NKEYS, P_QDIM // 2), lambda b, i: (0, 0, 0))],
        out_specs=[pl.BlockSpec((tm, D), lambda b, i: (b * nt + i, 0)),
                   pl.BlockSpec((2 * P_HEADS, P_NKEYS, tm), lambda b, i: (0, 0, b * nt + i))],
        out_shape=[jax.ShapeDtypeStruct((n, D), BF16),
                   jax.ShapeDtypeStruct((2 * P_HEADS, P_NKEYS, n), F32)],
        compiler_params=_cp("parallel", "parallel"),
        name="peer_scores",
    )(x2, norm_g.reshape(1, D), mod3, mod3, wq_bf, keys.reshape(2 * P_HEADS, P_NKEYS, P_QDIM // 2))


def _batcher_pairs(n):
    pairs = []
    p = 1
    while p < n:
        k = p
        while k >= 1:
            for j in range(k % p, n - k, 2 * k):
                for i in range(min(k, n - j - k)):
                    if (i + j) // (2 * p) == (i + j + k) // (2 * p):
                        pairs.append((i + j, i + j + k))
            k //= 2
        p *= 2
    return pairs


_SORT16 = _batcher_pairs(P_TOPK)
_BITONIC16 = [(i, i + d) for d in (8, 4, 2, 1) for i in range(P_TOPK) if i & d == 0]


def _compare_exchange(x, pairs):
    for i, j in pairs:
        x[i], x[j] = jnp.maximum(x[i], x[j]), jnp.minimum(x[i], x[j])
    return x


def _top16_sorted(slabs):
    x = _compare_exchange(list(slabs), _SORT16)
    for shift in (4, 2, 1):
        y = [pltpu.roll(x[P_TOPK - 1 - k], shift, 0) for k in range(P_TOPK)]
        x = [jnp.maximum(x[k], y[k]) for k in range(P_TOPK)]
        x = _compare_exchange(x, _BITONIC16)
    return x


def _peer_gate_tables(st_ref, rk_ref, be_ref, cnt_ref, al_ref, top_ref):
    nblk = st_ref.shape[-1] // LANES
    sub = lax.broadcasted_iota(jnp.int32, (8, LANES), 0)
    ninf = jnp.full((8, LANES), -jnp.inf, F32)

    def block(it, carry):
        h = it // nblk
        lb = it % nblk
        lanes = pl.ds(pl.multiple_of(lb * LANES, LANES), LANES)
        hf = lb // (PEER_HALF // LANES)
        hl = pl.ds(pl.multiple_of((lb % (PEER_HALF // LANES)) * LANES, LANES), LANES)
        for p in range(2):
            top = _top16_sorted([st_ref[2 * h + p, 8 * k:8 * k + 8, lanes] for k in range(P_TOPK)])
            for k in range(P_TOPK):
                top_ref[p * P_TOPK + k:p * P_TOPK + k + 1, lanes] = top[k][0:1]
        a16 = top_ref[0:P_TOPK, lanes]
        b16 = top_ref[P_TOPK:2 * P_TOPK, lanes]
        b8 = b16[0:8]
        cand = [a16[0:1] + b8, a16[0:1] + b16[8:16]]
        for p in range(2, 9):
            cand.append(jnp.where(sub < P_TOPK // p, a16[p - 1:p] + b8, -jnp.inf))
        cand.append(a16[8:16] + b16[0:1])
        best = _top16_sorted(cand + [ninf] * (P_TOPK - len(cand)))
        tau = best[P_TOPK - 1][0:1]
        z = jnp.ones_like(tau)
        for k in range(1, P_TOPK):
            z = z + jnp.exp(best[k][0:1] - best[0][0:1])
        s1 = st_ref[2 * h, :, lanes]
        s2 = st_ref[2 * h + 1, :, lanes]
        cnt = jnp.zeros_like(s1)
        rk = jnp.ones_like(s2)
        for q in range(P_TOPK):
            bq = b16[q:q + 1]
            theta = jnp.min(jnp.where(a16 + bq >= tau, a16, jnp.inf), axis=0, keepdims=True)
            cnt = jnp.where(s1 >= theta, q + 1.0, cnt)
            rk = jnp.where(bq > s2, q + 2.0, rk)
        cnt_ref[hf, h, :, hl] = cnt
        rk_ref[hf, h, :, hl] = rk.astype(BF16)
        al_ref[hf, h, :, hl] = jnp.exp(s1 - a16[0:1]) / z
        be_ref[hf, h, :, hl] = jnp.exp(s2 - b16[0:1]).astype(BF16)
        return carry

    lax.fori_loop(0, P_HEADS * nblk, block, 0)


def _peer_chunk(chunk, act_ref, rk_ref, be_ref, cnt_ref, al_ref, w_ref):
    nrow = w_ref.shape[0] // P_NKEYS
    for ii in range(nrow):
        i = chunk * nrow + ii
        wrow = None
        for h in range(P_HEADS):
            cnt = cnt_ref[h, pl.ds(i, 1), :].astype(BF16)
            al = al_ref[h, pl.ds(i, 1), :].astype(BF16)
            term = jnp.where(rk_ref[h] <= cnt, al * be_ref[h], 0.0)
            wrow = term if wrow is None else wrow + term
        rows = slice(ii * P_NKEYS, (ii + 1) * P_NKEYS)
        act = act_ref[rows, :]
        gl = 0.5 * act * (1.0 + lax.erf(act * (2.0 ** -0.5)))
        w_ref[rows, :] = wrow * gl.astype(BF16)


def _peer_dense_kernel(x_ref, h_ref, st_ref, u_ref, vt_ref, gate_ref, o_ref,
                       rk_ref, be_ref, cnt_ref, al_ref, top_ref, act0_ref, act1_ref, w_ref, acc_ref, *, ne):
    e = pl.program_id(1)
    halves = range(w_ref.shape[0])

    def rows(hf):
        return slice(hf * PEER_HALF, (hf + 1) * PEER_HALF)

    @pl.when(e == 0)
    def _():
        for hf in halves:
            act0_ref[hf] = _dot_nt(u_ref[...], h_ref[rows(hf), :])
        _peer_gate_tables(st_ref, rk_ref, be_ref, cnt_ref, al_ref, top_ref)
        acc_ref[...] = jnp.zeros_like(acc_ref)

    def step(cur_ref, nxt_ref):
        for hf in halves:
            nxt_ref[hf] = _dot_nt(u_ref[...], h_ref[rows(hf), :])
            _peer_chunk(e - 1, cur_ref.at[hf], rk_ref.at[hf], be_ref.at[hf], cnt_ref.at[hf], al_ref.at[hf],
                        w_ref.at[hf])
            acc_ref[hf] += _dot(vt_ref[...], w_ref[hf])

    @pl.when(e % 2 == 1)
    def _():
        step(act0_ref, act1_ref)

    @pl.when((e > 0) & (e % 2 == 0))
    def _():
        step(act1_ref, act0_ref)

    @pl.when(e == ne)
    def _():
        for hf in halves:
            o_ref[rows(hf), :] = x_ref[rows(hf), :] + gate_ref[...] * acc_ref[hf].T


def _peer_dense(x2, h2, st, u_bf, vt_bf, mod3, mod_row_of_batch, nb, t):
    tm = _row_tile(t)
    nh = tm // PEER_HALF
    ec = PEER_EC
    nt = t // tm
    ne = P_EXPERTS // ec
    return pl.pallas_call(
        functools.partial(_peer_dense_kernel, ne=ne),
        grid=(nb * nt, ne + 1),
        in_specs=[pl.BlockSpec((tm, D), lambda i, e: (i, 0)),
                  pl.BlockSpec((tm, D), lambda i, e: (i, 0)),
                  pl.BlockSpec((2 * P_HEADS, P_NKEYS, tm), lambda i, e: (0, 0, i)),
                  pl.BlockSpec((ec, D), lambda i, e: (jnp.minimum(e, ne - 1), 0)),
                  pl.BlockSpec((None, D, ec), lambda i, e: (jnp.maximum(e - 1, 0), 0, 0)),
                  pl.BlockSpec((None, 1, D), lambda i, e: (mod_row_of_batch(i // nt), 0, 5))],
        out_specs=pl.BlockSpec((tm, D), lambda i, e: (i, 0)),
        out_shape=jax.ShapeDtypeStruct((nb * t, D), F32),
        scratch_shapes=[pltpu.VMEM((nh, P_HEADS, P_NKEYS, PEER_HALF), BF16),
                        pltpu.VMEM((nh, P_HEADS, P_NKEYS, PEER_HALF), BF16),
                        pltpu.VMEM((nh, P_HEADS, P_NKEYS, PEER_HALF), F32),
                        pltpu.VMEM((nh, P_HEADS, P_NKEYS, PEER_HALF), F32),
                        pltpu.VMEM((2 * P_TOPK, tm), F32),
                        pltpu.VMEM((nh, ec, PEER_HALF), F32),
                        pltpu.VMEM((nh, ec, PEER_HALF), F32),
                        pltpu.VMEM((nh, ec, PEER_HALF), BF16),
                        pltpu.VMEM((nh, D, PEER_HALF), F32)],
        compiler_params=pltpu.CompilerParams(dimension_semantics=("parallel", "arbitrary"),
                                             vmem_limit_bytes=PEER_VMEM_LIMIT),
        name="peer_dense",
    )(x2, h2, st, u_bf, vt_bf, mod3)


def _peer(x2, norm_g, mod3, mod_row_of_batch, wq_bf, keys, u_bf, vt_bf, nb, t):
    h2, st = _peer_scores(x2, norm_g, mod3, mod_row_of_batch, wq_bf, keys, nb, t)
    return _peer_dense(x2, h2, st, u_bf, vt_bf, mod3, mod_row_of_batch, nb, t)


def kernel(x, c, ctx, c_ctx, norm_mix, norm_ffn, w_mod, b_mod, w_in, w_out, a_qnorm, a_knorm, a_sink, b_qnorm,
           b_knorm, b_rpb, r7_conv, r7_w0, r7_w2, r7_a0, r7_a2, r7_g2, r7_kk, r7_ka, r7_rk, r7_lnw, r7_lnb,
           peer_wq, peer_keys, peer_u, peer_v):
    nb, s, _ = x.shape
    lc = ctx.shape[1]
    depth = w_in.shape[0]
    assert nb < 16 and nb % SCAN_NB == 0 and s % 512 == 0 and lc % 256 == 0
    rows = s // GRID_W

    cc = jnp.zeros((16, D), F32).at[:nb].set(c).at[nb].set(c_ctx)
    mod = _modulation(cc, w_mod, b_mod)
    rope_tabs = _rope_tables(s)
    lat_row = lambda b: b
    ctx_row = lambda b: nb

    x_lat = x.reshape(nb * s, D)
    x_ctx = ctx.reshape(nb * lc, D)
    scale = HD ** -0.5
    for l in range(depth):
        with_ctx = l < depth - 1
        mod3 = mod[l].reshape(16, 1, 6 * D)
        lp = {'r7_conv': r7_conv[l], 'r7_w0': r7_w0[l], 'r7_w2': r7_w2[l], 'r7_a0': r7_a0[l], 'r7_a2': r7_a2[l],
              'r7_g2': r7_g2[l], 'r7_kk': r7_kk[l], 'r7_ka': r7_ka[l], 'r7_rk': r7_rk[l], 'r7_lnw': r7_lnw[l],
              'r7_lnb': r7_lnb[l]}
        w_in_bf = w_in[l].astype(BF16)
        w_out_bf = w_out[l].astype(BF16)
        gain_a = jnp.concatenate([jnp.tile(a_qnorm[l] * scale, A_HEADS), jnp.tile(a_knorm[l], A_KV)]).reshape(1, 384)
        gain_b = jnp.concatenate([jnp.tile(b_qnorm[l] * scale, B_HEADS), jnp.tile(b_knorm[l], B_HEADS)]).reshape(1, 512)

        ab_lat, pc_lat = _inproj(x_lat, mod3, lat_row, norm_mix[l], w_in_bf, gain_a, gain_b, rope_tabs, nb, s)
        ab_ctx, pc_ctx = _inproj(x_ctx, mod3, ctx_row, norm_mix[l], w_in_bf, gain_a, gain_b, None, 1, nb * lc)

        o_a = _attn_a(ab_lat, ab_ctx, a_sink[l], nb, s, lc)
        o_b = _attn_b(ab_lat, ab_ctx, _na_bias_tables(b_rpb[l], rows), nb, s, lc)

        prep_ctx = _r7prep(pc_ctx, lp, nb, lc)
        prep_lat = _r7prep(pc_lat, lp, nb, s)
        zero_state = jnp.zeros((2, nb, C_HEADS, HD, HD), F32)
        y_ctx, s_ctx = _scan(prep_ctx[:6], zero_state, nb, lc)
        y_lat, _ = _scan(prep_lat[:6], s_ctx, nb, s)

        x_lat = _outproj(x_lat, o_a, o_b, y_lat, prep_lat[7], prep_lat[6], lp, w_out_bf, mod3, lat_row, nb, s)
        wq_bf = peer_wq[l].astype(BF16)
        u_bf = peer_u[l].astype(BF16)
        vt_bf = peer_v[l].astype(BF16).reshape(P_EXPERTS // PEER_EC, PEER_EC, D).transpose(0, 2, 1)
        x_lat = _peer(x_lat, norm_ffn[l], mod3, lat_row, wq_bf, peer_keys[l], u_bf, vt_bf, nb, s)
        if with_ctx:
            o_ac, o_bc = _attn_ctx(ab_ctx, a_sink[l], nb, lc)
            x_ctx = _outproj(x_ctx, o_ac, o_bc, y_ctx, prep_ctx[7], prep_ctx[6], lp, w_out_bf, mod3, ctx_row,
                             1, nb * lc)
            x_ctx = _peer(x_ctx, norm_ffn[l], mod3, ctx_row, wq_bf, peer_keys[l], u_bf, vt_bf, 1, nb * lc)
    return x_lat.reshape(nb, s, D)
```

```python
import functools

import numpy as np
import jax
import jax.numpy as jnp
from jax import lax
from jax.experimental import pallas as pl
from jax.experimental.pallas import tpu as pltpu

F32 = jnp.float32
BF16 = jnp.bfloat16
HI = lax.Precision.HIGHEST

D = 1024
GRID_W = 64
HD = 64
LANES = 128
EPS = 1e-6
NEG = -1e30
A_HEADS, A_KV, A_BLOCK, A_WINDOW = 4, 2, 128, 128
A_STEP_BLOCKS = 2
B_HEADS, NA_ROWS, NA_COLS = 4, 8, 16
NA_STEP_ROWS = 8
C_HEADS = 8
C_W = 512
C_IN = 1920
AB_W = 1280
IN_W = AB_W + C_IN
GN_EPS = 64e-5
DECAY_SCALE = 0.6065306597126334
ROPE_BASE = 10000.0
P_HEADS, P_NKEYS, P_QDIM, P_TOPK = 8, 128, 256, 16
P_EXPERTS = P_NKEYS * P_NKEYS
SCAN_C = 64
SCAN_NB = 2
VMEM_LIMIT = 48 * 1024 * 1024
PEER_VMEM_LIMIT = 60 * 1024 * 1024
PEER_EC = 2048
PEER_HALF = 256


def _row_tile(t):
    return 512 if t % 512 == 0 else 256


def _cp(*sem):
    return pltpu.CompilerParams(dimension_semantics=sem, vmem_limit_bytes=VMEM_LIMIT)


def _dot(a, b, prec=None):
    return jnp.dot(a, b, precision=prec, preferred_element_type=F32)


def _dot_nt(a, b, prec=None):
    return lax.dot_general(a, b, (((1,), (1,)), ((), ())), precision=prec, preferred_element_type=F32)


SEG_W = 256


def _block_diag_ones():
    i = np.arange(SEG_W) // HD
    return jnp.asarray((i[:, None] == i[None, :]).astype(np.float32), BF16)


def _split2(x):
    hi = x.astype(BF16)
    return hi, (x - hi.astype(F32)).astype(BF16)


def _segsum(x, bd):
    hi, lo = _split2(x)
    outs = []
    for g0 in range(0, x.shape[1], SEG_W):
        w = min(SEG_W, x.shape[1] - g0)
        outs.append(_dot(hi[:, g0:g0 + w], bd[0:w, 0:w]) + _dot(lo[:, g0:g0 + w], bd[0:w, 0:w]))
    return outs[0] if len(outs) == 1 else jnp.concatenate(outs, axis=1)


def _dot3(a, b, nt=False):
    f = _dot_nt if nt else _dot
    a_hi, a_lo = _split2(a)
    b_hi, b_lo = _split2(b)
    return f(a_hi, b_hi) + (f(a_lo, b_hi) + f(a_hi, b_lo))


def _mod_kernel(c_ref, w_ref, b_ref, o_ref):
    c = c_ref[...]
    s = c * jax.nn.sigmoid(c)
    o_ref[...] = _dot(s, w_ref[...], HI) + b_ref[...]


def _modulation(cc, w_mod, b_mod):
    L, _, n = w_mod.shape
    tn = 2048
    return pl.pallas_call(
        _mod_kernel,
        grid=(L, n // tn),
        in_specs=[pl.BlockSpec((16, D), lambda l, j: (0, 0)),
                  pl.BlockSpec((None, D, tn), lambda l, j: (l, 0, j)),
                  pl.BlockSpec((None, 1, tn), lambda l, j: (l, 0, j))],
        out_specs=pl.BlockSpec((None, 16, tn), lambda l, j: (l, 0, j)),
        out_shape=jax.ShapeDtypeStruct((L, 16, n), F32),
        compiler_params=_cp("parallel", "parallel"),
        name="modulation",
    )(cc, w_mod, b_mod.reshape(L, 1, n))


def _swap16(x):
    n = x.shape[-1]
    lane = lax.broadcasted_iota(jnp.int32, x.shape, 1)
    fwd = pltpu.roll(x, n - 16, 1)
    bwd = pltpu.roll(x, 16, 1)
    return jnp.where((lane % 32) < 16, fwd, bwd)


def _head_rms(x, bd, gain):
    ss = _segsum(x * x, bd)
    return x * lax.rsqrt(ss * (1.0 / HD) + EPS) * gain


def _inproj_kernel(*refs, rope):
    if rope:
        (x_ref, g_ref, sc_ref, sh_ref, w_ref, ga_ref, gb_ref, bd_ref, cos_ref, sin_ref,
         ab_ref, c_ref) = refs
    else:
        (x_ref, g_ref, sc_ref, sh_ref, w_ref, ga_ref, gb_ref, bd_ref, ab_ref, c_ref) = refs
    x = x_ref[...]
    y = x * lax.rsqrt(jnp.mean(x * x, axis=-1, keepdims=True) + EPS) * g_ref[...]
    h = y * (1.0 + sc_ref[...]) + sh_ref[...]
    acc = _dot(h.astype(BF16), w_ref[...])
    bd = bd_ref[...]
    qa = _head_rms(acc[:, 0:384], bd, ga_ref[...])
    if rope:
        qa = qa * cos_ref[...] + _swap16(qa) * sin_ref[...]
    qb = _head_rms(acc[:, 512:1024], bd, gb_ref[...])
    ab_ref[:, 0:384] = qa.astype(BF16)
    ab_ref[:, 384:512] = acc[:, 384:512].astype(BF16)
    ab_ref[:, 512:1024] = qb.astype(BF16)
    ab_ref[:, 1024:1280] = acc[:, 1024:1280].astype(BF16)
    c_ref[...] = acc[:, AB_W:IN_W]


def _inproj(x2, mod3, mod_row_of_batch, norm_g, w_in_bf, gain_a, gain_b, rope_tabs, nb, t):
    tm = _row_tile(t)
    nt = t // tm
    rope = rope_tabs is not None
    in_specs = [
        pl.BlockSpec((tm, D), lambda b, i: (b * nt + i, 0)),
        pl.BlockSpec((1, D), lambda b, i: (0, 0)),
        pl.BlockSpec((None, 1, D), lambda b, i: (mod_row_of_batch(b), 0, 1)),
        pl.BlockSpec((None, 1, D), lambda b, i: (mod_row_of_batch(b), 0, 0)),
        pl.BlockSpec((D, IN_W), lambda b, i: (0, 0)),
        pl.BlockSpec((1, 384), lambda b, i: (0, 0)),
        pl.BlockSpec((1, 512), lambda b, i: (0, 0)),
        pl.BlockSpec((SEG_W, SEG_W), lambda b, i: (0, 0)),
    ]
    args = [x2, norm_g.reshape(1, D), mod3, mod3, w_in_bf, gain_a, gain_b,
            _block_diag_ones()]
    if rope:
        in_specs += [pl.BlockSpec((tm, 384), lambda b, i: (i, 0)),
                     pl.BlockSpec((tm, 384), lambda b, i: (i, 0))]
        args += list(rope_tabs)
    return pl.pallas_call(
        functools.partial(_inproj_kernel, rope=rope),
        grid=(nb, nt),
        in_specs=in_specs,
        out_specs=[pl.BlockSpec((tm, AB_W), lambda b, i: (b * nt + i, 0)),
                   pl.BlockSpec((tm, C_IN), lambda b, i: (b * nt + i, 0))],
        out_shape=[jax.ShapeDtypeStruct((nb * t, AB_W), BF16),
                   jax.ShapeDtypeStruct((nb * t, C_IN), F32)],
        compiler_params=_cp("parallel", "parallel"),
        name="inproj_rope" if rope else "inproj_ctx",
    )(*args)


def _rope_tables(s):
    tok = np.arange(s)
    inv = ROPE_BASE ** (-np.arange(0, 32, 2) / 32.0)
    ar = (tok // GRID_W)[:, None] * inv[None]
    ac = (tok % GRID_W)[:, None] * inv[None]
    cos = np.concatenate([np.cos(ar), np.cos(ar), np.cos(ac), np.cos(ac)], axis=1)
    sin = np.concatenate([-np.sin(ar), np.sin(ar), -np.sin(ac), np.sin(ac)], axis=1)
    return (jnp.asarray(np.tile(cos, (1, 6)), F32), jnp.asarray(np.tile(sin, (1, 6)), F32))


def _softmax_pv(s, v, sink=None):
    m = jnp.max(s, axis=-1, keepdims=True)
    if sink is not None:
        m = jnp.maximum(m, sink)
    p = jnp.exp(s - m)
    den = jnp.sum(p, axis=-1, keepdims=True)
    if sink is not None:
        den = den + jnp.exp(sink - m)
    return _dot(p.astype(BF16), v) / den


def _attn_a_kernel(q_ref, k0_ref, k1_ref, k2_ref, k3_ref, v0_ref, v1_ref, v2_ref, v3_ref, kc_ref, vc_ref,
                   sink_ref, o_ref, *, nblk):
    kb = [k0_ref[...], k1_ref[...], k2_ref[...], k3_ref[...]]
    vb = [v0_ref[...], v1_ref[...], v2_ref[...], v3_ref[...]]
    kc = kc_ref[...]
    vc = vc_ref[...]
    nk = 3 * A_BLOCK + kc.shape[0]
    g = A_HEADS // A_KV
    row = lax.broadcasted_iota(jnp.int32, (g * A_BLOCK, nk), 0) % A_BLOCK
    col = lax.broadcasted_iota(jnp.int32, (g * A_BLOCK, nk), 1)
    rel = col - A_BLOCK - row
    band = (jnp.abs(rel) <= A_WINDOW) | (col >= 3 * A_BLOCK)
    chains, ok, k_all, v_all = [], [], [], []
    for j in range(A_STEP_BLOCKS):
        n = pl.program_id(1) * A_STEP_BLOCKS + j
        ok.append(band & ((n > 0) | (col >= A_BLOCK)) & ((n < nblk - 1) | (col < 2 * A_BLOCK) | (col >= 3 * A_BLOCK)))
        k_all.append(jnp.concatenate(kb[j:j + 3] + [kc], axis=0))
        v_all.append(jnp.concatenate(vb[j:j + 3] + [vc], axis=0))
        chains += [(j, hk) for hk in range(A_KV)]

    def qg(j, hk):
        rows = slice(j * A_BLOCK, (j + 1) * A_BLOCK)
        return jnp.concatenate([q_ref[rows, (hk * g + i) * HD:(hk * g + i + 1) * HD] for i in range(g)], axis=0)

    sink = [jnp.concatenate([jnp.broadcast_to(sink_ref[:, hk * g + i:hk * g + i + 1], (A_BLOCK, 1))
                             for i in range(g)], axis=0) for hk in range(A_KV)]
    s = [jnp.where(ok[j], _dot_nt(qg(j, hk), k_all[j][:, hk * HD:(hk + 1) * HD]), NEG) for j, hk in chains]
    m = [jnp.maximum(jnp.max(s[i], axis=-1, keepdims=True), sink[hk]) for i, (j, hk) in enumerate(chains)]
    p = [jnp.exp(s[i] - m[i]) for i in range(len(chains))]
    den = [jnp.sum(p[i], axis=-1, keepdims=True) + jnp.exp(sink[hk] - m[i]) for i, (j, hk) in enumerate(chains)]
    o = [_dot(p[i].astype(BF16), v_all[j][:, hk * HD:(hk + 1) * HD]) / den[i] for i, (j, hk) in enumerate(chains)]
    for i, (j, hk) in enumerate(chains):
        for gi in range(g):
            hq = hk * g + gi
            o_ref[j * A_BLOCK:(j + 1) * A_BLOCK, hq * HD:(hq + 1) * HD] = (
                o[i][gi * A_BLOCK:(gi + 1) * A_BLOCK].astype(BF16))


def _attn_a(ab_lat, ab_ctx, sink, nb, s, lc):
    nblk = s // A_BLOCK
    steps = nblk // A_STEP_BLOCKS

    def kv(col, d):
        return pl.BlockSpec(
            (A_BLOCK, 128), lambda b, i: (b * nblk + jnp.clip(i * A_STEP_BLOCKS + d, 0, nblk - 1), col))

    tq = A_STEP_BLOCKS * A_BLOCK
    return pl.pallas_call(
        functools.partial(_attn_a_kernel, nblk=nblk),
        grid=(nb, steps),
        in_specs=[pl.BlockSpec((tq, 256), lambda b, i: (b * steps + i, 0)),
                  kv(2, -1), kv(2, 0), kv(2, 1), kv(2, 2), kv(3, -1), kv(3, 0), kv(3, 1), kv(3, 2),
                  pl.BlockSpec((lc, 128), lambda b, i: (b, 2)),
                  pl.BlockSpec((lc, 128), lambda b, i: (b, 3)),
                  pl.BlockSpec((1, A_HEADS), lambda b, i: (0, 0))],
        out_specs=pl.BlockSpec((tq, 256), lambda b, i: (b * steps + i, 0)),
        out_shape=jax.ShapeDtypeStruct((nb * s, 256), BF16),
        compiler_params=_cp("parallel", "parallel"),
        name="attn_a",
    )(ab_lat, *([ab_lat] * 8), ab_ctx, ab_ctx, sink.reshape(1, A_HEADS))


def _attn_b_kernel(q_ref, k_ref, v_ref, kc_ref, vc_ref, bias_ref, o_ref, *, rows):
    nwin = NA_ROWS * GRID_W
    kc = kc_ref[...]
    vc = vc_ref[...]
    q, kw, vw, tab = [], [], [], []
    for rr in range(NA_STEP_ROWS):
        r = pl.program_id(1) * NA_STEP_ROWS + rr
        rs = jnp.clip(r - NA_ROWS // 2, 0, rows - NA_ROWS)
        start = pl.multiple_of(rs * GRID_W, GRID_W)
        tab.append(rs - r + NA_ROWS - 1)
        q.append(q_ref[rr * GRID_W:(rr + 1) * GRID_W, :])
        kw.append(k_ref[pl.ds(start, nwin), :])
        vw.append(v_ref[pl.ds(start, nwin), :])
    chains = [(rr, h) for rr in range(NA_STEP_ROWS) for h in range(B_HEADS)]

    def hs(x, h):
        return x[:, h * HD:(h + 1) * HD]

    s_loc = [_dot_nt(hs(q[rr], h), hs(kw[rr], h)) + bias_ref[tab[rr], h] for rr, h in chains]
    s_ctx = [_dot_nt(hs(q[rr], h), hs(kc, h)) for rr, h in chains]
    m = [jnp.maximum(jnp.max(a, axis=-1, keepdims=True), jnp.max(b, axis=-1, keepdims=True))
         for a, b in zip(s_loc, s_ctx)]
    p_loc = [jnp.exp(a - mm) for a, mm in zip(s_loc, m)]
    p_ctx = [jnp.exp(b - mm) for b, mm in zip(s_ctx, m)]
    den = [jnp.sum(a, axis=-1, keepdims=True) + jnp.sum(b, axis=-1, keepdims=True) for a, b in zip(p_loc, p_ctx)]
    o = [(_dot(p_loc[i].astype(BF16), hs(vw[rr], h)) + _dot(p_ctx[i].astype(BF16), hs(vc, h))) / den[i]
         for i, (rr, h) in enumerate(chains)]
    for i, (rr, h) in enumerate(chains):
        o_ref[rr * GRID_W:(rr + 1) * GRID_W, h * HD:(h + 1) * HD] = o[i].astype(BF16)


def _na_bias_tables(rpb, rows):
    kr = NA_ROWS
    qc = np.arange(GRID_W)
    win_start = np.clip(qc - NA_COLS // 2, 0, GRID_W - NA_COLS)
    kcol = np.arange(GRID_W)
    valid = (kcol[None, :] >= win_start[:, None]) & (kcol[None, :] < win_start[:, None] + NA_COLS)
    pad = GRID_W - NA_COLS
    rp = jnp.pad(rpb.astype(F32), ((0, 0), (0, 0), (pad, pad)))
    toe = jnp.stack([rp[:, :, GRID_W - 1 - q:2 * GRID_W - 1 - q] for q in range(GRID_W)], axis=2)
    toe = jnp.where(jnp.asarray(valid[None, None]), toe, NEG)
    tabs = [toe[:, off:off + kr].transpose(0, 2, 1, 3).reshape(rpb.shape[0], GRID_W, kr * GRID_W)
            for off in range(kr)]
    return jnp.stack(tabs, axis=0)


def _attn_b(ab_lat, ab_ctx, bias_tabs, nb, s, lc):
    rows = s // GRID_W
    steps = rows // NA_STEP_ROWS
    tq = NA_STEP_ROWS * GRID_W
    return pl.pallas_call(
        functools.partial(_attn_b_kernel, rows=rows),
        grid=(nb, steps),
        in_specs=[pl.BlockSpec((tq, 256), lambda b, r: (b * steps + r, 2)),
                  pl.BlockSpec((s, 256), lambda b, r: (b, 3)),
                  pl.BlockSpec((s, 256), lambda b, r: (b, 4)),
                  pl.BlockSpec((lc, 256), lambda b, r: (b, 3)),
                  pl.BlockSpec((lc, 256), lambda b, r: (b, 4)),
                  pl.BlockSpec((NA_ROWS, B_HEADS, GRID_W, NA_ROWS * GRID_W), lambda b, r: (0, 0, 0, 0))],
        out_specs=pl.BlockSpec((tq, 256), lambda b, r: (b * steps + r, 0)),
        out_shape=jax.ShapeDtypeStruct((nb * s, 256), BF16),
        compiler_params=_cp("parallel", "arbitrary"),
        name="attn_b",
    )(ab_lat, ab_lat, ab_lat, ab_ctx, ab_ctx, bias_tabs)


def _attn_ctx_kernel(ab_ref, sink_ref, oa_ref, ob_ref):
    lc = ab_ref.shape[0]
    g = A_HEADS // A_KV
    for hk in range(A_KV):
        qg = jnp.concatenate([ab_ref[:, (hk * g + j) * HD:(hk * g + j + 1) * HD] for j in range(g)], axis=0)
        s = _dot_nt(qg, ab_ref[:, 256 + hk * HD:256 + (hk + 1) * HD])
        sink = jnp.concatenate(
            [jnp.broadcast_to(sink_ref[:, hk * g + j:hk * g + j + 1], (lc, 1)) for j in range(g)], axis=0)
        o = _softmax_pv(s, ab_ref[:, 384 + hk * HD:384 + (hk + 1) * HD], sink)
        for j in range(g):
            hq = hk * g + j
            oa_ref[:, hq * HD:(hq + 1) * HD] = o[j * lc:(j + 1) * lc].astype(BF16)
    for h in range(B_HEADS):
        s = _dot_nt(ab_ref[:, 512 + h * HD:512 + (h + 1) * HD], ab_ref[:, 768 + h * HD:768 + (h + 1) * HD])
        o = _softmax_pv(s, ab_ref[:, 1024 + h * HD:1024 + (h + 1) * HD])
        ob_ref[:, h * HD:(h + 1) * HD] = o.astype(BF16)


def _attn_ctx(ab_ctx, sink, nb, lc):
    return pl.pallas_call(
        _attn_ctx_kernel,
        grid=(nb,),
        in_specs=[pl.BlockSpec((lc, AB_W), lambda b: (b, 0)),
                  pl.BlockSpec((1, A_HEADS), lambda b: (0, 0))],
        out_specs=[pl.BlockSpec((lc, 256), lambda b: (b, 0)),
                   pl.BlockSpec((lc, 256), lambda b: (b, 0))],
        out_shape=[jax.ShapeDtypeStruct((nb * lc, 256), BF16),
                   jax.ShapeDtypeStruct((nb * lc, 256), BF16)],
        compiler_params=_cp("parallel"),
        name="attn_ctx",
    )(ab_ctx, sink.reshape(1, A_HEADS))


def _r7prep_kernel(x_ref, prev_ref, next_ref, cw_ref, kkw_ref, ka_ref, rk_ref, w0_ref, w2_ref, a0_ref, a2_ref,
                   g2_ref, bd_ref,
                   r_ref, v_ref, kk_ref, lw_ref, beta_ref, kd_ref, g_ref, bonus_ref, *, nt):
    i = pl.program_id(1)
    tt = x_ref.shape[0]
    x = x_ref[...]
    row = lax.broadcasted_iota(jnp.int32, x.shape, 0)
    before = jnp.where(i > 0, prev_ref[7:8, :], 0.0)
    after = jnp.where(i < nt - 1, next_ref[0:1, :], 0.0)
    x_m1 = jnp.where(row == 0, before, pltpu.roll(x, 1, 0))
    x_p1 = jnp.where(row == tt - 1, after, pltpu.roll(x, tt - 1, 0))
    xc = x_m1 * cw_ref[0:1, :] + x * cw_ref[1:2, :] + x_p1 * cw_ref[2:3, :]
    r = xc[:, 0:512]
    k = xc[:, 512:1024]
    v = xc[:, 1024:1536]
    wd = xc[:, 1536:1664]
    ad = xc[:, 1664:1792]
    gd = xc[:, 1792:1920]
    bd = bd_ref[...]
    kkh = k * kkw_ref[...]
    kk = kkh / jnp.maximum(jnp.sqrt(_segsum(kkh * kkh, bd)), 1e-12)
    zw = w0_ref[...] + _dot3(jnp.tanh(wd), w2_ref[...])
    za = a0_ref[...] + _dot3(ad, a2_ref[...])
    a = jax.nn.sigmoid(za)
    ka = ka_ref[...]
    kd0 = k * (1.0 + (a[:, 0:512] - 1.0) * ka)
    kd1 = k * (1.0 + (a[:, 512:1024] - 1.0) * ka)
    r_ref[...] = r.astype(BF16)
    v_ref[...] = v.astype(BF16)
    kk_ref[...] = kk.astype(BF16)
    lw_ref[...] = -DECAY_SCALE * jax.nn.sigmoid(zw)
    beta_ref[:, 0:512] = (kk * a[:, 0:512]).astype(BF16)
    beta_ref[:, 512:1024] = (kk * a[:, 512:1024]).astype(BF16)
    kd_ref[:, 0:512] = kd0.astype(BF16)
    kd_ref[:, 512:1024] = kd1.astype(BF16)
    g_ref[...] = _dot3(jax.nn.sigmoid(gd), g2_ref[...])
    bonus_ref[...] = _segsum(r * (kd0 + kd1) * rk_ref[...], bd) * v


def _bd2(w):
    z = jnp.zeros_like(w[0])
    return jnp.concatenate([jnp.concatenate([w[0], z], axis=1), jnp.concatenate([z, w[1]], axis=1)], axis=0)


def _r7prep(pc, lp, nb, t):
    tt = _row_tile(t)
    nt = t // tt
    nrow8 = t // 8

    def full(shape):
        return pl.BlockSpec(shape, lambda b, i: (0,) * len(shape))

    def rowspec(w):
        return pl.BlockSpec((tt, w), lambda b, i: (b * nt + i, 0))

    outs = [C_W, C_W, C_W, 2 * C_W, 2 * C_W, 2 * C_W, C_W, C_W]
    dtypes = [BF16, BF16, BF16, F32, BF16, BF16, F32, F32]
    return pl.pallas_call(
        functools.partial(_r7prep_kernel, nt=nt),
        grid=(nb, nt),
        in_specs=[rowspec(C_IN),
                  pl.BlockSpec((8, C_IN), lambda b, i: (b * nrow8 + jnp.maximum(i * (tt // 8) - 1, 0), 0)),
                  pl.BlockSpec((8, C_IN), lambda b, i: (b * nrow8 + jnp.minimum((i + 1) * (tt // 8), nrow8 - 1), 0)),
                  full((3, C_IN)), full((1, C_W)), full((1, C_W)), full((1, C_W)),
                  full((1, 2 * C_W)), full((128, 2 * C_W)), full((1, 2 * C_W)), full((128, 2 * C_W)),
                  full((128, C_W)), full((SEG_W, SEG_W))],
        out_specs=[rowspec(w) for w in outs],
        out_shape=[jax.ShapeDtypeStruct((nb * t, w), dt) for w, dt in zip(outs, dtypes)],
        compiler_params=_cp("parallel", "parallel"),
        name="r7prep",
    )(pc, pc, pc, lp['r7_conv'], lp['r7_kk'].reshape(1, C_W), lp['r7_ka'].reshape(1, C_W),
      lp['r7_rk'].reshape(1, C_W), lp['r7_w0'].reshape(1, 2 * C_W), _bd2(lp['r7_w2']),
      lp['r7_a0'].reshape(1, 2 * C_W), _bd2(lp['r7_a2']), lp['r7_g2'], _block_diag_ones())


def _cumsum_rows(tri, x):
    t = tri.astype(BF16)
    hi = x.astype(BF16)
    rest = x - hi.astype(F32)
    mid = rest.astype(BF16)
    lo = (rest - mid.astype(F32)).astype(BF16)
    return _dot(t, hi) + (_dot(t, mid) + _dot(t, lo))


def _mm(a, b):
    return _dot(a.astype(BF16), b.astype(BF16))


def _mm_nt(a, b):
    return _dot_nt(a.astype(BF16), b.astype(BF16))


def _scan_kernel(rf_ref, vf_ref, kkf_ref, lwf_ref, betaf_ref, kdf_ref,
                 rb_ref, vb_ref, kkb_ref, lwb_ref, betab_ref, kdb_ref, s0_ref,
                 yf_ref, yb_ref, sf_ref, st_ref, *, nc):
    c = pl.program_id(1)
    cl = SCAN_C

    @pl.when(c == 0)
    def _():
        st_ref[...] = s0_ref[...]

    ti = lax.broadcasted_iota(jnp.int32, (cl, cl), 0)
    si = lax.broadcasted_iota(jnp.int32, (cl, cl), 1)
    eye = (ti == si).astype(F32)
    incl = [si <= ti, si >= ti]
    strict = [si < ti, si > ti]

    def hs(x, h):
        return x[:, h * HD:(h + 1) * HD]

    pre = {}
    for bb in range(SCAN_NB):
        for d, (r_ref, v_ref, kk_ref, lw_ref, beta_ref, kd_ref) in enumerate(
                [(rf_ref, vf_ref, kkf_ref, lwf_ref, betaf_ref, kdf_ref),
                 (rb_ref, vb_ref, kkb_ref, lwb_ref, betab_ref, kdb_ref)]):
            lw = lw_ref[bb]
            cum = _cumsum_rows(incl[d], lw)
            e_neg = jnp.exp(-cum)
            pre[bb, d] = dict(r_hat=r_ref[bb] * jnp.exp(cum), a_hat=-kk_ref[bb] * jnp.exp(cum - lw),
                              b_til=beta_ref[bb] * e_neg, k_til=kd_ref[bb] * e_neg,
                              wtot=jnp.exp(jnp.sum(lw, axis=0, keepdims=True)), vv=v_ref[bb])

    chains = [(bb, d, h) for bb in range(SCAN_NB) for d in range(2) for h in range(C_HEADS)]
    n_ch = range(len(chains))

    def part(name, i):
        bb, d, h = chains[i]
        return hs(pre[bb, d][name], h)

    def dirn(i):
        return chains[i][1]

    ar = [jnp.concatenate([part('a_hat', i), part('r_hat', i)], axis=0) for i in n_ch]
    bk = [jnp.concatenate([part('b_til', i), part('k_til', i)], axis=0) for i in n_ch]
    m = [_mm_nt(ar[i], bk[i]) for i in n_ch]
    l_k = [jnp.where(strict[dirn(i)], m[i][0:cl, cl:2 * cl], 0.0) for i in n_ch]
    r_b = [jnp.where(incl[dirn(i)], m[i][cl:2 * cl, 0:cl], 0.0) for i in n_ch]
    r_k = [jnp.where(incl[dirn(i)], m[i][cl:2 * cl, cl:2 * cl], 0.0) for i in n_ch]
    l_mat = [jnp.where(strict[dirn(i)], m[i][0:cl, 0:cl], 0.0) for i in n_ch]
    halves_of = lambda b: ((ti // (2 * b)) == (si // (2 * b))) & ((ti // b) != (si // b))
    t_inv = [eye + jnp.where(halves_of(1), l_mat[i], 0.0) for i in n_ch]
    b = 2
    while b < cl:
        couple = halves_of(b)
        nt_ = [_mm(jnp.where(couple, l_mat[i], 0.0), t_inv[i]) for i in n_ch]
        t_inv = [t_inv[i] + _mm(t_inv[i], nt_[i]) for i in n_ch]
        b *= 2
    x1 = [_mm(l_k[i], part('vv', i)) for i in n_ch]
    y0 = [_mm(r_k[i], part('vv', i)) for i in n_ch]
    ua = [_mm(t_inv[i], jnp.concatenate([x1[i], part('a_hat', i)], axis=1)) for i in n_ch]
    s0 = [st_ref[d, bb, h] for bb, d, h in chains]
    as0 = [_mm_nt(jnp.concatenate([ua[i][:, HD:2 * HD], part('r_hat', i)], axis=0), s0[i])
           for i in n_ch]
    u = [ua[i][:, 0:HD] + as0[i][0:cl] for i in n_ch]
    y = [y0[i] + as0[i][cl:2 * cl] + _mm(r_b[i], u[i]) for i in n_ch]
    for i in n_ch:
        bb, d, h = chains[i]
        (yf_ref, yb_ref)[d][bb, :, h * HD:(h + 1) * HD] = y[i]
    for i in n_ch:
        bb, d, h = chains[i]
        uv_t = jnp.concatenate([u[i], part('vv', i)], axis=0).T
        st_ref[d, bb, h] = (s0[i] + _mm(uv_t, bk[i])) * part('wtot', i)

    @pl.when(c == nc - 1)
    def _():
        sf_ref[...] = st_ref[...]


def _scan(prep, s0, nb, t):
    nc = t // SCAN_C
    assert nb % SCAN_NB == 0
    r, v, kk, lw, beta, kd = [a.reshape(nb, t, a.shape[-1]) for a in prep]

    def chunk(d, c):
        return nc - 1 - c if d else c

    def specs(d):
        shared = pl.BlockSpec((SCAN_NB, SCAN_C, C_W), lambda b, c: (b, chunk(d, c), 0))
        perdir = pl.BlockSpec((SCAN_NB, SCAN_C, C_W), lambda b, c: (b, chunk(d, c), d))
        return [shared, shared, shared, perdir, perdir, perdir]

    st_spec = pl.BlockSpec((2, SCAN_NB, C_HEADS, HD, HD), lambda b, c: (0, b, 0, 0, 0))
    yf, yb, sf = pl.pallas_call(
        functools.partial(_scan_kernel, nc=nc),
        grid=(nb // SCAN_NB, nc),
        in_specs=specs(0) + specs(1) + [st_spec],
        out_specs=[pl.BlockSpec((SCAN_NB, SCAN_C, C_W), lambda b, c: (b, chunk(0, c), 0)),
                   pl.BlockSpec((SCAN_NB, SCAN_C, C_W), lambda b, c: (b, chunk(1, c), 0)), st_spec],
        out_shape=[jax.ShapeDtypeStruct((nb, t, C_W), F32), jax.ShapeDtypeStruct((nb, t, C_W), F32),
                   jax.ShapeDtypeStruct((2, nb, C_HEADS, HD, HD), F32)],
        scratch_shapes=[pltpu.VMEM((2, SCAN_NB, C_HEADS, HD, HD), F32)],
        compiler_params=_cp("parallel", "arbitrary"),
        name="r7scan",
    )(r, v, kk, lw, beta, kd, r, v, kk, lw, beta, kd, s0)
    return (yf.reshape(nb * t, C_W), yb.reshape(nb * t, C_W)), sf


def _outproj_kernel(x_ref, oa_ref, ob_ref, y0_ref, y1_ref, bonus_ref, g_ref, lnw_ref, lnb_ref, bd_ref, w_ref,
                    gate_ref, o_ref):
    bd = bd_ref[...]
    y = y0_ref[...] + y1_ref[...]
    mu = _segsum(y, bd) * (1.0 / HD)
    yc = y - mu
    var = _segsum(yc * yc, bd) * (1.0 / HD)
    yn = yc * lax.rsqrt(var + GN_EPS) * lnw_ref[...] + lnb_ref[...]
    oc = ((yn + bonus_ref[...]) * g_ref[...]).astype(BF16)
    acc = (_dot(oa_ref[...], w_ref[0:256, :]) + _dot(ob_ref[...], w_ref[256:512, :])
           + _dot(oc, w_ref[512:1024, :]))
    o_ref[...] = x_ref[...] + gate_ref[...] * acc


def _outproj(x2, oa, ob, y, bonus, g, lp, w_out_bf, mod3, mod_row_of_batch, nb, t):
    tm = _row_tile(t)
    nt = t // tm

    def rowspec(w):
        return pl.BlockSpec((tm, w), lambda b, i: (b * nt + i, 0))

    def full(shape):
        return pl.BlockSpec(shape, lambda b, i: (0,) * len(shape))

    return pl.pallas_call(
        _outproj_kernel,
        grid=(nb, nt),
        in_specs=[rowspec(D), rowspec(256), rowspec(256),
                  rowspec(C_W), rowspec(C_W),
                  rowspec(C_W), rowspec(C_W), full((1, C_W)), full((1, C_W)), full((SEG_W, SEG_W)), full((D, D)),
                  pl.BlockSpec((None, 1, D), lambda b, i: (mod_row_of_batch(b), 0, 2))],
        out_specs=rowspec(D),
        out_shape=jax.ShapeDtypeStruct((nb * t, D), F32),
        compiler_params=_cp("parallel", "parallel"),
        name="outproj",
    )(x2, oa, ob, y[0], y[1], bonus, g, lp['r7_lnw'].reshape(1, C_W), lp['r7_lnb'].reshape(1, C_W),
      _block_diag_ones(), w_out_bf, mod3)


def _peer_score_kernel(x_ref, g_ref, sc_ref, sh_ref, wq_ref, keys_ref, h_ref, st_ref):
    x = x_ref[...]
    y = x * lax.rsqrt(jnp.mean(x * x, axis=-1, keepdims=True) + EPS) * g_ref[...]
    h = (y * (1.0 + sc_ref[...]) + sh_ref[...]).astype(BF16)
    h_ref[...] = h
    q = _dot(h, wq_ref[...])
    half = P_QDIM // 2
    for hp in range(2 * P_HEADS):
        st_ref[hp] = _dot3(keys_ref[hp], q[:, hp * half:(hp + 1) * half], nt=True)


def _peer_scores(x2, norm_g, mod3, mod_row_of_batch, wq_bf, keys, nb, t):
    tm = _row_tile(t)
    nt = t // tm
    n = nb * t
    return pl.pallas_call(
        _peer_score_kernel,
        grid=(nb, nt),
        in_specs=[pl.BlockSpec((tm, D), lambda b, i: (b * nt + i, 0)),
                  pl.BlockSpec((1, D), lambda b, i: (0, 0)),
                  pl.BlockSpec((None, 1, D), lambda b, i: (mod_row_of_batch(b), 0, 4)),
                  pl.BlockSpec((None, 1, D), lambda b, i: (mod_row_of_batch(b), 0, 3)),
                  pl.BlockSpec((D, P_HEADS * P_QDIM), lambda b, i: (0, 0)),
                  pl.BlockSpec((2 * P_HEADS, P_NKEYS, P_QDIM // 2), lambda b, i: (0, 0, 0))],
        out_specs=[pl.BlockSpec((tm, D), lambda b, i: (b * nt + i, 0)),
                   pl.BlockSpec((2 * P_HEADS, P_NKEYS, tm), lambda b, i: (0, 0, b * nt + i))],
        out_shape=[jax.ShapeDtypeStruct((n, D), BF16),
                   jax.ShapeDtypeStruct((2 * P_HEADS, P_NKEYS, n), F32)],
        compiler_params=_cp("parallel", "parallel"),
        name="peer_scores",
    )(x2, norm_g.reshape(1, D), mod3, mod3, wq_bf, keys.reshape(2 * P_HEADS, P_NKEYS, P_QDIM // 2))


def _batcher_pairs(n):
    pairs = []
    p = 1
    while p < n:
        k = p
        while k >= 1:
            for j in range(k % p, n - k, 2 * k):
                for i in range(min(k, n - j - k)):
                    if (i + j) // (2 * p) == (i + j + k) // (2 * p):
                        pairs.append((i + j, i + j + k))
            k //= 2
        p *= 2
    return pairs


_SORT16 = _batcher_pairs(P_TOPK)
_BITONIC16 = [(i, i + d) for d in (8, 4, 2, 1) for i in range(P_TOPK) if i & d == 0]


def _compare_exchange(x, pairs):
    for i, j in pairs:
        x[i], x[j] = jnp.maximum(x[i], x[j]), jnp.minimum(x[i], x[j])
    return x


def _top16_sorted(slabs):
    x = _compare_exchange(list(slabs), _SORT16)
    for shift in (4, 2, 1):
        y = [pltpu.roll(x[P_TOPK - 1 - k], shift, 0) for k in range(P_TOPK)]
        x = [jnp.maximum(x[k], y[k]) for k in range(P_TOPK)]
        x = _compare_exchange(x, _BITONIC16)
    return x


def _peer_gate_tables(st_ref, rk_ref, be_ref, cnt_ref, al_ref, top_ref):
    nblk = st_ref.shape[-1] // LANES
    sub = lax.broadcasted_iota(jnp.int32, (8, LANES), 0)
    ninf = jnp.full((8, LANES), -jnp.inf, F32)

    def block(it, carry):
        h = it // nblk
        lb = it % nblk
        lanes = pl.ds(pl.multiple_of(lb * LANES, LANES), LANES)
        hf = lb // (PEER_HALF // LANES)
        hl = pl.ds(pl.multiple_of((lb % (PEER_HALF // LANES)) * LANES, LANES), LANES)
        for p in range(2):
            top = _top16_sorted([st_ref[2 * h + p, 8 * k:8 * k + 8, lanes] for k in range(P_TOPK)])
            for k in range(P_TOPK):
                top_ref[p * P_TOPK + k:p * P_TOPK + k + 1, lanes] = top[k][0:1]
        a16 = top_ref[0:P_TOPK, lanes]
        b16 = top_ref[P_TOPK:2 * P_TOPK, lanes]
        b8 = b16[0:8]
        cand = [a16[0:1] + b8, a16[0:1] + b16[8:16]]
        for p in range(2, 9):
            cand.append(jnp.where(sub < P_TOPK // p, a16[p - 1:p] + b8, -jnp.inf))
        cand.append(a16[8:16] + b16[0:1])
        best = _top16_sorted(cand + [ninf] * (P_TOPK - len(cand)))
        tau = best[P_TOPK - 1][0:1]
        z = jnp.ones_like(tau)
        for k in range(1, P_TOPK):
            z = z + jnp.exp(best[k][0:1] - best[0][0:1])
        s1 = st_ref[2 * h, :, lanes]
        s2 = st_ref[2 * h + 1, :, lanes]
        cnt = jnp.zeros_like(s1)
        rk = jnp.ones_like(s2)
        for q in range(P_TOPK):
            bq = b16[q:q + 1]
            theta = jnp.min(jnp.where(a16 + bq >= tau, a16, jnp.inf), axis=0, keepdims=True)
            cnt = jnp.where(s1 >= theta, q + 1.0, cnt)
            rk = jnp.where(bq > s2, q + 2.0, rk)
        cnt_ref[hf, h, :, hl] = cnt
        rk_ref[hf, h, :, hl] = rk.astype(BF16)
        al_ref[hf, h, :, hl] = jnp.exp(s1 - a16[0:1]) / z
        be_ref[hf, h, :, hl] = jnp.exp(s2 - b16[0:1]).astype(BF16)
        return carry

    lax.fori_loop(0, P_HEADS * nblk, block, 0)


def _peer_chunk(chunk, act_ref, rk_ref, be_ref, cnt_ref, al_ref, w_ref):
    nrow = w_ref.shape[0] // P_NKEYS
    for ii in range(nrow):
        i = chunk * nrow + ii
        wrow = None
        for h in range(P_HEADS):
            cnt = cnt_ref[h, pl.ds(i, 1), :].astype(BF16)
            al = al_ref[h, pl.ds(i, 1), :].astype(BF16)
            term = jnp.where(rk_ref[h] <= cnt, al * be_ref[h], 0.0)
            wrow = term if wrow is None else wrow + term
        rows = slice(ii * P_NKEYS, (ii + 1) * P_NKEYS)
        act = act_ref[rows, :]
        gl = 0.5 * act * (1.0 + lax.erf(act * (2.0 ** -0.5)))
        w_ref[rows, :] = wrow * gl.astype(BF16)


def _peer_dense_kernel(x_ref, h_ref, st_ref, u_ref, vt_ref, gate_ref, o_ref,
                       rk_ref, be_ref, cnt_ref, al_ref, top_ref, act0_ref, act1_ref, w_ref, acc_ref, *, ne):
    e = pl.program_id(1)
    halves = range(w_ref.shape[0])

    def rows(hf):
        return slice(hf * PEER_HALF, (hf + 1) * PEER_HALF)

    @pl.when(e == 0)
    def _():
        for hf in halves:
            act0_ref[hf] = _dot_nt(u_ref[...], h_ref[rows(hf), :])
        _peer_gate_tables(st_ref, rk_ref, be_ref, cnt_ref, al_ref, top_ref)
        acc_ref[...] = jnp.zeros_like(acc_ref)

    def step(cur_ref, nxt_ref):
        for hf in halves:
            nxt_ref[hf] = _dot_nt(u_ref[...], h_ref[rows(hf), :])
            _peer_chunk(e - 1, cur_ref.at[hf], rk_ref.at[hf], be_ref.at[hf], cnt_ref.at[hf], al_ref.at[hf],
                        w_ref.at[hf])
            acc_ref[hf] += _dot(vt_ref[...], w_ref[hf])

    @pl.when(e % 2 == 1)
    def _():
        step(act0_ref, act1_ref)

    @pl.when((e > 0) & (e % 2 == 0))
    def _():
        step(act1_ref, act0_ref)

    @pl.when(e == ne)
    def _():
        for hf in halves:
            o_ref[rows(hf), :] = x_ref[rows(hf), :] + gate_ref[...] * acc_ref[hf].T


def _peer_dense(x2, h2, st, u_bf, vt_bf, mod3, mod_row_of_batch, nb, t):
    tm = _row_tile(t)
    nh = tm // PEER_HALF
    ec = PEER_EC
    nt = t // tm
    ne = P_EXPERTS // ec
    return pl.pallas_call(
        functools.partial(_peer_dense_kernel, ne=ne),
        grid=(nb * nt, ne + 1),
        in_specs=[pl.BlockSpec((tm, D), lambda i, e: (jnp.where(e >= ne - 1, i, jnp.maximum(i - 1, 0)), 0)),
                  pl.BlockSpec((tm, D), lambda i, e: (i, 0)),
                  pl.BlockSpec((2 * P_HEADS, P_NKEYS, tm),
                               lambda i, e: (0, 0, jnp.minimum(jnp.where(e == ne, i + 1, i), nb * nt - 1))),
                  pl.BlockSpec((ec, D), lambda i, e: (jnp.minimum(e, ne - 1), 0)),
                  pl.BlockSpec((None, D, ec), lambda i, e: ((e + ne - 1) % ne, 0, 0)),
                  pl.BlockSpec((None, 1, D), lambda i, e: (mod_row_of_batch(i // nt), 0, 5))],
        out_specs=pl.BlockSpec((tm, D), lambda i, e: (i, 0)),
        out_shape=jax.ShapeDtypeStruct((nb * t, D), F32),
        scratch_shapes=[pltpu.VMEM((nh, P_HEADS, P_NKEYS, PEER_HALF), BF16),
                        pltpu.VMEM((nh, P_HEADS, P_NKEYS, PEER_HALF), BF16),
                        pltpu.VMEM((nh, P_HEADS, P_NKEYS, PEER_HALF), F32),
                        pltpu.VMEM((nh, P_HEADS, P_NKEYS, PEER_HALF), F32),
                        pltpu.VMEM((2 * P_TOPK, tm), F32),
                        pltpu.VMEM((nh, ec, PEER_HALF), F32),
                        pltpu.VMEM((nh, ec, PEER_HALF), F32),
                        pltpu.VMEM((nh, ec, PEER_HALF), BF16),
                        pltpu.VMEM((nh, D, PEER_HALF), F32)],
        compiler_params=pltpu.CompilerParams(dimension_semantics=("parallel", "arbitrary"),
                                             vmem_limit_bytes=PEER_VMEM_LIMIT),
        name="peer_dense",
    )(x2, h2, st, u_bf, vt_bf, mod3)


def _peer(x2, norm_g, mod3, mod_row_of_batch, wq_bf, keys, u_bf, vt_bf, nb, t):
    h2, st = _peer_scores(x2, norm_g, mod3, mod_row_of_batch, wq_bf, keys, nb, t)
    return _peer_dense(x2, h2, st, u_bf, vt_bf, mod3, mod_row_of_batch, nb, t)


def kernel(x, c, ctx, c_ctx, norm_mix, norm_ffn, w_mod, b_mod, w_in, w_out, a_qnorm, a_knorm, a_sink, b_qnorm,
           b_knorm, b_rpb, r7_conv, r7_w0, r7_w2, r7_a0, r7_a2, r7_g2, r7_kk, r7_ka, r7_rk, r7_lnw, r7_lnb,
           peer_wq, peer_keys, peer_u, peer_v):
    nb, s, _ = x.shape
    lc = ctx.shape[1]
    depth = w_in.shape[0]
    assert nb < 16 and nb % SCAN_NB == 0 and s % 512 == 0 and lc % 256 == 0
    rows = s // GRID_W

    cc = jnp.zeros((16, D), F32).at[:nb].set(c).at[nb].set(c_ctx)
    mod = _modulation(cc, w_mod, b_mod)
    rope_tabs = _rope_tables(s)
    lat_row = lambda b: b
    ctx_row = lambda b: nb

    x_lat = x.reshape(nb * s, D)
    x_ctx = ctx.reshape(nb * lc, D)
    scale = HD ** -0.5
    for l in range(depth):
        with_ctx = l < depth - 1
        mod3 = mod[l].reshape(16, 1, 6 * D)
        lp = {'r7_conv': r7_conv[l], 'r7_w0': r7_w0[l], 'r7_w2': r7_w2[l], 'r7_a0': r7_a0[l], 'r7_a2': r7_a2[l],
              'r7_g2': r7_g2[l], 'r7_kk': r7_kk[l], 'r7_ka': r7_ka[l], 'r7_rk': r7_rk[l], 'r7_lnw': r7_lnw[l],
              'r7_lnb': r7_lnb[l]}
        w_in_bf = w_in[l].astype(BF16)
        w_out_bf = w_out[l].astype(BF16)
        gain_a = jnp.concatenate([jnp.tile(a_qnorm[l] * scale, A_HEADS), jnp.tile(a_knorm[l], A_KV)]).reshape(1, 384)
        gain_b = jnp.concatenate([jnp.tile(b_qnorm[l] * scale, B_HEADS), jnp.tile(b_knorm[l], B_HEADS)]).reshape(1, 512)

        ab_lat, pc_lat = _inproj(x_lat, mod3, lat_row, norm_mix[l], w_in_bf, gain_a, gain_b, rope_tabs, nb, s)
        ab_ctx, pc_ctx = _inproj(x_ctx, mod3, ctx_row, norm_mix[l], w_in_bf, gain_a, gain_b, None, 1, nb * lc)

        o_a = _attn_a(ab_lat, ab_ctx, a_sink[l], nb, s, lc)
        o_b = _attn_b(ab_lat, ab_ctx, _na_bias_tables(b_rpb[l], rows), nb, s, lc)

        prep_ctx = _r7prep(pc_ctx, lp, nb, lc)
        prep_lat = _r7prep(pc_lat, lp, nb, s)
        zero_state = jnp.zeros((2, nb, C_HEADS, HD, HD), F32)
        y_ctx, s_ctx = _scan(prep_ctx[:6], zero_state, nb, lc)
        y_lat, _ = _scan(prep_lat[:6], s_ctx, nb, s)

        x_lat = _outproj(x_lat, o_a, o_b, y_lat, prep_lat[7], prep_lat[6], lp, w_out_bf, mod3, lat_row, nb, s)
        wq_bf = peer_wq[l].astype(BF16)
        u_bf = peer_u[l].astype(BF16)
        vt_bf = peer_v[l].astype(BF16).reshape(P_EXPERTS // PEER_EC, PEER_EC, D).transpose(0, 2, 1)
        x_lat = _peer(x_lat, norm_ffn[l], mod3, lat_row, wq_bf, peer_keys[l], u_bf, vt_bf, nb, s)
        if with_ctx:
            o_ac, o_bc = _attn_ctx(ab_ctx, a_sink[l], nb, lc)
            x_ctx = _outproj(x_ctx, o_ac, o_bc, y_ctx, prep_ctx[7], prep_ctx[6], lp, w_out_bf, mod3, ctx_row,
                             1, nb * lc)
            x_ctx = _peer(x_ctx, norm_ffn[l], mod3, ctx_row, wq_bf, peer_keys[l], u_bf, vt_bf, 1, nb * lc)
    return x_lat.reshape(nb, s, D)
```

```python
import functools

import numpy as np
import jax
import jax.numpy as jnp
from jax import lax
from jax.experimental import pallas as pl
from jax.experimental.pallas import tpu as pltpu

F32 = jnp.float32
BF16 = jnp.bfloat16
HI = lax.Precision.HIGHEST

D = 1024
GRID_W = 64
HD = 64
LANES = 128
EPS = 1e-6
NEG = -1e30
A_HEADS, A_KV, A_BLOCK, A_WINDOW = 4, 2, 128, 128
A_STEP_BLOCKS = 4
B_HEADS, NA_ROWS, NA_COLS = 4, 8, 16
NA_STEP_ROWS = 16
C_HEADS = 8
C_W = 512
C_IN = 1920
AB_W = 1280
IN_W = AB_W + C_IN
GN_EPS = 64e-5
DECAY_SCALE = 0.6065306597126334
ROPE_BASE = 10000.0
P_HEADS, P_NKEYS, P_QDIM, P_TOPK = 8, 128, 256, 16
P_EXPERTS = P_NKEYS * P_NKEYS
SCAN_C = 64
SCAN_NB = 2
VMEM_LIMIT = 48 * 1024 * 1024
PEER_VMEM_LIMIT = 60 * 1024 * 1024
PEER_EC = 2048
PEER_HALF = 256


def _row_tile(t):
    return 512 if t % 512 == 0 else 256


def _cp(*sem):
    return pltpu.CompilerParams(dimension_semantics=sem, vmem_limit_bytes=VMEM_LIMIT)


def _dot(a, b, prec=None):
    return jnp.dot(a, b, precision=prec, preferred_element_type=F32)


def _dot_nt(a, b, prec=None):
    return lax.dot_general(a, b, (((1,), (1,)), ((), ())), precision=prec, preferred_element_type=F32)


SEG_W = 256


def _block_diag_ones():
    i = np.arange(SEG_W) // HD
    return jnp.asarray((i[:, None] == i[None, :]).astype(np.float32), BF16)


def _split2(x):
    hi = x.astype(BF16)
    return hi, (x - hi.astype(F32)).astype(BF16)


def _segsum(x, bd):
    hi, lo = _split2(x)
    outs = []
    for g0 in range(0, x.shape[1], SEG_W):
        w = min(SEG_W, x.shape[1] - g0)
        outs.append(_dot(hi[:, g0:g0 + w], bd[0:w, 0:w]) + _dot(lo[:, g0:g0 + w], bd[0:w, 0:w]))
    return outs[0] if len(outs) == 1 else jnp.concatenate(outs, axis=1)


def _dot3(a, b, nt=False):
    f = _dot_nt if nt else _dot
    a_hi, a_lo = _split2(a)
    b_hi, b_lo = _split2(b)
    return f(a_hi, b_hi) + (f(a_lo, b_hi) + f(a_hi, b_lo))


def _mod_kernel(c_ref, w_ref, b_ref, o_ref):
    c = c_ref[...]
    s = c * jax.nn.sigmoid(c)
    o_ref[...] = _dot(s, w_ref[...], HI) + b_ref[...]


def _modulation(cc, w_mod, b_mod):
    L, _, n = w_mod.shape
    tn = 2048
    return pl.pallas_call(
        _mod_kernel,
        grid=(L, n // tn),
        in_specs=[pl.BlockSpec((16, D), lambda l, j: (0, 0)),
                  pl.BlockSpec((None, D, tn), lambda l, j: (l, 0, j)),
                  pl.BlockSpec((None, 1, tn), lambda l, j: (l, 0, j))],
        out_specs=pl.BlockSpec((None, 16, tn), lambda l, j: (l, 0, j)),
        out_shape=jax.ShapeDtypeStruct((L, 16, n), F32),
        compiler_params=_cp("parallel", "parallel"),
        name="modulation",
    )(cc, w_mod, b_mod.reshape(L, 1, n))


def _swap16(x):
    n = x.shape[-1]
    lane = lax.broadcasted_iota(jnp.int32, x.shape, 1)
    fwd = pltpu.roll(x, n - 16, 1)
    bwd = pltpu.roll(x, 16, 1)
    return jnp.where((lane % 32) < 16, fwd, bwd)


def _head_rms(x, bd, gain):
    ss = _segsum(x * x, bd)
    return x * lax.rsqrt(ss * (1.0 / HD) + EPS) * gain


def _inproj_kernel(*refs, rope):
    if rope:
        (x_ref, g_ref, sc_ref, sh_ref, w_ref, ga_ref, gb_ref, bd_ref, cos_ref, sin_ref,
         ab_ref, c_ref) = refs
    else:
        (x_ref, g_ref, sc_ref, sh_ref, w_ref, ga_ref, gb_ref, bd_ref, ab_ref, c_ref) = refs
    x = x_ref[...]
    y = x * lax.rsqrt(jnp.mean(x * x, axis=-1, keepdims=True) + EPS) * g_ref[...]
    h = y * (1.0 + sc_ref[...]) + sh_ref[...]
    acc = _dot(h.astype(BF16), w_ref[...])
    bd = bd_ref[...]
    qa = _head_rms(acc[:, 0:384], bd, ga_ref[...])
    if rope:
        qa = qa * cos_ref[...] + _swap16(qa) * sin_ref[...]
    qb = _head_rms(acc[:, 512:1024], bd, gb_ref[...])
    ab_ref[:, 0:384] = qa.astype(BF16)
    ab_ref[:, 384:512] = acc[:, 384:512].astype(BF16)
    ab_ref[:, 512:1024] = qb.astype(BF16)
    ab_ref[:, 1024:1280] = acc[:, 1024:1280].astype(BF16)
    c_ref[...] = acc[:, AB_W:IN_W]


def _inproj(x2, mod3, mod_row_of_batch, norm_g, w_in_bf, gain_a, gain_b, rope_tabs, nb, t):
    tm = _row_tile(t)
    nt = t // tm
    rope = rope_tabs is not None
    in_specs = [
        pl.BlockSpec((tm, D), lambda b, i: (b * nt + i, 0)),
        pl.BlockSpec((1, D), lambda b, i: (0, 0)),
        pl.BlockSpec((None, 1, D), lambda b, i: (mod_row_of_batch(b), 0, 1)),
        pl.BlockSpec((None, 1, D), lambda b, i: (mod_row_of_batch(b), 0, 0)),
        pl.BlockSpec((D, IN_W), lambda b, i: (0, 0)),
        pl.BlockSpec((1, 384), lambda b, i: (0, 0)),
        pl.BlockSpec((1, 512), lambda b, i: (0, 0)),
        pl.BlockSpec((SEG_W, SEG_W), lambda b, i: (0, 0)),
    ]
    args = [x2, norm_g.reshape(1, D), mod3, mod3, w_in_bf, gain_a, gain_b,
            _block_diag_ones()]
    if rope:
        in_specs += [pl.BlockSpec((tm, 384), lambda b, i: (i, 0)),
                     pl.BlockSpec((tm, 384), lambda b, i: (i, 0))]
        args += list(rope_tabs)
    return pl.pallas_call(
        functools.partial(_inproj_kernel, rope=rope),
        grid=(nb, nt),
        in_specs=in_specs,
        out_specs=[pl.BlockSpec((tm, AB_W), lambda b, i: (b * nt + i, 0)),
                   pl.BlockSpec((tm, C_IN), lambda b, i: (b * nt + i, 0))],
        out_shape=[jax.ShapeDtypeStruct((nb * t, AB_W), BF16),
                   jax.ShapeDtypeStruct((nb * t, C_IN), F32)],
        compiler_params=_cp("parallel", "parallel"),
        name="inproj_rope" if rope else "inproj_ctx",
    )(*args)


def _rope_tables(s):
    tok = np.arange(s)
    inv = ROPE_BASE ** (-np.arange(0, 32, 2) / 32.0)
    ar = (tok // GRID_W)[:, None] * inv[None]
    ac = (tok % GRID_W)[:, None] * inv[None]
    cos = np.concatenate([np.cos(ar), np.cos(ar), np.cos(ac), np.cos(ac)], axis=1)
    sin = np.concatenate([-np.sin(ar), np.sin(ar), -np.sin(ac), np.sin(ac)], axis=1)
    return (jnp.asarray(np.tile(cos, (1, 6)), F32), jnp.asarray(np.tile(sin, (1, 6)), F32))


def _softmax_pv(s, v, sink=None):
    m = jnp.max(s, axis=-1, keepdims=True)
    if sink is not None:
        m = jnp.maximum(m, sink)
    p = jnp.exp(s - m)
    den = jnp.sum(p, axis=-1, keepdims=True)
    if sink is not None:
        den = den + jnp.exp(sink - m)
    return _dot(p.astype(BF16), v) / den


def _attn_a_kernel(*refs, nblk):
    nkb = A_STEP_BLOCKS + 2
    q_ref = refs[0]
    k_refs, v_refs = refs[1:1 + nkb], refs[1 + nkb:1 + 2 * nkb]
    kc_ref, vc_ref, sink_ref, o_ref = refs[1 + 2 * nkb:]
    kb = [r[...] for r in k_refs]
    vb = [r[...] for r in v_refs]
    kc = kc_ref[...]
    vc = vc_ref[...]
    nk = 3 * A_BLOCK + kc.shape[0]
    g = A_HEADS // A_KV
    row = lax.broadcasted_iota(jnp.int32, (g * A_BLOCK, nk), 0) % A_BLOCK
    col = lax.broadcasted_iota(jnp.int32, (g * A_BLOCK, nk), 1)
    rel = col - A_BLOCK - row
    band = (jnp.abs(rel) <= A_WINDOW) | (col >= 3 * A_BLOCK)
    chains, ok, k_all, v_all = [], [], [], []
    for j in range(A_STEP_BLOCKS):
        n = pl.program_id(1) * A_STEP_BLOCKS + j
        ok.append(band & ((n > 0) | (col >= A_BLOCK)) & ((n < nblk - 1) | (col < 2 * A_BLOCK) | (col >= 3 * A_BLOCK)))
        k_all.append(jnp.concatenate(kb[j:j + 3] + [kc], axis=0))
        v_all.append(jnp.concatenate(vb[j:j + 3] + [vc], axis=0))
        chains += [(j, hk) for hk in range(A_KV)]

    def qg(j, hk):
        rows = slice(j * A_BLOCK, (j + 1) * A_BLOCK)
        return jnp.concatenate([q_ref[rows, (hk * g + i) * HD:(hk * g + i + 1) * HD] for i in range(g)], axis=0)

    sink = [jnp.concatenate([jnp.broadcast_to(sink_ref[:, hk * g + i:hk * g + i + 1], (A_BLOCK, 1))
                             for i in range(g)], axis=0) for hk in range(A_KV)]
    s = [jnp.where(ok[j], _dot_nt(qg(j, hk), k_all[j][:, hk * HD:(hk + 1) * HD]), NEG) for j, hk in chains]
    m = [jnp.maximum(jnp.max(s[i], axis=-1, keepdims=True), sink[hk]) for i, (j, hk) in enumerate(chains)]
    p = [jnp.exp(s[i] - m[i]) for i in range(len(chains))]
    den = [jnp.sum(p[i], axis=-1, keepdims=True) + jnp.exp(sink[hk] - m[i]) for i, (j, hk) in enumerate(chains)]
    o = [_dot(p[i].astype(BF16), v_all[j][:, hk * HD:(hk + 1) * HD]) / den[i] for i, (j, hk) in enumerate(chains)]
    for i, (j, hk) in enumerate(chains):
        for gi in range(g):
            hq = hk * g + gi
            o_ref[j * A_BLOCK:(j + 1) * A_BLOCK, hq * HD:(hq + 1) * HD] = (
                o[i][gi * A_BLOCK:(gi + 1) * A_BLOCK].astype(BF16))


def _attn_a(ab_lat, ab_ctx, sink, nb, s, lc):
    nblk = s // A_BLOCK
    steps = nblk // A_STEP_BLOCKS

    def kv(col, d):
        return pl.BlockSpec(
            (A_BLOCK, 128), lambda b, i: (b * nblk + jnp.clip(i * A_STEP_BLOCKS + d, 0, nblk - 1), col))

    tq = A_STEP_BLOCKS * A_BLOCK
    return pl.pallas_call(
        functools.partial(_attn_a_kernel, nblk=nblk),
        grid=(nb, steps),
        in_specs=[pl.BlockSpec((tq, 256), lambda b, i: (b * steps + i, 0)),
                  *[kv(2, d) for d in range(-1, A_STEP_BLOCKS + 1)], *[kv(3, d) for d in range(-1, A_STEP_BLOCKS + 1)],
                  pl.BlockSpec((lc, 128), lambda b, i: (b, 2)),
                  pl.BlockSpec((lc, 128), lambda b, i: (b, 3)),
                  pl.BlockSpec((1, A_HEADS), lambda b, i: (0, 0))],
        out_specs=pl.BlockSpec((tq, 256), lambda b, i: (b * steps + i, 0)),
        out_shape=jax.ShapeDtypeStruct((nb * s, 256), BF16),
        compiler_params=_cp("parallel", "parallel"),
        name="attn_a",
    )(ab_lat, *([ab_lat] * (2 * (A_STEP_BLOCKS + 2))), ab_ctx, ab_ctx, sink.reshape(1, A_HEADS))


def _attn_b_kernel(q_ref, k_ref, v_ref, kc_ref, vc_ref, bias_ref, o_ref, *, rows):
    nwin = NA_ROWS * GRID_W
    kc = kc_ref[...]
    vc = vc_ref[...]
    q, kw, vw, tab = [], [], [], []
    for rr in range(NA_STEP_ROWS):
        r = pl.program_id(1) * NA_STEP_ROWS + rr
        rs = jnp.clip(r - NA_ROWS // 2, 0, rows - NA_ROWS)
        start = pl.multiple_of(rs * GRID_W, GRID_W)
        tab.append(rs - r + NA_ROWS - 1)
        q.append(q_ref[rr * GRID_W:(rr + 1) * GRID_W, :])
        kw.append(k_ref[pl.ds(start, nwin), :])
        vw.append(v_ref[pl.ds(start, nwin), :])
    chains = [(rr, h) for rr in range(NA_STEP_ROWS) for h in range(B_HEADS)]

    def hs(x, h):
        return x[:, h * HD:(h + 1) * HD]

    s_loc = [_dot_nt(hs(q[rr], h), hs(kw[rr], h)) + bias_ref[tab[rr], h] for rr, h in chains]
    s_ctx = [_dot_nt(hs(q[rr], h), hs(kc, h)) for rr, h in chains]
    m = [jnp.maximum(jnp.max(a, axis=-1, keepdims=True), jnp.max(b, axis=-1, keepdims=True))
         for a, b in zip(s_loc, s_ctx)]
    p_loc = [jnp.exp(a - mm) for a, mm in zip(s_loc, m)]
    p_ctx = [jnp.exp(b - mm) for b, mm in zip(s_ctx, m)]
    den = [jnp.sum(a, axis=-1, keepdims=True) + jnp.sum(b, axis=-1, keepdims=True) for a, b in zip(p_loc, p_ctx)]
    o = [(_dot(p_loc[i].astype(BF16), hs(vw[rr], h)) + _dot(p_ctx[i].astype(BF16), hs(vc, h))) / den[i]
         for i, (rr, h) in enumerate(chains)]
    for i, (rr, h) in enumerate(chains):
        o_ref[rr * GRID_W:(rr + 1) * GRID_W, h * HD:(h + 1) * HD] = o[i].astype(BF16)


def _na_bias_tables(rpb, rows):
    kr = NA_ROWS
    qc = np.arange(GRID_W)
    win_start = np.clip(qc - NA_COLS // 2, 0, GRID_W - NA_COLS)
    kcol = np.arange(GRID_W)
    valid = (kcol[None, :] >= win_start[:, None]) & (kcol[None, :] < win_start[:, None] + NA_COLS)
    pad = GRID_W - NA_COLS
    rp = jnp.pad(rpb.astype(F32), ((0, 0), (0, 0), (pad, pad)))
    toe = jnp.stack([rp[:, :, GRID_W - 1 - q:2 * GRID_W - 1 - q] for q in range(GRID_W)], axis=2)
    toe = jnp.where(jnp.asarray(valid[None, None]), toe, NEG)
    tabs = [toe[:, off:off + kr].transpose(0, 2, 1, 3).reshape(rpb.shape[0], GRID_W, kr * GRID_W)
            for off in range(kr)]
    return jnp.stack(tabs, axis=0)


def _attn_b(ab_lat, ab_ctx, bias_tabs, nb, s, lc):
    rows = s // GRID_W
    steps = rows // NA_STEP_ROWS
    tq = NA_STEP_ROWS * GRID_W
    return pl.pallas_call(
        functools.partial(_attn_b_kernel, rows=rows),
        grid=(nb, steps),
        in_specs=[pl.BlockSpec((tq, 256), lambda b, r: (b * steps + r, 2)),
                  pl.BlockSpec((s, 256), lambda b, r: (b, 3)),
                  pl.BlockSpec((s, 256), lambda b, r: (b, 4)),
                  pl.BlockSpec((lc, 256), lambda b, r: (b, 3)),
                  pl.BlockSpec((lc, 256), lambda b, r: (b, 4)),
                  pl.BlockSpec((NA_ROWS, B_HEADS, GRID_W, NA_ROWS * GRID_W), lambda b, r: (0, 0, 0, 0))],
        out_specs=pl.BlockSpec((tq, 256), lambda b, r: (b * steps + r, 0)),
        out_shape=jax.ShapeDtypeStruct((nb * s, 256), BF16),
        compiler_params=_cp("parallel", "arbitrary"),
        name="attn_b",
    )(ab_lat, ab_lat, ab_lat, ab_ctx, ab_ctx, bias_tabs)


def _attn_ctx_kernel(ab_ref, sink_ref, oa_ref, ob_ref):
    lc = ab_ref.shape[0]
    g = A_HEADS // A_KV
    for hk in range(A_KV):
        qg = jnp.concatenate([ab_ref[:, (hk * g + j) * HD:(hk * g + j + 1) * HD] for j in range(g)], axis=0)
        s = _dot_nt(qg, ab_ref[:, 256 + hk * HD:256 + (hk + 1) * HD])
        sink = jnp.concatenate(
            [jnp.broadcast_to(sink_ref[:, hk * g + j:hk * g + j + 1], (lc, 1)) for j in range(g)], axis=0)
        o = _softmax_pv(s, ab_ref[:, 384 + hk * HD:384 + (hk + 1) * HD], sink)
        for j in range(g):
            hq = hk * g + j
            oa_ref[:, hq * HD:(hq + 1) * HD] = o[j * lc:(j + 1) * lc].astype(BF16)
    for h in range(B_HEADS):
        s = _dot_nt(ab_ref[:, 512 + h * HD:512 + (h + 1) * HD], ab_ref[:, 768 + h * HD:768 + (h + 1) * HD])
        o = _softmax_pv(s, ab_ref[:, 1024 + h * HD:1024 + (h + 1) * HD])
        ob_ref[:, h * HD:(h + 1) * HD] = o.astype(BF16)


def _attn_ctx(ab_ctx, sink, nb, lc):
    return pl.pallas_call(
        _attn_ctx_kernel,
        grid=(nb,),
        in_specs=[pl.BlockSpec((lc, AB_W), lambda b: (b, 0)),
                  pl.BlockSpec((1, A_HEADS), lambda b: (0, 0))],
        out_specs=[pl.BlockSpec((lc, 256), lambda b: (b, 0)),
                   pl.BlockSpec((lc, 256), lambda b: (b, 0))],
        out_shape=[jax.ShapeDtypeStruct((nb * lc, 256), BF16),
                   jax.ShapeDtypeStruct((nb * lc, 256), BF16)],
        compiler_params=_cp("parallel"),
        name="attn_ctx",
    )(ab_ctx, sink.reshape(1, A_HEADS))


def _r7prep_kernel(x_ref, prev_ref, next_ref, cw_ref, kkw_ref, ka_ref, rk_ref, w0_ref, w2_ref, a0_ref, a2_ref,
                   g2_ref, bd_ref,
                   r_ref, v_ref, kk_ref, lw_ref, beta_ref, kd_ref, g_ref, bonus_ref, *, nt):
    i = pl.program_id(1)
    tt = x_ref.shape[0]
    x = x_ref[...]
    row = lax.broadcasted_iota(jnp.int32, x.shape, 0)
    before = jnp.where(i > 0, prev_ref[7:8, :], 0.0)
    after = jnp.where(i < nt - 1, next_ref[0:1, :], 0.0)
    x_m1 = jnp.where(row == 0, before, pltpu.roll(x, 1, 0))
    x_p1 = jnp.where(row == tt - 1, after, pltpu.roll(x, tt - 1, 0))
    xc = x_m1 * cw_ref[0:1, :] + x * cw_ref[1:2, :] + x_p1 * cw_ref[2:3, :]
    r = xc[:, 0:512]
    k = xc[:, 512:1024]
    v = xc[:, 1024:1536]
    wd = xc[:, 1536:1664]
    ad = xc[:, 1664:1792]
    gd = xc[:, 1792:1920]
    bd = bd_ref[...]
    kkh = k * kkw_ref[...]
    kk = kkh / jnp.maximum(jnp.sqrt(_segsum(kkh * kkh, bd)), 1e-12)
    zw = w0_ref[...] + _dot3(jnp.tanh(wd), w2_ref[...])
    za = a0_ref[...] + _dot3(ad, a2_ref[...])
    a = jax.nn.sigmoid(za)
    ka = ka_ref[...]
    kd0 = k * (1.0 + (a[:, 0:512] - 1.0) * ka)
    kd1 = k * (1.0 + (a[:, 512:1024] - 1.0) * ka)
    r_ref[...] = r.astype(BF16)
    v_ref[...] = v.astype(BF16)
    kk_ref[...] = kk.astype(BF16)
    lw_ref[...] = -DECAY_SCALE * jax.nn.sigmoid(zw)
    beta_ref[:, 0:512] = (kk * a[:, 0:512]).astype(BF16)
    beta_ref[:, 512:1024] = (kk * a[:, 512:1024]).astype(BF16)
    kd_ref[:, 0:512] = kd0.astype(BF16)
    kd_ref[:, 512:1024] = kd1.astype(BF16)
    g_ref[...] = _dot3(jax.nn.sigmoid(gd), g2_ref[...])
    bonus_ref[...] = _segsum(r * (kd0 + kd1) * rk_ref[...], bd) * v


def _bd2(w):
    z = jnp.zeros_like(w[0])
    return jnp.concatenate([jnp.concatenate([w[0], z], axis=1), jnp.concatenate([z, w[1]], axis=1)], axis=0)


def _r7prep(pc, lp, nb, t):
    tt = _row_tile(t)
    nt = t // tt
    nrow8 = t // 8

    def full(shape):
        return pl.BlockSpec(shape, lambda b, i: (0,) * len(shape))

    def rowspec(w):
        return pl.BlockSpec((tt, w), lambda b, i: (b * nt + i, 0))

    outs = [C_W, C_W, C_W, 2 * C_W, 2 * C_W, 2 * C_W, C_W, C_W]
    dtypes = [BF16, BF16, BF16, F32, BF16, BF16, F32, F32]
    return pl.pallas_call(
        functools.partial(_r7prep_kernel, nt=nt),
        grid=(nb, nt),
        in_specs=[rowspec(C_IN),
                  pl.BlockSpec((8, C_IN), lambda b, i: (b * nrow8 + jnp.maximum(i * (tt // 8) - 1, 0), 0)),
                  pl.BlockSpec((8, C_IN), lambda b, i: (b * nrow8 + jnp.minimum((i + 1) * (tt // 8), nrow8 - 1), 0)),
                  full((3, C_IN)), full((1, C_W)), full((1, C_W)), full((1, C_W)),
                  full((1, 2 * C_W)), full((128, 2 * C_W)), full((1, 2 * C_W)), full((128, 2 * C_W)),
                  full((128, C_W)), full((SEG_W, SEG_W))],
        out_specs=[rowspec(w) for w in outs],
        out_shape=[jax.ShapeDtypeStruct((nb * t, w), dt) for w, dt in zip(outs, dtypes)],
        compiler_params=_cp("parallel", "parallel"),
        name="r7prep",
    )(pc, pc, pc, lp['r7_conv'], lp['r7_kk'].reshape(1, C_W), lp['r7_ka'].reshape(1, C_W),
      lp['r7_rk'].reshape(1, C_W), lp['r7_w0'].reshape(1, 2 * C_W), _bd2(lp['r7_w2']),
      lp['r7_a0'].reshape(1, 2 * C_W), _bd2(lp['r7_a2']), lp['r7_g2'], _block_diag_ones())


def _cumsum_rows(tri, x):
    t = tri.astype(BF16)
    hi = x.astype(BF16)
    rest = x - hi.astype(F32)
    mid = rest.astype(BF16)
    lo = (rest - mid.astype(F32)).astype(BF16)
    return _dot(t, hi) + (_dot(t, mid) + _dot(t, lo))


def _mm(a, b):
    return _dot(a.astype(BF16), b.astype(BF16))


def _mm_nt(a, b):
    return _dot_nt(a.astype(BF16), b.astype(BF16))


def _scan_kernel(rf_ref, vf_ref, kkf_ref, lwf_ref, betaf_ref, kdf_ref,
                 rb_ref, vb_ref, kkb_ref, lwb_ref, betab_ref, kdb_ref, s0_ref,
                 yf_ref, yb_ref, sf_ref, st_ref, *, nc):
    c = pl.program_id(1)
    cl = SCAN_C

    @pl.when(c == 0)
    def _():
        st_ref[...] = s0_ref[...]

    ti = lax.broadcasted_iota(jnp.int32, (cl, cl), 0)
    si = lax.broadcasted_iota(jnp.int32, (cl, cl), 1)
    eye = (ti == si).astype(F32)
    incl = [si <= ti, si >= ti]
    strict = [si < ti, si > ti]

    def hs(x, h):
        return x[:, h * HD:(h + 1) * HD]

    pre = {}
    for bb in range(SCAN_NB):
        for d, (r_ref, v_ref, kk_ref, lw_ref, beta_ref, kd_ref) in enumerate(
                [(rf_ref, vf_ref, kkf_ref, lwf_ref, betaf_ref, kdf_ref),
                 (rb_ref, vb_ref, kkb_ref, lwb_ref, betab_ref, kdb_ref)]):
            lw = lw_ref[bb]
            cum = _cumsum_rows(incl[d], lw)
            e_neg = jnp.exp(-cum)
            pre[bb, d] = dict(r_hat=r_ref[bb] * jnp.exp(cum), a_hat=-kk_ref[bb] * jnp.exp(cum - lw),
                              b_til=beta_ref[bb] * e_neg, k_til=kd_ref[bb] * e_neg,
                              wtot=jnp.exp(jnp.sum(lw, axis=0, keepdims=True)), vv=v_ref[bb])

    chains = [(bb, d, h) for bb in range(SCAN_NB) for d in range(2) for h in range(C_HEADS)]
    n_ch = range(len(chains))

    def part(name, i):
        bb, d, h = chains[i]
        return hs(pre[bb, d][name], h)

    def dirn(i):
        return chains[i][1]

    ar = [jnp.concatenate([part('a_hat', i), part('r_hat', i)], axis=0) for i in n_ch]
    bk = [jnp.concatenate([part('b_til', i), part('k_til', i)], axis=0) for i in n_ch]
    m = [_mm_nt(ar[i], bk[i]) for i in n_ch]
    l_k = [jnp.where(strict[dirn(i)], m[i][0:cl, cl:2 * cl], 0.0) for i in n_ch]
    r_b = [jnp.where(incl[dirn(i)], m[i][cl:2 * cl, 0:cl], 0.0) for i in n_ch]
    r_k = [jnp.where(incl[dirn(i)], m[i][cl:2 * cl, cl:2 * cl], 0.0) for i in n_ch]
    l_mat = [jnp.where(strict[dirn(i)], m[i][0:cl, 0:cl], 0.0) for i in n_ch]
    halves_of = lambda b: ((ti // (2 * b)) == (si // (2 * b))) & ((ti // b) != (si // b))
    t_inv = [eye + jnp.where(halves_of(1), l_mat[i], 0.0) for i in n_ch]
    b = 2
    while b < cl:
        couple = halves_of(b)
        nt_ = [_mm(jnp.where(couple, l_mat[i], 0.0), t_inv[i]) for i in n_ch]
        t_inv = [t_inv[i] + _mm(t_inv[i], nt_[i]) for i in n_ch]
        b *= 2
    x1 = [_mm(l_k[i], part('vv', i)) for i in n_ch]
    y0 = [_mm(r_k[i], part('vv', i)) for i in n_ch]
    ua = [_mm(t_inv[i], jnp.concatenate([x1[i], part('a_hat', i)], axis=1)) for i in n_ch]
    s0 = [st_ref[d, bb, h] for bb, d, h in chains]
    as0 = [_mm_nt(jnp.concatenate([ua[i][:, HD:2 * HD], part('r_hat', i)], axis=0), s0[i])
           for i in n_ch]
    u = [ua[i][:, 0:HD] + as0[i][0:cl] for i in n_ch]
    y = [y0[i] + as0[i][cl:2 * cl] + _mm(r_b[i], u[i]) for i in n_ch]
    for i in n_ch:
        bb, d, h = chains[i]
        (yf_ref, yb_ref)[d][bb, :, h * HD:(h + 1) * HD] = y[i]
    for i in n_ch:
        bb, d, h = chains[i]
        uv_t = jnp.concatenate([u[i], part('vv', i)], axis=0).T
        st_ref[d, bb, h] = (s0[i] + _mm(uv_t, bk[i])) * part('wtot', i)

    @pl.when(c == nc - 1)
    def _():
        sf_ref[...] = st_ref[...]


def _scan(prep, s0, nb, t):
    nc = t // SCAN_C
    assert nb % SCAN_NB == 0
    r, v, kk, lw, beta, kd = [a.reshape(nb, t, a.shape[-1]) for a in prep]

    def chunk(d, c):
        return nc - 1 - c if d else c

    def specs(d):
        shared = pl.BlockSpec((SCAN_NB, SCAN_C, C_W), lambda b, c: (b, chunk(d, c), 0))
        perdir = pl.BlockSpec((SCAN_NB, SCAN_C, C_W), lambda b, c: (b, chunk(d, c), d))
        return [shared, shared, shared, perdir, perdir, perdir]

    st_spec = pl.BlockSpec((2, SCAN_NB, C_HEADS, HD, HD), lambda b, c: (0, b, 0, 0, 0))
    yf, yb, sf = pl.pallas_call(
        functools.partial(_scan_kernel, nc=nc),
        grid=(nb // SCAN_NB, nc),
        in_specs=specs(0) + specs(1) + [st_spec],
        out_specs=[pl.BlockSpec((SCAN_NB, SCAN_C, C_W), lambda b, c: (b, chunk(0, c), 0)),
                   pl.BlockSpec((SCAN_NB, SCAN_C, C_W), lambda b, c: (b, chunk(1, c), 0)), st_spec],
        out_shape=[jax.ShapeDtypeStruct((nb, t, C_W), F32), jax.ShapeDtypeStruct((nb, t, C_W), F32),
                   jax.ShapeDtypeStruct((2, nb, C_HEADS, HD, HD), F32)],
        scratch_shapes=[pltpu.VMEM((2, SCAN_NB, C_HEADS, HD, HD), F32)],
        compiler_params=_cp("parallel", "arbitrary"),
        name="r7scan",
    )(r, v, kk, lw, beta, kd, r, v, kk, lw, beta, kd, s0)
    return (yf.reshape(nb * t, C_W), yb.reshape(nb * t, C_W)), sf


def _outproj_kernel(x_ref, oa_ref, ob_ref, y0_ref, y1_ref, bonus_ref, g_ref, lnw_ref, lnb_ref, bd_ref, w_ref,
                    gate_ref, o_ref):
    bd = bd_ref[...]
    y = y0_ref[...] + y1_ref[...]
    mu = _segsum(y, bd) * (1.0 / HD)
    yc = y - mu
    var = _segsum(yc * yc, bd) * (1.0 / HD)
    yn = yc * lax.rsqrt(var + GN_EPS) * lnw_ref[...] + lnb_ref[...]
    oc = ((yn + bonus_ref[...]) * g_ref[...]).astype(BF16)
    acc = (_dot(oa_ref[...], w_ref[0:256, :]) + _dot(ob_ref[...], w_ref[256:512, :])
           + _dot(oc, w_ref[512:1024, :]))
    o_ref[...] = x_ref[...] + gate_ref[...] * acc


def _outproj(x2, oa, ob, y, bonus, g, lp, w_out_bf, mod3, mod_row_of_batch, nb, t):
    tm = _row_tile(t)
    nt = t // tm

    def rowspec(w):
        return pl.BlockSpec((tm, w), lambda b, i: (b * nt + i, 0))

    def full(shape):
        return pl.BlockSpec(shape, lambda b, i: (0,) * len(shape))

    return pl.pallas_call(
        _outproj_kernel,
        grid=(nb, nt),
        in_specs=[rowspec(D), rowspec(256), rowspec(256),
                  rowspec(C_W), rowspec(C_W),
                  rowspec(C_W), rowspec(C_W), full((1, C_W)), full((1, C_W)), full((SEG_W, SEG_W)), full((D, D)),
                  pl.BlockSpec((None, 1, D), lambda b, i: (mod_row_of_batch(b), 0, 2))],
        out_specs=rowspec(D),
        out_shape=jax.ShapeDtypeStruct((nb * t, D), F32),
        compiler_params=_cp("parallel", "parallel"),
        name="outproj",
    )(x2, oa, ob, y[0], y[1], bonus, g, lp['r7_lnw'].reshape(1, C_W), lp['r7_lnb'].reshape(1, C_W),
      _block_diag_ones(), w_out_bf, mod3)


def _peer_score_kernel(x_ref, g_ref, sc_ref, sh_ref, wq_ref, keys_ref, h_ref, st_ref):
    x = x_ref[...]
    y = x * lax.rsqrt(jnp.mean(x * x, axis=-1, keepdims=True) + EPS) * g_ref[...]
    h = (y * (1.0 + sc_ref[...]) + sh_ref[...]).astype(BF16)
    h_ref[...] = h
    q = _dot(h, wq_ref[...])
    half = P_QDIM // 2
    for hp in range(2 * P_HEADS):
        st_ref[hp] = _dot3(keys_ref[hp], q[:, hp * half:(hp + 1) * half], nt=True)


def _peer_scores(x2, norm_g, mod3, mod_row_of_batch, wq_bf, keys, nb, t):
    tm = _row_tile(t)
    nt = t // tm
    n = nb * t
    return pl.pallas_call(
        _peer_score_kernel,
        grid=(nb, nt),
        in_specs=[pl.BlockSpec((tm, D), lambda b, i: (b * nt + i, 0)),
                  pl.BlockSpec((1, D), lambda b, i: (0, 0)),
                  pl.BlockSpec((None, 1, D), lambda b, i: (mod_row_of_batch(b), 0, 4)),
                  pl.BlockSpec((None, 1, D), lambda b, i: (mod_row_of_batch(b), 0, 3)),
                  pl.BlockSpec((D, P_HEADS * P_QDIM), lambda b, i: (0, 0)),
                  pl.BlockSpec((2 * P_HEADS, P_NKEYS, P_QDIM // 2), lambda b, i: (0, 0, 0))],
        out_specs=[pl.BlockSpec((tm, D), lambda b, i: (b * nt + i, 0)),
                   pl.BlockSpec((2 * P_HEADS, P_NKEYS, tm), lambda b, i: (0, 0, b * nt + i))],
        out_shape=[jax.ShapeDtypeStruct((n, D), BF16),
                   jax.ShapeDtypeStruct((2 * P_HEADS, P_NKEYS, n), F32)],
        compiler_params=_cp("parallel", "parallel"),
        name="peer_scores",
    )(x2, norm_g.reshape(1, D), mod3, mod3, wq_bf, keys.reshape(2 * P_HEADS, P_NKEYS, P_QDIM // 2))


def _batcher_pairs(n):
    pairs = []
    p = 1
    while p < n:
        k = p
        while k >= 1:
            for j in range(k % p, n - k, 2 * k):
                for i in range(min(k, n - j - k)):
                    if (i + j) // (2 * p) == (i + j + k) // (2 * p):
                        pairs.append((i + j, i + j + k))
            k //= 2
        p *= 2
    return pairs


_SORT16 = _batcher_pairs(P_TOPK)
_BITONIC16 = [(i, i + d) for d in (8, 4, 2, 1) for i in range(P_TOPK) if i & d == 0]


def _compare_exchange(x, pairs):
    for i, j in pairs:
        x[i], x[j] = jnp.maximum(x[i], x[j]), jnp.minimum(x[i], x[j])
    return x


def _top16_sorted(slabs):
    x = _compare_exchange(list(slabs), _SORT16)
    for shift in (4, 2, 1):
        y = [pltpu.roll(x[P_TOPK - 1 - k], shift, 0) for k in range(P_TOPK)]
        x = [jnp.maximum(x[k], y[k]) for k in range(P_TOPK)]
        x = _compare_exchange(x, _BITONIC16)
    return x


def _peer_gate_tables(st_ref, rk_ref, be_ref, cnt_ref, al_ref, top_ref):
    nblk = st_ref.shape[-1] // LANES
    sub = lax.broadcasted_iota(jnp.int32, (8, LANES), 0)
    ninf = jnp.full((8, LANES), -jnp.inf, F32)

    def block(it, carry):
        h = it // nblk
        lb = it % nblk
        lanes = pl.ds(pl.multiple_of(lb * LANES, LANES), LANES)
        hf = lb // (PEER_HALF // LANES)
        hl = pl.ds(pl.multiple_of((lb % (PEER_HALF // LANES)) * LANES, LANES), LANES)
        for p in range(2):
            top = _top16_sorted([st_ref[2 * h + p, 8 * k:8 * k + 8, lanes] for k in range(P_TOPK)])
            for k in range(P_TOPK):
                top_ref[p * P_TOPK + k:p * P_TOPK + k + 1, lanes] = top[k][0:1]
        a16 = top_ref[0:P_TOPK, lanes]
        b16 = top_ref[P_TOPK:2 * P_TOPK, lanes]
        b8 = b16[0:8]
        cand = [a16[0:1] + b8, a16[0:1] + b16[8:16]]
        for p in range(2, 9):
            cand.append(jnp.where(sub < P_TOPK // p, a16[p - 1:p] + b8, -jnp.inf))
        cand.append(a16[8:16] + b16[0:1])
        best = _top16_sorted(cand + [ninf] * (P_TOPK - len(cand)))
        tau = best[P_TOPK - 1][0:1]
        z = jnp.ones_like(tau)
        for k in range(1, P_TOPK):
            z = z + jnp.exp(best[k][0:1] - best[0][0:1])
        s1 = st_ref[2 * h, :, lanes]
        s2 = st_ref[2 * h + 1, :, lanes]
        cnt = jnp.zeros_like(s1)
        rk = jnp.ones_like(s2)
        for q in range(P_TOPK):
            bq = b16[q:q + 1]
            theta = jnp.min(jnp.where(a16 + bq >= tau, a16, jnp.inf), axis=0, keepdims=True)
            cnt = jnp.where(s1 >= theta, q + 1.0, cnt)
            rk = jnp.where(bq > s2, q + 2.0, rk)
        cnt_ref[hf, h, :, hl] = cnt
        rk_ref[hf, h, :, hl] = rk.astype(BF16)
        al_ref[hf, h, :, hl] = jnp.exp(s1 - a16[0:1]) / z
        be_ref[hf, h, :, hl] = jnp.exp(s2 - b16[0:1]).astype(BF16)
        return carry

    lax.fori_loop(0, P_HEADS * nblk, block, 0)


def _peer_chunk(chunk, act_ref, rk_ref, be_ref, cnt_ref, al_ref, w_ref):
    nrow = w_ref.shape[0] // P_NKEYS
    for ii in range(nrow):
        i = chunk * nrow + ii
        wrow = None
        for h in range(P_HEADS):
            cnt = cnt_ref[h, pl.ds(i, 1), :].astype(BF16)
            al = al_ref[h, pl.ds(i, 1), :].astype(BF16)
            term = jnp.where(rk_ref[h] <= cnt, al * be_ref[h], 0.0)
            wrow = term if wrow is None else wrow + term
        rows = slice(ii * P_NKEYS, (ii + 1) * P_NKEYS)
        act = act_ref[rows, :]
        gl = 0.5 * act * (1.0 + lax.erf(act * (2.0 ** -0.5)))
        w_ref[rows, :] = wrow * gl.astype(BF16)


def _peer_dense_kernel(x_ref, h_ref, st_ref, u_ref, vt_ref, gate_ref, o_ref,
                       rk_ref, be_ref, cnt_ref, al_ref, top_ref, act0_ref, act1_ref, w_ref, acc_ref, *, ne):
    e = pl.program_id(1)
    halves = range(w_ref.shape[0])

    def rows(hf):
        return slice(hf * PEER_HALF, (hf + 1) * PEER_HALF)

    @pl.when(e == 0)
    def _():
        for hf in halves:
            act0_ref[hf] = _dot_nt(u_ref[...], h_ref[rows(hf), :])
        _peer_gate_tables(st_ref, rk_ref, be_ref, cnt_ref, al_ref, top_ref)
        acc_ref[...] = jnp.zeros_like(acc_ref)

    def step(cur_ref, nxt_ref):
        for hf in halves:
            nxt_ref[hf] = _dot_nt(u_ref[...], h_ref[rows(hf), :])
            _peer_chunk(e - 1, cur_ref.at[hf], rk_ref.at[hf], be_ref.at[hf], cnt_ref.at[hf], al_ref.at[hf],
                        w_ref.at[hf])
            acc_ref[hf] += _dot(vt_ref[...], w_ref[hf])

    @pl.when(e % 2 == 1)
    def _():
        step(act0_ref, act1_ref)

    @pl.when((e > 0) & (e % 2 == 0))
    def _():
        step(act1_ref, act0_ref)

    @pl.when(e == ne)
    def _():
        for hf in halves:
            o_ref[rows(hf), :] = x_ref[rows(hf), :] + gate_ref[...] * acc_ref[hf].T


def _peer_dense(x2, h2, st, u_bf, vt_bf, mod3, mod_row_of_batch, nb, t):
    tm = _row_tile(t)
    nh = tm // PEER_HALF
    ec = PEER_EC
    nt = t // tm
    ne = P_EXPERTS // ec
    return pl.pallas_call(
        functools.partial(_peer_dense_kernel, ne=ne),
        grid=(nb * nt, ne + 1),
        in_specs=[pl.BlockSpec((tm, D), lambda i, e: (i, 0)),
                  pl.BlockSpec((tm, D), lambda i, e: (i, 0)),
                  pl.BlockSpec((2 * P_HEADS, P_NKEYS, tm), lambda i, e: (0, 0, i)),
                  pl.BlockSpec((ec, D), lambda i, e: (jnp.minimum(e, ne - 1), 0)),
                  pl.BlockSpec((None, D, ec), lambda i, e: (jnp.maximum(e - 1, 0), 0, 0)),
                  pl.BlockSpec((None, 1, D), lambda i, e: (mod_row_of_batch(i // nt), 0, 5))],
        out_specs=pl.BlockSpec((tm, D), lambda i, e: (i, 0)),
        out_shape=jax.ShapeDtypeStruct((nb * t, D), F32),
        scratch_shapes=[pltpu.VMEM((nh, P_HEADS, P_NKEYS, PEER_HALF), BF16),
                        pltpu.VMEM((nh, P_HEADS, P_NKEYS, PEER_HALF), BF16),
                        pltpu.VMEM((nh, P_HEADS, P_NKEYS, PEER_HALF), F32),
                        pltpu.VMEM((nh, P_HEADS, P_NKEYS, PEER_HALF), F32),
                        pltpu.VMEM((2 * P_TOPK, tm), F32),
                        pltpu.VMEM((nh, ec, PEER_HALF), F32),
                        pltpu.VMEM((nh, ec, PEER_HALF), F32),
                        pltpu.VMEM((nh, ec, PEER_HALF), BF16),
                        pltpu.VMEM((nh, D, PEER_HALF), F32)],
        compiler_params=pltpu.CompilerParams(dimension_semantics=("parallel", "arbitrary"),
                                             vmem_limit_bytes=PEER_VMEM_LIMIT),
        name="peer_dense",
    )(x2, h2, st, u_bf, vt_bf, mod3)


def _peer(x2, norm_g, mod3, mod_row_of_batch, wq_bf, keys, u_bf, vt_bf, nb, t):
    h2, st = _peer_scores(x2, norm_g, mod3, mod_row_of_batch, wq_bf, keys, nb, t)
    return _peer_dense(x2, h2, st, u_bf, vt_bf, mod3, mod_row_of_batch, nb, t)


def kernel(x, c, ctx, c_ctx, norm_mix, norm_ffn, w_mod, b_mod, w_in, w_out, a_qnorm, a_knorm, a_sink, b_qnorm,
           b_knorm, b_rpb, r7_conv, r7_w0, r7_w2, r7_a0, r7_a2, r7_g2, r7_kk, r7_ka, r7_rk, r7_lnw, r7_lnb,
           peer_wq, peer_keys, peer_u, peer_v):
    nb, s, _ = x.shape
    lc = ctx.shape[1]
    depth = w_in.shape[0]
    assert nb < 16 and nb % SCAN_NB == 0 and s % 512 == 0 and lc % 256 == 0
    rows = s // GRID_W

    cc = jnp.zeros((16, D), F32).at[:nb].set(c).at[nb].set(c_ctx)
    mod = _modulation(cc, w_mod, b_mod)
    rope_tabs = _rope_tables(s)
    lat_row = lambda b: b
    ctx_row = lambda b: nb

    x_lat = x.reshape(nb * s, D)
    x_ctx = ctx.reshape(nb * lc, D)
    scale = HD ** -0.5
    for l in range(depth):
        with_ctx = l < depth - 1
        mod3 = mod[l].reshape(16, 1, 6 * D)
        lp = {'r7_conv': r7_conv[l], 'r7_w0': r7_w0[l], 'r7_w2': r7_w2[l], 'r7_a0': r7_a0[l], 'r7_a2': r7_a2[l],
              'r7_g2': r7_g2[l], 'r7_kk': r7_kk[l], 'r7_ka': r7_ka[l], 'r7_rk': r7_rk[l], 'r7_lnw': r7_lnw[l],
              'r7_lnb': r7_lnb[l]}
        w_in_bf = w_in[l].astype(BF16)
        w_out_bf = w_out[l].astype(BF16)
        gain_a = jnp.concatenate([jnp.tile(a_qnorm[l] * scale, A_HEADS), jnp.tile(a_knorm[l], A_KV)]).reshape(1, 384)
        gain_b = jnp.concatenate([jnp.tile(b_qnorm[l] * scale, B_HEADS), jnp.tile(b_knorm[l], B_HEADS)]).reshape(1, 512)

        ab_lat, pc_lat = _inproj(x_lat, mod3, lat_row, norm_mix[l], w_in_bf, gain_a, gain_b, rope_tabs, nb, s)
        ab_ctx, pc_ctx = _inproj(x_ctx, mod3, ctx_row, norm_mix[l], w_in_bf, gain_a, gain_b, None, 1, nb * lc)

        o_a = _attn_a(ab_lat, ab_ctx, a_sink[l], nb, s, lc)
        o_b = _attn_b(ab_lat, ab_ctx, _na_bias_tables(b_rpb[l], rows), nb, s, lc)

        prep_ctx = _r7prep(pc_ctx, lp, nb, lc)
        prep_lat = _r7prep(pc_lat, lp, nb, s)
        zero_state = jnp.zeros((2, nb, C_HEADS, HD, HD), F32)
        y_ctx, s_ctx = _scan(prep_ctx[:6], zero_state, nb, lc)
        y_lat, _ = _scan(prep_lat[:6], s_ctx, nb, s)

        x_lat = _outproj(x_lat, o_a, o_b, y_lat, prep_lat[7], prep_lat[6], lp, w_out_bf, mod3, lat_row, nb, s)
        wq_bf = peer_wq[l].astype(BF16)
        u_bf = peer_u[l].astype(BF16)
        vt_bf = peer_v[l].astype(BF16).reshape(P_EXPERTS // PEER_EC, PEER_EC, D).transpose(0, 2, 1)
        x_lat = _peer(x_lat, norm_ffn[l], mod3, lat_row, wq_bf, peer_keys[l], u_bf, vt_bf, nb, s)
        if with_ctx:
            o_ac, o_bc = _attn_ctx(ab_ctx, a_sink[l], nb, lc)
            x_ctx = _outproj(x_ctx, o_ac, o_bc, y_ctx, prep_ctx[7], prep_ctx[6], lp, w_out_bf, mod3, ctx_row,
                             1, nb * lc)
            x_ctx = _peer(x_ctx, norm_ffn[l], mod3, ctx_row, wq_bf, peer_keys[l], u_bf, vt_bf, 1, nb * lc)
    return x_lat.reshape(nb, s, D)
```
